```python
import jax, jax.numpy as jnp
from jax import lax
import numpy as np

D_MODEL = 2048
BATCH = 4
SEQ = 4096
DEPTH = 2

N_A_LAYERS = DEPTH // 2
N_B_LAYERS = DEPTH - N_A_LAYERS
N_MEM = 256
MEM_HEADS = 4
MEM_HEAD_DIM = D_MODEL // 16
MEM_WIDTH = MEM_HEADS * MEM_HEAD_DIM
BRANCH_WIDTH = D_MODEL - MEM_WIDTH
GLA_HEADS = 4
GLA_DV = BRANCH_WIDTH // GLA_HEADS
GLA_DK = GLA_DV // 2
GLA_GATE_RANK = 16
GLA_GATE_NORM = 16.0
GLA_CHUNK = 64
MLA_HEADS = 12
MLA_V_DIM = BRANCH_WIDTH // MLA_HEADS
MLA_NOPE_DIM = 128
MLA_ROPE_DIM = 64
MLA_Q_RANK = D_MODEL // 4
MLA_KV_RANK = D_MODEL // 4
ROPE_THETA = 10000.0
Q_BLOCK = 128
EPS = 1e-6

A_SPLITS = [GLA_HEADS * GLA_DK, GLA_HEADS * GLA_DK, BRANCH_WIDTH, GLA_GATE_RANK,
            BRANCH_WIDTH, MEM_WIDTH, MEM_WIDTH]
A_IN_WIDTH = sum(A_SPLITS)
B_SPLITS = [MLA_Q_RANK, BRANCH_WIDTH, MEM_WIDTH, MEM_WIDTH]
B_IN_WIDTH = sum(B_SPLITS)
MLA_QK_DIM = MLA_NOPE_DIM + MLA_ROPE_DIM

kernel_name = 'yoco_gla_mla_memory_hybrid'


def _split(t, sizes):
    return jnp.split(t, list(np.cumsum(sizes)[:-1]), axis=-1)


def rmsnorm(x, g):
    xf = x.astype(jnp.float32)
    xf = xf * lax.rsqrt(jnp.mean(xf * xf, axis=-1, keepdims=True) + EPS)
    return (xf * g.astype(jnp.float32)).astype(x.dtype)


def rope(x, positions):
    r = x.shape[-1]
    freqs = ROPE_THETA ** (-jnp.arange(0, r, 2, dtype=jnp.float32) / r)
    ang = positions.astype(jnp.float32)[..., None] * freqs
    ang = ang.reshape(ang.shape[:2] + (1,) * (x.ndim - 3) + (r // 2,))
    cos, sin = jnp.cos(ang), jnp.sin(ang)
    xf = x.astype(jnp.float32)
    x1, x2 = xf[..., : r // 2], xf[..., r // 2:]
    return jnp.concatenate([x1 * cos - x2 * sin, x2 * cos + x1 * sin], axis=-1).astype(x.dtype)


def mem_attend(q, mem, mem_g, w_kv):
    b_, s_ = q.shape[:2]
    m_ = mem.shape[1]
    k, v = jnp.split(rmsnorm(mem, mem_g) @ w_kv, 2, axis=-1)
    k = k.reshape(b_, m_, MEM_HEADS, MEM_HEAD_DIM)
    v = v.reshape(b_, m_, MEM_HEADS, MEM_HEAD_DIM)
    qh = q.reshape(b_, s_, MEM_HEADS, MEM_HEAD_DIM)
    s = jnp.einsum('bshd,bmhd->bhsm', qh, k).astype(jnp.float32) * MEM_HEAD_DIM ** -0.5
    p = jax.nn.softmax(s, axis=-1).astype(v.dtype)
    return jnp.einsum('bhsm,bmhd->bshd', p, v).reshape(b_, s_, MEM_WIDTH)


def gla_chunked(q, k, v, log_a):
    b_, s_ = q.shape[:2]
    nc = s_ // GLA_CHUNK

    def to_chunks(t):
        return t.astype(jnp.float32).reshape(b_, nc, GLA_CHUNK, GLA_HEADS, t.shape[-1]).transpose(1, 0, 3, 2, 4)

    qc, kc, vc, gc = to_chunks(q), to_chunks(k), to_chunks(v), to_chunks(log_a)
    causal = jnp.tril(jnp.ones((GLA_CHUNK, GLA_CHUNK), dtype=bool))[None, None, :, :, None]

    def step(state, inp):
        qi, ki, vi, gi = inp
        cum = jnp.cumsum(gi, axis=2)
        o_inter = jnp.einsum('bhck,bhkv->bhcv', qi * jnp.exp(cum), state)
        diff = cum[:, :, :, None, :] - cum[:, :, None, :, :]
        decay = jnp.where(causal, jnp.exp(jnp.where(causal, diff, 0.0)), 0.0)
        attn = jnp.einsum('bhtk,bhsk,bhtsk->bhts', qi, ki, decay)
        o_intra = jnp.einsum('bhts,bhsv->bhtv', attn, vi)
        last = cum[:, :, -1:, :]
        state = jnp.exp(last[:, :, 0, :])[..., None] * state + jnp.einsum(
            'bhsk,bhsv->bhkv', ki * jnp.exp(last - cum), vi)
        return state, o_inter + o_intra

    s0 = jnp.zeros((b_, GLA_HEADS, GLA_DK, GLA_DV), jnp.float32)
    _, o = lax.scan(step, s0, (qc, kc, vc, gc))
    return o.transpose(1, 0, 3, 2, 4).reshape(b_, s_, GLA_HEADS, GLA_DV).astype(v.dtype)


def layer_a(x, mem, pre_g, w_in, w_g2, b_g, gla_g, mem_g, w_mem_kv, w_out, post_g):
    b_, s_ = x.shape[:2]
    h = rmsnorm(x, pre_g)
    q, k, v, g_lr, z, mq, mz = _split(h @ w_in, A_SPLITS)
    q = q.reshape(b_, s_, GLA_HEADS, GLA_DK) * GLA_DK ** -0.5
    k = k.reshape(b_, s_, GLA_HEADS, GLA_DK)
    v = v.reshape(b_, s_, GLA_HEADS, GLA_DV)
    log_a = jax.nn.log_sigmoid((g_lr @ w_g2 + b_g).astype(jnp.float32)) / GLA_GATE_NORM
    log_a = log_a.reshape(b_, s_, GLA_HEADS, GLA_DK)
    o = rmsnorm(gla_chunked(q, k, v, log_a), gla_g)
    gla_out = o.reshape(b_, s_, BRANCH_WIDTH) * jax.nn.silu(z)
    mem_out = mem_attend(mq, mem, mem_g, w_mem_kv) * jax.nn.silu(mz)
    y = jnp.concatenate([gla_out, mem_out], axis=-1) @ w_out
    return x + rmsnorm(y, post_g)


def shared_mla_kv(x, positions, kv_in_g, w_dkv, kv_g, w_uk, w_uv):
    b_, s_ = x.shape[:2]
    c, k_rope = _split(rmsnorm(x, kv_in_g) @ w_dkv, [MLA_KV_RANK, MLA_ROPE_DIM])
    c = rmsnorm(c, kv_g)
    k_nope = (c @ w_uk).reshape(b_, s_, MLA_HEADS, MLA_NOPE_DIM)
    v = (c @ w_uv).reshape(b_, s_, MLA_HEADS, MLA_V_DIM)
    k_rope = rope(k_rope, positions)
    return k_nope, k_rope, v


def mla_attention(q_nope, q_rope, k_nope, k_rope, v):
    b_, s_ = q_nope.shape[:2]
    nb = s_ // Q_BLOCK
    qn = q_nope.reshape(b_, nb, Q_BLOCK, MLA_HEADS, MLA_NOPE_DIM).transpose(1, 0, 3, 2, 4)
    qr = q_rope.reshape(b_, nb, Q_BLOCK, MLA_HEADS, MLA_ROPE_DIM).transpose(1, 0, 3, 2, 4)
    kpos = jnp.arange(s_)

    def block(args):
        qn_b, qr_b, start = args
        s = jnp.einsum('bhqd,bkhd->bhqk', qn_b, k_nope) + jnp.einsum('bhqr,bkr->bhqk', qr_b, k_rope)
        s = s.astype(jnp.float32) * MLA_QK_DIM ** -0.5
        qpos = start + jnp.arange(Q_BLOCK)
        s = jnp.where(kpos[None, :] <= qpos[:, None], s, -jnp.inf)
        p = jax.nn.softmax(s, axis=-1).astype(v.dtype)
        return jnp.einsum('bhqk,bkhd->bqhd', p, v)

    starts = jnp.arange(nb, dtype=jnp.int32) * Q_BLOCK
    o = lax.map(block, (qn, qr, starts))
    return o.transpose(1, 0, 2, 3, 4).reshape(b_, s_, MLA_HEADS, MLA_V_DIM)


def layer_b(x, mem, positions, k_nope, k_rope, v, pre_g, w_in, q_g, w_uq, mem_g, w_mem_kv, w_out, post_g):
    b_, s_ = x.shape[:2]
    h = rmsnorm(x, pre_g)
    cq, z, mq, mz = _split(h @ w_in, B_SPLITS)
    q = (rmsnorm(cq, q_g) @ w_uq).reshape(b_, s_, MLA_HEADS, MLA_QK_DIM)
    q_nope, q_rope = q[..., :MLA_NOPE_DIM], rope(q[..., MLA_NOPE_DIM:], positions)
    o = mla_attention(q_nope, q_rope, k_nope, k_rope, v)
    mla_out = o.reshape(b_, s_, BRANCH_WIDTH) * jax.nn.silu(z)
    mem_out = mem_attend(mq, mem, mem_g, w_mem_kv) * jax.nn.silu(mz)
    y = jnp.concatenate([mla_out, mem_out], axis=-1) @ w_out
    return x + rmsnorm(y, post_g)


def setup_inputs(seed: int = 0) -> dict:
    key = jax.random.key(seed)
    ks = jax.random.split(key, 32)
    f32 = jnp.float32

    def w(k, shape, fan_in):
        return jax.random.normal(k, shape, f32) * fan_in ** -0.5

    def gain(k, shape):
        return 1.0 + 0.05 * jax.random.normal(k, shape, f32)

    na, nbl = N_A_LAYERS, N_B_LAYERS
    return {
        'x': jax.random.normal(ks[0], (BATCH, SEQ, D_MODEL), f32),
        'mem': jax.random.normal(ks[1], (BATCH, N_MEM, D_MODEL), f32),
        'positions': (jax.random.randint(ks[2], (BATCH, 1), 0, 512, dtype=jnp.int32)
                      + jnp.arange(SEQ, dtype=jnp.int32)[None, :]),
        'a_pre_norm': gain(ks[3], (na, D_MODEL)),
        'a_w_in': w(ks[4], (na, D_MODEL, A_IN_WIDTH), D_MODEL),
        'a_w_g2': w(ks[5], (na, GLA_GATE_RANK, GLA_HEADS * GLA_DK), GLA_GATE_RANK),
        'a_b_g': 0.5 + 0.1 * jax.random.normal(ks[6], (na, GLA_HEADS * GLA_DK), f32),
        'a_gla_norm': gain(ks[7], (na, GLA_DV)),
        'a_mem_norm': gain(ks[8], (na, D_MODEL)),
        'a_w_mem_kv': w(ks[9], (na, D_MODEL, 2 * MEM_WIDTH), D_MODEL),
        'a_w_out': w(ks[10], (na, D_MODEL, D_MODEL), D_MODEL),
        'a_post_norm': gain(ks[11], (na, D_MODEL)),
        'kv_in_norm': gain(ks[12], (D_MODEL,)),
        'w_dkv': w(ks[13], (D_MODEL, MLA_KV_RANK + MLA_ROPE_DIM), D_MODEL),
        'kv_norm': gain(ks[14], (MLA_KV_RANK,)),
        'w_uk': w(ks[15], (MLA_KV_RANK, MLA_HEADS * MLA_NOPE_DIM), MLA_KV_RANK),
        'w_uv': w(ks[16], (MLA_KV_RANK, MLA_HEADS * MLA_V_DIM), MLA_KV_RANK),
        'b_pre_norm': gain(ks[17], (nbl, D_MODEL)),
        'b_w_in': w(ks[18], (nbl, D_MODEL, B_IN_WIDTH), D_MODEL),
        'b_q_norm': gain(ks[19], (nbl, MLA_Q_RANK)),
        'b_w_uq': w(ks[20], (nbl, MLA_Q_RANK, MLA_HEADS * MLA_QK_DIM), MLA_Q_RANK),
        'b_mem_norm': gain(ks[21], (nbl, D_MODEL)),
        'b_w_mem_kv': w(ks[22], (nbl, D_MODEL, 2 * MEM_WIDTH), D_MODEL),
        'b_w_out': w(ks[23], (nbl, D_MODEL, D_MODEL), D_MODEL),
        'b_post_norm': gain(ks[24], (nbl, D_MODEL)),
    }


def reference(x, mem, positions, a_pre_norm, a_w_in, a_w_g2, a_b_g, a_gla_norm, a_mem_norm,
              a_w_mem_kv, a_w_out, a_post_norm, kv_in_norm, w_dkv, kv_norm, w_uk, w_uv,
              b_pre_norm, b_w_in, b_q_norm, b_w_uq, b_mem_norm, b_w_mem_kv, b_w_out, b_post_norm):
    k_nope = k_rope = v_sh = None
    for i in range(DEPTH):
        if i < N_A_LAYERS:
            x = layer_a(x, mem, a_pre_norm[i], a_w_in[i], a_w_g2[i], a_b_g[i], a_gla_norm[i],
                        a_mem_norm[i], a_w_mem_kv[i], a_w_out[i], a_post_norm[i])
        else:
            j = i - N_A_LAYERS
            if j == 0:
                k_nope, k_rope, v_sh = shared_mla_kv(x, positions, kv_in_norm, w_dkv, kv_norm, w_uk, w_uv)
            x = layer_b(x, mem, positions, k_nope, k_rope, v_sh, b_pre_norm[j], b_w_in[j], b_q_norm[j],
                        b_w_uq[j], b_mem_norm[j], b_w_mem_kv[j], b_w_out[j], b_post_norm[j])
    return x
```

```python
import functools

import numpy as np
import jax
import jax.numpy as jnp
from jax import lax
from jax.experimental import pallas as pl
from jax.experimental.pallas import tpu as pltpu

F32 = jnp.float32
BF16 = jnp.bfloat16

D_MODEL = 2048
N_MEM = 256
MEM_HEADS = 4
MEM_HEAD_DIM = 128
MEM_WIDTH = MEM_HEADS * MEM_HEAD_DIM
BRANCH_WIDTH = D_MODEL - MEM_WIDTH
GLA_HEADS = 4
GLA_DV = BRANCH_WIDTH // GLA_HEADS
GLA_DK = GLA_DV // 2
GLA_GATE_RANK = 16
GLA_GATE_NORM = 16.0
MLA_HEADS = 12
MLA_V_DIM = 128
MLA_NOPE_DIM = 128
MLA_ROPE_DIM = 64
MLA_QK_DIM = MLA_NOPE_DIM + MLA_ROPE_DIM
MLA_Q_RANK = 512
MLA_KV_RANK = 512
ROPE_THETA = 10000.0
EPS = 1e-6

LANES = 128
MXU_DIM = 256
GLA_DK_PAD = MXU_DIM
MLA_QK_PAD = MXU_DIM
VMEM_LIMIT = 56 * 1024 * 1024

PROJ_TM = 1024
PROJ_NORM_ROWS = 256
OUT_TM = 512
GLA_SPAN = 256
GLA_CHUNK = 64
GLA_SUB = 32
MEM_TM = 512
PREP_TM = 512
ATT_TQ = 512
ATT_TK = 512

A_V_OFF = 0
A_Z_OFF = BRANCH_WIDTH
A_Q_OFF = 2 * BRANCH_WIDTH
A_K_OFF = A_Q_OFF + GLA_HEADS * GLA_DK_PAD
A_MQ_OFF = A_K_OFF + GLA_HEADS * GLA_DK_PAD
A_MZ_OFF = A_MQ_OFF + MEM_WIDTH
A_WIDTH = A_MZ_OFF + MEM_WIDTH
A_TN = 1024

B_Z_OFF = 0
B_CQ_OFF = BRANCH_WIDTH
B_MQ_OFF = B_CQ_OFF + MLA_Q_RANK
B_MZ_OFF = B_MQ_OFF + MEM_WIDTH
B_C_OFF = B_MZ_OFF + MEM_WIDTH
B_KR_OFF = B_C_OFF + MLA_KV_RANK
B_KROT_OFF = B_KR_OFF + LANES
B_WIDTH = B_KROT_OFF + LANES
B_TN = 768
B_GROUP_SPLIT = B_C_OFF // B_TN


def _silu(z):
    return z * (1.0 / (1.0 + jnp.exp(-z)))


def _log_sigmoid(g):
    return jnp.minimum(g, 0.0) - jnp.log1p(jnp.exp(-jnp.abs(g)))


def _dot(a, b):
    return jnp.dot(a, b, preferred_element_type=F32)


def _dot_nt(a, b):
    return lax.dot_general(a, b, (((1,), (1,)), ((), ())), preferred_element_type=F32)


def _dot_tn(a, b):
    return lax.dot_general(a, b, (((0,), (0,)), ((), ())), preferred_element_type=F32)


def _norm_matmul_kernel(x_ref, g_ref, w_ref, o_ref, h_ref, *, n_groups, group_split, norm_rows):
    j = pl.program_id(1)

    @pl.when(j == 0)
    def _():
        tm = x_ref.shape[0]
        for r in range(tm // norm_rows):
            rows = pl.ds(r * norm_rows, norm_rows)
            x = x_ref[rows, :]
            xn = x * lax.rsqrt(jnp.mean(x * x, axis=-1, keepdims=True) + EPS)
            for g in range(n_groups):
                h_ref[g, rows, :] = (xn * g_ref[g:g + 1, :]).astype(BF16)

    if n_groups == 1:
        h = h_ref[0]
    else:
        h = h_ref[(j >= group_split).astype(jnp.int32)]
    o_ref[...] = _dot(h, w_ref[...]).astype(o_ref.dtype)


def _norm_matmul(x, gains, w, *, tm, tn, group_split=0):
    m, d = x.shape
    n = w.shape[1]
    n_groups = gains.shape[0]
    tm = min(tm, m)
    norm_rows = min(PROJ_NORM_ROWS, tm)
    kern = functools.partial(_norm_matmul_kernel, n_groups=n_groups, group_split=group_split,
                             norm_rows=norm_rows)
    return pl.pallas_call(
        kern,
        grid=(m // tm, n // tn),
        in_specs=[
            pl.BlockSpec((tm, d), lambda i, j: (i, 0)),
            pl.BlockSpec((n_groups, d), lambda i, j: (0, 0)),
            pl.BlockSpec((d, tn), lambda i, j: (0, j)),
        ],
        out_specs=pl.BlockSpec((tm, tn), lambda i, j: (i, j)),
        out_shape=jax.ShapeDtypeStruct((m, n), BF16),
        scratch_shapes=[pltpu.VMEM((n_groups, tm, d), BF16)],
        compiler_params=pltpu.CompilerParams(
            dimension_semantics=("parallel", "arbitrary"), vmem_limit_bytes=VMEM_LIMIT),
        name="norm_matmul",
    )(x, gains, w)


def _out_proj_kernel(a_ref, m_ref, wa_ref, wm_ref, x_ref, g_ref, o_ref):
    y = _dot(a_ref[...], wa_ref[...]) + _dot(m_ref[...], wm_ref[...])
    ms = jnp.mean(y * y, axis=-1, keepdims=True)
    o_ref[...] = x_ref[...] + y * lax.rsqrt(ms + EPS) * g_ref[...]


def _out_proj(a, mo, w_out, x, gain):
    t = x.shape[0]
    tm = min(OUT_TM, t)
    mem_blk = BRANCH_WIDTH // MEM_WIDTH
    return pl.pallas_call(
        _out_proj_kernel,
        grid=(t // tm,),
        in_specs=[
            pl.BlockSpec((tm, BRANCH_WIDTH), lambda i: (i, 0)),
            pl.BlockSpec((tm, MEM_WIDTH), lambda i: (i, 0)),
            pl.BlockSpec((BRANCH_WIDTH, D_MODEL), lambda i: (0, 0)),
            pl.BlockSpec((MEM_WIDTH, D_MODEL), lambda i: (mem_blk, 0)),
            pl.BlockSpec((tm, D_MODEL), lambda i: (i, 0)),
            pl.BlockSpec((1, D_MODEL), lambda i: (0, 0)),
        ],
        out_specs=pl.BlockSpec((tm, D_MODEL), lambda i: (i, 0)),
        out_shape=jax.ShapeDtypeStruct((t, D_MODEL), F32),
        compiler_params=pltpu.CompilerParams(
            dimension_semantics=("parallel",), vmem_limit_bytes=VMEM_LIMIT),
        name="out_proj",
    )(a, mo, w_out, w_out, x, gain)


def _gla_kernel(v_ref, z_ref, q_ref, k_ref, wg_ref, bg_ref, gn_ref, o_ref, s_ref, *, chunk, sub):
    @pl.when(pl.program_id(2) == 0)
    def _():
        s_ref[...] = jnp.zeros_like(s_ref)

    span = q_ref.shape[0]
    scale = GLA_DK ** -0.5
    row = lax.broadcasted_iota(jnp.int32, (chunk, chunk), 0)
    col = lax.broadcasted_iota(jnp.int32, (chunk, chunk), 1)
    tril = (row >= col).astype(F32)
    wg = wg_ref[...]
    bg = bg_ref[...]
    gn = gn_ref[...]

    for c in range(span // chunk):
        rows = pl.ds(c * chunk, chunk)
        qb = q_ref[rows, :]
        q = qb.astype(F32)
        k = k_ref[rows, :].astype(F32)
        v = v_ref[rows, :]
        log_a = _log_sigmoid(_dot(qb, wg) + bg) * (1.0 / GLA_GATE_NORM)
        cum = jnp.dot(tril, log_a, preferred_element_type=F32, precision=lax.Precision.HIGHEST)
        last = cum[chunk - 1:chunk, :]

        s_prev = s_ref[...]
        q_inter = (q * (jnp.exp(cum) * scale)).astype(BF16)
        o_inter = _dot(q_inter, s_prev.astype(BF16))

        o_parts = []
        for i in range(chunk // sub):
            lo, hi = i * sub, (i + 1) * sub
            ref_row = cum[lo:lo + 1, :]
            qi = (q[lo:hi] * (jnp.exp(cum[lo:hi] - ref_row) * scale)).astype(BF16)
            ki = (k[:hi] * jnp.exp(ref_row - cum[:hi])).astype(BF16)
            att = _dot_nt(qi, ki)
            t_idx = lax.broadcasted_iota(jnp.int32, (sub, hi), 0) + lo
            s_idx = lax.broadcasted_iota(jnp.int32, (sub, hi), 1)
            att = jnp.where(t_idx >= s_idx, att, 0.0)
            o_parts.append(_dot(att.astype(BF16), v[:hi]))
        o = o_inter + jnp.concatenate(o_parts, axis=0)

        k_upd = (k * jnp.exp(last - cum)).astype(BF16)
        decay_col = jnp.transpose(jnp.exp(cum[chunk - 8:chunk, :]))[:, 7:8]
        s_ref[...] = decay_col * s_prev + _dot_tn(k_upd, v)

        on = o * lax.rsqrt(jnp.mean(o * o, axis=-1, keepdims=True) + EPS) * gn
        o_ref[rows, :] = (on * _silu(z_ref[rows, :].astype(F32))).astype(o_ref.dtype)


def _gla(proj, wg, bg, gn, batch, seq):
    span = min(GLA_SPAN, seq)
    ns = seq // span
    kern = functools.partial(_gla_kernel, chunk=GLA_CHUNK, sub=GLA_SUB)
    v_blk, z_blk = A_V_OFF // GLA_DV, A_Z_OFF // GLA_DV
    q_blk, k_blk = A_Q_OFF // GLA_DK_PAD, A_K_OFF // GLA_DK_PAD
    return pl.pallas_call(
        kern,
        grid=(batch, GLA_HEADS, ns),
        in_specs=[
            pl.BlockSpec((span, GLA_DV), lambda b, h, s: (b * ns + s, v_blk + h)),
            pl.BlockSpec((span, GLA_DV), lambda b, h, s: (b * ns + s, z_blk + h)),
            pl.BlockSpec((span, GLA_DK_PAD), lambda b, h, s: (b * ns + s, q_blk + h)),
            pl.BlockSpec((span, GLA_DK_PAD), lambda b, h, s: (b * ns + s, k_blk + h)),
            pl.BlockSpec((None, GLA_DK_PAD, GLA_DK_PAD), lambda b, h, s: (h, 0, 0)),
            pl.BlockSpec((None, 1, GLA_DK_PAD), lambda b, h, s: (h, 0, 0)),
            pl.BlockSpec((1, GLA_DV), lambda b, h, s: (0, 0)),
        ],
        out_specs=pl.BlockSpec((span, GLA_DV), lambda b, h, s: (b * ns + s, h)),
        out_shape=jax.ShapeDtypeStruct((batch * seq, BRANCH_WIDTH), BF16),
        scratch_shapes=[pltpu.VMEM((GLA_DK_PAD, GLA_DV), F32)],
        compiler_params=pltpu.CompilerParams(
            dimension_semantics=("parallel", "parallel", "arbitrary"), vmem_limit_bytes=VMEM_LIMIT),
        name="gla",
    )(proj, proj, proj, proj, wg, bg, gn)


def _mem_attn_kernel(q_ref, z_ref, k_ref, v_ref, o_ref):
    scale = MEM_HEAD_DIM ** -0.5
    for h in range(MEM_HEADS):
        cols = slice(h * MEM_HEAD_DIM, (h + 1) * MEM_HEAD_DIM)
        s = _dot_nt(q_ref[:, cols], k_ref[:, cols]) * scale
        p = jnp.exp(s - jnp.max(s, axis=-1, keepdims=True))
        l = jnp.sum(p, axis=-1, keepdims=True)
        o = _dot(p.astype(BF16), v_ref[:, cols]) * (1.0 / l)
        o_ref[:, cols] = (o * _silu(z_ref[:, cols].astype(F32))).astype(o_ref.dtype)


def _mem_attn(proj, memkv, seq, q_off, z_off, kv_blk):
    t = proj.shape[0]
    tm = min(MEM_TM, seq)
    per_batch = seq // tm
    q_blk, z_blk = q_off // MEM_WIDTH, z_off // MEM_WIDTH
    return pl.pallas_call(
        _mem_attn_kernel,
        grid=(t // tm,),
        in_specs=[
            pl.BlockSpec((tm, MEM_WIDTH), lambda i: (i, q_blk)),
            pl.BlockSpec((tm, MEM_WIDTH), lambda i: (i, z_blk)),
            pl.BlockSpec((N_MEM, MEM_WIDTH), lambda i: (i // per_batch, kv_blk)),
            pl.BlockSpec((N_MEM, MEM_WIDTH), lambda i: (i // per_batch, kv_blk + 1)),
        ],
        out_specs=pl.BlockSpec((tm, MEM_WIDTH), lambda i: (i, 0)),
        out_shape=jax.ShapeDtypeStruct((t, MEM_WIDTH), BF16),
        compiler_params=pltpu.CompilerParams(
            dimension_semantics=("parallel",), vmem_limit_bytes=VMEM_LIMIT),
        name="mem_attn",
    )(proj, proj, memkv, memkv)


def _mla_prep_kernel(cq_ref, c_ref, kr_ref, krot_ref, pos_ref, freq_ref, sign_ref, qg_ref, kvg_ref,
                     wuq_ref, wuk_ref, wuvt_ref, q_ref, k_ref, vt_ref, *, tk):
    tm = cq_ref.shape[0]
    ang = pos_ref[...].astype(F32) * freq_ref[...]
    cos = jnp.cos(ang)
    sin = jnp.sin(ang) * sign_ref[...]
    lane = lax.broadcasted_iota(jnp.int32, (1, LANES), 1)
    slot_mask = [(lane < MLA_ROPE_DIM).astype(F32), (lane >= MLA_ROPE_DIM).astype(F32)]

    cq = cq_ref[...].astype(F32)
    cqn = cq * lax.rsqrt(jnp.mean(cq * cq, axis=-1, keepdims=True) + EPS)
    cqn = (cqn * (qg_ref[...] * MLA_QK_DIM ** -0.5)).astype(BF16)
    n_nope = MLA_HEADS * MLA_NOPE_DIM
    n_rope = MLA_HEADS * MLA_ROPE_DIM
    q_nope = _dot(cqn, wuq_ref[:, :n_nope])
    q_r = _dot(cqn, wuq_ref[:, n_nope:n_nope + n_rope])
    q_rot = _dot(cqn, wuq_ref[:, n_nope + n_rope:])
    for h in range(MLA_HEADS):
        q_ref[h, :, :MLA_NOPE_DIM] = q_nope[:, h * MLA_NOPE_DIM:(h + 1) * MLA_NOPE_DIM].astype(BF16)
    for u in range(MLA_HEADS // 2):
        cols = slice(u * LANES, (u + 1) * LANES)
        roped = q_r[:, cols] * cos + q_rot[:, cols] * sin
        for half in range(2):
            q_ref[2 * u + half, :, MLA_NOPE_DIM:] = (roped * slot_mask[half]).astype(BF16)

    c = c_ref[...].astype(F32)
    cn = c * lax.rsqrt(jnp.mean(c * c, axis=-1, keepdims=True) + EPS)
    cn = (cn * kvg_ref[...]).astype(BF16)
    k_nope = _dot(cn, wuk_ref[...])
    k_rope = (kr_ref[...].astype(F32) * cos + krot_ref[...].astype(F32) * sin).astype(BF16)
    for h in range(MLA_HEADS):
        k_ref[h, :, :MLA_NOPE_DIM] = k_nope[:, h * MLA_NOPE_DIM:(h + 1) * MLA_NOPE_DIM].astype(BF16)
        k_ref[h, :, MLA_NOPE_DIM:] = k_rope
    v_t = _dot_nt(wuvt_ref[...], cn).astype(BF16)
    for h in range(MLA_HEADS):
        for u in range(tm // tk):
            vt_ref[h, u] = v_t[h * MLA_V_DIM:(h + 1) * MLA_V_DIM, u * tk:(u + 1) * tk]


def _mla_prep(proj, pos_col, freq, sign, q_gain, kv_gain, w_uq, w_uk, w_uv_t, batch, seq, tk):
    tm = min(PREP_TM, seq)
    per_batch = seq // tm
    kern = functools.partial(_mla_prep_kernel, tk=tk)
    const = lambda i: (0, 0)
    head_map = lambda i: (i // per_batch, 0, i % per_batch, 0)
    return pl.pallas_call(
        kern,
        grid=(batch * per_batch,),
        in_specs=[
            pl.BlockSpec((tm, MLA_Q_RANK), lambda i: (i, B_CQ_OFF // MLA_Q_RANK)),
            pl.BlockSpec((tm, MLA_KV_RANK), lambda i: (i, B_C_OFF // MLA_KV_RANK)),
            pl.BlockSpec((tm, LANES), lambda i: (i, B_KR_OFF // LANES)),
            pl.BlockSpec((tm, LANES), lambda i: (i, B_KROT_OFF // LANES)),
            pl.BlockSpec((tm, 1), lambda i: (i, 0)),
            pl.BlockSpec((1, LANES), const),
            pl.BlockSpec((1, LANES), const),
            pl.BlockSpec((1, MLA_Q_RANK), const),
            pl.BlockSpec((1, MLA_KV_RANK), const),
            pl.BlockSpec(w_uq.shape, const),
            pl.BlockSpec(w_uk.shape, const),
            pl.BlockSpec(w_uv_t.shape, const),
        ],
        out_specs=[
            pl.BlockSpec((None, MLA_HEADS, tm, MLA_QK_PAD), head_map),
            pl.BlockSpec((None, MLA_HEADS, tm, MLA_QK_PAD), head_map),
            pl.BlockSpec((None, MLA_HEADS, tm // tk, MLA_V_DIM, tk),
                         lambda i: (i // per_batch, 0, i % per_batch, 0, 0)),
        ],
        out_shape=[
            jax.ShapeDtypeStruct((batch, MLA_HEADS, seq, MLA_QK_PAD), BF16),
            jax.ShapeDtypeStruct((batch, MLA_HEADS, seq, MLA_QK_PAD), BF16),
            jax.ShapeDtypeStruct((batch, MLA_HEADS, seq // tk, MLA_V_DIM, tk), BF16),
        ],
        compiler_params=pltpu.CompilerParams(
            dimension_semantics=("parallel",), vmem_limit_bytes=VMEM_LIMIT),
        name="mla_prep",
    )(proj, proj, proj, proj, pos_col, freq, sign, q_gain, kv_gain, w_uq, w_uk, w_uv_t)


def _mla_attn_kernel(q_ref, k_ref, vt_ref, z_ref, o_ref, acc_ref, m_ref, l_ref, *, tq, tk):
    seq = q_ref.shape[0]
    kpos = lax.broadcasted_iota(jnp.int32, (tk, tq), 0)
    qpos = lax.broadcasted_iota(jnp.int32, (tk, tq), 1)
    diag_mask = kpos <= qpos

    def kv_step(q, kj, masked):
        k = k_ref[pl.ds(pl.multiple_of(kj * tk, tk), tk), :]
        s_t = _dot_nt(k, q)
        if masked:
            s_t = jnp.where(diag_mask, s_t, -jnp.inf)
        m_prev = m_ref[...]
        m_new = jnp.maximum(m_prev, jnp.max(s_t, axis=0, keepdims=True))
        alpha = jnp.exp(m_prev - m_new)
        p = jnp.exp(s_t - m_new)
        l_ref[...] = alpha * l_ref[...] + jnp.sum(p, axis=0, keepdims=True)
        acc_ref[...] = alpha * acc_ref[...] + _dot(vt_ref[kj], p.astype(BF16))
        m_ref[...] = m_new

    def q_block(qi, carry):
        rows = pl.ds(pl.multiple_of(qi * tq, tq), tq)
        q = q_ref[rows, :]
        m_ref[...] = jnp.full_like(m_ref, -jnp.inf)
        l_ref[...] = jnp.zeros_like(l_ref)
        acc_ref[...] = jnp.zeros_like(acc_ref)

        def full_step(kj, c):
            kv_step(q, kj, False)
            return c

        lax.fori_loop(0, qi, full_step, 0)
        kv_step(q, qi, True)
        o = jnp.transpose(acc_ref[...] * (1.0 / l_ref[...]))
        o_ref[rows, :] = (o * _silu(z_ref[rows, :].astype(F32))).astype(o_ref.dtype)
        return carry

    lax.fori_loop(0, seq // tq, q_block, 0)


def _mla_attn(q_cat, k_cat, v_t, proj, batch, seq, tq, tk):
    kern = functools.partial(_mla_attn_kernel, tq=tq, tk=tk)
    nb = seq // tk
    z_blk = B_Z_OFF // MLA_V_DIM
    return pl.pallas_call(
        kern,
        grid=(batch, MLA_HEADS),
        in_specs=[
            pl.BlockSpec((None, None, seq, MLA_QK_PAD), lambda b, h: (b, h, 0, 0)),
            pl.BlockSpec((None, None, seq, MLA_QK_PAD), lambda b, h: (b, h, 0, 0)),
            pl.BlockSpec((None, None, nb, MLA_V_DIM, tk), lambda b, h: (b, h, 0, 0, 0)),
            pl.BlockSpec((seq, MLA_V_DIM), lambda b, h: (b, z_blk + h)),
        ],
        out_specs=pl.BlockSpec((seq, MLA_V_DIM), lambda b, h: (b, h)),
        out_shape=jax.ShapeDtypeStruct((batch * seq, BRANCH_WIDTH), BF16),
        scratch_shapes=[
            pltpu.VMEM((MLA_V_DIM, tq), F32),
            pltpu.VMEM((1, tq), F32),
            pltpu.VMEM((1, tq), F32),
        ],
        compiler_params=pltpu.CompilerParams(
            dimension_semantics=("parallel", "parallel"), vmem_limit_bytes=VMEM_LIMIT),
        name="mla_attn",
    )(q_cat, k_cat, v_t, proj)


def _pad_heads(w, heads, width, padded):
    lead = w.shape[:-1]
    w = w.reshape(lead + (heads, width))
    w = jnp.pad(w, [(0, 0)] * len(lead) + [(0, 0), (0, padded - width)])
    return w.reshape(lead + (heads * padded,))


def _rot_half(w, heads):
    lead = w.shape[:-1]
    half = MLA_ROPE_DIM // 2
    w = w.reshape(lead + (heads, 2, half))
    return w[..., ::-1, :].reshape(lead + (heads * MLA_ROPE_DIM,))


def _layer_a_weights(w_in, w_g2, b_g):
    hk = GLA_HEADS * GLA_DK
    o = np.cumsum([0, hk, hk, BRANCH_WIDTH, GLA_GATE_RANK, BRANCH_WIDTH, MEM_WIDTH, MEM_WIDTH])
    wq, wk, wv, wlr, wz, wmq, wmz = [w_in[:, o[i]:o[i + 1]] for i in range(7)]
    wq = wq.reshape(D_MODEL, GLA_HEADS, GLA_DK)
    wlr_rep = jnp.broadcast_to(wlr[:, None, :], (D_MODEL, GLA_HEADS, GLA_GATE_RANK))
    zeros = jnp.zeros((D_MODEL, GLA_HEADS, GLA_DK_PAD - GLA_DK - GLA_GATE_RANK), w_in.dtype)
    wq = jnp.concatenate([wq, wlr_rep, zeros], axis=-1).reshape(D_MODEL, GLA_HEADS * GLA_DK_PAD)
    wk = _pad_heads(wk, GLA_HEADS, GLA_DK, GLA_DK_PAD)
    w_all = jnp.concatenate([wv, wz, wq, wk, wmq, wmz], axis=1).astype(BF16)
    wg = _pad_heads(w_g2, GLA_HEADS, GLA_DK, GLA_DK_PAD)
    wg = wg.reshape(GLA_GATE_RANK, GLA_HEADS, GLA_DK_PAD).transpose(1, 0, 2)
    wg = jnp.pad(wg, [(0, 0), (GLA_DK, GLA_DK_PAD - GLA_DK - GLA_GATE_RANK), (0, 0)]).astype(BF16)
    bg = _pad_heads(b_g, GLA_HEADS, GLA_DK, GLA_DK_PAD).reshape(GLA_HEADS, 1, GLA_DK_PAD)
    return w_all, wg, bg


def _layer_b_weights(w_in, w_dkv):
    o = np.cumsum([0, MLA_Q_RANK, BRANCH_WIDTH, MEM_WIDTH, MEM_WIDTH])
    wcq, wz, wmq, wmz = [w_in[:, o[i]:o[i + 1]] for i in range(4)]
    wc, wkr = w_dkv[:, :MLA_KV_RANK], w_dkv[:, MLA_KV_RANK:]
    wkrot = _rot_half(wkr, 1)
    return jnp.concatenate([wz, wcq, wmq, wmz, wc, wkr, wkr, wkrot, wkrot], axis=1).astype(BF16)


def _uq_weights(w_uq):
    w = w_uq.reshape(MLA_Q_RANK, MLA_HEADS, MLA_QK_DIM)
    w_nope = w[:, :, :MLA_NOPE_DIM].reshape(MLA_Q_RANK, MLA_HEADS * MLA_NOPE_DIM)
    w_rope = w[:, :, MLA_NOPE_DIM:].reshape(MLA_Q_RANK, MLA_HEADS * MLA_ROPE_DIM)
    return jnp.concatenate([w_nope, w_rope, _rot_half(w_rope, MLA_HEADS)], axis=1).astype(BF16)


def _rope_tables():
    r = MLA_ROPE_DIM
    freqs = ROPE_THETA ** (-jnp.arange(0, r, 2, dtype=F32) / r)
    freq = jnp.tile(freqs, LANES // (r // 2)).reshape(1, LANES)
    sign = jnp.tile(jnp.concatenate([-jnp.ones(r // 2, F32), jnp.ones(r // 2, F32)]),
                    LANES // r).reshape(1, LANES)
    return freq, sign


def kernel(x, mem, positions, a_pre_norm, a_w_in, a_w_g2, a_b_g, a_gla_norm, a_mem_norm, a_w_mem_kv,
           a_w_out, a_post_norm, kv_in_norm, w_dkv, kv_norm, w_uk, w_uv, b_pre_norm, b_w_in, b_q_norm,
           b_w_uq, b_mem_norm, b_w_mem_kv, b_w_out, b_post_norm):
    assert a_w_in.shape[0] == 1 and b_w_in.shape[0] == 1, "one A layer followed by one B layer"
    batch, seq, _ = x.shape
    t = batch * seq
    x2 = x.reshape(t, D_MODEL)

    mem_gains = jnp.concatenate([a_mem_norm, b_mem_norm], axis=0)
    w_memkv = jnp.concatenate([a_w_mem_kv[0], b_w_mem_kv[0]], axis=1).astype(BF16)
    memkv = _norm_matmul(mem.reshape(batch * N_MEM, D_MODEL), mem_gains, w_memkv,
                         tm=batch * N_MEM, tn=2 * MEM_WIDTH, group_split=1)

    w_a, wg, bg = _layer_a_weights(a_w_in[0], a_w_g2[0], a_b_g[0])
    proj_a = _norm_matmul(x2, a_pre_norm, w_a, tm=PROJ_TM, tn=A_TN)
    gla_out = _gla(proj_a, wg, bg, a_gla_norm, batch, seq)
    mem_out = _mem_attn(proj_a, memkv, seq, A_MQ_OFF, A_MZ_OFF, kv_blk=0)
    x1 = _out_proj(gla_out, mem_out, a_w_out[0].astype(BF16), x2, a_post_norm)

    w_b = _layer_b_weights(b_w_in[0], w_dkv)
    gains_b = jnp.concatenate([b_pre_norm, kv_in_norm[None, :]], axis=0)
    proj_b = _norm_matmul(x1, gains_b, w_b, tm=PROJ_TM, tn=B_TN, group_split=B_GROUP_SPLIT)
    freq, sign = _rope_tables()
    tk = min(ATT_TK, seq)
    tq = min(ATT_TQ, seq)
    q_cat, k_cat, v_t = _mla_prep(
        proj_b, positions.reshape(t, 1), freq, sign, b_q_norm, kv_norm[None, :],
        _uq_weights(b_w_uq[0]), w_uk.astype(BF16), w_uv.T.astype(BF16), batch, seq, tk)
    mla_out = _mla_attn(q_cat, k_cat, v_t, proj_b, batch, seq, tq, tk)
    mem_out_b = _mem_attn(proj_b, memkv, seq, B_MQ_OFF, B_MZ_OFF, kv_blk=2)
    out = _out_proj(mla_out, mem_out_b, b_w_out[0].astype(BF16), x1, b_post_norm)
    return out.reshape(batch, seq, D_MODEL)
```

```python
import functools

import numpy as np
import jax
import jax.numpy as jnp
from jax import lax
from jax.experimental import pallas as pl
from jax.experimental.pallas import tpu as pltpu

F32 = jnp.float32
BF16 = jnp.bfloat16

D_MODEL = 2048
N_MEM = 256
MEM_HEADS = 4
MEM_HEAD_DIM = 128
MEM_WIDTH = MEM_HEADS * MEM_HEAD_DIM
BRANCH_WIDTH = D_MODEL - MEM_WIDTH
GLA_HEADS = 4
GLA_DV = BRANCH_WIDTH // GLA_HEADS
GLA_DK = GLA_DV // 2
GLA_GATE_RANK = 16
GLA_GATE_NORM = 16.0
MLA_HEADS = 12
MLA_V_DIM = 128
MLA_NOPE_DIM = 128
MLA_ROPE_DIM = 64
MLA_QK_DIM = MLA_NOPE_DIM + MLA_ROPE_DIM
MLA_Q_RANK = 512
MLA_KV_RANK = 512
ROPE_THETA = 10000.0
EPS = 1e-6

LANES = 128
MXU_DIM = 256
GLA_DK_PAD = MXU_DIM
MLA_QK_PAD = MXU_DIM
BF16_SUBLANES = 16
MLA_VT_ROWS = MLA_V_DIM + BF16_SUBLANES
ATT_HEADS_PER_STEP = 2
LOG2_E = 1.4426950408889634
VMEM_LIMIT = 56 * 1024 * 1024

PROJ_TM = 1024
PROJ_NORM_ROWS = 256
OUT_TM = 512
GLA_SPAN = 256
GLA_CHUNK = 64
GLA_SUB = 32
MEM_TM = 512
PREP_TM = 512
ATT_TQ = 512
ATT_TK = 512

A_V_OFF = 0
A_Z_OFF = BRANCH_WIDTH
A_Q_OFF = 2 * BRANCH_WIDTH
A_K_OFF = A_Q_OFF + GLA_HEADS * GLA_DK_PAD
A_MQ_OFF = A_K_OFF + GLA_HEADS * GLA_DK_PAD
A_MZ_OFF = A_MQ_OFF + MEM_WIDTH
A_WIDTH = A_MZ_OFF + MEM_WIDTH
A_TN = 1024

B_Z_OFF = 0
B_CQ_OFF = BRANCH_WIDTH
B_MQ_OFF = B_CQ_OFF + MLA_Q_RANK
B_MZ_OFF = B_MQ_OFF + MEM_WIDTH
B_C_OFF = B_MZ_OFF + MEM_WIDTH
B_KR_OFF = B_C_OFF + MLA_KV_RANK
B_KROT_OFF = B_KR_OFF + LANES
B_WIDTH = B_KROT_OFF + LANES
B_TN = 768
B_GROUP_SPLIT = B_C_OFF // B_TN


def _silu(z):
    return z * (1.0 / (1.0 + jnp.exp(-z)))


def _log_sigmoid(g):
    return jnp.minimum(g, 0.0) - jnp.log1p(jnp.exp(-jnp.abs(g)))


def _dot(a, b):
    return jnp.dot(a, b, preferred_element_type=F32)


def _dot_nt(a, b):
    return lax.dot_general(a, b, (((1,), (1,)), ((), ())), preferred_element_type=F32)


def _dot_tn(a, b):
    return lax.dot_general(a, b, (((0,), (0,)), ((), ())), preferred_element_type=F32)


def _norm_matmul_kernel(x_ref, g_ref, w_ref, o_ref, h_ref, *, n_groups, group_split, norm_rows):
    j = pl.program_id(1)

    @pl.when(j == 0)
    def _():
        tm = x_ref.shape[0]
        for r in range(tm // norm_rows):
            rows = pl.ds(r * norm_rows, norm_rows)
            x = x_ref[rows, :]
            xn = x * lax.rsqrt(jnp.mean(x * x, axis=-1, keepdims=True) + EPS)
            for g in range(n_groups):
                h_ref[g, rows, :] = (xn * g_ref[g:g + 1, :]).astype(BF16)

    if n_groups == 1:
        h = h_ref[0]
    else:
        h = h_ref[(j >= group_split).astype(jnp.int32)]
    o_ref[...] = _dot(h, w_ref[...]).astype(o_ref.dtype)


def _norm_matmul(x, gains, w, *, tm, tn, group_split=0):
    m, d = x.shape
    n = w.shape[1]
    n_groups = gains.shape[0]
    tm = min(tm, m)
    norm_rows = min(PROJ_NORM_ROWS, tm)
    kern = functools.partial(_norm_matmul_kernel, n_groups=n_groups, group_split=group_split,
                             norm_rows=norm_rows)
    return pl.pallas_call(
        kern,
        grid=(m // tm, n // tn),
        in_specs=[
            pl.BlockSpec((tm, d), lambda i, j: (i, 0)),
            pl.BlockSpec((n_groups, d), lambda i, j: (0, 0)),
            pl.BlockSpec((d, tn), lambda i, j: (0, j)),
        ],
        out_specs=pl.BlockSpec((tm, tn), lambda i, j: (i, j)),
        out_shape=jax.ShapeDtypeStruct((m, n), BF16),
        scratch_shapes=[pltpu.VMEM((n_groups, tm, d), BF16)],
        compiler_params=pltpu.CompilerParams(
            dimension_semantics=("parallel", "arbitrary"), vmem_limit_bytes=VMEM_LIMIT),
        name="norm_matmul",
    )(x, gains, w)


def _out_proj_kernel(a_ref, m_ref, wa_ref, wm_ref, x_ref, g_ref, o_ref):
    y = _dot(a_ref[...], wa_ref[...]) + _dot(m_ref[...], wm_ref[...])
    ms = jnp.mean(y * y, axis=-1, keepdims=True)
    o_ref[...] = x_ref[...] + y * lax.rsqrt(ms + EPS) * g_ref[...]


def _out_proj(a, mo, w_out, x, gain):
    t = x.shape[0]
    tm = min(OUT_TM, t)
    mem_blk = BRANCH_WIDTH // MEM_WIDTH
    return pl.pallas_call(
        _out_proj_kernel,
        grid=(t // tm,),
        in_specs=[
            pl.BlockSpec((tm, BRANCH_WIDTH), lambda i: (i, 0)),
            pl.BlockSpec((tm, MEM_WIDTH), lambda i: (i, 0)),
            pl.BlockSpec((BRANCH_WIDTH, D_MODEL), lambda i: (0, 0)),
            pl.BlockSpec((MEM_WIDTH, D_MODEL), lambda i: (mem_blk, 0)),
            pl.BlockSpec((tm, D_MODEL), lambda i: (i, 0)),
            pl.BlockSpec((1, D_MODEL), lambda i: (0, 0)),
        ],
        out_specs=pl.BlockSpec((tm, D_MODEL), lambda i: (i, 0)),
        out_shape=jax.ShapeDtypeStruct((t, D_MODEL), F32),
        compiler_params=pltpu.CompilerParams(
            dimension_semantics=("parallel",), vmem_limit_bytes=VMEM_LIMIT),
        name="out_proj",
    )(a, mo, w_out, w_out, x, gain)


def _gla_kernel(v_ref, z_ref, q_ref, k_ref, wg_ref, bg_ref, gn_ref, o_ref, s_ref, *, chunk, sub):
    @pl.when(pl.program_id(2) == 0)
    def _():
        s_ref[...] = jnp.zeros_like(s_ref)

    span = q_ref.shape[0]
    scale = GLA_DK ** -0.5
    row = lax.broadcasted_iota(jnp.int32, (chunk, chunk), 0)
    col = lax.broadcasted_iota(jnp.int32, (chunk, chunk), 1)
    tril = (row >= col).astype(F32)
    wg = wg_ref[...]
    bg = bg_ref[...]
    gn = gn_ref[...]

    for c in range(span // chunk):
        rows = pl.ds(c * chunk, chunk)
        qb = q_ref[rows, :]
        q = qb.astype(F32)
        k = k_ref[rows, :].astype(F32)
        v = v_ref[rows, :]
        log_a = _log_sigmoid(_dot(qb, wg) + bg) * (1.0 / GLA_GATE_NORM)
        cum = jnp.dot(tril, log_a, preferred_element_type=F32, precision=lax.Precision.HIGHEST)
        last = cum[chunk - 1:chunk, :]

        s_prev = s_ref[...]
        q_inter = (q * (jnp.exp(cum) * scale)).astype(BF16)
        o_inter = _dot(q_inter, s_prev.astype(BF16))

        o_parts = []
        for i in range(chunk // sub):
            lo, hi = i * sub, (i + 1) * sub
            ref_row = cum[lo:lo + 1, :]
            qi = (q[lo:hi] * (jnp.exp(cum[lo:hi] - ref_row) * scale)).astype(BF16)
            ki = (k[:hi] * jnp.exp(ref_row - cum[:hi])).astype(BF16)
            att = _dot_nt(qi, ki)
            t_idx = lax.broadcasted_iota(jnp.int32, (sub, hi), 0) + lo
            s_idx = lax.broadcasted_iota(jnp.int32, (sub, hi), 1)
            att = jnp.where(t_idx >= s_idx, att, 0.0)
            o_parts.append(_dot(att.astype(BF16), v[:hi]))
        o = o_inter + jnp.concatenate(o_parts, axis=0)

        k_upd = (k * jnp.exp(last - cum)).astype(BF16)
        decay_col = jnp.transpose(jnp.exp(cum[chunk - 8:chunk, :]))[:, 7:8]
        s_ref[...] = decay_col * s_prev + _dot_tn(k_upd, v)

        on = o * lax.rsqrt(jnp.mean(o * o, axis=-1, keepdims=True) + EPS) * gn
        o_ref[rows, :] = (on * _silu(z_ref[rows, :].astype(F32))).astype(o_ref.dtype)


def _gla(proj, wg, bg, gn, batch, seq):
    span = min(GLA_SPAN, seq)
    ns = seq // span
    kern = functools.partial(_gla_kernel, chunk=GLA_CHUNK, sub=GLA_SUB)
    v_blk, z_blk = A_V_OFF // GLA_DV, A_Z_OFF // GLA_DV
    q_blk, k_blk = A_Q_OFF // GLA_DK_PAD, A_K_OFF // GLA_DK_PAD
    return pl.pallas_call(
        kern,
        grid=(batch, GLA_HEADS, ns),
        in_specs=[
            pl.BlockSpec((span, GLA_DV), lambda b, h, s: (b * ns + s, v_blk + h)),
            pl.BlockSpec((span, GLA_DV), lambda b, h, s: (b * ns + s, z_blk + h)),
            pl.BlockSpec((span, GLA_DK_PAD), lambda b, h, s: (b * ns + s, q_blk + h)),
            pl.BlockSpec((span, GLA_DK_PAD), lambda b, h, s: (b * ns + s, k_blk + h)),
            pl.BlockSpec((None, GLA_DK_PAD, GLA_DK_PAD), lambda b, h, s: (h, 0, 0)),
            pl.BlockSpec((None, 1, GLA_DK_PAD), lambda b, h, s: (h, 0, 0)),
            pl.BlockSpec((1, GLA_DV), lambda b, h, s: (0, 0)),
        ],
        out_specs=pl.BlockSpec((span, GLA_DV), lambda b, h, s: (b * ns + s, h)),
        out_shape=jax.ShapeDtypeStruct((batch * seq, BRANCH_WIDTH), BF16),
        scratch_shapes=[pltpu.VMEM((GLA_DK_PAD, GLA_DV), F32)],
        compiler_params=pltpu.CompilerParams(
            dimension_semantics=("parallel", "parallel", "arbitrary"), vmem_limit_bytes=VMEM_LIMIT),
        name="gla",
    )(proj, proj, proj, proj, wg, bg, gn)


def _mem_attn_kernel(q_ref, z_ref, k_ref, v_ref, o_ref):
    scale = MEM_HEAD_DIM ** -0.5
    for h in range(MEM_HEADS):
        cols = slice(h * MEM_HEAD_DIM, (h + 1) * MEM_HEAD_DIM)
        s = _dot_nt(q_ref[:, cols], k_ref[:, cols]) * scale
        p = jnp.exp(s - jnp.max(s, axis=-1, keepdims=True))
        l = jnp.sum(p, axis=-1, keepdims=True)
        o = _dot(p.astype(BF16), v_ref[:, cols]) * (1.0 / l)
        o_ref[:, cols] = (o * _silu(z_ref[:, cols].astype(F32))).astype(o_ref.dtype)


def _mem_attn(proj, memkv, seq, q_off, z_off, kv_blk):
    t = proj.shape[0]
    tm = min(MEM_TM, seq)
    per_batch = seq // tm
    q_blk, z_blk = q_off // MEM_WIDTH, z_off // MEM_WIDTH
    return pl.pallas_call(
        _mem_attn_kernel,
        grid=(t // tm,),
        in_specs=[
            pl.BlockSpec((tm, MEM_WIDTH), lambda i: (i, q_blk)),
            pl.BlockSpec((tm, MEM_WIDTH), lambda i: (i, z_blk)),
            pl.BlockSpec((N_MEM, MEM_WIDTH), lambda i: (i // per_batch, kv_blk)),
            pl.BlockSpec((N_MEM, MEM_WIDTH), lambda i: (i // per_batch, kv_blk + 1)),
        ],
        out_specs=pl.BlockSpec((tm, MEM_WIDTH), lambda i: (i, 0)),
        out_shape=jax.ShapeDtypeStruct((t, MEM_WIDTH), BF16),
        compiler_params=pltpu.CompilerParams(
            dimension_semantics=("parallel",), vmem_limit_bytes=VMEM_LIMIT),
        name="mem_attn",
    )(proj, proj, memkv, memkv)


def _mla_prep_kernel(cq_ref, c_ref, kr_ref, krot_ref, pos_ref, freq_ref, sign_ref, qg_ref, kvg_ref,
                     wuq_ref, wuk_ref, wuvt_ref, q_ref, k_ref, vt_ref, *, tk):
    tm = cq_ref.shape[0]
    ang = pos_ref[...].astype(F32) * freq_ref[...]
    cos = jnp.cos(ang)
    sin = jnp.sin(ang) * sign_ref[...]
    lane = lax.broadcasted_iota(jnp.int32, (1, LANES), 1)
    slot_mask = [(lane < MLA_ROPE_DIM).astype(F32), (lane >= MLA_ROPE_DIM).astype(F32)]

    cq = cq_ref[...].astype(F32)
    cqn = cq * lax.rsqrt(jnp.mean(cq * cq, axis=-1, keepdims=True) + EPS)
    cqn = (cqn * (qg_ref[...] * (MLA_QK_DIM ** -0.5 * LOG2_E))).astype(BF16)
    n_nope = MLA_HEADS * MLA_NOPE_DIM
    n_rope = MLA_HEADS * MLA_ROPE_DIM
    q_nope = _dot(cqn, wuq_ref[:, :n_nope])
    q_r = _dot(cqn, wuq_ref[:, n_nope:n_nope + n_rope])
    q_rot = _dot(cqn, wuq_ref[:, n_nope + n_rope:])
    for h in range(MLA_HEADS):
        q_ref[h, :, :MLA_NOPE_DIM] = q_nope[:, h * MLA_NOPE_DIM:(h + 1) * MLA_NOPE_DIM].astype(BF16)
    for u in range(MLA_HEADS // 2):
        cols = slice(u * LANES, (u + 1) * LANES)
        roped = q_r[:, cols] * cos + q_rot[:, cols] * sin
        for half in range(2):
            q_ref[2 * u + half, :, MLA_NOPE_DIM:] = (roped * slot_mask[half]).astype(BF16)

    c = c_ref[...].astype(F32)
    cn = c * lax.rsqrt(jnp.mean(c * c, axis=-1, keepdims=True) + EPS)
    cn = (cn * kvg_ref[...]).astype(BF16)
    k_nope = _dot(cn, wuk_ref[...])
    k_rope = (kr_ref[...].astype(F32) * cos + krot_ref[...].astype(F32) * sin).astype(BF16)
    for h in range(MLA_HEADS):
        k_ref[h, :, :MLA_NOPE_DIM] = k_nope[:, h * MLA_NOPE_DIM:(h + 1) * MLA_NOPE_DIM].astype(BF16)
        k_ref[h, :, MLA_NOPE_DIM:] = k_rope
    v_t = _dot_nt(wuvt_ref[...], cn).astype(BF16)
    ones_rows = (lax.broadcasted_iota(jnp.int32, (MLA_VT_ROWS - MLA_V_DIM, tk), 0) == 0).astype(BF16)
    for h in range(MLA_HEADS):
        for u in range(tm // tk):
            vt_ref[h, u, :MLA_V_DIM, :] = v_t[h * MLA_V_DIM:(h + 1) * MLA_V_DIM, u * tk:(u + 1) * tk]
            vt_ref[h, u, MLA_V_DIM:, :] = ones_rows


def _mla_prep(proj, pos_col, freq, sign, q_gain, kv_gain, w_uq, w_uk, w_uv_t, batch, seq, tk):
    tm = min(PREP_TM, seq)
    per_batch = seq // tm
    kern = functools.partial(_mla_prep_kernel, tk=tk)
    const = lambda i: (0, 0)
    head_map = lambda i: (i // per_batch, 0, i % per_batch, 0)
    return pl.pallas_call(
        kern,
        grid=(batch * per_batch,),
        in_specs=[
            pl.BlockSpec((tm, MLA_Q_RANK), lambda i: (i, B_CQ_OFF // MLA_Q_RANK)),
            pl.BlockSpec((tm, MLA_KV_RANK), lambda i: (i, B_C_OFF // MLA_KV_RANK)),
            pl.BlockSpec((tm, LANES), lambda i: (i, B_KR_OFF // LANES)),
            pl.BlockSpec((tm, LANES), lambda i: (i, B_KROT_OFF // LANES)),
            pl.BlockSpec((tm, 1), lambda i: (i, 0)),
            pl.BlockSpec((1, LANES), const),
            pl.BlockSpec((1, LANES), const),
            pl.BlockSpec((1, MLA_Q_RANK), const),
            pl.BlockSpec((1, MLA_KV_RANK), const),
            pl.BlockSpec(w_uq.shape, const),
            pl.BlockSpec(w_uk.shape, const),
            pl.BlockSpec(w_uv_t.shape, const),
        ],
        out_specs=[
            pl.BlockSpec((None, MLA_HEADS, tm, MLA_QK_PAD), head_map),
            pl.BlockSpec((None, MLA_HEADS, tm, MLA_QK_PAD), head_map),
            pl.BlockSpec((None, MLA_HEADS, tm // tk, MLA_VT_ROWS, tk),
                         lambda i: (i // per_batch, 0, i % per_batch, 0, 0)),
        ],
        out_shape=[
            jax.ShapeDtypeStruct((batch, MLA_HEADS, seq, MLA_QK_PAD), BF16),
            jax.ShapeDtypeStruct((batch, MLA_HEADS, seq, MLA_QK_PAD), BF16),
            jax.ShapeDtypeStruct((batch, MLA_HEADS, seq // tk, MLA_VT_ROWS, tk), BF16),
        ],
        compiler_params=pltpu.CompilerParams(
            dimension_semantics=("parallel",), vmem_limit_bytes=VMEM_LIMIT),
        name="mla_prep",
    )(proj, proj, proj, proj, pos_col, freq, sign, q_gain, kv_gain, w_uq, w_uk, w_uv_t)


def _max_over_rows(x, ways=8):
    rows = x.shape[0]
    slab = rows // ways
    parts = [x[i * slab:(i + 1) * slab] for i in range(ways)] if slab >= 8 and rows % ways == 0 else [x]
    while len(parts) > 1:
        parts = [jnp.maximum(parts[i], parts[i + 1]) for i in range(0, len(parts), 2)]
    return jnp.max(parts[0], axis=0, keepdims=True)


def _mla_attn_kernel(q_ref, k_ref, vt_ref, z_ref, o_ref, acc_ref, m_ref, *, tq, tk):
    n_heads, seq = q_ref.shape[0], q_ref.shape[1]
    tn = MXU_DIM
    units = [(h, n) for h in range(n_heads) for n in range(tq // tn)]

    def kv_step(qi, kj, diagonal):
        s, p, alpha = {}, {}, {}

        def n_keys(n):
            return min(tk, (n + 1) * tn) if diagonal else tk

        def scores(u):
            h, n = units[u]
            nk = n_keys(n)
            krows = pl.ds(pl.multiple_of(kj * tk, tk), nk)
            qrows = pl.ds(pl.multiple_of(qi * tq + n * tn, tn), tn)
            s_t = _dot_nt(k_ref[h, krows, :], q_ref[h, qrows, :])
            if diagonal:
                kpos = lax.broadcasted_iota(jnp.int32, (nk, tn), 0)
                qpos = lax.broadcasted_iota(jnp.int32, (nk, tn), 1) + n * tn
                s_t = jnp.where(kpos <= qpos, s_t, -jnp.inf)
            s[u] = s_t

        def softmax(u):
            h, n = units[u]
            cols = slice(n * tn, (n + 1) * tn)
            m_prev = m_ref[h, :, cols]
            m_new = jnp.maximum(m_prev, _max_over_rows(s[u]))
            alpha[u] = jnp.exp2(m_prev - m_new)
            p[u] = jnp.exp2(s.pop(u) - m_new).astype(BF16)
            m_ref[h, :, cols] = m_new

        def weighted_sum(u):
            h, n = units[u]
            cols = slice(n * tn, (n + 1) * tn)
            pv = _dot(vt_ref[h, kj, :, :n_keys(n)], p.pop(u))
            acc_ref[h, :, cols] = alpha.pop(u) * acc_ref[h, :, cols] + pv

        for step in range(len(units) + 2):
            if step < len(units):
                scores(step)
            if 1 <= step <= len(units):
                softmax(step - 1)
            if step >= 2:
                weighted_sum(step - 2)

    def q_block(qi, carry):
        rows = pl.ds(pl.multiple_of(qi * tq, tq), tq)
        m_ref[...] = jnp.full_like(m_ref, -jnp.inf)
        acc_ref[...] = jnp.zeros_like(acc_ref)

        def full_step(kj, c):
            kv_step(qi, kj, False)
            return c

        lax.fori_loop(0, qi, full_step, 0)
        kv_step(qi, qi, True)
        for h in range(n_heads):
            acc = acc_ref[h]
            o = jnp.transpose(acc[:MLA_V_DIM] * (1.0 / acc[MLA_V_DIM:MLA_V_DIM + 1]))
            cols = slice(h * MLA_V_DIM, (h + 1) * MLA_V_DIM)
            o_ref[rows, cols] = (o * _silu(z_ref[rows, cols].astype(F32))).astype(o_ref.dtype)
        return carry

    lax.fori_loop(0, seq // tq, q_block, 0)


def _mla_attn(q_cat, k_cat, v_t, proj, batch, seq, tq, tk):
    kern = functools.partial(_mla_attn_kernel, tq=tq, tk=tk)
    nb = seq // tk
    nh = ATT_HEADS_PER_STEP
    width = nh * MLA_V_DIM
    z_blk = B_Z_OFF // width
    return pl.pallas_call(
        kern,
        grid=(batch, MLA_HEADS // nh),
        in_specs=[
            pl.BlockSpec((None, nh, seq, MLA_QK_PAD), lambda b, g: (b, g, 0, 0)),
            pl.BlockSpec((None, nh, seq, MLA_QK_PAD), lambda b, g: (b, g, 0, 0)),
            pl.BlockSpec((None, nh, nb, MLA_VT_ROWS, tk), lambda b, g: (b, g, 0, 0, 0)),
            pl.BlockSpec((seq, width), lambda b, g: (b, z_blk + g)),
        ],
        out_specs=pl.BlockSpec((seq, width), lambda b, g: (b, g)),
        out_shape=jax.ShapeDtypeStruct((batch * seq, BRANCH_WIDTH), BF16),
        scratch_shapes=[
            pltpu.VMEM((nh, MLA_VT_ROWS, tq), F32),
            pltpu.VMEM((nh, 1, tq), F32),
        ],
        compiler_params=pltpu.CompilerParams(
            dimension_semantics=("parallel", "parallel"), vmem_limit_bytes=VMEM_LIMIT),
        name="mla_attn",
    )(q_cat, k_cat, v_t, proj)


def _pad_heads(w, heads, width, padded):
    lead = w.shape[:-1]
    w = w.reshape(lead + (heads, width))
    w = jnp.pad(w, [(0, 0)] * len(lead) + [(0, 0), (0, padded - width)])
    return w.reshape(lead + (heads * padded,))


def _rot_half(w, heads):
    lead = w.shape[:-1]
    half = MLA_ROPE_DIM // 2
    w = w.reshape(lead + (heads, 2, half))
    return w[..., ::-1, :].reshape(lead + (heads * MLA_ROPE_DIM,))


def _layer_a_weights(w_in, w_g2, b_g):
    hk = GLA_HEADS * GLA_DK
    o = np.cumsum([0, hk, hk, BRANCH_WIDTH, GLA_GATE_RANK, BRANCH_WIDTH, MEM_WIDTH, MEM_WIDTH])
    wq, wk, wv, wlr, wz, wmq, wmz = [w_in[:, o[i]:o[i + 1]] for i in range(7)]
    wq = wq.reshape(D_MODEL, GLA_HEADS, GLA_DK)
    wlr_rep = jnp.broadcast_to(wlr[:, None, :], (D_MODEL, GLA_HEADS, GLA_GATE_RANK))
    zeros = jnp.zeros((D_MODEL, GLA_HEADS, GLA_DK_PAD - GLA_DK - GLA_GATE_RANK), w_in.dtype)
    wq = jnp.concatenate([wq, wlr_rep, zeros], axis=-1).reshape(D_MODEL, GLA_HEADS * GLA_DK_PAD)
    wk = _pad_heads(wk, GLA_HEADS, GLA_DK, GLA_DK_PAD)
    w_all = jnp.concatenate([wv, wz, wq, wk, wmq, wmz], axis=1).astype(BF16)
    wg = _pad_heads(w_g2, GLA_HEADS, GLA_DK, GLA_DK_PAD)
    wg = wg.reshape(GLA_GATE_RANK, GLA_HEADS, GLA_DK_PAD).transpose(1, 0, 2)
    wg = jnp.pad(wg, [(0, 0), (GLA_DK, GLA_DK_PAD - GLA_DK - GLA_GATE_RANK), (0, 0)]).astype(BF16)
    bg = _pad_heads(b_g, GLA_HEADS, GLA_DK, GLA_DK_PAD).reshape(GLA_HEADS, 1, GLA_DK_PAD)
    return w_all, wg, bg


def _layer_b_weights(w_in, w_dkv):
    o = np.cumsum([0, MLA_Q_RANK, BRANCH_WIDTH, MEM_WIDTH, MEM_WIDTH])
    wcq, wz, wmq, wmz = [w_in[:, o[i]:o[i + 1]] for i in range(4)]
    wc, wkr = w_dkv[:, :MLA_KV_RANK], w_dkv[:, MLA_KV_RANK:]
    wkrot = _rot_half(wkr, 1)
    return jnp.concatenate([wz, wcq, wmq, wmz, wc, wkr, wkr, wkrot, wkrot], axis=1).astype(BF16)


def _uq_weights(w_uq):
    w = w_uq.reshape(MLA_Q_RANK, MLA_HEADS, MLA_QK_DIM)
    w_nope = w[:, :, :MLA_NOPE_DIM].reshape(MLA_Q_RANK, MLA_HEADS * MLA_NOPE_DIM)
    w_rope = w[:, :, MLA_NOPE_DIM:].reshape(MLA_Q_RANK, MLA_HEADS * MLA_ROPE_DIM)
    return jnp.concatenate([w_nope, w_rope, _rot_half(w_rope, MLA_HEADS)], axis=1).astype(BF16)


def _rope_tables():
    r = MLA_ROPE_DIM
    freqs = ROPE_THETA ** (-jnp.arange(0, r, 2, dtype=F32) / r)
    freq = jnp.tile(freqs, LANES // (r // 2)).reshape(1, LANES)
    sign = jnp.tile(jnp.concatenate([-jnp.ones(r // 2, F32), jnp.ones(r // 2, F32)]),
                    LANES // r).reshape(1, LANES)
    return freq, sign


def kernel(x, mem, positions, a_pre_norm, a_w_in, a_w_g2, a_b_g, a_gla_norm, a_mem_norm, a_w_mem_kv,
           a_w_out, a_post_norm, kv_in_norm, w_dkv, kv_norm, w_uk, w_uv, b_pre_norm, b_w_in, b_q_norm,
           b_w_uq, b_mem_norm, b_w_mem_kv, b_w_out, b_post_norm):
    assert a_w_in.shape[0] == 1 and b_w_in.shape[0] == 1, "one A layer followed by one B layer"
    batch, seq, _ = x.shape
    t = batch * seq
    x2 = x.reshape(t, D_MODEL)

    mem_gains = jnp.concatenate([a_mem_norm, b_mem_norm], axis=0)
    w_memkv = jnp.concatenate([a_w_mem_kv[0], b_w_mem_kv[0]], axis=1).astype(BF16)
    memkv = _norm_matmul(mem.reshape(batch * N_MEM, D_MODEL), mem_gains, w_memkv,
                         tm=batch * N_MEM, tn=2 * MEM_WIDTH, group_split=1)

    w_a, wg, bg = _layer_a_weights(a_w_in[0], a_w_g2[0], a_b_g[0])
    proj_a = _norm_matmul(x2, a_pre_norm, w_a, tm=PROJ_TM, tn=A_TN)
    gla_out = _gla(proj_a, wg, bg, a_gla_norm, batch, seq)
    mem_out = _mem_attn(proj_a, memkv, seq, A_MQ_OFF, A_MZ_OFF, kv_blk=0)
    x1 = _out_proj(gla_out, mem_out, a_w_out[0].astype(BF16), x2, a_post_norm)

    w_b = _layer_b_weights(b_w_in[0], w_dkv)
    gains_b = jnp.concatenate([b_pre_norm, kv_in_norm[None, :]], axis=0)
    proj_b = _norm_matmul(x1, gains_b, w_b, tm=PROJ_TM, tn=B_TN, group_split=B_GROUP_SPLIT)
    freq, sign = _rope_tables()
    tk = min(ATT_TK, seq)
    tq = min(ATT_TQ, seq)
    q_cat, k_cat, v_t = _mla_prep(
        proj_b, positions.reshape(t, 1), freq, sign, b_q_norm, kv_norm[None, :],
        _uq_weights(b_w_uq[0]), w_uk.astype(BF16), w_uv.T.astype(BF16), batch, seq, tk)
    mla_out = _mla_attn(q_cat, k_cat, v_t, proj_b, batch, seq, tq, tk)
    mem_out_b = _mem_attn(proj_b, memkv, seq, B_MQ_OFF, B_MZ_OFF, kv_blk=2)
    out = _out_proj(mla_out, mem_out_b, b_w_out[0].astype(BF16), x1, b_post_norm)
    return out.reshape(batch, seq, D_MODEL)
```

```python
import functools

import numpy as np
import jax
import jax.numpy as jnp
from jax import lax
from jax.experimental import pallas as pl
from jax.experimental.pallas import tpu as pltpu

F32 = jnp.float32
BF16 = jnp.bfloat16

D_MODEL = 2048
N_MEM = 256
MEM_HEADS = 4
MEM_HEAD_DIM = 128
MEM_WIDTH = MEM_HEADS * MEM_HEAD_DIM
BRANCH_WIDTH = D_MODEL - MEM_WIDTH
GLA_HEADS = 4
GLA_DV = BRANCH_WIDTH // GLA_HEADS
GLA_DK = GLA_DV // 2
GLA_GATE_RANK = 16
GLA_GATE_NORM = 16.0
MLA_HEADS = 12
MLA_V_DIM = 128
MLA_NOPE_DIM = 128
MLA_ROPE_DIM = 64
MLA_QK_DIM = MLA_NOPE_DIM + MLA_ROPE_DIM
MLA_Q_RANK = 512
MLA_KV_RANK = 512
ROPE_THETA = 10000.0
EPS = 1e-6

LANES = 128
MXU_DIM = 256
GLA_DK_PAD = MXU_DIM
MLA_QK_PAD = MXU_DIM
BF16_SUBLANES = 16
MLA_VT_ROWS = MLA_V_DIM + BF16_SUBLANES
ATT_HEADS_PER_STEP = 2
LOG2_E = 1.4426950408889634
VMEM_LIMIT = 56 * 1024 * 1024

PROJ_TM = 1024
PROJ_NORM_ROWS = 256
OUT_TM = 512
GLA_SPAN = 256
GLA_CHUNK = 64
GLA_SUB = 32
GLA_HEADS_PER_STEP = 4
GLA_STAGE_SKEW = 1
MEM_TM = 512
PREP_TM = 512
ATT_TQ = 512
ATT_TK = 512

A_V_OFF = 0
A_Z_OFF = BRANCH_WIDTH
A_Q_OFF = 2 * BRANCH_WIDTH
A_K_OFF = A_Q_OFF + GLA_HEADS * GLA_DK_PAD
A_MQ_OFF = A_K_OFF + GLA_HEADS * GLA_DK_PAD
A_MZ_OFF = A_MQ_OFF + MEM_WIDTH
A_WIDTH = A_MZ_OFF + MEM_WIDTH
A_TN = 1024

B_Z_OFF = 0
B_CQ_OFF = BRANCH_WIDTH
B_MQ_OFF = B_CQ_OFF + MLA_Q_RANK
B_MZ_OFF = B_MQ_OFF + MEM_WIDTH
B_C_OFF = B_MZ_OFF + MEM_WIDTH
B_KR_OFF = B_C_OFF + MLA_KV_RANK
B_KROT_OFF = B_KR_OFF + LANES
B_WIDTH = B_KROT_OFF + LANES
B_TN = 768
B_GROUP_SPLIT = B_C_OFF // B_TN


def _silu(z):
    return z * (1.0 / (1.0 + jnp.exp(-z)))


def _log_sigmoid(g):
    return jnp.minimum(g, 0.0) - jnp.log(1.0 + jnp.exp(-jnp.abs(g)))


def _dot(a, b):
    return jnp.dot(a, b, preferred_element_type=F32)


def _dot_nt(a, b):
    return lax.dot_general(a, b, (((1,), (1,)), ((), ())), preferred_element_type=F32)


def _dot_tn(a, b):
    return lax.dot_general(a, b, (((0,), (0,)), ((), ())), preferred_element_type=F32)


def _norm_matmul_kernel(x_ref, g_ref, w_ref, o_ref, h_ref, *, n_groups, group_split, norm_rows):
    j = pl.program_id(1)

    @pl.when(j == 0)
    def _():
        tm = x_ref.shape[0]
        for r in range(tm // norm_rows):
            rows = pl.ds(r * norm_rows, norm_rows)
            x = x_ref[rows, :]
            xn = x * lax.rsqrt(jnp.mean(x * x, axis=-1, keepdims=True) + EPS)
            for g in range(n_groups):
                h_ref[g, rows, :] = (xn * g_ref[g:g + 1, :]).astype(BF16)

    if n_groups == 1:
        h = h_ref[0]
    else:
        h = h_ref[(j >= group_split).astype(jnp.int32)]
    o_ref[...] = _dot(h, w_ref[...]).astype(o_ref.dtype)


def _norm_matmul(x, gains, w, *, tm, tn, group_split=0):
    m, d = x.shape
    n = w.shape[1]
    n_groups = gains.shape[0]
    tm = min(tm, m)
    norm_rows = min(PROJ_NORM_ROWS, tm)
    kern = functools.partial(_norm_matmul_kernel, n_groups=n_groups, group_split=group_split,
                             norm_rows=norm_rows)
    return pl.pallas_call(
        kern,
        grid=(m // tm, n // tn),
        in_specs=[
            pl.BlockSpec((tm, d), lambda i, j: (i, 0)),
            pl.BlockSpec((n_groups, d), lambda i, j: (0, 0)),
            pl.BlockSpec((d, tn), lambda i, j: (0, j)),
        ],
        out_specs=pl.BlockSpec((tm, tn), lambda i, j: (i, j)),
        out_shape=jax.ShapeDtypeStruct((m, n), BF16),
        scratch_shapes=[pltpu.VMEM((n_groups, tm, d), BF16)],
        compiler_params=pltpu.CompilerParams(
            dimension_semantics=("parallel", "arbitrary"), vmem_limit_bytes=VMEM_LIMIT),
        name="norm_matmul",
    )(x, gains, w)


def _out_proj_kernel(a_ref, m_ref, wa_ref, wm_ref, x_ref, g_ref, o_ref):
    y = _dot(a_ref[...], wa_ref[...]) + _dot(m_ref[...], wm_ref[...])
    ms = jnp.mean(y * y, axis=-1, keepdims=True)
    o_ref[...] = x_ref[...] + y * lax.rsqrt(ms + EPS) * g_ref[...]


def _out_proj(a, mo, w_out, x, gain):
    t = x.shape[0]
    tm = min(OUT_TM, t)
    mem_blk = BRANCH_WIDTH // MEM_WIDTH
    return pl.pallas_call(
        _out_proj_kernel,
        grid=(t // tm,),
        in_specs=[
            pl.BlockSpec((tm, BRANCH_WIDTH), lambda i: (i, 0)),
            pl.BlockSpec((tm, MEM_WIDTH), lambda i: (i, 0)),
            pl.BlockSpec((BRANCH_WIDTH, D_MODEL), lambda i: (0, 0)),
            pl.BlockSpec((MEM_WIDTH, D_MODEL), lambda i: (mem_blk, 0)),
            pl.BlockSpec((tm, D_MODEL), lambda i: (i, 0)),
            pl.BlockSpec((1, D_MODEL), lambda i: (0, 0)),
        ],
        out_specs=pl.BlockSpec((tm, D_MODEL), lambda i: (i, 0)),
        out_shape=jax.ShapeDtypeStruct((t, D_MODEL), F32),
        compiler_params=pltpu.CompilerParams(
            dimension_semantics=("parallel",), vmem_limit_bytes=VMEM_LIMIT),
        name="out_proj",
    )(a, mo, w_out, w_out, x, gain)


def _gla_head_stages(h, v_ref, z_ref, q_ref, k_ref, wg_ref, bg_ref, gn_ref, tril_ref, o_ref, s_ref,
                     chunk, sub):
    span = q_ref.shape[0]
    group = tril_ref.shape[0]
    n_chunks = span // chunk
    scale = GLA_DK ** -0.5
    kcols = slice(h * GLA_DK_PAD, (h + 1) * GLA_DK_PAD)
    vcols = slice(h * GLA_DV, (h + 1) * GLA_DV)

    qb = q_ref[:, kcols]
    g = _dot(qb, wg_ref[h]) + bg_ref[h]
    yield
    log_a = _log_sigmoid(g) * (1.0 / GLA_GATE_NORM)
    a_hi = log_a.astype(BF16)
    rem = log_a - a_hi.astype(F32)
    a_mid = rem.astype(BF16)
    a_lo = (rem - a_mid.astype(F32)).astype(BF16)
    yield
    tril = tril_ref[...]
    cum = jnp.concatenate(
        [_dot(tril, a_hi[r:r + group]) + _dot(tril, a_mid[r:r + group]) + _dot(tril, a_lo[r:r + group])
         for r in range(0, span, group)], axis=0)
    yield
    q = qb.astype(F32)
    k = k_ref[:, kcols].astype(F32)
    v = v_ref[:, vcols]
    last = [cum[(c + 1) * chunk - 1:(c + 1) * chunk, :] for c in range(n_chunks)]
    last_b = jnp.concatenate([jnp.broadcast_to(l, (chunk, l.shape[1])) for l in last], axis=0)
    q_inter = (q * (jnp.exp(cum) * scale)).astype(BF16)
    k_upd = (k * jnp.exp(last_b - cum)).astype(BF16)
    yield
    updates = [_dot_tn(k_upd[c * chunk:(c + 1) * chunk], v[c * chunk:(c + 1) * chunk])
               for c in range(n_chunks)]
    yield

    def intra(c):
        base = c * chunk
        parts = []
        for i in range(chunk // sub):
            lo, hi = base + i * sub, base + (i + 1) * sub
            ref_row = cum[lo:lo + 1, :]
            qi = (q[lo:hi] * (jnp.exp(cum[lo:hi] - ref_row) * scale)).astype(BF16)
            ki = (k[base:hi] * jnp.exp(ref_row - cum[base:hi])).astype(BF16)
            att = _dot_nt(qi, ki)
            t_idx = lax.broadcasted_iota(jnp.int32, att.shape, 0) + i * sub
            s_idx = lax.broadcasted_iota(jnp.int32, att.shape, 1)
            att = jnp.where(t_idx >= s_idx, att, 0.0)
            parts.append(_dot(att.astype(BF16), v[base:hi]))
        return parts

    state = s_ref[h]
    o_rows = []
    for c in range(n_chunks):
        o_inter = _dot(q_inter[c * chunk:(c + 1) * chunk], state.astype(BF16))
        decay_col = jnp.transpose(jnp.exp(cum[(c + 1) * chunk - 8:(c + 1) * chunk, :]))[:, 7:8]
        state = decay_col * state + updates[c]
        o_rows.append(o_inter + jnp.concatenate(intra(c), axis=0))
        yield
    s_ref[h] = state

    o = jnp.concatenate(o_rows, axis=0)
    on = o * lax.rsqrt(jnp.mean(o * o, axis=-1, keepdims=True) + EPS) * gn_ref[...]
    o_ref[:, vcols] = (on * _silu(z_ref[:, vcols].astype(F32))).astype(o_ref.dtype)
    yield


def _gla_kernel(v_ref, z_ref, q_ref, k_ref, wg_ref, bg_ref, gn_ref, tril_ref, o_ref, s_ref, *,
                chunk, sub, skew):
    @pl.when(pl.program_id(2) == 0)
    def _():
        s_ref[...] = jnp.zeros_like(s_ref)

    heads = [_gla_head_stages(h, v_ref, z_ref, q_ref, k_ref, wg_ref, bg_ref, gn_ref, tril_ref, o_ref,
                              s_ref, chunk, sub) for h in range(s_ref.shape[0])]
    live = list(range(len(heads)))
    step = 0
    while live:
        for i in list(live):
            if step >= i * skew:
                try:
                    next(heads[i])
                except StopIteration:
                    live.remove(i)
        step += 1


def _prefix_matrix(rows, chunk):
    r = np.arange(rows)
    return jnp.asarray((r[:, None] // chunk == r[None, :] // chunk) & (r[:, None] >= r[None, :]), BF16)


def _gla(proj, wg, bg, gn, batch, seq):
    span = min(GLA_SPAN, seq)
    ns = seq // span
    nh = GLA_HEADS_PER_STEP
    kern = functools.partial(_gla_kernel, chunk=GLA_CHUNK, sub=GLA_SUB, skew=GLA_STAGE_SKEW)
    vw, kw = nh * GLA_DV, nh * GLA_DK_PAD
    v_blk, z_blk = A_V_OFF // vw, A_Z_OFF // vw
    q_blk, k_blk = A_Q_OFF // kw, A_K_OFF // kw
    group = min(span, MXU_DIM)
    return pl.pallas_call(
        kern,
        grid=(batch, GLA_HEADS // nh, ns),
        in_specs=[
            pl.BlockSpec((span, vw), lambda b, g, s: (b * ns + s, v_blk + g)),
            pl.BlockSpec((span, vw), lambda b, g, s: (b * ns + s, z_blk + g)),
            pl.BlockSpec((span, kw), lambda b, g, s: (b * ns + s, q_blk + g)),
            pl.BlockSpec((span, kw), lambda b, g, s: (b * ns + s, k_blk + g)),
            pl.BlockSpec((nh, GLA_DK_PAD, GLA_DK_PAD), lambda b, g, s: (g, 0, 0)),
            pl.BlockSpec((nh, 1, GLA_DK_PAD), lambda b, g, s: (g, 0, 0)),
            pl.BlockSpec((1, GLA_DV), lambda b, g, s: (0, 0)),
            pl.BlockSpec((group, group), lambda b, g, s: (0, 0)),
        ],
        out_specs=pl.BlockSpec((span, vw), lambda b, g, s: (b * ns + s, g)),
        out_shape=jax.ShapeDtypeStruct((batch * seq, BRANCH_WIDTH), BF16),
        scratch_shapes=[pltpu.VMEM((nh, GLA_DK_PAD, GLA_DV), F32)],
        compiler_params=pltpu.CompilerParams(
            dimension_semantics=("parallel", "parallel", "arbitrary"), vmem_limit_bytes=VMEM_LIMIT),
        name="gla",
    )(proj, proj, proj, proj, wg, bg, gn, _prefix_matrix(group, GLA_CHUNK))


def _mem_attn_kernel(q_ref, z_ref, k_ref, v_ref, o_ref):
    scale = MEM_HEAD_DIM ** -0.5
    for h in range(MEM_HEADS):
        cols = slice(h * MEM_HEAD_DIM, (h + 1) * MEM_HEAD_DIM)
        s = _dot_nt(q_ref[:, cols], k_ref[:, cols]) * scale
        p = jnp.exp(s - jnp.max(s, axis=-1, keepdims=True))
        l = jnp.sum(p, axis=-1, keepdims=True)
        o = _dot(p.astype(BF16), v_ref[:, cols]) * (1.0 / l)
        o_ref[:, cols] = (o * _silu(z_ref[:, cols].astype(F32))).astype(o_ref.dtype)


def _mem_attn(proj, memkv, seq, q_off, z_off, kv_blk):
    t = proj.shape[0]
    tm = min(MEM_TM, seq)
    per_batch = seq // tm
    q_blk, z_blk = q_off // MEM_WIDTH, z_off // MEM_WIDTH
    return pl.pallas_call(
        _mem_attn_kernel,
        grid=(t // tm,),
        in_specs=[
            pl.BlockSpec((tm, MEM_WIDTH), lambda i: (i, q_blk)),
            pl.BlockSpec((tm, MEM_WIDTH), lambda i: (i, z_blk)),
            pl.BlockSpec((N_MEM, MEM_WIDTH), lambda i: (i // per_batch, kv_blk)),
            pl.BlockSpec((N_MEM, MEM_WIDTH), lambda i: (i // per_batch, kv_blk + 1)),
        ],
        out_specs=pl.BlockSpec((tm, MEM_WIDTH), lambda i: (i, 0)),
        out_shape=jax.ShapeDtypeStruct((t, MEM_WIDTH), BF16),
        compiler_params=pltpu.CompilerParams(
            dimension_semantics=("parallel",), vmem_limit_bytes=VMEM_LIMIT),
        name="mem_attn",
    )(proj, proj, memkv, memkv)


def _mla_prep_kernel(cq_ref, c_ref, kr_ref, krot_ref, pos_ref, freq_ref, sign_ref, qg_ref, kvg_ref,
                     wuq_ref, wuk_ref, wuvt_ref, q_ref, k_ref, vt_ref, *, tk):
    tm = cq_ref.shape[0]
    ang = pos_ref[...].astype(F32) * freq_ref[...]
    cos = jnp.cos(ang)
    sin = jnp.sin(ang) * sign_ref[...]
    lane = lax.broadcasted_iota(jnp.int32, (1, LANES), 1)
    slot_mask = [(lane < MLA_ROPE_DIM).astype(F32), (lane >= MLA_ROPE_DIM).astype(F32)]

    cq = cq_ref[...].astype(F32)
    cqn = cq * lax.rsqrt(jnp.mean(cq * cq, axis=-1, keepdims=True) + EPS)
    cqn = (cqn * (qg_ref[...] * (MLA_QK_DIM ** -0.5 * LOG2_E))).astype(BF16)
    n_nope = MLA_HEADS * MLA_NOPE_DIM
    n_rope = MLA_HEADS * MLA_ROPE_DIM
    q_nope = _dot(cqn, wuq_ref[:, :n_nope])
    q_r = _dot(cqn, wuq_ref[:, n_nope:n_nope + n_rope])
    q_rot = _dot(cqn, wuq_ref[:, n_nope + n_rope:])
    for h in range(MLA_HEADS):
        q_ref[h, :, :MLA_NOPE_DIM] = q_nope[:, h * MLA_NOPE_DIM:(h + 1) * MLA_NOPE_DIM].astype(BF16)
    for u in range(MLA_HEADS // 2):
        cols = slice(u * LANES, (u + 1) * LANES)
        roped = q_r[:, cols] * cos + q_rot[:, cols] * sin
        for half in range(2):
            q_ref[2 * u + half, :, MLA_NOPE_DIM:] = (roped * slot_mask[half]).astype(BF16)

    c = c_ref[...].astype(F32)
    cn = c * lax.rsqrt(jnp.mean(c * c, axis=-1, keepdims=True) + EPS)
    cn = (cn * kvg_ref[...]).astype(BF16)
    k_nope = _dot(cn, wuk_ref[...])
    k_rope = (kr_ref[...].astype(F32) * cos + krot_ref[...].astype(F32) * sin).astype(BF16)
    for h in range(MLA_HEADS):
        k_ref[h, :, :MLA_NOPE_DIM] = k_nope[:, h * MLA_NOPE_DIM:(h + 1) * MLA_NOPE_DIM].astype(BF16)
        k_ref[h, :, MLA_NOPE_DIM:] = k_rope
    v_t = _dot_nt(wuvt_ref[...], cn).astype(BF16)
    ones_rows = (lax.broadcasted_iota(jnp.int32, (MLA_VT_ROWS - MLA_V_DIM, tk), 0) == 0).astype(BF16)
    for h in range(MLA_HEADS):
        for u in range(tm // tk):
            vt_ref[h, u, :MLA_V_DIM, :] = v_t[h * MLA_V_DIM:(h + 1) * MLA_V_DIM, u * tk:(u + 1) * tk]
            vt_ref[h, u, MLA_V_DIM:, :] = ones_rows


def _mla_prep(proj, pos_col, freq, sign, q_gain, kv_gain, w_uq, w_uk, w_uv_t, batch, seq, tk):
    tm = min(PREP_TM, seq)
    per_batch = seq // tm
    kern = functools.partial(_mla_prep_kernel, tk=tk)
    const = lambda i: (0, 0)
    head_map = lambda i: (i // per_batch, 0, i % per_batch, 0)
    return pl.pallas_call(
        kern,
        grid=(batch * per_batch,),
        in_specs=[
            pl.BlockSpec((tm, MLA_Q_RANK), lambda i: (i, B_CQ_OFF // MLA_Q_RANK)),
            pl.BlockSpec((tm, MLA_KV_RANK), lambda i: (i, B_C_OFF // MLA_KV_RANK)),
            pl.BlockSpec((tm, LANES), lambda i: (i, B_KR_OFF // LANES)),
            pl.BlockSpec((tm, LANES), lambda i: (i, B_KROT_OFF // LANES)),
            pl.BlockSpec((tm, 1), lambda i: (i, 0)),
            pl.BlockSpec((1, LANES), const),
            pl.BlockSpec((1, LANES), const),
            pl.BlockSpec((1, MLA_Q_RANK), const),
            pl.BlockSpec((1, MLA_KV_RANK), const),
            pl.BlockSpec(w_uq.shape, const),
            pl.BlockSpec(w_uk.shape, const),
            pl.BlockSpec(w_uv_t.shape, const),
        ],
        out_specs=[
            pl.BlockSpec((None, MLA_HEADS, tm, MLA_QK_PAD), head_map),
            pl.BlockSpec((None, MLA_HEADS, tm, MLA_QK_PAD), head_map),
            pl.BlockSpec((None, MLA_HEADS, tm // tk, MLA_VT_ROWS, tk),
                         lambda i: (i // per_batch, 0, i % per_batch, 0, 0)),
        ],
        out_shape=[
            jax.ShapeDtypeStruct((batch, MLA_HEADS, seq, MLA_QK_PAD), BF16),
            jax.ShapeDtypeStruct((batch, MLA_HEADS, seq, MLA_QK_PAD), BF16),
            jax.ShapeDtypeStruct((batch, MLA_HEADS, seq // tk, MLA_VT_ROWS, tk), BF16),
        ],
        compiler_params=pltpu.CompilerParams(
            dimension_semantics=("parallel",), vmem_limit_bytes=VMEM_LIMIT),
        name="mla_prep",
    )(proj, proj, proj, proj, pos_col, freq, sign, q_gain, kv_gain, w_uq, w_uk, w_uv_t)


def _max_over_rows(x, ways=8):
    rows = x.shape[0]
    slab = rows // ways
    parts = [x[i * slab:(i + 1) * slab] for i in range(ways)] if slab >= 8 and rows % ways == 0 else [x]
    while len(parts) > 1:
        parts = [jnp.maximum(parts[i], parts[i + 1]) for i in range(0, len(parts), 2)]
    return jnp.max(parts[0], axis=0, keepdims=True)


def _mla_attn_kernel(q_ref, k_ref, vt_ref, z_ref, o_ref, acc_ref, m_ref, *, tq, tk):
    n_heads, seq = q_ref.shape[0], q_ref.shape[1]
    tn = MXU_DIM
    units = [(h, n) for h in range(n_heads) for n in range(tq // tn)]

    def kv_step(qi, kj, diagonal):
        s, p, alpha = {}, {}, {}

        def n_keys(n):
            return min(tk, (n + 1) * tn) if diagonal else tk

        def scores(u):
            h, n = units[u]
            nk = n_keys(n)
            krows = pl.ds(pl.multiple_of(kj * tk, tk), nk)
            qrows = pl.ds(pl.multiple_of(qi * tq + n * tn, tn), tn)
            s_t = _dot_nt(k_ref[h, krows, :], q_ref[h, qrows, :])
            if diagonal:
                kpos = lax.broadcasted_iota(jnp.int32, (nk, tn), 0)
                qpos = lax.broadcasted_iota(jnp.int32, (nk, tn), 1) + n * tn
                s_t = jnp.where(kpos <= qpos, s_t, -jnp.inf)
            s[u] = s_t

        def softmax(u):
            h, n = units[u]
            cols = slice(n * tn, (n + 1) * tn)
            m_prev = m_ref[h, :, cols]
            m_new = jnp.maximum(m_prev, _max_over_rows(s[u]))
            alpha[u] = jnp.exp2(m_prev - m_new)
            p[u] = jnp.exp2(s.pop(u) - m_new).astype(BF16)
            m_ref[h, :, cols] = m_new

        def weighted_sum(u):
            h, n = units[u]
            cols = slice(n * tn, (n + 1) * tn)
            pv = _dot(vt_ref[h, kj, :, :n_keys(n)], p.pop(u))
            acc_ref[h, :, cols] = alpha.pop(u) * acc_ref[h, :, cols] + pv

        for step in range(len(units) + 2):
            if step < len(units):
                scores(step)
            if 1 <= step <= len(units):
                softmax(step - 1)
            if step >= 2:
                weighted_sum(step - 2)

    def q_block(qi, carry):
        rows = pl.ds(pl.multiple_of(qi * tq, tq), tq)
        m_ref[...] = jnp.full_like(m_ref, -jnp.inf)
        acc_ref[...] = jnp.zeros_like(acc_ref)

        def full_step(kj, c):
            kv_step(qi, kj, False)
            return c

        lax.fori_loop(0, qi, full_step, 0)
        kv_step(qi, qi, True)
        for h in range(n_heads):
            acc = acc_ref[h]
            o = jnp.transpose(acc[:MLA_V_DIM] * (1.0 / acc[MLA_V_DIM:MLA_V_DIM + 1]))
            cols = slice(h * MLA_V_DIM, (h + 1) * MLA_V_DIM)
            o_ref[rows, cols] = (o * _silu(z_ref[rows, cols].astype(F32))).astype(o_ref.dtype)
        return carry

    lax.fori_loop(0, seq // tq, q_block, 0)


def _mla_attn(q_cat, k_cat, v_t, proj, batch, seq, tq, tk):
    kern = functools.partial(_mla_attn_kernel, tq=tq, tk=tk)
    nb = seq // tk
    nh = ATT_HEADS_PER_STEP
    width = nh * MLA_V_DIM
    z_blk = B_Z_OFF // width
    return pl.pallas_call(
        kern,
        grid=(batch, MLA_HEADS // nh),
        in_specs=[
            pl.BlockSpec((None, nh, seq, MLA_QK_PAD), lambda b, g: (b, g, 0, 0)),
            pl.BlockSpec((None, nh, seq, MLA_QK_PAD), lambda b, g: (b, g, 0, 0)),
            pl.BlockSpec((None, nh, nb, MLA_VT_ROWS, tk), lambda b, g: (b, g, 0, 0, 0)),
            pl.BlockSpec((seq, width), lambda b, g: (b, z_blk + g)),
        ],
        out_specs=pl.BlockSpec((seq, width), lambda b, g: (b, g)),
        out_shape=jax.ShapeDtypeStruct((batch * seq, BRANCH_WIDTH), BF16),
        scratch_shapes=[
            pltpu.VMEM((nh, MLA_VT_ROWS, tq), F32),
            pltpu.VMEM((nh, 1, tq), F32),
        ],
        compiler_params=pltpu.CompilerParams(
            dimension_semantics=("parallel", "parallel"), vmem_limit_bytes=VMEM_LIMIT),
        name="mla_attn",
    )(q_cat, k_cat, v_t, proj)


def _pad_heads(w, heads, width, padded):
    lead = w.shape[:-1]
    w = w.reshape(lead + (heads, width))
    w = jnp.pad(w, [(0, 0)] * len(lead) + [(0, 0), (0, padded - width)])
    return w.reshape(lead + (heads * padded,))


def _rot_half(w, heads):
    lead = w.shape[:-1]
    half = MLA_ROPE_DIM // 2
    w = w.reshape(lead + (heads, 2, half))
    return w[..., ::-1, :].reshape(lead + (heads * MLA_ROPE_DIM,))


def _layer_a_weights(w_in, w_g2, b_g):
    hk = GLA_HEADS * GLA_DK
    o = np.cumsum([0, hk, hk, BRANCH_WIDTH, GLA_GATE_RANK, BRANCH_WIDTH, MEM_WIDTH, MEM_WIDTH])
    wq, wk, wv, wlr, wz, wmq, wmz = [w_in[:, o[i]:o[i + 1]] for i in range(7)]
    wq = wq.reshape(D_MODEL, GLA_HEADS, GLA_DK)
    wlr_rep = jnp.broadcast_to(wlr[:, None, :], (D_MODEL, GLA_HEADS, GLA_GATE_RANK))
    zeros = jnp.zeros((D_MODEL, GLA_HEADS, GLA_DK_PAD - GLA_DK - GLA_GATE_RANK), w_in.dtype)
    wq = jnp.concatenate([wq, wlr_rep, zeros], axis=-1).reshape(D_MODEL, GLA_HEADS * GLA_DK_PAD)
    wk = _pad_heads(wk, GLA_HEADS, GLA_DK, GLA_DK_PAD)
    w_all = jnp.concatenate([wv, wz, wq, wk, wmq, wmz], axis=1).astype(BF16)
    wg = _pad_heads(w_g2, GLA_HEADS, GLA_DK, GLA_DK_PAD)
    wg = wg.reshape(GLA_GATE_RANK, GLA_HEADS, GLA_DK_PAD).transpose(1, 0, 2)
    wg = jnp.pad(wg, [(0, 0), (GLA_DK, GLA_DK_PAD - GLA_DK - GLA_GATE_RANK), (0, 0)]).astype(BF16)
    bg = _pad_heads(b_g, GLA_HEADS, GLA_DK, GLA_DK_PAD).reshape(GLA_HEADS, 1, GLA_DK_PAD)
    return w_all, wg, bg


def _layer_b_weights(w_in, w_dkv):
    o = np.cumsum([0, MLA_Q_RANK, BRANCH_WIDTH, MEM_WIDTH, MEM_WIDTH])
    wcq, wz, wmq, wmz = [w_in[:, o[i]:o[i + 1]] for i in range(4)]
    wc, wkr = w_dkv[:, :MLA_KV_RANK], w_dkv[:, MLA_KV_RANK:]
    wkrot = _rot_half(wkr, 1)
    return jnp.concatenate([wz, wcq, wmq, wmz, wc, wkr, wkr, wkrot, wkrot], axis=1).astype(BF16)


def _uq_weights(w_uq):
    w = w_uq.reshape(MLA_Q_RANK, MLA_HEADS, MLA_QK_DIM)
    w_nope = w[:, :, :MLA_NOPE_DIM].reshape(MLA_Q_RANK, MLA_HEADS * MLA_NOPE_DIM)
    w_rope = w[:, :, MLA_NOPE_DIM:].reshape(MLA_Q_RANK, MLA_HEADS * MLA_ROPE_DIM)
    return jnp.concatenate([w_nope, w_rope, _rot_half(w_rope, MLA_HEADS)], axis=1).astype(BF16)


def _rope_tables():
    r = MLA_ROPE_DIM
    freqs = ROPE_THETA ** (-jnp.arange(0, r, 2, dtype=F32) / r)
    freq = jnp.tile(freqs, LANES // (r // 2)).reshape(1, LANES)
    sign = jnp.tile(jnp.concatenate([-jnp.ones(r // 2, F32), jnp.ones(r // 2, F32)]),
                    LANES // r).reshape(1, LANES)
    return freq, sign


def kernel(x, mem, positions, a_pre_norm, a_w_in, a_w_g2, a_b_g, a_gla_norm, a_mem_norm, a_w_mem_kv,
           a_w_out, a_post_norm, kv_in_norm, w_dkv, kv_norm, w_uk, w_uv, b_pre_norm, b_w_in, b_q_norm,
           b_w_uq, b_mem_norm, b_w_mem_kv, b_w_out, b_post_norm):
    assert a_w_in.shape[0] == 1 and b_w_in.shape[0] == 1, "one A layer followed by one B layer"
    batch, seq, _ = x.shape
    t = batch * seq
    x2 = x.reshape(t, D_MODEL)

    mem_gains = jnp.concatenate([a_mem_norm, b_mem_norm], axis=0)
    w_memkv = jnp.concatenate([a_w_mem_kv[0], b_w_mem_kv[0]], axis=1).astype(BF16)
    memkv = _norm_matmul(mem.reshape(batch * N_MEM, D_MODEL), mem_gains, w_memkv,
                         tm=batch * N_MEM, tn=2 * MEM_WIDTH, group_split=1)

    w_a, wg, bg = _layer_a_weights(a_w_in[0], a_w_g2[0], a_b_g[0])
    proj_a = _norm_matmul(x2, a_pre_norm, w_a, tm=PROJ_TM, tn=A_TN)
    gla_out = _gla(proj_a, wg, bg, a_gla_norm, batch, seq)
    mem_out = _mem_attn(proj_a, memkv, seq, A_MQ_OFF, A_MZ_OFF, kv_blk=0)
    x1 = _out_proj(gla_out, mem_out, a_w_out[0].astype(BF16), x2, a_post_norm)

    w_b = _layer_b_weights(b_w_in[0], w_dkv)
    gains_b = jnp.concatenate([b_pre_norm, kv_in_norm[None, :]], axis=0)
    proj_b = _norm_matmul(x1, gains_b, w_b, tm=PROJ_TM, tn=B_TN, group_split=B_GROUP_SPLIT)
    freq, sign = _rope_tables()
    tk = min(ATT_TK, seq)
    tq = min(ATT_TQ, seq)
    q_cat, k_cat, v_t = _mla_prep(
        proj_b, positions.reshape(t, 1), freq, sign, b_q_norm, kv_norm[None, :],
        _uq_weights(b_w_uq[0]), w_uk.astype(BF16), w_uv.T.astype(BF16), batch, seq, tk)
    mla_out = _mla_attn(q_cat, k_cat, v_t, proj_b, batch, seq, tq, tk)
    mem_out_b = _mem_attn(proj_b, memkv, seq, B_MQ_OFF, B_MZ_OFF, kv_blk=2)
    out = _out_proj(mla_out, mem_out_b, b_w_out[0].astype(BF16), x1, b_post_norm)
    return out.reshape(batch, seq, D_MODEL)
```

```python
import functools

import numpy as np
import jax
import jax.numpy as jnp
from jax import lax
from jax.experimental import pallas as pl
from jax.experimental.pallas import tpu as pltpu

F32 = jnp.float32
BF16 = jnp.bfloat16

D_MODEL = 2048
N_MEM = 256
MEM_HEADS = 4
MEM_HEAD_DIM = 128
MEM_WIDTH = MEM_HEADS * MEM_HEAD_DIM
BRANCH_WIDTH = D_MODEL - MEM_WIDTH
GLA_HEADS = 4
GLA_DV = BRANCH_WIDTH // GLA_HEADS
GLA_DK = GLA_DV // 2
GLA_GATE_RANK = 16
GLA_GATE_NORM = 16.0
MLA_HEADS = 12
MLA_V_DIM = 128
MLA_NOPE_DIM = 128
MLA_ROPE_DIM = 64
MLA_QK_DIM = MLA_NOPE_DIM + MLA_ROPE_DIM
MLA_Q_RANK = 512
MLA_KV_RANK = 512
ROPE_THETA = 10000.0
EPS = 1e-6

LANES = 128
MXU_DIM = 256
GLA_DK_PAD = MXU_DIM
MLA_QK_PAD = MXU_DIM
BF16_SUBLANES = 16
MLA_VT_ROWS = MLA_V_DIM + BF16_SUBLANES
ATT_HEADS_PER_STEP = 2
LOG2_E = 1.4426950408889634
VMEM_LIMIT = 56 * 1024 * 1024

PROJ_TM = 1024
PROJ_NORM_ROWS = 256
OUT_TM = 512
GLA_SPAN = 256
GLA_CHUNK = 64
GLA_SUB = 32
GLA_HEADS_PER_STEP = 4
GLA_STAGE_SKEW = 1
MEM_TM = 512
PREP_TM = 512
ATT_TQ = 512
ATT_TK = 512

A_V_OFF = 0
A_Z_OFF = BRANCH_WIDTH
A_Q_OFF = 2 * BRANCH_WIDTH
A_K_OFF = A_Q_OFF + GLA_HEADS * GLA_DK_PAD
A_MQ_OFF = A_K_OFF + GLA_HEADS * GLA_DK_PAD
A_MZ_OFF = A_MQ_OFF + MEM_WIDTH
A_WIDTH = A_MZ_OFF + MEM_WIDTH
A_TN = 1024

B_Z_OFF = 0
B_CQ_OFF = BRANCH_WIDTH
B_MQ_OFF = B_CQ_OFF + MLA_Q_RANK
B_MZ_OFF = B_MQ_OFF + MEM_WIDTH
B_C_OFF = B_MZ_OFF + MEM_WIDTH
B_KR_OFF = B_C_OFF + MLA_KV_RANK
B_KROT_OFF = B_KR_OFF + LANES
B_WIDTH = B_KROT_OFF + LANES
B_TN = 768
B_GROUP_SPLIT = B_C_OFF // B_TN


def _silu(z):
    return z * (1.0 / (1.0 + jnp.exp(-z)))


def _log_sigmoid(g):
    return jnp.minimum(g, 0.0) - jnp.log(1.0 + jnp.exp(-jnp.abs(g)))


def _dot(a, b):
    return jnp.dot(a, b, preferred_element_type=F32)


def _dot_nt(a, b):
    return lax.dot_general(a, b, (((1,), (1,)), ((), ())), preferred_element_type=F32)


def _dot_tn(a, b):
    return lax.dot_general(a, b, (((0,), (0,)), ((), ())), preferred_element_type=F32)


def _norm_matmul_kernel(x_ref, g_ref, w_ref, o_ref, h_ref, *, n_groups, group_split, norm_rows):
    j = pl.program_id(1)

    @pl.when(j == 0)
    def _():
        tm = x_ref.shape[0]
        for r in range(tm // norm_rows):
            rows = pl.ds(r * norm_rows, norm_rows)
            x = x_ref[rows, :]
            xn = x * lax.rsqrt(jnp.mean(x * x, axis=-1, keepdims=True) + EPS)
            for g in range(n_groups):
                h_ref[g, rows, :] = (xn * g_ref[g:g + 1, :]).astype(BF16)

    if n_groups == 1:
        h = h_ref[0]
    else:
        h = h_ref[(j >= group_split).astype(jnp.int32)]
    o_ref[...] = _dot(h, w_ref[...]).astype(o_ref.dtype)


def _norm_matmul(x, gains, w, *, tm, tn, group_split=0):
    m, d = x.shape
    n = w.shape[1]
    n_groups = gains.shape[0]
    tm = min(tm, m)
    norm_rows = min(PROJ_NORM_ROWS, tm)
    kern = functools.partial(_norm_matmul_kernel, n_groups=n_groups, group_split=group_split,
                             norm_rows=norm_rows)
    return pl.pallas_call(
        kern,
        grid=(m // tm, n // tn),
        in_specs=[
            pl.BlockSpec((tm, d), lambda i, j: (i, 0)),
            pl.BlockSpec((n_groups, d), lambda i, j: (0, 0)),
            pl.BlockSpec((d, tn), lambda i, j: (0, j)),
        ],
        out_specs=pl.BlockSpec((tm, tn), lambda i, j: (i, j)),
        out_shape=jax.ShapeDtypeStruct((m, n), BF16),
        scratch_shapes=[pltpu.VMEM((n_groups, tm, d), BF16)],
        compiler_params=pltpu.CompilerParams(
            dimension_semantics=("parallel", "arbitrary"), vmem_limit_bytes=VMEM_LIMIT),
        name="norm_matmul",
    )(x, gains, w)


def _out_proj_kernel(a_ref, m_ref, wa_ref, wm_ref, x_ref, g_ref, o_ref):
    y = _dot(a_ref[...], wa_ref[...]) + _dot(m_ref[...], wm_ref[...])
    ms = jnp.mean(y * y, axis=-1, keepdims=True)
    o_ref[...] = x_ref[...] + y * lax.rsqrt(ms + EPS) * g_ref[...]


def _out_proj(a, mo, w_out, x, gain):
    t = x.shape[0]
    tm = min(OUT_TM, t)
    mem_blk = BRANCH_WIDTH // MEM_WIDTH
    return pl.pallas_call(
        _out_proj_kernel,
        grid=(t // tm,),
        in_specs=[
            pl.BlockSpec((tm, BRANCH_WIDTH), lambda i: (i, 0)),
            pl.BlockSpec((tm, MEM_WIDTH), lambda i: (i, 0)),
            pl.BlockSpec((BRANCH_WIDTH, D_MODEL), lambda i: (0, 0)),
            pl.BlockSpec((MEM_WIDTH, D_MODEL), lambda i: (mem_blk, 0)),
            pl.BlockSpec((tm, D_MODEL), lambda i: (i, 0)),
            pl.BlockSpec((1, D_MODEL), lambda i: (0, 0)),
        ],
        out_specs=pl.BlockSpec((tm, D_MODEL), lambda i: (i, 0)),
        out_shape=jax.ShapeDtypeStruct((t, D_MODEL), F32),
        compiler_params=pltpu.CompilerParams(
            dimension_semantics=("parallel",), vmem_limit_bytes=VMEM_LIMIT),
        name="out_proj",
    )(a, mo, w_out, w_out, x, gain)


def _gla_head_stages(h, v_ref, z_ref, q_ref, k_ref, wg_ref, bg_ref, gn_ref, tril_ref, o_ref, s_ref,
                     chunk, sub):
    span = q_ref.shape[0]
    group = tril_ref.shape[0]
    n_chunks = span // chunk
    scale = GLA_DK ** -0.5
    kcols = slice(h * GLA_DK_PAD, (h + 1) * GLA_DK_PAD)
    vcols = slice(h * GLA_DV, (h + 1) * GLA_DV)

    qb = q_ref[:, kcols]
    g = _dot(qb, wg_ref[h]) + bg_ref[h]
    yield
    log_a = _log_sigmoid(g) * (1.0 / GLA_GATE_NORM)
    a_hi = log_a.astype(BF16)
    rem = log_a - a_hi.astype(F32)
    a_mid = rem.astype(BF16)
    a_lo = (rem - a_mid.astype(F32)).astype(BF16)
    yield
    tril = tril_ref[...]
    cum = jnp.concatenate(
        [_dot(tril, a_hi[r:r + group]) + _dot(tril, a_mid[r:r + group]) + _dot(tril, a_lo[r:r + group])
         for r in range(0, span, group)], axis=0)
    yield
    q = qb.astype(F32)
    k = k_ref[:, kcols].astype(F32)
    v = v_ref[:, vcols]
    last = [cum[(c + 1) * chunk - 1:(c + 1) * chunk, :] for c in range(n_chunks)]
    last_b = jnp.concatenate([jnp.broadcast_to(l, (chunk, l.shape[1])) for l in last], axis=0)
    q_inter = (q * (jnp.exp(cum) * scale)).astype(BF16)
    k_upd = (k * jnp.exp(last_b - cum)).astype(BF16)
    yield
    updates = [_dot_tn(k_upd[c * chunk:(c + 1) * chunk], v[c * chunk:(c + 1) * chunk])
               for c in range(n_chunks)]
    yield

    def intra(c):
        base = c * chunk
        parts = []
        for i in range(chunk // sub):
            lo, hi = base + i * sub, base + (i + 1) * sub
            ref_row = cum[lo:lo + 1, :]
            qi = (q[lo:hi] * (jnp.exp(cum[lo:hi] - ref_row) * scale)).astype(BF16)
            ki = (k[base:hi] * jnp.exp(ref_row - cum[base:hi])).astype(BF16)
            att = _dot_nt(qi, ki)
            t_idx = lax.broadcasted_iota(jnp.int32, att.shape, 0) + i * sub
            s_idx = lax.broadcasted_iota(jnp.int32, att.shape, 1)
            att = jnp.where(t_idx >= s_idx, att, 0.0)
            parts.append(_dot(att.astype(BF16), v[base:hi]))
        return parts

    state = s_ref[h]
    o_rows = []
    for c in range(n_chunks):
        o_inter = _dot(q_inter[c * chunk:(c + 1) * chunk], state.astype(BF16))
        decay_col = jnp.transpose(jnp.exp(cum[(c + 1) * chunk - 8:(c + 1) * chunk, :]))[:, 7:8]
        state = decay_col * state + updates[c]
        o_rows.append(o_inter + jnp.concatenate(intra(c), axis=0))
        yield
    s_ref[h] = state

    o = jnp.concatenate(o_rows, axis=0)
    on = o * lax.rsqrt(jnp.mean(o * o, axis=-1, keepdims=True) + EPS) * gn_ref[...]
    o_ref[:, vcols] = (on * _silu(z_ref[:, vcols].astype(F32))).astype(o_ref.dtype)
    yield


def _gla_kernel(v_ref, z_ref, q_ref, k_ref, wg_ref, bg_ref, gn_ref, tril_ref, o_ref, s_ref, *,
                chunk, sub, skew):
    @pl.when(pl.program_id(2) == 0)
    def _():
        s_ref[...] = jnp.zeros_like(s_ref)

    heads = [_gla_head_stages(h, v_ref, z_ref, q_ref, k_ref, wg_ref, bg_ref, gn_ref, tril_ref, o_ref,
                              s_ref, chunk, sub) for h in range(s_ref.shape[0])]
    live = list(range(len(heads)))
    step = 0
    while live:
        for i in list(live):
            if step >= i * skew:
                try:
                    next(heads[i])
                except StopIteration:
                    live.remove(i)
        step += 1


def _prefix_matrix(rows, chunk):
    r = np.arange(rows)
    return jnp.asarray((r[:, None] // chunk == r[None, :] // chunk) & (r[:, None] >= r[None, :]), BF16)


def _gla(proj, wg, bg, gn, batch, seq):
    span = min(GLA_SPAN, seq)
    ns = seq // span
    nh = GLA_HEADS_PER_STEP
    kern = functools.partial(_gla_kernel, chunk=GLA_CHUNK, sub=GLA_SUB, skew=GLA_STAGE_SKEW)
    vw, kw = nh * GLA_DV, nh * GLA_DK_PAD
    v_blk, z_blk = A_V_OFF // vw, A_Z_OFF // vw
    q_blk, k_blk = A_Q_OFF // kw, A_K_OFF // kw
    group = min(span, MXU_DIM)
    return pl.pallas_call(
        kern,
        grid=(batch, GLA_HEADS // nh, ns),
        in_specs=[
            pl.BlockSpec((span, vw), lambda b, g, s: (b * ns + s, v_blk + g)),
            pl.BlockSpec((span, vw), lambda b, g, s: (b * ns + s, z_blk + g)),
            pl.BlockSpec((span, kw), lambda b, g, s: (b * ns + s, q_blk + g)),
            pl.BlockSpec((span, kw), lambda b, g, s: (b * ns + s, k_blk + g)),
            pl.BlockSpec((nh, GLA_DK_PAD, GLA_DK_PAD), lambda b, g, s: (g, 0, 0)),
            pl.BlockSpec((nh, 1, GLA_DK_PAD), lambda b, g, s: (g, 0, 0)),
            pl.BlockSpec((1, GLA_DV), lambda b, g, s: (0, 0)),
            pl.BlockSpec((group, group), lambda b, g, s: (0, 0)),
        ],
        out_specs=pl.BlockSpec((span, vw), lambda b, g, s: (b * ns + s, g)),
        out_shape=jax.ShapeDtypeStruct((batch * seq, BRANCH_WIDTH), BF16),
        scratch_shapes=[pltpu.VMEM((nh, GLA_DK_PAD, GLA_DV), F32)],
        compiler_params=pltpu.CompilerParams(
            dimension_semantics=("parallel", "parallel", "arbitrary"), vmem_limit_bytes=VMEM_LIMIT),
        name="gla",
    )(proj, proj, proj, proj, wg, bg, gn, _prefix_matrix(group, GLA_CHUNK))


def _mem_attn_kernel(q_ref, z_ref, k_ref, v_ref, o_ref):
    scale = MEM_HEAD_DIM ** -0.5
    for h in range(MEM_HEADS):
        cols = slice(h * MEM_HEAD_DIM, (h + 1) * MEM_HEAD_DIM)
        s = _dot_nt(q_ref[:, cols], k_ref[:, cols]) * scale
        p = jnp.exp(s - jnp.max(s, axis=-1, keepdims=True))
        l = jnp.sum(p, axis=-1, keepdims=True)
        o = _dot(p.astype(BF16), v_ref[:, cols]) * (1.0 / l)
        o_ref[:, cols] = (o * _silu(z_ref[:, cols].astype(F32))).astype(o_ref.dtype)


def _mem_attn(proj, memkv, seq, q_off, z_off, kv_blk):
    t = proj.shape[0]
    tm = min(MEM_TM, seq)
    per_batch = seq // tm
    q_blk, z_blk = q_off // MEM_WIDTH, z_off // MEM_WIDTH
    return pl.pallas_call(
        _mem_attn_kernel,
        grid=(t // tm,),
        in_specs=[
            pl.BlockSpec((tm, MEM_WIDTH), lambda i: (i, q_blk)),
            pl.BlockSpec((tm, MEM_WIDTH), lambda i: (i, z_blk)),
            pl.BlockSpec((N_MEM, MEM_WIDTH), lambda i: (i // per_batch, kv_blk)),
            pl.BlockSpec((N_MEM, MEM_WIDTH), lambda i: (i // per_batch, kv_blk + 1)),
        ],
        out_specs=pl.BlockSpec((tm, MEM_WIDTH), lambda i: (i, 0)),
        out_shape=jax.ShapeDtypeStruct((t, MEM_WIDTH), BF16),
        compiler_params=pltpu.CompilerParams(
            dimension_semantics=("parallel",), vmem_limit_bytes=VMEM_LIMIT),
        name="mem_attn",
    )(proj, proj, memkv, memkv)


def _mla_prep_kernel(cq_ref, c_ref, kr_ref, krot_ref, pos_ref, freq_ref, sign_ref, qg_ref, kvg_ref,
                     wuq_ref, wuk_ref, wuvt_ref, q_ref, k_ref, vt_ref, *, tk):
    tm = cq_ref.shape[0]
    ang = pos_ref[...].astype(F32) * freq_ref[...]
    cos = jnp.cos(ang)
    sin = jnp.sin(ang) * sign_ref[...]
    lane = lax.broadcasted_iota(jnp.int32, (1, LANES), 1)
    slot_mask = [(lane < MLA_ROPE_DIM).astype(F32), (lane >= MLA_ROPE_DIM).astype(F32)]

    cq = cq_ref[...].astype(F32)
    cqn = cq * lax.rsqrt(jnp.mean(cq * cq, axis=-1, keepdims=True) + EPS)
    cqn = (cqn * (qg_ref[...] * (MLA_QK_DIM ** -0.5 * LOG2_E))).astype(BF16)
    n_nope = MLA_HEADS * MLA_NOPE_DIM
    n_rope = MLA_HEADS * MLA_ROPE_DIM
    q_nope = _dot(cqn, wuq_ref[:, :n_nope])
    q_r = _dot(cqn, wuq_ref[:, n_nope:n_nope + n_rope])
    q_rot = _dot(cqn, wuq_ref[:, n_nope + n_rope:])
    for h in range(MLA_HEADS):
        q_ref[h, :, :MLA_NOPE_DIM] = q_nope[:, h * MLA_NOPE_DIM:(h + 1) * MLA_NOPE_DIM].astype(BF16)
    for u in range(MLA_HEADS // 2):
        cols = slice(u * LANES, (u + 1) * LANES)
        roped = q_r[:, cols] * cos + q_rot[:, cols] * sin
        for half in range(2):
            q_ref[2 * u + half, :, MLA_NOPE_DIM:] = (roped * slot_mask[half]).astype(BF16)

    c = c_ref[...].astype(F32)
    cn = c * lax.rsqrt(jnp.mean(c * c, axis=-1, keepdims=True) + EPS)
    cn = (cn * kvg_ref[...]).astype(BF16)
    k_nope = _dot(cn, wuk_ref[...])
    k_rope = (kr_ref[...].astype(F32) * cos + krot_ref[...].astype(F32) * sin).astype(BF16)
    for h in range(MLA_HEADS):
        k_ref[h, :, :MLA_NOPE_DIM] = k_nope[:, h * MLA_NOPE_DIM:(h + 1) * MLA_NOPE_DIM].astype(BF16)
        k_ref[h, :, MLA_NOPE_DIM:] = k_rope
    v_t = _dot_nt(wuvt_ref[...], cn).astype(BF16)
    ones_rows = (lax.broadcasted_iota(jnp.int32, (MLA_VT_ROWS - MLA_V_DIM, tk), 0) == 0).astype(BF16)
    for h in range(MLA_HEADS):
        for u in range(tm // tk):
            vt_ref[h, u, :MLA_V_DIM, :] = v_t[h * MLA_V_DIM:(h + 1) * MLA_V_DIM, u * tk:(u + 1) * tk]
            vt_ref[h, u, MLA_V_DIM:, :] = ones_rows


def _mla_prep(proj, pos_col, freq, sign, q_gain, kv_gain, w_uq, w_uk, w_uv_t, batch, seq, tk):
    tm = min(PREP_TM, seq)
    per_batch = seq // tm
    kern = functools.partial(_mla_prep_kernel, tk=tk)
    const = lambda i: (0, 0)
    head_map = lambda i: (i // per_batch, 0, i % per_batch, 0)
    return pl.pallas_call(
        kern,
        grid=(batch * per_batch,),
        in_specs=[
            pl.BlockSpec((tm, MLA_Q_RANK), lambda i: (i, B_CQ_OFF // MLA_Q_RANK)),
            pl.BlockSpec((tm, MLA_KV_RANK), lambda i: (i, B_C_OFF // MLA_KV_RANK)),
            pl.BlockSpec((tm, LANES), lambda i: (i, B_KR_OFF // LANES)),
            pl.BlockSpec((tm, LANES), lambda i: (i, B_KROT_OFF // LANES)),
            pl.BlockSpec((tm, 1), lambda i: (i, 0)),
            pl.BlockSpec((1, LANES), const),
            pl.BlockSpec((1, LANES), const),
            pl.BlockSpec((1, MLA_Q_RANK), const),
            pl.BlockSpec((1, MLA_KV_RANK), const),
            pl.BlockSpec(w_uq.shape, const),
            pl.BlockSpec(w_uk.shape, const),
            pl.BlockSpec(w_uv_t.shape, const),
        ],
        out_specs=[
            pl.BlockSpec((None, MLA_HEADS, tm, MLA_QK_PAD), head_map),
            pl.BlockSpec((None, MLA_HEADS, tm, MLA_QK_PAD), head_map),
            pl.BlockSpec((None, MLA_HEADS, tm // tk, MLA_VT_ROWS, tk),
                         lambda i: (i // per_batch, 0, i % per_batch, 0, 0)),
        ],
        out_shape=[
            jax.ShapeDtypeStruct((batch, MLA_HEADS, seq, MLA_QK_PAD), BF16),
            jax.ShapeDtypeStruct((batch, MLA_HEADS, seq, MLA_QK_PAD), BF16),
            jax.ShapeDtypeStruct((batch, MLA_HEADS, seq // tk, MLA_VT_ROWS, tk), BF16),
        ],
        compiler_params=pltpu.CompilerParams(
            dimension_semantics=("parallel",), vmem_limit_bytes=VMEM_LIMIT),
        name="mla_prep",
    )(proj, proj, proj, proj, pos_col, freq, sign, q_gain, kv_gain, w_uq, w_uk, w_uv_t)


def _max_over_rows(x, ways=8):
    rows = x.shape[0]
    slab = rows // ways
    parts = [x[i * slab:(i + 1) * slab] for i in range(ways)] if slab >= 8 and rows % ways == 0 else [x]
    while len(parts) > 1:
        parts = [jnp.maximum(parts[i], parts[i + 1]) for i in range(0, len(parts), 2)]
    return jnp.max(parts[0], axis=0, keepdims=True)


def _mla_attn_kernel(q_ref, k_ref, vt_ref, z_ref, o_ref, acc_ref, m_ref, s0_ref, s1_ref, smax0_ref,
                     smax1_ref, *, tq, tk):
    n_heads, seq = q_ref.shape[0], q_ref.shape[1]
    tn = MXU_DIM
    units = [(h, n) for h in range(n_heads) for n in range(tq // tn)]
    s_slots = (s0_ref, s1_ref)
    smax_slots = (smax0_ref, smax1_ref)

    def n_keys(n, diagonal):
        return min(tk, (n + 1) * tn) if diagonal else tk

    def produce(qi, kj, slot, u, diagonal):
        h, n = units[u]
        nk = n_keys(n, diagonal)
        krows = pl.ds(pl.multiple_of(kj * tk, tk), nk)
        qrows = pl.ds(pl.multiple_of(qi * tq + n * tn, tn), tn)
        s_t = _dot_nt(k_ref[h, krows, :], q_ref[h, qrows, :])
        if diagonal:
            kpos = lax.broadcasted_iota(jnp.int32, (nk, tn), 0)
            qpos = lax.broadcasted_iota(jnp.int32, (nk, tn), 1) + n * tn
            s_t = jnp.where(kpos <= qpos, s_t, -jnp.inf)
        s_slots[slot][u, :nk, :] = s_t
        smax_slots[slot][u] = _max_over_rows(s_t)

    def consume(kj, slot, u, diagonal):
        h, n = units[u]
        nk = n_keys(n, diagonal)
        cols = slice(n * tn, (n + 1) * tn)
        m_prev = m_ref[h, :, cols]
        m_new = jnp.maximum(m_prev, smax_slots[slot][u])
        alpha = jnp.exp2(m_prev - m_new)
        p = jnp.exp2(s_slots[slot][u, :nk, :] - m_new).astype(BF16)
        m_ref[h, :, cols] = m_new
        acc_ref[h, :, cols] = alpha * acc_ref[h, :, cols] + _dot(vt_ref[h, kj, :, :nk], p)

    def produce_all(qi, kj, slot, diagonal):
        for u in range(len(units)):
            produce(qi, kj, slot, u, diagonal)

    def consume_all(kj, slot, diagonal):
        for u in range(len(units)):
            consume(kj, slot, u, diagonal)

    def consume_and_produce(qi, kj, slot, next_diagonal):
        produce(qi, kj + 1, 1 - slot, 0, next_diagonal)
        for u in range(len(units)):
            consume(kj, slot, u, False)
            if u + 1 < len(units):
                produce(qi, kj + 1, 1 - slot, u + 1, next_diagonal)

    def q_block(qi, carry):
        rows = pl.ds(pl.multiple_of(qi * tq, tq), tq)
        m_ref[...] = jnp.full_like(m_ref, -jnp.inf)
        acc_ref[...] = jnp.zeros_like(acc_ref)

        @pl.when(qi == 0)
        def _():
            produce_all(qi, 0, 0, True)
            consume_all(0, 0, True)

        @pl.when(qi > 0)
        def _():
            produce_all(qi, 0, 0, False)

        def steady_pair(i, c):
            consume_and_produce(qi, 2 * i, 0, False)
            consume_and_produce(qi, 2 * i + 1, 1, False)
            return c

        n_pairs = jnp.maximum(qi - 1, 0) // 2
        lax.fori_loop(0, n_pairs, steady_pair, 0)
        kj = 2 * n_pairs

        @pl.when(qi - kj == 1)
        def _():
            consume_and_produce(qi, kj, 0, True)
            consume_all(qi, 1, True)

        @pl.when(qi - kj == 2)
        def _():
            consume_and_produce(qi, kj, 0, False)
            consume_and_produce(qi, kj + 1, 1, True)
            consume_all(qi, 0, True)

        for h in range(n_heads):
            acc = acc_ref[h]
            o = jnp.transpose(acc[:MLA_V_DIM] * (1.0 / acc[MLA_V_DIM:MLA_V_DIM + 1]))
            cols = slice(h * MLA_V_DIM, (h + 1) * MLA_V_DIM)
            o_ref[rows, cols] = (o * _silu(z_ref[rows, cols].astype(F32))).astype(o_ref.dtype)
        return carry

    lax.fori_loop(0, seq // tq, q_block, 0)


def _mla_attn(q_cat, k_cat, v_t, proj, batch, seq, tq, tk):
    kern = functools.partial(_mla_attn_kernel, tq=tq, tk=tk)
    nb = seq // tk
    nh = ATT_HEADS_PER_STEP
    n_units = nh * (tq // MXU_DIM)
    width = nh * MLA_V_DIM
    z_blk = B_Z_OFF // width
    return pl.pallas_call(
        kern,
        grid=(batch, MLA_HEADS // nh),
        in_specs=[
            pl.BlockSpec((None, nh, seq, MLA_QK_PAD), lambda b, g: (b, g, 0, 0)),
            pl.BlockSpec((None, nh, seq, MLA_QK_PAD), lambda b, g: (b, g, 0, 0)),
            pl.BlockSpec((None, nh, nb, MLA_VT_ROWS, tk), lambda b, g: (b, g, 0, 0, 0)),
            pl.BlockSpec((seq, width), lambda b, g: (b, z_blk + g)),
        ],
        out_specs=pl.BlockSpec((seq, width), lambda b, g: (b, g)),
        out_shape=jax.ShapeDtypeStruct((batch * seq, BRANCH_WIDTH), BF16),
        scratch_shapes=[
            pltpu.VMEM((nh, MLA_VT_ROWS, tq), F32),
            pltpu.VMEM((nh, 1, tq), F32),
            pltpu.VMEM((n_units, tk, MXU_DIM), F32),
            pltpu.VMEM((n_units, tk, MXU_DIM), F32),
            pltpu.VMEM((n_units, 1, MXU_DIM), F32),
            pltpu.VMEM((n_units, 1, MXU_DIM), F32),
        ],
        compiler_params=pltpu.CompilerParams(
            dimension_semantics=("parallel", "parallel"), vmem_limit_bytes=VMEM_LIMIT),
        name="mla_attn",
    )(q_cat, k_cat, v_t, proj)


def _pad_heads(w, heads, width, padded):
    lead = w.shape[:-1]
    w = w.reshape(lead + (heads, width))
    w = jnp.pad(w, [(0, 0)] * len(lead) + [(0, 0), (0, padded - width)])
    return w.reshape(lead + (heads * padded,))


def _rot_half(w, heads):
    lead = w.shape[:-1]
    half = MLA_ROPE_DIM // 2
    w = w.reshape(lead + (heads, 2, half))
    return w[..., ::-1, :].reshape(lead + (heads * MLA_ROPE_DIM,))


def _layer_a_weights(w_in, w_g2, b_g):
    hk = GLA_HEADS * GLA_DK
    o = np.cumsum([0, hk, hk, BRANCH_WIDTH, GLA_GATE_RANK, BRANCH_WIDTH, MEM_WIDTH, MEM_WIDTH])
    wq, wk, wv, wlr, wz, wmq, wmz = [w_in[:, o[i]:o[i + 1]] for i in range(7)]
    wq = wq.reshape(D_MODEL, GLA_HEADS, GLA_DK)
    wlr_rep = jnp.broadcast_to(wlr[:, None, :], (D_MODEL, GLA_HEADS, GLA_GATE_RANK))
    zeros = jnp.zeros((D_MODEL, GLA_HEADS, GLA_DK_PAD - GLA_DK - GLA_GATE_RANK), w_in.dtype)
    wq = jnp.concatenate([wq, wlr_rep, zeros], axis=-1).reshape(D_MODEL, GLA_HEADS * GLA_DK_PAD)
    wk = _pad_heads(wk, GLA_HEADS, GLA_DK, GLA_DK_PAD)
    w_all = jnp.concatenate([wv, wz, wq, wk, wmq, wmz], axis=1).astype(BF16)
    wg = _pad_heads(w_g2, GLA_HEADS, GLA_DK, GLA_DK_PAD)
    wg = wg.reshape(GLA_GATE_RANK, GLA_HEADS, GLA_DK_PAD).transpose(1, 0, 2)
    wg = jnp.pad(wg, [(0, 0), (GLA_DK, GLA_DK_PAD - GLA_DK - GLA_GATE_RANK), (0, 0)]).astype(BF16)
    bg = _pad_heads(b_g, GLA_HEADS, GLA_DK, GLA_DK_PAD).reshape(GLA_HEADS, 1, GLA_DK_PAD)
    return w_all, wg, bg


def _layer_b_weights(w_in, w_dkv):
    o = np.cumsum([0, MLA_Q_RANK, BRANCH_WIDTH, MEM_WIDTH, MEM_WIDTH])
    wcq, wz, wmq, wmz = [w_in[:, o[i]:o[i + 1]] for i in range(4)]
    wc, wkr = w_dkv[:, :MLA_KV_RANK], w_dkv[:, MLA_KV_RANK:]
    wkrot = _rot_half(wkr, 1)
    return jnp.concatenate([wz, wcq, wmq, wmz, wc, wkr, wkr, wkrot, wkrot], axis=1).astype(BF16)


def _uq_weights(w_uq):
    w = w_uq.reshape(MLA_Q_RANK, MLA_HEADS, MLA_QK_DIM)
    w_nope = w[:, :, :MLA_NOPE_DIM].reshape(MLA_Q_RANK, MLA_HEADS * MLA_NOPE_DIM)
    w_rope = w[:, :, MLA_NOPE_DIM:].reshape(MLA_Q_RANK, MLA_HEADS * MLA_ROPE_DIM)
    return jnp.concatenate([w_nope, w_rope, _rot_half(w_rope, MLA_HEADS)], axis=1).astype(BF16)


def _rope_tables():
    r = MLA_ROPE_DIM
    freqs = ROPE_THETA ** (-jnp.arange(0, r, 2, dtype=F32) / r)
    freq = jnp.tile(freqs, LANES // (r // 2)).reshape(1, LANES)
    sign = jnp.tile(jnp.concatenate([-jnp.ones(r // 2, F32), jnp.ones(r // 2, F32)]),
                    LANES // r).reshape(1, LANES)
    return freq, sign


def kernel(x, mem, positions, a_pre_norm, a_w_in, a_w_g2, a_b_g, a_gla_norm, a_mem_norm, a_w_mem_kv,
           a_w_out, a_post_norm, kv_in_norm, w_dkv, kv_norm, w_uk, w_uv, b_pre_norm, b_w_in, b_q_norm,
           b_w_uq, b_mem_norm, b_w_mem_kv, b_w_out, b_post_norm):
    assert a_w_in.shape[0] == 1 and b_w_in.shape[0] == 1, "one A layer followed by one B layer"
    batch, seq, _ = x.shape
    t = batch * seq
    x2 = x.reshape(t, D_MODEL)

    mem_gains = jnp.concatenate([a_mem_norm, b_mem_norm], axis=0)
    w_memkv = jnp.concatenate([a_w_mem_kv[0], b_w_mem_kv[0]], axis=1).astype(BF16)
    memkv = _norm_matmul(mem.reshape(batch * N_MEM, D_MODEL), mem_gains, w_memkv,
                         tm=batch * N_MEM, tn=2 * MEM_WIDTH, group_split=1)

    w_a, wg, bg = _layer_a_weights(a_w_in[0], a_w_g2[0], a_b_g[0])
    proj_a = _norm_matmul(x2, a_pre_norm, w_a, tm=PROJ_TM, tn=A_TN)
    gla_out = _gla(proj_a, wg, bg, a_gla_norm, batch, seq)
    mem_out = _mem_attn(proj_a, memkv, seq, A_MQ_OFF, A_MZ_OFF, kv_blk=0)
    x1 = _out_proj(gla_out, mem_out, a_w_out[0].astype(BF16), x2, a_post_norm)

    w_b = _layer_b_weights(b_w_in[0], w_dkv)
    gains_b = jnp.concatenate([b_pre_norm, kv_in_norm[None, :]], axis=0)
    proj_b = _norm_matmul(x1, gains_b, w_b, tm=PROJ_TM, tn=B_TN, group_split=B_GROUP_SPLIT)
    freq, sign = _rope_tables()
    tk = min(ATT_TK, seq)
    tq = min(ATT_TQ, seq)
    q_cat, k_cat, v_t = _mla_prep(
        proj_b, positions.reshape(t, 1), freq, sign, b_q_norm, kv_norm[None, :],
        _uq_weights(b_w_uq[0]), w_uk.astype(BF16), w_uv.T.astype(BF16), batch, seq, tk)
    mla_out = _mla_attn(q_cat, k_cat, v_t, proj_b, batch, seq, tq, tk)
    mem_out_b = _mem_attn(proj_b, memkv, seq, B_MQ_OFF, B_MZ_OFF, kv_blk=2)
    out = _out_proj(mla_out, mem_out_b, b_w_out[0].astype(BF16), x1, b_post_norm)
    return out.reshape(batch, seq, D_MODEL)
```

```python
import functools

import numpy as np
import jax
import jax.numpy as jnp
from jax import lax
from jax.experimental import pallas as pl
from jax.experimental.pallas import tpu as pltpu

F32 = jnp.float32
BF16 = jnp.bfloat16

D_MODEL = 2048
N_MEM = 256
MEM_HEADS = 4
MEM_HEAD_DIM = 128
MEM_WIDTH = MEM_HEADS * MEM_HEAD_DIM
BRANCH_WIDTH = D_MODEL - MEM_WIDTH
GLA_HEADS = 4
GLA_DV = BRANCH_WIDTH // GLA_HEADS
GLA_DK = GLA_DV // 2
GLA_GATE_RANK = 16
GLA_GATE_NORM = 16.0
MLA_HEADS = 12
MLA_V_DIM = 128
MLA_NOPE_DIM = 128
MLA_ROPE_DIM = 64
MLA_QK_DIM = MLA_NOPE_DIM + MLA_ROPE_DIM
MLA_Q_RANK = 512
MLA_KV_RANK = 512
ROPE_THETA = 10000.0
EPS = 1e-6

LANES = 128
MXU_DIM = 256
GLA_DK_PAD = MXU_DIM
MLA_QK_PAD = MXU_DIM
BF16_SUBLANES = 16
MLA_VT_ROWS = MLA_V_DIM + BF16_SUBLANES
ATT_HEADS_PER_STEP = 2
LOG2_E = 1.4426950408889634
LN_2 = 0.6931471805599453
VMEM_LIMIT = 56 * 1024 * 1024

PROJ_TM = 1024
PROJ_NORM_ROWS = 256
OUT_TM = 512
GLA_SPAN = 256
GLA_SUB = 32
GLA_HEADS_PER_STEP = 4
GLA_STAGE_SKEW = 1
MEM_TM = 512
PREP_TM = 512
ATT_TQ = 512
ATT_TK = 512

A_V_OFF = 0
A_Z_OFF = BRANCH_WIDTH
A_Q_OFF = 2 * BRANCH_WIDTH
A_K_OFF = A_Q_OFF + GLA_HEADS * GLA_DK_PAD
A_MQ_OFF = A_K_OFF + GLA_HEADS * GLA_DK_PAD
A_MZ_OFF = A_MQ_OFF + MEM_WIDTH
A_WIDTH = A_MZ_OFF + MEM_WIDTH
A_TN = 1024

B_Z_OFF = 0
B_CQ_OFF = BRANCH_WIDTH
B_MQ_OFF = B_CQ_OFF + MLA_Q_RANK
B_MZ_OFF = B_MQ_OFF + MEM_WIDTH
B_C_OFF = B_MZ_OFF + MEM_WIDTH
B_KR_OFF = B_C_OFF + MLA_KV_RANK
B_KROT_OFF = B_KR_OFF + LANES
B_WIDTH = B_KROT_OFF + LANES
B_TN = 768
B_GROUP_SPLIT = B_C_OFF // B_TN


def _silu(z):
    return z * (1.0 / (1.0 + jnp.exp(-z)))


def _scaled_log_sigmoid(g, scale):
    softplus2 = jnp.log2(1.0 + jnp.exp2(jnp.abs(g) * -LOG2_E))
    return jnp.minimum(g, 0.0) * scale - softplus2 * (LN_2 * scale)


def _dot(a, b):
    return jnp.dot(a, b, preferred_element_type=F32)


def _dot_nt(a, b):
    return lax.dot_general(a, b, (((1,), (1,)), ((), ())), preferred_element_type=F32)


def _dot_tn(a, b):
    return lax.dot_general(a, b, (((0,), (0,)), ((), ())), preferred_element_type=F32)


def _norm_matmul_kernel(x_ref, g_ref, w_ref, o_ref, h_ref, *, n_groups, group_split, norm_rows):
    j = pl.program_id(1)

    @pl.when(j == 0)
    def _():
        tm = x_ref.shape[0]
        for r in range(tm // norm_rows):
            rows = pl.ds(r * norm_rows, norm_rows)
            x = x_ref[rows, :]
            xn = x * lax.rsqrt(jnp.mean(x * x, axis=-1, keepdims=True) + EPS)
            for g in range(n_groups):
                h_ref[g, rows, :] = (xn * g_ref[g:g + 1, :]).astype(BF16)

    if n_groups == 1:
        h = h_ref[0]
    else:
        h = h_ref[(j >= group_split).astype(jnp.int32)]
    o_ref[...] = _dot(h, w_ref[...]).astype(o_ref.dtype)


def _norm_matmul(x, gains, w, *, tm, tn, group_split=0):
    m, d = x.shape
    n = w.shape[1]
    n_groups = gains.shape[0]
    tm = min(tm, m)
    norm_rows = min(PROJ_NORM_ROWS, tm)
    kern = functools.partial(_norm_matmul_kernel, n_groups=n_groups, group_split=group_split,
                             norm_rows=norm_rows)
    return pl.pallas_call(
        kern,
        grid=(m // tm, n // tn),
        in_specs=[
            pl.BlockSpec((tm, d), lambda i, j: (i, 0)),
            pl.BlockSpec((n_groups, d), lambda i, j: (0, 0)),
            pl.BlockSpec((d, tn), lambda i, j: (0, j)),
        ],
        out_specs=pl.BlockSpec((tm, tn), lambda i, j: (i, j)),
        out_shape=jax.ShapeDtypeStruct((m, n), BF16),
        scratch_shapes=[pltpu.VMEM((n_groups, tm, d), BF16)],
        compiler_params=pltpu.CompilerParams(
            dimension_semantics=("parallel", "arbitrary"), vmem_limit_bytes=VMEM_LIMIT),
        name="norm_matmul",
    )(x, gains, w)


def _out_proj_kernel(a_ref, m_ref, wa_ref, wm_ref, x_ref, g_ref, o_ref):
    y = _dot(a_ref[...], wa_ref[...]) + _dot(m_ref[...], wm_ref[...])
    ms = jnp.mean(y * y, axis=-1, keepdims=True)
    o_ref[...] = x_ref[...] + y * lax.rsqrt(ms + EPS) * g_ref[...]


def _out_proj(a, mo, w_out, x, gain):
    t = x.shape[0]
    tm = min(OUT_TM, t)
    mem_blk = BRANCH_WIDTH // MEM_WIDTH
    return pl.pallas_call(
        _out_proj_kernel,
        grid=(t // tm,),
        in_specs=[
            pl.BlockSpec((tm, BRANCH_WIDTH), lambda i: (i, 0)),
            pl.BlockSpec((tm, MEM_WIDTH), lambda i: (i, 0)),
            pl.BlockSpec((BRANCH_WIDTH, D_MODEL), lambda i: (0, 0)),
            pl.BlockSpec((MEM_WIDTH, D_MODEL), lambda i: (mem_blk, 0)),
            pl.BlockSpec((tm, D_MODEL), lambda i: (i, 0)),
            pl.BlockSpec((1, D_MODEL), lambda i: (0, 0)),
        ],
        out_specs=pl.BlockSpec((tm, D_MODEL), lambda i: (i, 0)),
        out_shape=jax.ShapeDtypeStruct((t, D_MODEL), F32),
        compiler_params=pltpu.CompilerParams(
            dimension_semantics=("parallel",), vmem_limit_bytes=VMEM_LIMIT),
        name="out_proj",
    )(a, mo, w_out, w_out, x, gain)


def _gla_head_stages(h, v_ref, z_ref, q_ref, k_ref, wg_ref, bg_ref, gn_ref, tril_ref, lvl_ref, o_ref,
                     s_ref, sub):
    span = q_ref.shape[0]
    n_sub = span // sub
    n_levels = n_sub.bit_length() - 1
    scale = GLA_DK ** -0.5
    kcols = slice(h * GLA_DK_PAD, (h + 1) * GLA_DK_PAD)
    vcols = slice(h * GLA_DV, (h + 1) * GLA_DV)

    qb = q_ref[:, kcols]
    g = _dot(qb, wg_ref[h]) + bg_ref[h]
    yield
    log_a = _scaled_log_sigmoid(g, 1.0 / GLA_GATE_NORM)
    a_hi = log_a.astype(BF16)
    rem = log_a - a_hi.astype(F32)
    a_mid = rem.astype(BF16)
    a_lo = (rem - a_mid.astype(F32)).astype(BF16)
    yield
    tril = tril_ref[...]
    cum = _dot(tril, a_hi) + _dot(tril, a_mid) + _dot(tril, a_lo)
    yield

    refs = jnp.concatenate([cum[i * sub:i * sub + 1] for i in range(n_sub)] + [cum[span - 1:span]], axis=0)
    own = refs[:n_sub]

    def ref_rows(index_of):
        return jnp.concatenate([refs[index_of(i):index_of(i) + 1] for i in range(n_sub)], axis=0)

    def per_row(f):
        return jnp.concatenate(
            [jnp.broadcast_to(f[i:i + 1], (sub, f.shape[1])) for i in range(n_sub)], axis=0)

    own_b = per_row(own)
    qd = qb.astype(F32) * (jnp.exp(cum - own_b) * scale)
    kd = k_ref[:, kcols].astype(F32) * jnp.exp(own_b - cum)
    yield
    q_ops, k_ops = [qd.astype(BF16)], [kd.astype(BF16)]
    for lvl in range(1, n_levels + 1):
        n = 1 << (lvl - 1)
        block_start = ref_rows(lambda i: (i // n) * n)
        next_start = ref_rows(lambda i: (i // n + 1) * n)
        q_ops.append(q_ops[0] if n == 1 else (qd * per_row(jnp.exp(own - block_start))).astype(BF16))
        k_ops.append((kd * per_row(jnp.exp(next_start - own))).astype(BF16))
    q_state = (qd * per_row(jnp.exp(own))).astype(BF16)
    k_state = (kd * per_row(jnp.exp(refs[n_sub:] - own))).astype(BF16)
    yield
    atts = [_dot_nt(qo, ko) for qo, ko in zip(q_ops, k_ops)]
    yield
    level = lvl_ref[...]
    att = jnp.where(level == n_levels, atts[n_levels], 0.0)
    for lvl in reversed(range(n_levels)):
        att = jnp.where(level == lvl, atts[lvl], att)
    yield
    state = s_ref[h]
    v = v_ref[:, vcols]
    o = _dot(jnp.concatenate([att.astype(BF16), q_state], axis=1),
             jnp.concatenate([v, state.astype(BF16)], axis=0))
    decay_col = jnp.transpose(jnp.exp(cum[span - 8:span, :]))[:, 7:8]
    s_ref[h] = decay_col * state + _dot_tn(k_state, v)
    yield
    on = o * lax.rsqrt(jnp.mean(o * o, axis=-1, keepdims=True) + EPS) * gn_ref[...]
    o_ref[:, vcols] = (on * _silu(z_ref[:, vcols].astype(F32))).astype(o_ref.dtype)
    yield


def _gla_kernel(v_ref, z_ref, q_ref, k_ref, wg_ref, bg_ref, gn_ref, tril_ref, lvl_ref, o_ref, s_ref, *,
                sub, skew):
    @pl.when(pl.program_id(2) == 0)
    def _():
        s_ref[...] = jnp.zeros_like(s_ref)

    heads = [_gla_head_stages(h, v_ref, z_ref, q_ref, k_ref, wg_ref, bg_ref, gn_ref, tril_ref, lvl_ref,
                              o_ref, s_ref, sub) for h in range(s_ref.shape[0])]
    live = list(range(len(heads)))
    step = 0
    while live:
        for i in list(live):
            if step >= i * skew:
                try:
                    next(heads[i])
                except StopIteration:
                    live.remove(i)
        step += 1


def _prefix_matrix(rows):
    r = np.arange(rows)
    return jnp.asarray(r[:, None] >= r[None, :], BF16)


def _level_matrix(rows, sub):
    n_levels = (rows // sub).bit_length() - 1
    t, s = np.arange(rows)[:, None], np.arange(rows)[None, :]
    level = np.full((rows, rows), n_levels + 1, np.int32)
    level[(t // sub == s // sub) & (s <= t)] = 0
    for lvl in range(1, n_levels + 1):
        size = sub << (lvl - 1)
        level[((t // size) % 2 == 1) & (s // size == t // size - 1)] = lvl
    return jnp.asarray(level)


def _gla(proj, wg, bg, gn, batch, seq):
    span = min(GLA_SPAN, seq)
    assert span % GLA_SUB == 0 and (span // GLA_SUB) & (span // GLA_SUB - 1) == 0
    ns = seq // span
    nh = GLA_HEADS_PER_STEP
    kern = functools.partial(_gla_kernel, sub=GLA_SUB, skew=GLA_STAGE_SKEW)
    vw, kw = nh * GLA_DV, nh * GLA_DK_PAD
    v_blk, z_blk = A_V_OFF // vw, A_Z_OFF // vw
    q_blk, k_blk = A_Q_OFF // kw, A_K_OFF // kw
    group = span
    return pl.pallas_call(
        kern,
        grid=(batch, GLA_HEADS // nh, ns),
        in_specs=[
            pl.BlockSpec((span, vw), lambda b, g, s: (b * ns + s, v_blk + g)),
            pl.BlockSpec((span, vw), lambda b, g, s: (b * ns + s, z_blk + g)),
            pl.BlockSpec((span, kw), lambda b, g, s: (b * ns + s, q_blk + g)),
            pl.BlockSpec((span, kw), lambda b, g, s: (b * ns + s, k_blk + g)),
            pl.BlockSpec((nh, GLA_DK_PAD, GLA_DK_PAD), lambda b, g, s: (g, 0, 0)),
            pl.BlockSpec((nh, 1, GLA_DK_PAD), lambda b, g, s: (g, 0, 0)),
            pl.BlockSpec((1, GLA_DV), lambda b, g, s: (0, 0)),
            pl.BlockSpec((group, group), lambda b, g, s: (0, 0)),
            pl.BlockSpec((group, group), lambda b, g, s: (0, 0)),
        ],
        out_specs=pl.BlockSpec((span, vw), lambda b, g, s: (b * ns + s, g)),
        out_shape=jax.ShapeDtypeStruct((batch * seq, BRANCH_WIDTH), BF16),
        scratch_shapes=[pltpu.VMEM((nh, GLA_DK_PAD, GLA_DV), F32)],
        compiler_params=pltpu.CompilerParams(
            dimension_semantics=("parallel", "parallel", "arbitrary"), vmem_limit_bytes=VMEM_LIMIT),
        name="gla",
    )(proj, proj, proj, proj, wg, bg, gn, _prefix_matrix(group), _level_matrix(group, GLA_SUB))


def _mem_attn_kernel(q_ref, z_ref, k_ref, v_ref, o_ref):
    scale = MEM_HEAD_DIM ** -0.5
    for h in range(MEM_HEADS):
        cols = slice(h * MEM_HEAD_DIM, (h + 1) * MEM_HEAD_DIM)
        s = _dot_nt(q_ref[:, cols], k_ref[:, cols]) * scale
        p = jnp.exp(s - jnp.max(s, axis=-1, keepdims=True))
        l = jnp.sum(p, axis=-1, keepdims=True)
        o = _dot(p.astype(BF16), v_ref[:, cols]) * (1.0 / l)
        o_ref[:, cols] = (o * _silu(z_ref[:, cols].astype(F32))).astype(o_ref.dtype)


def _mem_attn(proj, memkv, seq, q_off, z_off, kv_blk):
    t = proj.shape[0]
    tm = min(MEM_TM, seq)
    per_batch = seq // tm
    q_blk, z_blk = q_off // MEM_WIDTH, z_off // MEM_WIDTH
    return pl.pallas_call(
        _mem_attn_kernel,
        grid=(t // tm,),
        in_specs=[
            pl.BlockSpec((tm, MEM_WIDTH), lambda i: (i, q_blk)),
            pl.BlockSpec((tm, MEM_WIDTH), lambda i: (i, z_blk)),
            pl.BlockSpec((N_MEM, MEM_WIDTH), lambda i: (i // per_batch, kv_blk)),
            pl.BlockSpec((N_MEM, MEM_WIDTH), lambda i: (i // per_batch, kv_blk + 1)),
        ],
        out_specs=pl.BlockSpec((tm, MEM_WIDTH), lambda i: (i, 0)),
        out_shape=jax.ShapeDtypeStruct((t, MEM_WIDTH), BF16),
        compiler_params=pltpu.CompilerParams(
            dimension_semantics=("parallel",), vmem_limit_bytes=VMEM_LIMIT),
        name="mem_attn",
    )(proj, proj, memkv, memkv)


def _mla_prep_kernel(cq_ref, c_ref, kr_ref, krot_ref, pos_ref, freq_ref, sign_ref, qg_ref, kvg_ref,
                     wuq_ref, wuk_ref, wuvt_ref, q_ref, k_ref, vt_ref, *, tk):
    tm = cq_ref.shape[0]
    ang = pos_ref[...].astype(F32) * freq_ref[...]
    cos = jnp.cos(ang)
    sin = jnp.sin(ang) * sign_ref[...]
    lane = lax.broadcasted_iota(jnp.int32, (1, LANES), 1)
    slot_mask = [(lane < MLA_ROPE_DIM).astype(F32), (lane >= MLA_ROPE_DIM).astype(F32)]

    cq = cq_ref[...].astype(F32)
    cqn = cq * lax.rsqrt(jnp.mean(cq * cq, axis=-1, keepdims=True) + EPS)
    cqn = (cqn * (qg_ref[...] * (MLA_QK_DIM ** -0.5 * LOG2_E))).astype(BF16)
    n_nope = MLA_HEADS * MLA_NOPE_DIM
    n_rope = MLA_HEADS * MLA_ROPE_DIM
    q_nope = _dot(cqn, wuq_ref[:, :n_nope])
    q_r = _dot(cqn, wuq_ref[:, n_nope:n_nope + n_rope])
    q_rot = _dot(cqn, wuq_ref[:, n_nope + n_rope:])
    for h in range(MLA_HEADS):
        q_ref[h, :, :MLA_NOPE_DIM] = q_nope[:, h * MLA_NOPE_DIM:(h + 1) * MLA_NOPE_DIM].astype(BF16)
    for u in range(MLA_HEADS // 2):
        cols = slice(u * LANES, (u + 1) * LANES)
        roped = q_r[:, cols] * cos + q_rot[:, cols] * sin
        for half in range(2):
            q_ref[2 * u + half, :, MLA_NOPE_DIM:] = (roped * slot_mask[half]).astype(BF16)

    c = c_ref[...].astype(F32)
    cn = c * lax.rsqrt(jnp.mean(c * c, axis=-1, keepdims=True) + EPS)
    cn = (cn * kvg_ref[...]).astype(BF16)
    k_nope = _dot(cn, wuk_ref[...])
    k_rope = (kr_ref[...].astype(F32) * cos + krot_ref[...].astype(F32) * sin).astype(BF16)
    for h in range(MLA_HEADS):
        k_ref[h, :, :MLA_NOPE_DIM] = k_nope[:, h * MLA_NOPE_DIM:(h + 1) * MLA_NOPE_DIM].astype(BF16)
        k_ref[h, :, MLA_NOPE_DIM:] = k_rope
    v_t = _dot_nt(wuvt_ref[...], cn).astype(BF16)
    ones_rows = (lax.broadcasted_iota(jnp.int32, (MLA_VT_ROWS - MLA_V_DIM, tk), 0) == 0).astype(BF16)
    for h in range(MLA_HEADS):
        for u in range(tm // tk):
            vt_ref[h, u, :MLA_V_DIM, :] = v_t[h * MLA_V_DIM:(h + 1) * MLA_V_DIM, u * tk:(u + 1) * tk]
            vt_ref[h, u, MLA_V_DIM:, :] = ones_rows


def _mla_prep(proj, pos_col, freq, sign, q_gain, kv_gain, w_uq, w_uk, w_uv_t, batch, seq, tk):
    tm = min(PREP_TM, seq)
    per_batch = seq // tm
    kern = functools.partial(_mla_prep_kernel, tk=tk)
    const = lambda i: (0, 0)
    head_map = lambda i: (i // per_batch, 0, i % per_batch, 0)
    return pl.pallas_call(
        kern,
        grid=(batch * per_batch,),
        in_specs=[
            pl.BlockSpec((tm, MLA_Q_RANK), lambda i: (i, B_CQ_OFF // MLA_Q_RANK)),
            pl.BlockSpec((tm, MLA_KV_RANK), lambda i: (i, B_C_OFF // MLA_KV_RANK)),
            pl.BlockSpec((tm, LANES), lambda i: (i, B_KR_OFF // LANES)),
            pl.BlockSpec((tm, LANES), lambda i: (i, B_KROT_OFF // LANES)),
            pl.BlockSpec((tm, 1), lambda i: (i, 0)),
            pl.BlockSpec((1, LANES), const),
            pl.BlockSpec((1, LANES), const),
            pl.BlockSpec((1, MLA_Q_RANK), const),
            pl.BlockSpec((1, MLA_KV_RANK), const),
            pl.BlockSpec(w_uq.shape, const),
            pl.BlockSpec(w_uk.shape, const),
            pl.BlockSpec(w_uv_t.shape, const),
        ],
        out_specs=[
            pl.BlockSpec((None, MLA_HEADS, tm, MLA_QK_PAD), head_map),
            pl.BlockSpec((None, MLA_HEADS, tm, MLA_QK_PAD), head_map),
            pl.BlockSpec((None, MLA_HEADS, tm // tk, MLA_VT_ROWS, tk),
                         lambda i: (i // per_batch, 0, i % per_batch, 0, 0)),
        ],
        out_shape=[
            jax.ShapeDtypeStruct((batch, MLA_HEADS, seq, MLA_QK_PAD), BF16),
            jax.ShapeDtypeStruct((batch, MLA_HEADS, seq, MLA_QK_PAD), BF16),
            jax.ShapeDtypeStruct((batch, MLA_HEADS, seq // tk, MLA_VT_ROWS, tk), BF16),
        ],
        compiler_params=pltpu.CompilerParams(
            dimension_semantics=("parallel",), vmem_limit_bytes=VMEM_LIMIT),
        name="mla_prep",
    )(proj, proj, proj, proj, pos_col, freq, sign, q_gain, kv_gain, w_uq, w_uk, w_uv_t)


def _max_over_rows(x, ways=8):
    rows = x.shape[0]
    slab = rows // ways
    parts = [x[i * slab:(i + 1) * slab] for i in range(ways)] if slab >= 8 and rows % ways == 0 else [x]
    while len(parts) > 1:
        parts = [jnp.maximum(parts[i], parts[i + 1]) for i in range(0, len(parts), 2)]
    return jnp.max(parts[0], axis=0, keepdims=True)


def _mla_attn_kernel(q_ref, k_ref, vt_ref, z_ref, o_ref, acc_ref, m_ref, s0_ref, s1_ref, smax0_ref,
                     smax1_ref, *, tq, tk):
    n_heads, seq = q_ref.shape[0], q_ref.shape[1]
    tn = MXU_DIM
    units = [(h, n) for h in range(n_heads) for n in range(tq // tn)]
    s_slots = (s0_ref, s1_ref)
    smax_slots = (smax0_ref, smax1_ref)

    def n_keys(n, diagonal):
        return min(tk, (n + 1) * tn) if diagonal else tk

    def produce(qi, kj, slot, u, diagonal):
        h, n = units[u]
        nk = n_keys(n, diagonal)
        krows = pl.ds(pl.multiple_of(kj * tk, tk), nk)
        qrows = pl.ds(pl.multiple_of(qi * tq + n * tn, tn), tn)
        s_t = _dot_nt(k_ref[h, krows, :], q_ref[h, qrows, :])
        if diagonal:
            kpos = lax.broadcasted_iota(jnp.int32, (nk, tn), 0)
            qpos = lax.broadcasted_iota(jnp.int32, (nk, tn), 1) + n * tn
            s_t = jnp.where(kpos <= qpos, s_t, -jnp.inf)
        s_slots[slot][u, :nk, :] = s_t
        smax_slots[slot][u] = _max_over_rows(s_t)

    def consume(kj, slot, u, diagonal):
        h, n = units[u]
        nk = n_keys(n, diagonal)
        cols = slice(n * tn, (n + 1) * tn)
        m_prev = m_ref[h, :, cols]
        m_new = jnp.maximum(m_prev, smax_slots[slot][u])
        alpha = jnp.exp2(m_prev - m_new)
        p = jnp.exp2(s_slots[slot][u, :nk, :] - m_new).astype(BF16)
        m_ref[h, :, cols] = m_new
        acc_ref[h, :, cols] = alpha * acc_ref[h, :, cols] + _dot(vt_ref[h, kj, :, :nk], p)

    def produce_all(qi, kj, slot, diagonal):
        for u in range(len(units)):
            produce(qi, kj, slot, u, diagonal)

    def consume_all(kj, slot, diagonal):
        for u in range(len(units)):
            consume(kj, slot, u, diagonal)

    def consume_and_produce(qi, kj, slot, next_diagonal):
        produce(qi, kj + 1, 1 - slot, 0, next_diagonal)
        for u in range(len(units)):
            consume(kj, slot, u, False)
            if u + 1 < len(units):
                produce(qi, kj + 1, 1 - slot, u + 1, next_diagonal)

    def q_block(qi, carry):
        rows = pl.ds(pl.multiple_of(qi * tq, tq), tq)
        m_ref[...] = jnp.full_like(m_ref, -jnp.inf)
        acc_ref[...] = jnp.zeros_like(acc_ref)

        @pl.when(qi == 0)
        def _():
            produce_all(qi, 0, 0, True)
            consume_all(0, 0, True)

        @pl.when(qi > 0)
        def _():
            produce_all(qi, 0, 0, False)

        def steady_pair(i, c):
            consume_and_produce(qi, 2 * i, 0, False)
            consume_and_produce(qi, 2 * i + 1, 1, False)
            return c

        n_pairs = jnp.maximum(qi - 1, 0) // 2
        lax.fori_loop(0, n_pairs, steady_pair, 0)
        kj = 2 * n_pairs

        @pl.when(qi - kj == 1)
        def _():
            consume_and_produce(qi, kj, 0, True)
            consume_all(qi, 1, True)

        @pl.when(qi - kj == 2)
        def _():
            consume_and_produce(qi, kj, 0, False)
            consume_and_produce(qi, kj + 1, 1, True)
            consume_all(qi, 0, True)

        for h in range(n_heads):
            acc = acc_ref[h]
            o = jnp.transpose(acc[:MLA_V_DIM] * (1.0 / acc[MLA_V_DIM:MLA_V_DIM + 1]))
            cols = slice(h * MLA_V_DIM, (h + 1) * MLA_V_DIM)
            o_ref[rows, cols] = (o * _silu(z_ref[rows, cols].astype(F32))).astype(o_ref.dtype)
        return carry

    lax.fori_loop(0, seq // tq, q_block, 0)


def _mla_attn(q_cat, k_cat, v_t, proj, batch, seq, tq, tk):
    kern = functools.partial(_mla_attn_kernel, tq=tq, tk=tk)
    nb = seq // tk
    nh = ATT_HEADS_PER_STEP
    n_units = nh * (tq // MXU_DIM)
    width = nh * MLA_V_DIM
    z_blk = B_Z_OFF // width
    return pl.pallas_call(
        kern,
        grid=(batch, MLA_HEADS // nh),
        in_specs=[
            pl.BlockSpec((None, nh, seq, MLA_QK_PAD), lambda b, g: (b, g, 0, 0)),
            pl.BlockSpec((None, nh, seq, MLA_QK_PAD), lambda b, g: (b, g, 0, 0)),
            pl.BlockSpec((None, nh, nb, MLA_VT_ROWS, tk), lambda b, g: (b, g, 0, 0, 0)),
            pl.BlockSpec((seq, width), lambda b, g: (b, z_blk + g)),
        ],
        out_specs=pl.BlockSpec((seq, width), lambda b, g: (b, g)),
        out_shape=jax.ShapeDtypeStruct((batch * seq, BRANCH_WIDTH), BF16),
        scratch_shapes=[
            pltpu.VMEM((nh, MLA_VT_ROWS, tq), F32),
            pltpu.VMEM((nh, 1, tq), F32),
            pltpu.VMEM((n_units, tk, MXU_DIM), F32),
            pltpu.VMEM((n_units, tk, MXU_DIM), F32),
            pltpu.VMEM((n_units, 1, MXU_DIM), F32),
            pltpu.VMEM((n_units, 1, MXU_DIM), F32),
        ],
        compiler_params=pltpu.CompilerParams(
            dimension_semantics=("parallel", "parallel"), vmem_limit_bytes=VMEM_LIMIT),
        name="mla_attn",
    )(q_cat, k_cat, v_t, proj)


def _pad_heads(w, heads, width, padded):
    lead = w.shape[:-1]
    w = w.reshape(lead + (heads, width))
    w = jnp.pad(w, [(0, 0)] * len(lead) + [(0, 0), (0, padded - width)])
    return w.reshape(lead + (heads * padded,))


def _rot_half(w, heads):
    lead = w.shape[:-1]
    half = MLA_ROPE_DIM // 2
    w = w.reshape(lead + (heads, 2, half))
    return w[..., ::-1, :].reshape(lead + (heads * MLA_ROPE_DIM,))


def _layer_a_weights(w_in, w_g2, b_g):
    hk = GLA_HEADS * GLA_DK
    o = np.cumsum([0, hk, hk, BRANCH_WIDTH, GLA_GATE_RANK, BRANCH_WIDTH, MEM_WIDTH, MEM_WIDTH])
    wq, wk, wv, wlr, wz, wmq, wmz = [w_in[:, o[i]:o[i + 1]] for i in range(7)]
    wq = wq.reshape(D_MODEL, GLA_HEADS, GLA_DK)
    wlr_rep = jnp.broadcast_to(wlr[:, None, :], (D_MODEL, GLA_HEADS, GLA_GATE_RANK))
    zeros = jnp.zeros((D_MODEL, GLA_HEADS, GLA_DK_PAD - GLA_DK - GLA_GATE_RANK), w_in.dtype)
    wq = jnp.concatenate([wq, wlr_rep, zeros], axis=-1).reshape(D_MODEL, GLA_HEADS * GLA_DK_PAD)
    wk = _pad_heads(wk, GLA_HEADS, GLA_DK, GLA_DK_PAD)
    w_all = jnp.concatenate([wv, wz, wq, wk, wmq, wmz], axis=1).astype(BF16)
    wg = _pad_heads(w_g2, GLA_HEADS, GLA_DK, GLA_DK_PAD)
    wg = wg.reshape(GLA_GATE_RANK, GLA_HEADS, GLA_DK_PAD).transpose(1, 0, 2)
    wg = jnp.pad(wg, [(0, 0), (GLA_DK, GLA_DK_PAD - GLA_DK - GLA_GATE_RANK), (0, 0)]).astype(BF16)
    bg = _pad_heads(b_g, GLA_HEADS, GLA_DK, GLA_DK_PAD).reshape(GLA_HEADS, 1, GLA_DK_PAD)
    return w_all, wg, bg


def _layer_b_weights(w_in, w_dkv):
    o = np.cumsum([0, MLA_Q_RANK, BRANCH_WIDTH, MEM_WIDTH, MEM_WIDTH])
    wcq, wz, wmq, wmz = [w_in[:, o[i]:o[i + 1]] for i in range(4)]
    wc, wkr = w_dkv[:, :MLA_KV_RANK], w_dkv[:, MLA_KV_RANK:]
    wkrot = _rot_half(wkr, 1)
    return jnp.concatenate([wz, wcq, wmq, wmz, wc, wkr, wkr, wkrot, wkrot], axis=1).astype(BF16)


def _uq_weights(w_uq):
    w = w_uq.reshape(MLA_Q_RANK, MLA_HEADS, MLA_QK_DIM)
    w_nope = w[:, :, :MLA_NOPE_DIM].reshape(MLA_Q_RANK, MLA_HEADS * MLA_NOPE_DIM)
    w_rope = w[:, :, MLA_NOPE_DIM:].reshape(MLA_Q_RANK, MLA_HEADS * MLA_ROPE_DIM)
    return jnp.concatenate([w_nope, w_rope, _rot_half(w_rope, MLA_HEADS)], axis=1).astype(BF16)


def _rope_tables():
    r = MLA_ROPE_DIM
    freqs = ROPE_THETA ** (-jnp.arange(0, r, 2, dtype=F32) / r)
    freq = jnp.tile(freqs, LANES // (r // 2)).reshape(1, LANES)
    sign = jnp.tile(jnp.concatenate([-jnp.ones(r // 2, F32), jnp.ones(r // 2, F32)]),
                    LANES // r).reshape(1, LANES)
    return freq, sign


def kernel(x, mem, positions, a_pre_norm, a_w_in, a_w_g2, a_b_g, a_gla_norm, a_mem_norm, a_w_mem_kv,
           a_w_out, a_post_norm, kv_in_norm, w_dkv, kv_norm, w_uk, w_uv, b_pre_norm, b_w_in, b_q_norm,
           b_w_uq, b_mem_norm, b_w_mem_kv, b_w_out, b_post_norm):
    assert a_w_in.shape[0] == 1 and b_w_in.shape[0] == 1, "one A layer followed by one B layer"
    batch, seq, _ = x.shape
    t = batch * seq
    x2 = x.reshape(t, D_MODEL)

    mem_gains = jnp.concatenate([a_mem_norm, b_mem_norm], axis=0)
    w_memkv = jnp.concatenate([a_w_mem_kv[0], b_w_mem_kv[0]], axis=1).astype(BF16)
    memkv = _norm_matmul(mem.reshape(batch * N_MEM, D_MODEL), mem_gains, w_memkv,
                         tm=batch * N_MEM, tn=2 * MEM_WIDTH, group_split=1)

    w_a, wg, bg = _layer_a_weights(a_w_in[0], a_w_g2[0], a_b_g[0])
    proj_a = _norm_matmul(x2, a_pre_norm, w_a, tm=PROJ_TM, tn=A_TN)
    gla_out = _gla(proj_a, wg, bg, a_gla_norm, batch, seq)
    mem_out = _mem_attn(proj_a, memkv, seq, A_MQ_OFF, A_MZ_OFF, kv_blk=0)
    x1 = _out_proj(gla_out, mem_out, a_w_out[0].astype(BF16), x2, a_post_norm)

    w_b = _layer_b_weights(b_w_in[0], w_dkv)
    gains_b = jnp.concatenate([b_pre_norm, kv_in_norm[None, :]], axis=0)
    proj_b = _norm_matmul(x1, gains_b, w_b, tm=PROJ_TM, tn=B_TN, group_split=B_GROUP_SPLIT)
    freq, sign = _rope_tables()
    tk = min(ATT_TK, seq)
    tq = min(ATT_TQ, seq)
    q_cat, k_cat, v_t = _mla_prep(
        proj_b, positions.reshape(t, 1), freq, sign, b_q_norm, kv_norm[None, :],
        _uq_weights(b_w_uq[0]), w_uk.astype(BF16), w_uv.T.astype(BF16), batch, seq, tk)
    mla_out = _mla_attn(q_cat, k_cat, v_t, proj_b, batch, seq, tq, tk)
    mem_out_b = _mem_attn(proj_b, memkv, seq, B_MQ_OFF, B_MZ_OFF, kv_blk=2)
    out = _out_proj(mla_out, mem_out_b, b_w_out[0].astype(BF16), x1, b_post_norm)
    return out.reshape(batch, seq, D_MODEL)
```

```python
import functools

import numpy as np
import jax
import jax.numpy as jnp
from jax import lax
from jax.experimental import pallas as pl
from jax.experimental.pallas import tpu as pltpu

F32 = jnp.float32
BF16 = jnp.bfloat16

D_MODEL = 2048
N_MEM = 256
MEM_HEADS = 4
MEM_HEAD_DIM = 128
MEM_WIDTH = MEM_HEADS * MEM_HEAD_DIM
BRANCH_WIDTH = D_MODEL - MEM_WIDTH
GLA_HEADS = 4
GLA_DV = BRANCH_WIDTH // GLA_HEADS
GLA_DK = GLA_DV // 2
GLA_GATE_RANK = 16
GLA_GATE_NORM = 16.0
MLA_HEADS = 12
MLA_V_DIM = 128
MLA_NOPE_DIM = 128
MLA_ROPE_DIM = 64
MLA_QK_DIM = MLA_NOPE_DIM + MLA_ROPE_DIM
MLA_Q_RANK = 512
MLA_KV_RANK = 512
ROPE_THETA = 10000.0
EPS = 1e-6

LANES = 128
MXU_DIM = 256
GLA_DK_PAD = MXU_DIM
MLA_QK_PAD = MXU_DIM
BF16_SUBLANES = 16
MLA_VT_ROWS = MLA_V_DIM + BF16_SUBLANES
ATT_HEADS_PER_STEP = 2
LOG2_E = 1.4426950408889634
LN_2 = 0.6931471805599453
VMEM_LIMIT = 56 * 1024 * 1024

PROJ_TM = 1024
PROJ_NORM_ROWS = 256
OUT_TM = 512
GLA_SPAN = 256
GLA_SUB = 32
GLA_HEADS_PER_STEP = 4
GLA_STAGE_SKEW = 1
MEM_TM = 512
PREP_TM = 512
ATT_TQ = 512
ATT_TK = 512

A_V_OFF = 0
A_Z_OFF = BRANCH_WIDTH
A_Q_OFF = 2 * BRANCH_WIDTH
A_K_OFF = A_Q_OFF + GLA_HEADS * GLA_DK_PAD
A_MQ_OFF = A_K_OFF + GLA_HEADS * GLA_DK_PAD
A_MZ_OFF = A_MQ_OFF + MEM_WIDTH
A_WIDTH = A_MZ_OFF + MEM_WIDTH
A_TN = 1024

B_Z_OFF = 0
B_CQ_OFF = BRANCH_WIDTH
B_MQ_OFF = B_CQ_OFF + MLA_Q_RANK
B_MZ_OFF = B_MQ_OFF + MEM_WIDTH
B_C_OFF = B_MZ_OFF + MEM_WIDTH
B_KR_OFF = B_C_OFF + MLA_KV_RANK
B_KROT_OFF = B_KR_OFF + LANES
B_WIDTH = B_KROT_OFF + LANES
B_TN = 768


def _silu(z):
    return z * (1.0 / (1.0 + jnp.exp(-z)))


def _scaled_log_sigmoid(g, scale):
    softplus2 = jnp.log2(1.0 + jnp.exp2(jnp.abs(g) * -LOG2_E))
    return jnp.minimum(g, 0.0) * scale - softplus2 * (LN_2 * scale)


def _dot(a, b):
    return jnp.dot(a, b, preferred_element_type=F32)


def _dot_nt(a, b):
    return lax.dot_general(a, b, (((1,), (1,)), ((), ())), preferred_element_type=F32)


def _dot_tn(a, b):
    return lax.dot_general(a, b, (((0,), (0,)), ((), ())), preferred_element_type=F32)


def _norm_matmul_kernel(x_ref, w_ref, o_ref, h_ref, r_ref, *, norm_rows):
    @pl.when(pl.program_id(1) == 0)
    def _():
        tm = x_ref.shape[0]
        for r in range(tm // norm_rows):
            rows = pl.ds(r * norm_rows, norm_rows)
            x = x_ref[rows, :]
            r_ref[rows, :] = lax.rsqrt(jnp.mean(x * x, axis=-1, keepdims=True) + EPS)
            h_ref[rows, :] = x.astype(BF16)

    o_ref[...] = (_dot(h_ref[...], w_ref[...]) * r_ref[...]).astype(o_ref.dtype)


def _norm_matmul(x, w, *, tm, tn):
    m, d = x.shape
    n = w.shape[1]
    tm = min(tm, m)
    kern = functools.partial(_norm_matmul_kernel, norm_rows=min(PROJ_NORM_ROWS, tm))
    return pl.pallas_call(
        kern,
        grid=(m // tm, n // tn),
        in_specs=[
            pl.BlockSpec((tm, d), lambda i, j: (i, 0)),
            pl.BlockSpec((d, tn), lambda i, j: (0, j)),
        ],
        out_specs=pl.BlockSpec((tm, tn), lambda i, j: (i, j)),
        out_shape=jax.ShapeDtypeStruct((m, n), BF16),
        scratch_shapes=[pltpu.VMEM((tm, d), BF16), pltpu.VMEM((tm, 1), F32)],
        compiler_params=pltpu.CompilerParams(
            dimension_semantics=("parallel", "arbitrary"), vmem_limit_bytes=VMEM_LIMIT),
        name="norm_matmul",
    )(x, w)


def _out_proj_kernel(a_ref, m_ref, wa_ref, wm_ref, x_ref, g_ref, o_ref):
    y = _dot(a_ref[...], wa_ref[...]) + _dot(m_ref[...], wm_ref[...])
    ms = jnp.mean(y * y, axis=-1, keepdims=True)
    o_ref[...] = x_ref[...] + y * lax.rsqrt(ms + EPS) * g_ref[...]


def _out_proj(a, mo, w_out, x, gain):
    t = x.shape[0]
    tm = min(OUT_TM, t)
    mem_blk = BRANCH_WIDTH // MEM_WIDTH
    return pl.pallas_call(
        _out_proj_kernel,
        grid=(t // tm,),
        in_specs=[
            pl.BlockSpec((tm, BRANCH_WIDTH), lambda i: (i, 0)),
            pl.BlockSpec((tm, MEM_WIDTH), lambda i: (i, 0)),
            pl.BlockSpec((BRANCH_WIDTH, D_MODEL), lambda i: (0, 0)),
            pl.BlockSpec((MEM_WIDTH, D_MODEL), lambda i: (mem_blk, 0)),
            pl.BlockSpec((tm, D_MODEL), lambda i: (i, 0)),
            pl.BlockSpec((1, D_MODEL), lambda i: (0, 0)),
        ],
        out_specs=pl.BlockSpec((tm, D_MODEL), lambda i: (i, 0)),
        out_shape=jax.ShapeDtypeStruct((t, D_MODEL), F32),
        compiler_params=pltpu.CompilerParams(
            dimension_semantics=("parallel",), vmem_limit_bytes=VMEM_LIMIT),
        name="out_proj",
    )(a, mo, w_out, w_out, x, gain)


def _gla_head_stages(h, v_ref, z_ref, q_ref, k_ref, wg_ref, bg_ref, gn_ref, tril_ref, lvl_ref, o_ref,
                     s_ref, sub):
    span = q_ref.shape[0]
    n_sub = span // sub
    n_levels = n_sub.bit_length() - 1
    scale = GLA_DK ** -0.5
    kcols = slice(h * GLA_DK_PAD, (h + 1) * GLA_DK_PAD)
    vcols = slice(h * GLA_DV, (h + 1) * GLA_DV)

    qb = q_ref[:, kcols]
    g = _dot(qb, wg_ref[h]) + bg_ref[h]
    yield
    log_a = _scaled_log_sigmoid(g, 1.0 / GLA_GATE_NORM)
    a_hi = log_a.astype(BF16)
    rem = log_a - a_hi.astype(F32)
    a_mid = rem.astype(BF16)
    a_lo = (rem - a_mid.astype(F32)).astype(BF16)
    yield
    tril = tril_ref[...]
    cum = _dot(tril, a_hi) + _dot(tril, a_mid) + _dot(tril, a_lo)
    yield

    refs = jnp.concatenate([cum[i * sub:i * sub + 1] for i in range(n_sub)] + [cum[span - 1:span]], axis=0)
    own = refs[:n_sub]

    def ref_rows(index_of):
        return jnp.concatenate([refs[index_of(i):index_of(i) + 1] for i in range(n_sub)], axis=0)

    def per_row(f):
        return jnp.concatenate(
            [jnp.broadcast_to(f[i:i + 1], (sub, f.shape[1])) for i in range(n_sub)], axis=0)

    own_b = per_row(own)
    qd = qb.astype(F32) * (jnp.exp(cum - own_b) * scale)
    kd = k_ref[:, kcols].astype(F32) * jnp.exp(own_b - cum)
    yield
    q_ops, k_ops = [qd.astype(BF16)], [kd.astype(BF16)]
    for lvl in range(1, n_levels + 1):
        n = 1 << (lvl - 1)
        block_start = ref_rows(lambda i: (i // n) * n)
        next_start = ref_rows(lambda i: (i // n + 1) * n)
        q_ops.append(q_ops[0] if n == 1 else (qd * per_row(jnp.exp(own - block_start))).astype(BF16))
        k_ops.append((kd * per_row(jnp.exp(next_start - own))).astype(BF16))
    q_state = (qd * per_row(jnp.exp(own))).astype(BF16)
    k_state = (kd * per_row(jnp.exp(refs[n_sub:] - own))).astype(BF16)
    yield
    atts = [_dot_nt(qo, ko) for qo, ko in zip(q_ops, k_ops)]
    yield
    level = lvl_ref[...]
    att = jnp.where(level == n_levels, atts[n_levels], 0.0)
    for lvl in reversed(range(n_levels)):
        att = jnp.where(level == lvl, atts[lvl], att)
    yield
    state = s_ref[h]
    v = v_ref[:, vcols]
    o = _dot(jnp.concatenate([att.astype(BF16), q_state], axis=1),
             jnp.concatenate([v, state.astype(BF16)], axis=0))
    decay_col = jnp.transpose(jnp.exp(cum[span - 8:span, :]))[:, 7:8]
    s_ref[h] = decay_col * state + _dot_tn(k_state, v)
    yield
    on = o * lax.rsqrt(jnp.mean(o * o, axis=-1, keepdims=True) + EPS) * gn_ref[...]
    o_ref[:, vcols] = (on * _silu(z_ref[:, vcols].astype(F32))).astype(o_ref.dtype)
    yield


def _gla_kernel(v_ref, z_ref, q_ref, k_ref, wg_ref, bg_ref, gn_ref, tril_ref, lvl_ref, o_ref, s_ref, *,
                sub, skew):
    @pl.when(pl.program_id(2) == 0)
    def _():
        s_ref[...] = jnp.zeros_like(s_ref)

    heads = [_gla_head_stages(h, v_ref, z_ref, q_ref, k_ref, wg_ref, bg_ref, gn_ref, tril_ref, lvl_ref,
                              o_ref, s_ref, sub) for h in range(s_ref.shape[0])]
    live = list(range(len(heads)))
    step = 0
    while live:
        for i in list(live):
            if step >= i * skew:
                try:
                    next(heads[i])
                except StopIteration:
                    live.remove(i)
        step += 1


def _prefix_matrix(rows):
    r = np.arange(rows)
    return jnp.asarray(r[:, None] >= r[None, :], BF16)


def _level_matrix(rows, sub):
    n_levels = (rows // sub).bit_length() - 1
    t, s = np.arange(rows)[:, None], np.arange(rows)[None, :]
    level = np.full((rows, rows), n_levels + 1, np.int32)
    level[(t // sub == s // sub) & (s <= t)] = 0
    for lvl in range(1, n_levels + 1):
        size = sub << (lvl - 1)
        level[((t // size) % 2 == 1) & (s // size == t // size - 1)] = lvl
    return jnp.asarray(level)


def _gla(proj, wg, bg, gn, batch, seq):
    span = min(GLA_SPAN, seq)
    assert span % GLA_SUB == 0 and (span // GLA_SUB) & (span // GLA_SUB - 1) == 0
    ns = seq // span
    nh = GLA_HEADS_PER_STEP
    kern = functools.partial(_gla_kernel, sub=GLA_SUB, skew=GLA_STAGE_SKEW)
    vw, kw = nh * GLA_DV, nh * GLA_DK_PAD
    v_blk, z_blk = A_V_OFF // vw, A_Z_OFF // vw
    q_blk, k_blk = A_Q_OFF // kw, A_K_OFF // kw
    group = span
    return pl.pallas_call(
        kern,
        grid=(batch, GLA_HEADS // nh, ns),
        in_specs=[
            pl.BlockSpec((span, vw), lambda b, g, s: (b * ns + s, v_blk + g)),
            pl.BlockSpec((span, vw), lambda b, g, s: (b * ns + s, z_blk + g)),
            pl.BlockSpec((span, kw), lambda b, g, s: (b * ns + s, q_blk + g)),
            pl.BlockSpec((span, kw), lambda b, g, s: (b * ns + s, k_blk + g)),
            pl.BlockSpec((nh, GLA_DK_PAD, GLA_DK_PAD), lambda b, g, s: (g, 0, 0)),
            pl.BlockSpec((nh, 1, GLA_DK_PAD), lambda b, g, s: (g, 0, 0)),
            pl.BlockSpec((1, GLA_DV), lambda b, g, s: (0, 0)),
            pl.BlockSpec((group, group), lambda b, g, s: (0, 0)),
            pl.BlockSpec((group, group), lambda b, g, s: (0, 0)),
        ],
        out_specs=pl.BlockSpec((span, vw), lambda b, g, s: (b * ns + s, g)),
        out_shape=jax.ShapeDtypeStruct((batch * seq, BRANCH_WIDTH), BF16),
        scratch_shapes=[pltpu.VMEM((nh, GLA_DK_PAD, GLA_DV), F32)],
        compiler_params=pltpu.CompilerParams(
            dimension_semantics=("parallel", "parallel", "arbitrary"), vmem_limit_bytes=VMEM_LIMIT),
        name="gla",
    )(proj, proj, proj, proj, wg, bg, gn, _prefix_matrix(group), _level_matrix(group, GLA_SUB))


def _mem_attn_kernel(q_ref, z_ref, k_ref, v_ref, o_ref):
    scale = MEM_HEAD_DIM ** -0.5
    for h in range(MEM_HEADS):
        cols = slice(h * MEM_HEAD_DIM, (h + 1) * MEM_HEAD_DIM)
        s = _dot_nt(q_ref[:, cols], k_ref[:, cols]) * scale
        p = jnp.exp(s - jnp.max(s, axis=-1, keepdims=True))
        l = jnp.sum(p, axis=-1, keepdims=True)
        o = _dot(p.astype(BF16), v_ref[:, cols]) * (1.0 / l)
        o_ref[:, cols] = (o * _silu(z_ref[:, cols].astype(F32))).astype(o_ref.dtype)


def _mem_attn(proj, memkv, seq, q_off, z_off, kv_blk):
    t = proj.shape[0]
    tm = min(MEM_TM, seq)
    per_batch = seq // tm
    q_blk, z_blk = q_off // MEM_WIDTH, z_off // MEM_WIDTH
    return pl.pallas_call(
        _mem_attn_kernel,
        grid=(t // tm,),
        in_specs=[
            pl.BlockSpec((tm, MEM_WIDTH), lambda i: (i, q_blk)),
            pl.BlockSpec((tm, MEM_WIDTH), lambda i: (i, z_blk)),
            pl.BlockSpec((N_MEM, MEM_WIDTH), lambda i: (i // per_batch, kv_blk)),
            pl.BlockSpec((N_MEM, MEM_WIDTH), lambda i: (i // per_batch, kv_blk + 1)),
        ],
        out_specs=pl.BlockSpec((tm, MEM_WIDTH), lambda i: (i, 0)),
        out_shape=jax.ShapeDtypeStruct((t, MEM_WIDTH), BF16),
        compiler_params=pltpu.CompilerParams(
            dimension_semantics=("parallel",), vmem_limit_bytes=VMEM_LIMIT),
        name="mem_attn",
    )(proj, proj, memkv, memkv)


def _mla_prep_kernel(cq_ref, c_ref, kr_ref, krot_ref, pos_ref, freq_ref, sign_ref, qg_ref, kvg_ref,
                     wuq_ref, wuk_ref, wuvt_ref, q_ref, k_ref, vt_ref, *, tk):
    tm = cq_ref.shape[0]
    ang = pos_ref[...].astype(F32) * freq_ref[...]
    cos = jnp.cos(ang)
    sin = jnp.sin(ang) * sign_ref[...]
    lane = lax.broadcasted_iota(jnp.int32, (1, LANES), 1)
    slot_mask = [(lane < MLA_ROPE_DIM).astype(F32), (lane >= MLA_ROPE_DIM).astype(F32)]

    cq = cq_ref[...].astype(F32)
    cqn = cq * lax.rsqrt(jnp.mean(cq * cq, axis=-1, keepdims=True) + EPS)
    cqn = (cqn * (qg_ref[...] * (MLA_QK_DIM ** -0.5 * LOG2_E))).astype(BF16)
    n_nope = MLA_HEADS * MLA_NOPE_DIM
    n_rope = MLA_HEADS * MLA_ROPE_DIM
    q_nope = _dot(cqn, wuq_ref[:, :n_nope])
    q_r = _dot(cqn, wuq_ref[:, n_nope:n_nope + n_rope])
    q_rot = _dot(cqn, wuq_ref[:, n_nope + n_rope:])
    for h in range(MLA_HEADS):
        q_ref[h, :, :MLA_NOPE_DIM] = q_nope[:, h * MLA_NOPE_DIM:(h + 1) * MLA_NOPE_DIM].astype(BF16)
    for u in range(MLA_HEADS // 2):
        cols = slice(u * LANES, (u + 1) * LANES)
        roped = q_r[:, cols] * cos + q_rot[:, cols] * sin
        for half in range(2):
            q_ref[2 * u + half, :, MLA_NOPE_DIM:] = (roped * slot_mask[half]).astype(BF16)

    c = c_ref[...].astype(F32)
    cn = c * lax.rsqrt(jnp.mean(c * c, axis=-1, keepdims=True) + EPS)
    cn = (cn * kvg_ref[...]).astype(BF16)
    k_nope = _dot(cn, wuk_ref[...])
    k_rope = (kr_ref[...].astype(F32) * cos + krot_ref[...].astype(F32) * sin).astype(BF16)
    for h in range(MLA_HEADS):
        k_ref[h, :, :MLA_NOPE_DIM] = k_nope[:, h * MLA_NOPE_DIM:(h + 1) * MLA_NOPE_DIM].astype(BF16)
        k_ref[h, :, MLA_NOPE_DIM:] = k_rope
    v_t = _dot_nt(wuvt_ref[...], cn).astype(BF16)
    ones_rows = (lax.broadcasted_iota(jnp.int32, (MLA_VT_ROWS - MLA_V_DIM, tk), 0) == 0).astype(BF16)
    for h in range(MLA_HEADS):
        for u in range(tm // tk):
            vt_ref[h, u, :MLA_V_DIM, :] = v_t[h * MLA_V_DIM:(h + 1) * MLA_V_DIM, u * tk:(u + 1) * tk]
            vt_ref[h, u, MLA_V_DIM:, :] = ones_rows


def _mla_prep(proj, pos_col, freq, sign, q_gain, kv_gain, w_uq, w_uk, w_uv_t, batch, seq, tk):
    tm = min(PREP_TM, seq)
    per_batch = seq // tm
    kern = functools.partial(_mla_prep_kernel, tk=tk)
    const = lambda i: (0, 0)
    head_map = lambda i: (i // per_batch, 0, i % per_batch, 0)
    return pl.pallas_call(
        kern,
        grid=(batch * per_batch,),
        in_specs=[
            pl.BlockSpec((tm, MLA_Q_RANK), lambda i: (i, B_CQ_OFF // MLA_Q_RANK)),
            pl.BlockSpec((tm, MLA_KV_RANK), lambda i: (i, B_C_OFF // MLA_KV_RANK)),
            pl.BlockSpec((tm, LANES), lambda i: (i, B_KR_OFF // LANES)),
            pl.BlockSpec((tm, LANES), lambda i: (i, B_KROT_OFF // LANES)),
            pl.BlockSpec((tm, 1), lambda i: (i, 0)),
            pl.BlockSpec((1, LANES), const),
            pl.BlockSpec((1, LANES), const),
            pl.BlockSpec((1, MLA_Q_RANK), const),
            pl.BlockSpec((1, MLA_KV_RANK), const),
            pl.BlockSpec(w_uq.shape, const),
            pl.BlockSpec(w_uk.shape, const),
            pl.BlockSpec(w_uv_t.shape, const),
        ],
        out_specs=[
            pl.BlockSpec((None, MLA_HEADS, tm, MLA_QK_PAD), head_map),
            pl.BlockSpec((None, MLA_HEADS, tm, MLA_QK_PAD), head_map),
            pl.BlockSpec((None, MLA_HEADS, tm // tk, MLA_VT_ROWS, tk),
                         lambda i: (i // per_batch, 0, i % per_batch, 0, 0)),
        ],
        out_shape=[
            jax.ShapeDtypeStruct((batch, MLA_HEADS, seq, MLA_QK_PAD), BF16),
            jax.ShapeDtypeStruct((batch, MLA_HEADS, seq, MLA_QK_PAD), BF16),
            jax.ShapeDtypeStruct((batch, MLA_HEADS, seq // tk, MLA_VT_ROWS, tk), BF16),
        ],
        compiler_params=pltpu.CompilerParams(
            dimension_semantics=("parallel",), vmem_limit_bytes=VMEM_LIMIT),
        name="mla_prep",
    )(proj, proj, proj, proj, pos_col, freq, sign, q_gain, kv_gain, w_uq, w_uk, w_uv_t)


def _max_over_rows(x, ways=8):
    rows = x.shape[0]
    slab = rows // ways
    parts = [x[i * slab:(i + 1) * slab] for i in range(ways)] if slab >= 8 and rows % ways == 0 else [x]
    while len(parts) > 1:
        parts = [jnp.maximum(parts[i], parts[i + 1]) for i in range(0, len(parts), 2)]
    return jnp.max(parts[0], axis=0, keepdims=True)


def _mla_attn_kernel(q_ref, k_ref, vt_ref, z_ref, o_ref, acc_ref, m_ref, s0_ref, s1_ref, smax0_ref,
                     smax1_ref, *, tq, tk):
    n_heads, seq = q_ref.shape[0], q_ref.shape[1]
    tn = MXU_DIM
    units = [(h, n) for h in range(n_heads) for n in range(tq // tn)]
    s_slots = (s0_ref, s1_ref)
    smax_slots = (smax0_ref, smax1_ref)

    def n_keys(n, diagonal):
        return min(tk, (n + 1) * tn) if diagonal else tk

    def produce(qi, kj, slot, u, diagonal):
        h, n = units[u]
        nk = n_keys(n, diagonal)
        krows = pl.ds(pl.multiple_of(kj * tk, tk), nk)
        qrows = pl.ds(pl.multiple_of(qi * tq + n * tn, tn), tn)
        s_t = _dot_nt(k_ref[h, krows, :], q_ref[h, qrows, :])
        if diagonal:
            kpos = lax.broadcasted_iota(jnp.int32, (nk, tn), 0)
            qpos = lax.broadcasted_iota(jnp.int32, (nk, tn), 1) + n * tn
            s_t = jnp.where(kpos <= qpos, s_t, -jnp.inf)
        s_slots[slot][u, :nk, :] = s_t
        smax_slots[slot][u] = _max_over_rows(s_t)

    def consume(kj, slot, u, diagonal):
        h, n = units[u]
        nk = n_keys(n, diagonal)
        cols = slice(n * tn, (n + 1) * tn)
        m_prev = m_ref[h, :, cols]
        m_new = jnp.maximum(m_prev, smax_slots[slot][u])
        alpha = jnp.exp2(m_prev - m_new)
        p = jnp.exp2(s_slots[slot][u, :nk, :] - m_new).astype(BF16)
        m_ref[h, :, cols] = m_new
        acc_ref[h, :, cols] = alpha * acc_ref[h, :, cols] + _dot(vt_ref[h, kj, :, :nk], p)

    def produce_all(qi, kj, slot, diagonal):
        for u in range(len(units)):
            produce(qi, kj, slot, u, diagonal)

    def consume_all(kj, slot, diagonal):
        for u in range(len(units)):
            consume(kj, slot, u, diagonal)

    def consume_and_produce(qi, kj, slot, next_diagonal):
        produce(qi, kj + 1, 1 - slot, 0, next_diagonal)
        for u in range(len(units)):
            consume(kj, slot, u, False)
            if u + 1 < len(units):
                produce(qi, kj + 1, 1 - slot, u + 1, next_diagonal)

    def q_block(qi, carry):
        rows = pl.ds(pl.multiple_of(qi * tq, tq), tq)
        m_ref[...] = jnp.full_like(m_ref, -jnp.inf)
        acc_ref[...] = jnp.zeros_like(acc_ref)

        @pl.when(qi == 0)
        def _():
            produce_all(qi, 0, 0, True)
            consume_all(0, 0, True)

        @pl.when(qi > 0)
        def _():
            produce_all(qi, 0, 0, False)

        def steady_pair(i, c):
            consume_and_produce(qi, 2 * i, 0, False)
            consume_and_produce(qi, 2 * i + 1, 1, False)
            return c

        n_pairs = jnp.maximum(qi - 1, 0) // 2
        lax.fori_loop(0, n_pairs, steady_pair, 0)
        kj = 2 * n_pairs

        @pl.when(qi - kj == 1)
        def _():
            consume_and_produce(qi, kj, 0, True)
            consume_all(qi, 1, True)

        @pl.when(qi - kj == 2)
        def _():
            consume_and_produce(qi, kj, 0, False)
            consume_and_produce(qi, kj + 1, 1, True)
            consume_all(qi, 0, True)

        for h in range(n_heads):
            acc = acc_ref[h]
            o = jnp.transpose(acc[:MLA_V_DIM] * (1.0 / acc[MLA_V_DIM:MLA_V_DIM + 1]))
            cols = slice(h * MLA_V_DIM, (h + 1) * MLA_V_DIM)
            o_ref[rows, cols] = (o * _silu(z_ref[rows, cols].astype(F32))).astype(o_ref.dtype)
        return carry

    lax.fori_loop(0, seq // tq, q_block, 0)


def _mla_attn(q_cat, k_cat, v_t, proj, batch, seq, tq, tk):
    kern = functools.partial(_mla_attn_kernel, tq=tq, tk=tk)
    nb = seq // tk
    nh = ATT_HEADS_PER_STEP
    n_units = nh * (tq // MXU_DIM)
    width = nh * MLA_V_DIM
    z_blk = B_Z_OFF // width
    return pl.pallas_call(
        kern,
        grid=(batch, MLA_HEADS // nh),
        in_specs=[
            pl.BlockSpec((None, nh, seq, MLA_QK_PAD), lambda b, g: (b, g, 0, 0)),
            pl.BlockSpec((None, nh, seq, MLA_QK_PAD), lambda b, g: (b, g, 0, 0)),
            pl.BlockSpec((None, nh, nb, MLA_VT_ROWS, tk), lambda b, g: (b, g, 0, 0, 0)),
            pl.BlockSpec((seq, width), lambda b, g: (b, z_blk + g)),
        ],
        out_specs=pl.BlockSpec((seq, width), lambda b, g: (b, g)),
        out_shape=jax.ShapeDtypeStruct((batch * seq, BRANCH_WIDTH), BF16),
        scratch_shapes=[
            pltpu.VMEM((nh, MLA_VT_ROWS, tq), F32),
            pltpu.VMEM((nh, 1, tq), F32),
            pltpu.VMEM((n_units, tk, MXU_DIM), F32),
            pltpu.VMEM((n_units, tk, MXU_DIM), F32),
            pltpu.VMEM((n_units, 1, MXU_DIM), F32),
            pltpu.VMEM((n_units, 1, MXU_DIM), F32),
        ],
        compiler_params=pltpu.CompilerParams(
            dimension_semantics=("parallel", "parallel"), vmem_limit_bytes=VMEM_LIMIT),
        name="mla_attn",
    )(q_cat, k_cat, v_t, proj)


def _pad_heads(w, heads, width, padded):
    lead = w.shape[:-1]
    w = w.reshape(lead + (heads, width))
    w = jnp.pad(w, [(0, 0)] * len(lead) + [(0, 0), (0, padded - width)])
    return w.reshape(lead + (heads * padded,))


def _rot_half(w, heads):
    lead = w.shape[:-1]
    half = MLA_ROPE_DIM // 2
    w = w.reshape(lead + (heads, 2, half))
    return w[..., ::-1, :].reshape(lead + (heads * MLA_ROPE_DIM,))


def _layer_a_weights(w_in, gain, w_g2, b_g):
    hk = GLA_HEADS * GLA_DK
    o = np.cumsum([0, hk, hk, BRANCH_WIDTH, GLA_GATE_RANK, BRANCH_WIDTH, MEM_WIDTH, MEM_WIDTH])
    wq, wk, wv, wlr, wz, wmq, wmz = [w_in[:, o[i]:o[i + 1]] for i in range(7)]
    q_pad = jnp.zeros((D_MODEL, GLA_DK_PAD - GLA_DK - GLA_GATE_RANK), w_in.dtype)
    k_pad = jnp.zeros((D_MODEL, GLA_DK_PAD - GLA_DK), w_in.dtype)
    pieces = [wv, wz]
    for h in range(GLA_HEADS):
        pieces += [wq[:, h * GLA_DK:(h + 1) * GLA_DK], wlr, q_pad]
    for h in range(GLA_HEADS):
        pieces += [wk[:, h * GLA_DK:(h + 1) * GLA_DK], k_pad]
    pieces += [wmq, wmz]
    w_all = (jnp.concatenate(pieces, axis=1) * gain[:, None]).astype(BF16)
    wg = _pad_heads(w_g2, GLA_HEADS, GLA_DK, GLA_DK_PAD)
    wg = wg.reshape(GLA_GATE_RANK, GLA_HEADS, GLA_DK_PAD).transpose(1, 0, 2)
    wg = jnp.pad(wg, [(0, 0), (GLA_DK, GLA_DK_PAD - GLA_DK - GLA_GATE_RANK), (0, 0)]).astype(BF16)
    bg = _pad_heads(b_g, GLA_HEADS, GLA_DK, GLA_DK_PAD).reshape(GLA_HEADS, 1, GLA_DK_PAD)
    return w_all, wg, bg


def _layer_b_weights(w_in, gain_in, w_dkv, gain_dkv):
    o = np.cumsum([0, MLA_Q_RANK, BRANCH_WIDTH, MEM_WIDTH, MEM_WIDTH])
    wcq, wz, wmq, wmz = [w_in[:, o[i]:o[i + 1]] for i in range(4)]
    wc, wkr = w_dkv[:, :MLA_KV_RANK], w_dkv[:, MLA_KV_RANK:]
    wkrot = _rot_half(wkr, 1)
    w_b = jnp.concatenate([wz, wcq, wmq, wmz], axis=1) * gain_in[:, None]
    w_kv = jnp.concatenate([wc, wkr, wkr, wkrot, wkrot], axis=1) * gain_dkv[:, None]
    return jnp.concatenate([w_b, w_kv], axis=1).astype(BF16)


def _uq_weights(w_uq):
    w = w_uq.reshape(MLA_Q_RANK, MLA_HEADS, MLA_QK_DIM)
    w_nope = w[:, :, :MLA_NOPE_DIM].reshape(MLA_Q_RANK, MLA_HEADS * MLA_NOPE_DIM)
    w_rope = w[:, :, MLA_NOPE_DIM:].reshape(MLA_Q_RANK, MLA_HEADS * MLA_ROPE_DIM)
    return jnp.concatenate([w_nope, w_rope, _rot_half(w_rope, MLA_HEADS)], axis=1).astype(BF16)


def _rope_tables():
    r = MLA_ROPE_DIM
    freqs = ROPE_THETA ** (-jnp.arange(0, r, 2, dtype=F32) / r)
    freq = jnp.tile(freqs, LANES // (r // 2)).reshape(1, LANES)
    sign = jnp.tile(jnp.concatenate([-jnp.ones(r // 2, F32), jnp.ones(r // 2, F32)]),
                    LANES // r).reshape(1, LANES)
    return freq, sign


def kernel(x, mem, positions, a_pre_norm, a_w_in, a_w_g2, a_b_g, a_gla_norm, a_mem_norm, a_w_mem_kv,
           a_w_out, a_post_norm, kv_in_norm, w_dkv, kv_norm, w_uk, w_uv, b_pre_norm, b_w_in, b_q_norm,
           b_w_uq, b_mem_norm, b_w_mem_kv, b_w_out, b_post_norm):
    assert a_w_in.shape[0] == 1 and b_w_in.shape[0] == 1, "one A layer followed by one B layer"
    batch, seq, _ = x.shape
    t = batch * seq
    x2 = x.reshape(t, D_MODEL)

    w_memkv = jnp.concatenate([a_w_mem_kv[0] * a_mem_norm[0][:, None],
                               b_w_mem_kv[0] * b_mem_norm[0][:, None]], axis=1).astype(BF16)
    memkv = _norm_matmul(mem.reshape(batch * N_MEM, D_MODEL), w_memkv, tm=batch * N_MEM, tn=2 * MEM_WIDTH)

    w_a, wg, bg = _layer_a_weights(a_w_in[0], a_pre_norm[0], a_w_g2[0], a_b_g[0])
    proj_a = _norm_matmul(x2, w_a, tm=PROJ_TM, tn=A_TN)
    gla_out = _gla(proj_a, wg, bg, a_gla_norm, batch, seq)
    mem_out = _mem_attn(proj_a, memkv, seq, A_MQ_OFF, A_MZ_OFF, kv_blk=0)
    x1 = _out_proj(gla_out, mem_out, a_w_out[0].astype(BF16), x2, a_post_norm)

    w_b = _layer_b_weights(b_w_in[0], b_pre_norm[0], w_dkv, kv_in_norm)
    proj_b = _norm_matmul(x1, w_b, tm=PROJ_TM, tn=B_TN)
    freq, sign = _rope_tables()
    tk = min(ATT_TK, seq)
    tq = min(ATT_TQ, seq)
    q_cat, k_cat, v_t = _mla_prep(
        proj_b, positions.reshape(t, 1), freq, sign, b_q_norm, kv_norm[None, :],
        _uq_weights(b_w_uq[0]), w_uk.astype(BF16), w_uv.T.astype(BF16), batch, seq, tk)
    mla_out = _mla_attn(q_cat, k_cat, v_t, proj_b, batch, seq, tq, tk)
    mem_out_b = _mem_attn(proj_b, memkv, seq, B_MQ_OFF, B_MZ_OFF, kv_blk=2)
    out = _out_proj(mla_out, mem_out_b, b_w_out[0].astype(BF16), x1, b_post_norm)
    return out.reshape(batch, seq, D_MODEL)
```

```python
import functools

import numpy as np
import jax
import jax.numpy as jnp
from jax import lax
from jax.experimental import pallas as pl
from jax.experimental.pallas import tpu as pltpu

F32 = jnp.float32
BF16 = jnp.bfloat16

D_MODEL = 2048
N_MEM = 256
MEM_HEADS = 4
MEM_HEAD_DIM = 128
MEM_WIDTH = MEM_HEADS * MEM_HEAD_DIM
BRANCH_WIDTH = D_MODEL - MEM_WIDTH
GLA_HEADS = 4
GLA_DV = BRANCH_WIDTH // GLA_HEADS
GLA_DK = GLA_DV // 2
GLA_GATE_RANK = 16
GLA_GATE_NORM = 16.0
MLA_HEADS = 12
MLA_V_DIM = 128
MLA_NOPE_DIM = 128
MLA_ROPE_DIM = 64
MLA_QK_DIM = MLA_NOPE_DIM + MLA_ROPE_DIM
MLA_Q_RANK = 512
MLA_KV_RANK = 512
ROPE_THETA = 10000.0
EPS = 1e-6

LANES = 128
MXU_DIM = 256
GLA_DK_PAD = MXU_DIM
MLA_QK_PAD = MXU_DIM
BF16_SUBLANES = 16
MLA_VT_ROWS = MLA_V_DIM + BF16_SUBLANES
ATT_HEADS_PER_STEP = 2
LOG2_E = 1.4426950408889634
LN_2 = 0.6931471805599453
VMEM_LIMIT = 56 * 1024 * 1024

PROJ_TM = 1024
PROJ_NORM_ROWS = 256
REPACK_ROWS = 256
OUT_TM = 512
GLA_SPAN = 256
GLA_SUB = 32
GLA_HEADS_PER_STEP = 4
GLA_STAGE_SKEW = 1
MEM_TM = 512
PREP_TM = 512
ATT_TQ = 512
ATT_TK = 512

A_V_OFF = 0
A_Z_OFF = BRANCH_WIDTH
A_MQ_OFF = 2 * BRANCH_WIDTH
A_MZ_OFF = A_MQ_OFF + MEM_WIDTH
A_Q_OFF = A_MZ_OFF + MEM_WIDTH
A_K_OFF = A_Q_OFF + GLA_HEADS * GLA_DK_PAD
A_WIDTH = A_K_OFF + GLA_HEADS * GLA_DK_PAD
A_TN = 1024

B_Z_OFF = 0
B_CQ_OFF = BRANCH_WIDTH
B_MQ_OFF = B_CQ_OFF + MLA_Q_RANK
B_MZ_OFF = B_MQ_OFF + MEM_WIDTH
B_C_OFF = B_MZ_OFF + MEM_WIDTH
B_KR_OFF = B_C_OFF + MLA_KV_RANK
B_KROT_OFF = B_KR_OFF + LANES
B_WIDTH = B_KROT_OFF + LANES
B_TN = 768


def _silu(z):
    return z * (1.0 / (1.0 + jnp.exp(-z)))


def _scaled_log_sigmoid(g, scale):
    softplus2 = jnp.log2(1.0 + jnp.exp2(jnp.abs(g) * -LOG2_E))
    return jnp.minimum(g, 0.0) * scale - softplus2 * (LN_2 * scale)


def _dot(a, b):
    return jnp.dot(a, b, preferred_element_type=F32)


def _dot_nt(a, b):
    return lax.dot_general(a, b, (((1,), (1,)), ((), ())), preferred_element_type=F32)


def _dot_tn(a, b):
    return lax.dot_general(a, b, (((0,), (0,)), ((), ())), preferred_element_type=F32)


def _norm_matmul_kernel(x_ref, w_ref, o_ref, h_ref, r_ref, *, norm_rows):
    @pl.when(pl.program_id(1) == 0)
    def _():
        tm = x_ref.shape[0]
        for r in range(tm // norm_rows):
            rows = pl.ds(r * norm_rows, norm_rows)
            x = x_ref[rows, :]
            r_ref[rows, :] = lax.rsqrt(jnp.mean(x * x, axis=-1, keepdims=True) + EPS)
            h_ref[rows, :] = x.astype(BF16)

    o_ref[...] = (_dot(h_ref[...], w_ref[...]) * r_ref[...]).astype(o_ref.dtype)


def _norm_matmul(x, w, *, tm, tn):
    m, d = x.shape
    n = w.shape[1]
    tm = min(tm, m)
    kern = functools.partial(_norm_matmul_kernel, norm_rows=min(PROJ_NORM_ROWS, tm))
    return pl.pallas_call(
        kern,
        grid=(m // tm, n // tn),
        in_specs=[
            pl.BlockSpec((tm, d), lambda i, j: (i, 0)),
            pl.BlockSpec((d, tn), lambda i, j: (0, j)),
        ],
        out_specs=pl.BlockSpec((tm, tn), lambda i, j: (i, j)),
        out_shape=jax.ShapeDtypeStruct((m, n), BF16),
        scratch_shapes=[pltpu.VMEM((tm, d), BF16), pltpu.VMEM((tm, 1), F32)],
        compiler_params=pltpu.CompilerParams(
            dimension_semantics=("parallel", "arbitrary"), vmem_limit_bytes=VMEM_LIMIT),
        name="norm_matmul",
    )(x, w)


def _out_proj_kernel(a_ref, m_ref, wa_ref, wm_ref, x_ref, g_ref, o_ref):
    y = _dot(a_ref[...], wa_ref[...]) + _dot(m_ref[...], wm_ref[...])
    ms = jnp.mean(y * y, axis=-1, keepdims=True)
    o_ref[...] = x_ref[...] + y * lax.rsqrt(ms + EPS) * g_ref[...]


def _out_proj(a, mo, w_out, x, gain):
    t = x.shape[0]
    tm = min(OUT_TM, t)
    mem_blk = BRANCH_WIDTH // MEM_WIDTH
    return pl.pallas_call(
        _out_proj_kernel,
        grid=(t // tm,),
        in_specs=[
            pl.BlockSpec((tm, BRANCH_WIDTH), lambda i: (i, 0)),
            pl.BlockSpec((tm, MEM_WIDTH), lambda i: (i, 0)),
            pl.BlockSpec((BRANCH_WIDTH, D_MODEL), lambda i: (0, 0)),
            pl.BlockSpec((MEM_WIDTH, D_MODEL), lambda i: (mem_blk, 0)),
            pl.BlockSpec((tm, D_MODEL), lambda i: (i, 0)),
            pl.BlockSpec((1, D_MODEL), lambda i: (0, 0)),
        ],
        out_specs=pl.BlockSpec((tm, D_MODEL), lambda i: (i, 0)),
        out_shape=jax.ShapeDtypeStruct((t, D_MODEL), F32),
        compiler_params=pltpu.CompilerParams(
            dimension_semantics=("parallel",), vmem_limit_bytes=VMEM_LIMIT),
        name="out_proj",
    )(a, mo, w_out, w_out, x, gain)


def _gla_head_stages(h, v_ref, z_ref, q_ref, k_ref, wg_ref, bg_ref, gn_ref, tril_ref, lvl_ref, o_ref,
                     s_ref, sub):
    span = q_ref.shape[0]
    n_sub = span // sub
    n_levels = n_sub.bit_length() - 1
    scale = GLA_DK ** -0.5
    kcols = slice(h * GLA_DK_PAD, (h + 1) * GLA_DK_PAD)
    vcols = slice(h * GLA_DV, (h + 1) * GLA_DV)

    qb = q_ref[:, kcols]
    g = _dot(qb, wg_ref[h]) + bg_ref[h]
    yield
    log_a = _scaled_log_sigmoid(g, 1.0 / GLA_GATE_NORM)
    a_hi = log_a.astype(BF16)
    rem = log_a - a_hi.astype(F32)
    a_mid = rem.astype(BF16)
    a_lo = (rem - a_mid.astype(F32)).astype(BF16)
    yield
    tril = tril_ref[...]
    cum = _dot(tril, a_hi) + _dot(tril, a_mid) + _dot(tril, a_lo)
    yield

    refs = jnp.concatenate([cum[i * sub:i * sub + 1] for i in range(n_sub)] + [cum[span - 1:span]], axis=0)
    own = refs[:n_sub]

    def ref_rows(index_of):
        return jnp.concatenate([refs[index_of(i):index_of(i) + 1] for i in range(n_sub)], axis=0)

    def per_row(f):
        return jnp.concatenate(
            [jnp.broadcast_to(f[i:i + 1], (sub, f.shape[1])) for i in range(n_sub)], axis=0)

    own_b = per_row(own)
    qd = qb.astype(F32) * (jnp.exp(cum - own_b) * scale)
    kd = k_ref[:, kcols].astype(F32) * jnp.exp(own_b - cum)
    yield
    q_ops, k_ops = [qd.astype(BF16)], [kd.astype(BF16)]
    for lvl in range(1, n_levels + 1):
        n = 1 << (lvl - 1)
        block_start = ref_rows(lambda i: (i // n) * n)
        next_start = ref_rows(lambda i: (i // n + 1) * n)
        q_ops.append(q_ops[0] if n == 1 else (qd * per_row(jnp.exp(own - block_start))).astype(BF16))
        k_ops.append((kd * per_row(jnp.exp(next_start - own))).astype(BF16))
    q_state = (qd * per_row(jnp.exp(own))).astype(BF16)
    k_state = (kd * per_row(jnp.exp(refs[n_sub:] - own))).astype(BF16)
    yield
    atts = [_dot_nt(qo, ko) for qo, ko in zip(q_ops, k_ops)]
    yield
    level = lvl_ref[...]
    att = jnp.where(level == n_levels, atts[n_levels], 0.0)
    for lvl in reversed(range(n_levels)):
        att = jnp.where(level == lvl, atts[lvl], att)
    yield
    state = s_ref[h]
    v = v_ref[:, vcols]
    o = _dot(jnp.concatenate([att.astype(BF16), q_state], axis=1),
             jnp.concatenate([v, state.astype(BF16)], axis=0))
    decay_col = jnp.transpose(jnp.exp(cum[span - 8:span, :]))[:, 7:8]
    s_ref[h] = decay_col * state + _dot_tn(k_state, v)
    yield
    on = o * lax.rsqrt(jnp.mean(o * o, axis=-1, keepdims=True) + EPS) * gn_ref[...]
    o_ref[:, vcols] = (on * _silu(z_ref[:, vcols].astype(F32))).astype(o_ref.dtype)
    yield


def _gla_kernel(v_ref, z_ref, q_ref, k_ref, wg_ref, bg_ref, gn_ref, tril_ref, lvl_ref, o_ref, s_ref, *,
                sub, skew):
    @pl.when(pl.program_id(2) == 0)
    def _():
        s_ref[...] = jnp.zeros_like(s_ref)

    heads = [_gla_head_stages(h, v_ref, z_ref, q_ref, k_ref, wg_ref, bg_ref, gn_ref, tril_ref, lvl_ref,
                              o_ref, s_ref, sub) for h in range(s_ref.shape[0])]
    live = list(range(len(heads)))
    step = 0
    while live:
        for i in list(live):
            if step >= i * skew:
                try:
                    next(heads[i])
                except StopIteration:
                    live.remove(i)
        step += 1


def _prefix_matrix(rows):
    r = np.arange(rows)
    return jnp.asarray(r[:, None] >= r[None, :], BF16)


def _level_matrix(rows, sub):
    n_levels = (rows // sub).bit_length() - 1
    t, s = np.arange(rows)[:, None], np.arange(rows)[None, :]
    level = np.full((rows, rows), n_levels + 1, np.int32)
    level[(t // sub == s // sub) & (s <= t)] = 0
    for lvl in range(1, n_levels + 1):
        size = sub << (lvl - 1)
        level[((t // size) % 2 == 1) & (s // size == t // size - 1)] = lvl
    return jnp.asarray(level)


def _gla(proj, wg, bg, gn, batch, seq):
    span = min(GLA_SPAN, seq)
    assert span % GLA_SUB == 0 and (span // GLA_SUB) & (span // GLA_SUB - 1) == 0
    ns = seq // span
    nh = GLA_HEADS_PER_STEP
    kern = functools.partial(_gla_kernel, sub=GLA_SUB, skew=GLA_STAGE_SKEW)
    vw, kw = nh * GLA_DV, nh * GLA_DK_PAD
    v_blk, z_blk = A_V_OFF // vw, A_Z_OFF // vw
    q_blk, k_blk = A_Q_OFF // kw, A_K_OFF // kw
    group = span
    return pl.pallas_call(
        kern,
        grid=(batch, GLA_HEADS // nh, ns),
        in_specs=[
            pl.BlockSpec((span, vw), lambda b, g, s: (b * ns + s, v_blk + g)),
            pl.BlockSpec((span, vw), lambda b, g, s: (b * ns + s, z_blk + g)),
            pl.BlockSpec((span, kw), lambda b, g, s: (b * ns + s, q_blk + g)),
            pl.BlockSpec((span, kw), lambda b, g, s: (b * ns + s, k_blk + g)),
            pl.BlockSpec((nh, GLA_DK_PAD, GLA_DK_PAD), lambda b, g, s: (g, 0, 0)),
            pl.BlockSpec((nh, 1, GLA_DK_PAD), lambda b, g, s: (g, 0, 0)),
            pl.BlockSpec((1, GLA_DV), lambda b, g, s: (0, 0)),
            pl.BlockSpec((group, group), lambda b, g, s: (0, 0)),
            pl.BlockSpec((group, group), lambda b, g, s: (0, 0)),
        ],
        out_specs=pl.BlockSpec((span, vw), lambda b, g, s: (b * ns + s, g)),
        out_shape=jax.ShapeDtypeStruct((batch * seq, BRANCH_WIDTH), BF16),
        scratch_shapes=[pltpu.VMEM((nh, GLA_DK_PAD, GLA_DV), F32)],
        compiler_params=pltpu.CompilerParams(
            dimension_semantics=("parallel", "parallel", "arbitrary"), vmem_limit_bytes=VMEM_LIMIT),
        name="gla",
    )(proj, proj, proj, proj, wg, bg, gn, _prefix_matrix(group), _level_matrix(group, GLA_SUB))


def _mem_attn_kernel(q_ref, z_ref, k_ref, v_ref, o_ref):
    scale = MEM_HEAD_DIM ** -0.5
    for h in range(MEM_HEADS):
        cols = slice(h * MEM_HEAD_DIM, (h + 1) * MEM_HEAD_DIM)
        s = _dot_nt(q_ref[:, cols], k_ref[:, cols]) * scale
        p = jnp.exp(s - jnp.max(s, axis=-1, keepdims=True))
        l = jnp.sum(p, axis=-1, keepdims=True)
        o = _dot(p.astype(BF16), v_ref[:, cols]) * (1.0 / l)
        o_ref[:, cols] = (o * _silu(z_ref[:, cols].astype(F32))).astype(o_ref.dtype)


def _mem_attn(proj, memkv, seq, q_off, z_off, kv_blk):
    t = proj.shape[0]
    tm = min(MEM_TM, seq)
    per_batch = seq // tm
    q_blk, z_blk = q_off // MEM_WIDTH, z_off // MEM_WIDTH
    return pl.pallas_call(
        _mem_attn_kernel,
        grid=(t // tm,),
        in_specs=[
            pl.BlockSpec((tm, MEM_WIDTH), lambda i: (i, q_blk)),
            pl.BlockSpec((tm, MEM_WIDTH), lambda i: (i, z_blk)),
            pl.BlockSpec((N_MEM, MEM_WIDTH), lambda i: (i // per_batch, kv_blk)),
            pl.BlockSpec((N_MEM, MEM_WIDTH), lambda i: (i // per_batch, kv_blk + 1)),
        ],
        out_specs=pl.BlockSpec((tm, MEM_WIDTH), lambda i: (i, 0)),
        out_shape=jax.ShapeDtypeStruct((t, MEM_WIDTH), BF16),
        compiler_params=pltpu.CompilerParams(
            dimension_semantics=("parallel",), vmem_limit_bytes=VMEM_LIMIT),
        name="mem_attn",
    )(proj, proj, memkv, memkv)


def _mla_prep_kernel(cq_ref, c_ref, kr_ref, krot_ref, pos_ref, freq_ref, sign_ref, qg_ref, kvg_ref,
                     wuq_ref, wuk_ref, wuvt_ref, q_ref, k_ref, vt_ref, *, tk):
    tm = cq_ref.shape[0]
    ang = pos_ref[...].astype(F32) * freq_ref[...]
    cos = jnp.cos(ang)
    sin = jnp.sin(ang) * sign_ref[...]
    lane = lax.broadcasted_iota(jnp.int32, (1, LANES), 1)
    slot_mask = [(lane < MLA_ROPE_DIM).astype(F32), (lane >= MLA_ROPE_DIM).astype(F32)]

    cq = cq_ref[...].astype(F32)
    cqn = cq * lax.rsqrt(jnp.mean(cq * cq, axis=-1, keepdims=True) + EPS)
    cqn = (cqn * (qg_ref[...] * (MLA_QK_DIM ** -0.5 * LOG2_E))).astype(BF16)
    n_nope = MLA_HEADS * MLA_NOPE_DIM
    n_rope = MLA_HEADS * MLA_ROPE_DIM
    q_nope = _dot(cqn, wuq_ref[:, :n_nope])
    q_r = _dot(cqn, wuq_ref[:, n_nope:n_nope + n_rope])
    q_rot = _dot(cqn, wuq_ref[:, n_nope + n_rope:])
    for h in range(MLA_HEADS):
        q_ref[h, :, :MLA_NOPE_DIM] = q_nope[:, h * MLA_NOPE_DIM:(h + 1) * MLA_NOPE_DIM].astype(BF16)
    for u in range(MLA_HEADS // 2):
        cols = slice(u * LANES, (u + 1) * LANES)
        roped = q_r[:, cols] * cos + q_rot[:, cols] * sin
        for half in range(2):
            q_ref[2 * u + half, :, MLA_NOPE_DIM:] = (roped * slot_mask[half]).astype(BF16)

    c = c_ref[...].astype(F32)
    cn = c * lax.rsqrt(jnp.mean(c * c, axis=-1, keepdims=True) + EPS)
    cn = (cn * kvg_ref[...]).astype(BF16)
    k_nope = _dot(cn, wuk_ref[...])
    k_rope = (kr_ref[...].astype(F32) * cos + krot_ref[...].astype(F32) * sin).astype(BF16)
    for h in range(MLA_HEADS):
        k_ref[h, :, :MLA_NOPE_DIM] = k_nope[:, h * MLA_NOPE_DIM:(h + 1) * MLA_NOPE_DIM].astype(BF16)
        k_ref[h, :, MLA_NOPE_DIM:] = k_rope
    v_t = _dot_nt(wuvt_ref[...], cn).astype(BF16)
    ones_rows = (lax.broadcasted_iota(jnp.int32, (MLA_VT_ROWS - MLA_V_DIM, tk), 0) == 0).astype(BF16)
    for h in range(MLA_HEADS):
        for u in range(tm // tk):
            vt_ref[h, u, :MLA_V_DIM, :] = v_t[h * MLA_V_DIM:(h + 1) * MLA_V_DIM, u * tk:(u + 1) * tk]
            vt_ref[h, u, MLA_V_DIM:, :] = ones_rows


def _mla_prep(proj, pos_col, freq, sign, q_gain, kv_gain, w_uq, w_uk, w_uv_t, batch, seq, tk):
    tm = min(PREP_TM, seq)
    per_batch = seq // tm
    kern = functools.partial(_mla_prep_kernel, tk=tk)
    const = lambda i: (0, 0)
    head_map = lambda i: (i // per_batch, 0, i % per_batch, 0)
    return pl.pallas_call(
        kern,
        grid=(batch * per_batch,),
        in_specs=[
            pl.BlockSpec((tm, MLA_Q_RANK), lambda i: (i, B_CQ_OFF // MLA_Q_RANK)),
            pl.BlockSpec((tm, MLA_KV_RANK), lambda i: (i, B_C_OFF // MLA_KV_RANK)),
            pl.BlockSpec((tm, LANES), lambda i: (i, B_KR_OFF // LANES)),
            pl.BlockSpec((tm, LANES), lambda i: (i, B_KROT_OFF // LANES)),
            pl.BlockSpec((tm, 1), lambda i: (i, 0)),
            pl.BlockSpec((1, LANES), const),
            pl.BlockSpec((1, LANES), const),
            pl.BlockSpec((1, MLA_Q_RANK), const),
            pl.BlockSpec((1, MLA_KV_RANK), const),
            pl.BlockSpec(w_uq.shape, const),
            pl.BlockSpec(w_uk.shape, const),
            pl.BlockSpec(w_uv_t.shape, const),
        ],
        out_specs=[
            pl.BlockSpec((None, MLA_HEADS, tm, MLA_QK_PAD), head_map),
            pl.BlockSpec((None, MLA_HEADS, tm, MLA_QK_PAD), head_map),
            pl.BlockSpec((None, MLA_HEADS, tm // tk, MLA_VT_ROWS, tk),
                         lambda i: (i // per_batch, 0, i % per_batch, 0, 0)),
        ],
        out_shape=[
            jax.ShapeDtypeStruct((batch, MLA_HEADS, seq, MLA_QK_PAD), BF16),
            jax.ShapeDtypeStruct((batch, MLA_HEADS, seq, MLA_QK_PAD), BF16),
            jax.ShapeDtypeStruct((batch, MLA_HEADS, seq // tk, MLA_VT_ROWS, tk), BF16),
        ],
        compiler_params=pltpu.CompilerParams(
            dimension_semantics=("parallel",), vmem_limit_bytes=VMEM_LIMIT),
        name="mla_prep",
    )(proj, proj, proj, proj, pos_col, freq, sign, q_gain, kv_gain, w_uq, w_uk, w_uv_t)


def _max_over_rows(x, ways=8):
    rows = x.shape[0]
    slab = rows // ways
    parts = [x[i * slab:(i + 1) * slab] for i in range(ways)] if slab >= 8 and rows % ways == 0 else [x]
    while len(parts) > 1:
        parts = [jnp.maximum(parts[i], parts[i + 1]) for i in range(0, len(parts), 2)]
    return jnp.max(parts[0], axis=0, keepdims=True)


def _mla_attn_kernel(q_ref, k_ref, vt_ref, z_ref, o_ref, acc_ref, m_ref, s0_ref, s1_ref, smax0_ref,
                     smax1_ref, *, tq, tk):
    n_heads, seq = q_ref.shape[0], q_ref.shape[1]
    tn = MXU_DIM
    units = [(h, n) for h in range(n_heads) for n in range(tq // tn)]
    s_slots = (s0_ref, s1_ref)
    smax_slots = (smax0_ref, smax1_ref)

    def n_keys(n, diagonal):
        return min(tk, (n + 1) * tn) if diagonal else tk

    def produce(qi, kj, slot, u, diagonal):
        h, n = units[u]
        nk = n_keys(n, diagonal)
        krows = pl.ds(pl.multiple_of(kj * tk, tk), nk)
        qrows = pl.ds(pl.multiple_of(qi * tq + n * tn, tn), tn)
        s_t = _dot_nt(k_ref[h, krows, :], q_ref[h, qrows, :])
        if diagonal:
            kpos = lax.broadcasted_iota(jnp.int32, (nk, tn), 0)
            qpos = lax.broadcasted_iota(jnp.int32, (nk, tn), 1) + n * tn
            s_t = jnp.where(kpos <= qpos, s_t, -jnp.inf)
        s_slots[slot][u, :nk, :] = s_t
        smax_slots[slot][u] = _max_over_rows(s_t)

    def consume(kj, slot, u, diagonal):
        h, n = units[u]
        nk = n_keys(n, diagonal)
        cols = slice(n * tn, (n + 1) * tn)
        m_prev = m_ref[h, :, cols]
        m_new = jnp.maximum(m_prev, smax_slots[slot][u])
        alpha = jnp.exp2(m_prev - m_new)
        p = jnp.exp2(s_slots[slot][u, :nk, :] - m_new).astype(BF16)
        m_ref[h, :, cols] = m_new
        acc_ref[h, :, cols] = alpha * acc_ref[h, :, cols] + _dot(vt_ref[h, kj, :, :nk], p)

    def produce_all(qi, kj, slot, diagonal):
        for u in range(len(units)):
            produce(qi, kj, slot, u, diagonal)

    def consume_all(kj, slot, diagonal):
        for u in range(len(units)):
            consume(kj, slot, u, diagonal)

    def consume_and_produce(qi, kj, slot, next_diagonal):
        produce(qi, kj + 1, 1 - slot, 0, next_diagonal)
        for u in range(len(units)):
            consume(kj, slot, u, False)
            if u + 1 < len(units):
                produce(qi, kj + 1, 1 - slot, u + 1, next_diagonal)

    def q_block(qi, carry):
        rows = pl.ds(pl.multiple_of(qi * tq, tq), tq)
        m_ref[...] = jnp.full_like(m_ref, -jnp.inf)
        acc_ref[...] = jnp.zeros_like(acc_ref)

        @pl.when(qi == 0)
        def _():
            produce_all(qi, 0, 0, True)
            consume_all(0, 0, True)

        @pl.when(qi > 0)
        def _():
            produce_all(qi, 0, 0, False)

        def steady_pair(i, c):
            consume_and_produce(qi, 2 * i, 0, False)
            consume_and_produce(qi, 2 * i + 1, 1, False)
            return c

        n_pairs = jnp.maximum(qi - 1, 0) // 2
        lax.fori_loop(0, n_pairs, steady_pair, 0)
        kj = 2 * n_pairs

        @pl.when(qi - kj == 1)
        def _():
            consume_and_produce(qi, kj, 0, True)
            consume_all(qi, 1, True)

        @pl.when(qi - kj == 2)
        def _():
            consume_and_produce(qi, kj, 0, False)
            consume_and_produce(qi, kj + 1, 1, True)
            consume_all(qi, 0, True)

        for h in range(n_heads):
            acc = acc_ref[h]
            o = jnp.transpose(acc[:MLA_V_DIM] * (1.0 / acc[MLA_V_DIM:MLA_V_DIM + 1]))
            cols = slice(h * MLA_V_DIM, (h + 1) * MLA_V_DIM)
            o_ref[rows, cols] = (o * _silu(z_ref[rows, cols].astype(F32))).astype(o_ref.dtype)
        return carry

    lax.fori_loop(0, seq // tq, q_block, 0)


def _mla_attn(q_cat, k_cat, v_t, proj, batch, seq, tq, tk):
    kern = functools.partial(_mla_attn_kernel, tq=tq, tk=tk)
    nb = seq // tk
    nh = ATT_HEADS_PER_STEP
    n_units = nh * (tq // MXU_DIM)
    width = nh * MLA_V_DIM
    z_blk = B_Z_OFF // width
    return pl.pallas_call(
        kern,
        grid=(batch, MLA_HEADS // nh),
        in_specs=[
            pl.BlockSpec((None, nh, seq, MLA_QK_PAD), lambda b, g: (b, g, 0, 0)),
            pl.BlockSpec((None, nh, seq, MLA_QK_PAD), lambda b, g: (b, g, 0, 0)),
            pl.BlockSpec((None, nh, nb, MLA_VT_ROWS, tk), lambda b, g: (b, g, 0, 0, 0)),
            pl.BlockSpec((seq, width), lambda b, g: (b, z_blk + g)),
        ],
        out_specs=pl.BlockSpec((seq, width), lambda b, g: (b, g)),
        out_shape=jax.ShapeDtypeStruct((batch * seq, BRANCH_WIDTH), BF16),
        scratch_shapes=[
            pltpu.VMEM((nh, MLA_VT_ROWS, tq), F32),
            pltpu.VMEM((nh, 1, tq), F32),
            pltpu.VMEM((n_units, tk, MXU_DIM), F32),
            pltpu.VMEM((n_units, tk, MXU_DIM), F32),
            pltpu.VMEM((n_units, 1, MXU_DIM), F32),
            pltpu.VMEM((n_units, 1, MXU_DIM), F32),
        ],
        compiler_params=pltpu.CompilerParams(
            dimension_semantics=("parallel", "parallel"), vmem_limit_bytes=VMEM_LIMIT),
        name="mla_attn",
    )(q_cat, k_cat, v_t, proj)


def _pad_heads(w, heads, width, padded):
    lead = w.shape[:-1]
    w = w.reshape(lead + (heads, width))
    w = jnp.pad(w, [(0, 0)] * len(lead) + [(0, 0), (0, padded - width)])
    return w.reshape(lead + (heads * padded,))


def _repack_kernel(*refs, n_in, with_gain, pieces, zero_ranges):
    ins = refs[:n_in]
    gains = refs[n_in:2 * n_in] if with_gain else None
    o_ref = refs[-1]
    for dst, idx, src, width in pieces:
        val = ins[idx][:, src:src + width]
        if with_gain:
            val = val * gains[idx][...]
        o_ref[:, dst:dst + width] = val.astype(o_ref.dtype)
    for lo, hi in zero_ranges:
        o_ref[:, lo:hi] = jnp.zeros((o_ref.shape[0], hi - lo), o_ref.dtype)


def _repack(ws, gains, pieces, out_width, zero_ranges=()):
    rows = ws[0].shape[0]
    tr = min(REPACK_ROWS, rows)
    with_gain = gains is not None
    covered = sorted([(d, d + w) for d, _, _, w in pieces] + list(zero_ranges))
    assert covered[0][0] == 0 and covered[-1][1] == out_width
    assert all(a[1] == b[0] for a, b in zip(covered, covered[1:])), "output columns written exactly once"
    kern = functools.partial(_repack_kernel, n_in=len(ws), with_gain=with_gain, pieces=tuple(pieces),
                             zero_ranges=tuple(zero_ranges))
    in_specs = [pl.BlockSpec((tr, w.shape[1]), lambda i: (i, 0)) for w in ws]
    args = list(ws)
    if with_gain:
        in_specs += [pl.BlockSpec((tr, 1), lambda i: (i, 0)) for _ in gains]
        args += [g.reshape(rows, 1) for g in gains]
    return pl.pallas_call(
        kern,
        grid=(rows // tr,),
        in_specs=in_specs,
        out_specs=pl.BlockSpec((tr, out_width), lambda i: (i, 0)),
        out_shape=jax.ShapeDtypeStruct((rows, out_width), BF16),
        compiler_params=pltpu.CompilerParams(
            dimension_semantics=("parallel",), vmem_limit_bytes=VMEM_LIMIT),
        name="repack",
    )(*args)


def _layer_a_weights(w_in, gain, w_g2, b_g):
    hk = GLA_HEADS * GLA_DK
    o = [int(v) for v in np.cumsum([0, hk, hk, BRANCH_WIDTH, GLA_GATE_RANK, BRANCH_WIDTH, MEM_WIDTH,
                                    MEM_WIDTH])]
    q_src, k_src, v_src, lr_src, z_src = o[0], o[1], o[2], o[3], o[4]
    pieces = [(A_V_OFF, 0, v_src, BRANCH_WIDTH), (A_Z_OFF, 0, z_src, BRANCH_WIDTH + 2 * MEM_WIDTH)]
    zeros = []
    for h in range(GLA_HEADS):
        q_dst, k_dst = A_Q_OFF + h * GLA_DK_PAD, A_K_OFF + h * GLA_DK_PAD
        pieces += [(q_dst, 0, q_src + h * GLA_DK, GLA_DK), (q_dst + GLA_DK, 0, lr_src, GLA_GATE_RANK),
                   (k_dst, 0, k_src + h * GLA_DK, GLA_DK)]
        zeros += [(q_dst + GLA_DK + GLA_GATE_RANK, q_dst + GLA_DK_PAD), (k_dst + GLA_DK, k_dst + GLA_DK_PAD)]
    w_all = _repack([w_in], [gain], pieces, A_WIDTH, zeros)
    wg = _pad_heads(w_g2, GLA_HEADS, GLA_DK, GLA_DK_PAD)
    wg = wg.reshape(GLA_GATE_RANK, GLA_HEADS, GLA_DK_PAD).transpose(1, 0, 2)
    wg = jnp.pad(wg, [(0, 0), (GLA_DK, GLA_DK_PAD - GLA_DK - GLA_GATE_RANK), (0, 0)]).astype(BF16)
    bg = _pad_heads(b_g, GLA_HEADS, GLA_DK, GLA_DK_PAD).reshape(GLA_HEADS, 1, GLA_DK_PAD)
    return w_all, wg, bg


def _layer_b_weights(w_in, gain_in, w_dkv, gain_dkv):
    half = MLA_ROPE_DIM // 2
    pieces = [(B_Z_OFF, 0, MLA_Q_RANK, BRANCH_WIDTH), (B_CQ_OFF, 0, 0, MLA_Q_RANK),
              (B_MQ_OFF, 0, MLA_Q_RANK + BRANCH_WIDTH, 2 * MEM_WIDTH), (B_C_OFF, 1, 0, MLA_KV_RANK)]
    for rep in range(LANES // MLA_ROPE_DIM):
        kr, krot = B_KR_OFF + rep * MLA_ROPE_DIM, B_KROT_OFF + rep * MLA_ROPE_DIM
        pieces += [(kr, 1, MLA_KV_RANK, MLA_ROPE_DIM),
                   (krot, 1, MLA_KV_RANK + half, half), (krot + half, 1, MLA_KV_RANK, half)]
    return _repack([w_in, w_dkv], [gain_in, gain_dkv], pieces, B_WIDTH)


def _uq_weights(w_uq):
    half = MLA_ROPE_DIM // 2
    n_nope, n_rope = MLA_HEADS * MLA_NOPE_DIM, MLA_HEADS * MLA_ROPE_DIM
    pieces = []
    for h in range(MLA_HEADS):
        src = h * MLA_QK_DIM
        rope, rot = n_nope + h * MLA_ROPE_DIM, n_nope + n_rope + h * MLA_ROPE_DIM
        pieces += [(h * MLA_NOPE_DIM, 0, src, MLA_NOPE_DIM), (rope, 0, src + MLA_NOPE_DIM, MLA_ROPE_DIM),
                   (rot, 0, src + MLA_NOPE_DIM + half, half), (rot + half, 0, src + MLA_NOPE_DIM, half)]
    return _repack([w_uq], None, pieces, n_nope + 2 * n_rope)


def _rope_tables():
    r = MLA_ROPE_DIM
    freqs = ROPE_THETA ** (-jnp.arange(0, r, 2, dtype=F32) / r)
    freq = jnp.tile(freqs, LANES // (r // 2)).reshape(1, LANES)
    sign = jnp.tile(jnp.concatenate([-jnp.ones(r // 2, F32), jnp.ones(r // 2, F32)]),
                    LANES // r).reshape(1, LANES)
    return freq, sign


def kernel(x, mem, positions, a_pre_norm, a_w_in, a_w_g2, a_b_g, a_gla_norm, a_mem_norm, a_w_mem_kv,
           a_w_out, a_post_norm, kv_in_norm, w_dkv, kv_norm, w_uk, w_uv, b_pre_norm, b_w_in, b_q_norm,
           b_w_uq, b_mem_norm, b_w_mem_kv, b_w_out, b_post_norm):
    assert a_w_in.shape[0] == 1 and b_w_in.shape[0] == 1, "one A layer followed by one B layer"
    batch, seq, _ = x.shape
    t = batch * seq
    x2 = x.reshape(t, D_MODEL)

    kv_w = 2 * MEM_WIDTH
    w_memkv = _repack([a_w_mem_kv[0], b_w_mem_kv[0]], [a_mem_norm[0], b_mem_norm[0]],
                      [(0, 0, 0, kv_w), (kv_w, 1, 0, kv_w)], 2 * kv_w)
    memkv = _norm_matmul(mem.reshape(batch * N_MEM, D_MODEL), w_memkv, tm=batch * N_MEM, tn=2 * MEM_WIDTH)

    w_a, wg, bg = _layer_a_weights(a_w_in[0], a_pre_norm[0], a_w_g2[0], a_b_g[0])
    proj_a = _norm_matmul(x2, w_a, tm=PROJ_TM, tn=A_TN)
    gla_out = _gla(proj_a, wg, bg, a_gla_norm, batch, seq)
    mem_out = _mem_attn(proj_a, memkv, seq, A_MQ_OFF, A_MZ_OFF, kv_blk=0)
    x1 = _out_proj(gla_out, mem_out, a_w_out[0].astype(BF16), x2, a_post_norm)

    w_b = _layer_b_weights(b_w_in[0], b_pre_norm[0], w_dkv, kv_in_norm)
    proj_b = _norm_matmul(x1, w_b, tm=PROJ_TM, tn=B_TN)
    freq, sign = _rope_tables()
    tk = min(ATT_TK, seq)
    tq = min(ATT_TQ, seq)
    q_cat, k_cat, v_t = _mla_prep(
        proj_b, positions.reshape(t, 1), freq, sign, b_q_norm, kv_norm[None, :],
        _uq_weights(b_w_uq[0]), w_uk.astype(BF16), w_uv.T.astype(BF16), batch, seq, tk)
    mla_out = _mla_attn(q_cat, k_cat, v_t, proj_b, batch, seq, tq, tk)
    mem_out_b = _mem_attn(proj_b, memkv, seq, B_MQ_OFF, B_MZ_OFF, kv_blk=2)
    out = _out_proj(mla_out, mem_out_b, b_w_out[0].astype(BF16), x1, b_post_norm)
    return out.reshape(batch, seq, D_MODEL)
```

```python
import functools

import numpy as np
import jax
import jax.numpy as jnp
from jax import lax
from jax.experimental import pallas as pl
from jax.experimental.pallas import tpu as pltpu

F32 = jnp.float32
BF16 = jnp.bfloat16

D_MODEL = 2048
N_MEM = 256
MEM_HEADS = 4
MEM_HEAD_DIM = 128
MEM_WIDTH = MEM_HEADS * MEM_HEAD_DIM
BRANCH_WIDTH = D_MODEL - MEM_WIDTH
GLA_HEADS = 4
GLA_DV = BRANCH_WIDTH // GLA_HEADS
GLA_DK = GLA_DV // 2
GLA_GATE_RANK = 16
GLA_GATE_NORM = 16.0
MLA_HEADS = 12
MLA_V_DIM = 128
MLA_NOPE_DIM = 128
MLA_ROPE_DIM = 64
MLA_QK_DIM = MLA_NOPE_DIM + MLA_ROPE_DIM
MLA_Q_RANK = 512
MLA_KV_RANK = 512
ROPE_THETA = 10000.0
EPS = 1e-6

LANES = 128
MXU_DIM = 256
GLA_DK_PAD = MXU_DIM
MLA_QK_PAD = MXU_DIM
BF16_SUBLANES = 16
MLA_VT_ROWS = MLA_V_DIM + BF16_SUBLANES
ATT_HEADS_PER_STEP = 2
LOG2_E = 1.4426950408889634
LN_2 = 0.6931471805599453
VMEM_LIMIT = 56 * 1024 * 1024

PROJ_TM = 1024
PROJ_NORM_ROWS = 256
REPACK_ROWS = 256
OUT_TM = 512
GLA_SPAN = 256
GLA_SUB = 32
GLA_HEADS_PER_STEP = 4
GLA_STAGE_SKEW = 1
MEM_TM = 512
PREP_TM = 512
ATT_TQ = 512
ATT_TK = 512

A_V_OFF = 0
A_Z_OFF = BRANCH_WIDTH
A_MQ_OFF = 2 * BRANCH_WIDTH
A_MZ_OFF = A_MQ_OFF + MEM_WIDTH
A_Q_OFF = A_MZ_OFF + MEM_WIDTH
A_K_OFF = A_Q_OFF + GLA_HEADS * GLA_DK_PAD
A_WIDTH = A_K_OFF + GLA_HEADS * GLA_DK_PAD
A_TN = 1024

B_Z_OFF = 0
B_CQ_OFF = BRANCH_WIDTH
B_MQ_OFF = B_CQ_OFF + MLA_Q_RANK
B_MZ_OFF = B_MQ_OFF + MEM_WIDTH
B_C_OFF = B_MZ_OFF + MEM_WIDTH
B_KR_OFF = B_C_OFF + MLA_KV_RANK
B_KROT_OFF = B_KR_OFF + LANES
B_WIDTH = B_KROT_OFF + LANES
B_TN = 768


def _silu(z):
    return z * (1.0 / (1.0 + jnp.exp(-z)))


def _scaled_log_sigmoid(g, scale):
    softplus2 = jnp.log2(1.0 + jnp.exp2(jnp.abs(g) * -LOG2_E))
    return jnp.minimum(g, 0.0) * scale - softplus2 * (LN_2 * scale)


def _dot(a, b):
    return jnp.dot(a, b, preferred_element_type=F32)


def _dot_nt(a, b):
    return lax.dot_general(a, b, (((1,), (1,)), ((), ())), preferred_element_type=F32)


def _dot_tn(a, b):
    return lax.dot_general(a, b, (((0,), (0,)), ((), ())), preferred_element_type=F32)


def _norm_matmul_kernel(x_ref, w_ref, o_ref, h_ref, r_ref, *, norm_rows, w_transposed):
    @pl.when(pl.program_id(1) == 0)
    def _():
        tm = x_ref.shape[0]
        for r in range(tm // norm_rows):
            rows = pl.ds(r * norm_rows, norm_rows)
            x = x_ref[rows, :]
            r_ref[rows, :] = lax.rsqrt(jnp.mean(x * x, axis=-1, keepdims=True) + EPS)
            h_ref[rows, :] = x.astype(BF16)

    acc = _dot_nt(h_ref[...], w_ref[...]) if w_transposed else _dot(h_ref[...], w_ref[...])
    o_ref[...] = (acc * r_ref[...]).astype(o_ref.dtype)


def _norm_matmul(x, w, *, tm, tn, w_transposed=False):
    m, d = x.shape
    n = w.shape[0] if w_transposed else w.shape[1]
    tm = min(tm, m)
    kern = functools.partial(_norm_matmul_kernel, norm_rows=min(PROJ_NORM_ROWS, tm),
                             w_transposed=w_transposed)
    w_spec = (pl.BlockSpec((tn, d), lambda i, j: (j, 0)) if w_transposed
              else pl.BlockSpec((d, tn), lambda i, j: (0, j)))
    return pl.pallas_call(
        kern,
        grid=(m // tm, n // tn),
        in_specs=[
            pl.BlockSpec((tm, d), lambda i, j: (i, 0)),
            w_spec,
        ],
        out_specs=pl.BlockSpec((tm, tn), lambda i, j: (i, j)),
        out_shape=jax.ShapeDtypeStruct((m, n), BF16),
        scratch_shapes=[pltpu.VMEM((tm, d), BF16), pltpu.VMEM((tm, 1), F32)],
        compiler_params=pltpu.CompilerParams(
            dimension_semantics=("parallel", "arbitrary"), vmem_limit_bytes=VMEM_LIMIT),
        name="norm_matmul",
    )(x, w)


def _out_proj_kernel(a_ref, m_ref, wa_ref, wm_ref, x_ref, g_ref, o_ref):
    y = _dot(a_ref[...], wa_ref[...]) + _dot(m_ref[...], wm_ref[...])
    ms = jnp.mean(y * y, axis=-1, keepdims=True)
    o_ref[...] = x_ref[...] + y * lax.rsqrt(ms + EPS) * g_ref[...]


def _out_proj(a, mo, w_out, x, gain):
    t = x.shape[0]
    tm = min(OUT_TM, t)
    mem_blk = BRANCH_WIDTH // MEM_WIDTH
    return pl.pallas_call(
        _out_proj_kernel,
        grid=(t // tm,),
        in_specs=[
            pl.BlockSpec((tm, BRANCH_WIDTH), lambda i: (i, 0)),
            pl.BlockSpec((tm, MEM_WIDTH), lambda i: (i, 0)),
            pl.BlockSpec((BRANCH_WIDTH, D_MODEL), lambda i: (0, 0)),
            pl.BlockSpec((MEM_WIDTH, D_MODEL), lambda i: (mem_blk, 0)),
            pl.BlockSpec((tm, D_MODEL), lambda i: (i, 0)),
            pl.BlockSpec((1, D_MODEL), lambda i: (0, 0)),
        ],
        out_specs=pl.BlockSpec((tm, D_MODEL), lambda i: (i, 0)),
        out_shape=jax.ShapeDtypeStruct((t, D_MODEL), F32),
        compiler_params=pltpu.CompilerParams(
            dimension_semantics=("parallel",), vmem_limit_bytes=VMEM_LIMIT),
        name="out_proj",
    )(a, mo, w_out, w_out, x, gain)


def _gla_head_stages(h, v_ref, z_ref, q_ref, k_ref, wg_ref, bg_ref, gn_ref, tril_ref, lvl_ref, o_ref,
                     s_ref, sub):
    span = q_ref.shape[0]
    n_sub = span // sub
    n_levels = n_sub.bit_length() - 1
    scale = GLA_DK ** -0.5
    kcols = slice(h * GLA_DK_PAD, (h + 1) * GLA_DK_PAD)
    vcols = slice(h * GLA_DV, (h + 1) * GLA_DV)

    qb = q_ref[:, kcols]
    g = _dot(qb, wg_ref[h]) + bg_ref[h]
    yield
    log_a = _scaled_log_sigmoid(g, 1.0 / GLA_GATE_NORM)
    a_hi = log_a.astype(BF16)
    rem = log_a - a_hi.astype(F32)
    a_mid = rem.astype(BF16)
    a_lo = (rem - a_mid.astype(F32)).astype(BF16)
    yield
    tril = tril_ref[...]
    cum = _dot(tril, a_hi) + _dot(tril, a_mid) + _dot(tril, a_lo)
    yield

    refs = jnp.concatenate([cum[i * sub:i * sub + 1] for i in range(n_sub)] + [cum[span - 1:span]], axis=0)
    own = refs[:n_sub]

    def ref_rows(index_of):
        return jnp.concatenate([refs[index_of(i):index_of(i) + 1] for i in range(n_sub)], axis=0)

    def per_row(f):
        return jnp.concatenate(
            [jnp.broadcast_to(f[i:i + 1], (sub, f.shape[1])) for i in range(n_sub)], axis=0)

    own_b = per_row(own)
    qd = qb.astype(F32) * (jnp.exp(cum - own_b) * scale)
    kd = k_ref[:, kcols].astype(F32) * jnp.exp(own_b - cum)
    yield
    q_ops, k_ops = [qd.astype(BF16)], [kd.astype(BF16)]
    for lvl in range(1, n_levels + 1):
        n = 1 << (lvl - 1)
        block_start = ref_rows(lambda i: (i // n) * n)
        next_start = ref_rows(lambda i: (i // n + 1) * n)
        q_ops.append(q_ops[0] if n == 1 else (qd * per_row(jnp.exp(own - block_start))).astype(BF16))
        k_ops.append((kd * per_row(jnp.exp(next_start - own))).astype(BF16))
    q_state = (qd * per_row(jnp.exp(own))).astype(BF16)
    k_state = (kd * per_row(jnp.exp(refs[n_sub:] - own))).astype(BF16)
    yield
    atts = [_dot_nt(qo, ko) for qo, ko in zip(q_ops, k_ops)]
    yield
    level = lvl_ref[...]
    att = jnp.where(level == n_levels, atts[n_levels], 0.0)
    for lvl in reversed(range(n_levels)):
        att = jnp.where(level == lvl, atts[lvl], att)
    yield
    state = s_ref[h]
    v = v_ref[:, vcols]
    o = _dot(jnp.concatenate([att.astype(BF16), q_state], axis=1),
             jnp.concatenate([v, state.astype(BF16)], axis=0))
    decay_col = jnp.transpose(jnp.exp(cum[span - 8:span, :]))[:, 7:8]
    s_ref[h] = decay_col * state + _dot_tn(k_state, v)
    yield
    on = o * lax.rsqrt(jnp.mean(o * o, axis=-1, keepdims=True) + EPS) * gn_ref[...]
    o_ref[:, vcols] = (on * _silu(z_ref[:, vcols].astype(F32))).astype(o_ref.dtype)
    yield


def _gla_kernel(v_ref, z_ref, q_ref, k_ref, wg_ref, bg_ref, gn_ref, tril_ref, lvl_ref, o_ref, s_ref, *,
                sub, skew):
    @pl.when(pl.program_id(2) == 0)
    def _():
        s_ref[...] = jnp.zeros_like(s_ref)

    heads = [_gla_head_stages(h, v_ref, z_ref, q_ref, k_ref, wg_ref, bg_ref, gn_ref, tril_ref, lvl_ref,
                              o_ref, s_ref, sub) for h in range(s_ref.shape[0])]
    live = list(range(len(heads)))
    step = 0
    while live:
        for i in list(live):
            if step >= i * skew:
                try:
                    next(heads[i])
                except StopIteration:
                    live.remove(i)
        step += 1


def _prefix_matrix(rows):
    r = np.arange(rows)
    return jnp.asarray(r[:, None] >= r[None, :], BF16)


def _level_matrix(rows, sub):
    n_levels = (rows // sub).bit_length() - 1
    t, s = np.arange(rows)[:, None], np.arange(rows)[None, :]
    level = np.full((rows, rows), n_levels + 1, np.int32)
    level[(t // sub == s // sub) & (s <= t)] = 0
    for lvl in range(1, n_levels + 1):
        size = sub << (lvl - 1)
        level[((t // size) % 2 == 1) & (s // size == t // size - 1)] = lvl
    return jnp.asarray(level)


def _gla(proj, wg, bg, gn, batch, seq):
    span = min(GLA_SPAN, seq)
    assert span % GLA_SUB == 0 and (span // GLA_SUB) & (span // GLA_SUB - 1) == 0
    ns = seq // span
    nh = GLA_HEADS_PER_STEP
    kern = functools.partial(_gla_kernel, sub=GLA_SUB, skew=GLA_STAGE_SKEW)
    vw, kw = nh * GLA_DV, nh * GLA_DK_PAD
    v_blk, z_blk = A_V_OFF // vw, A_Z_OFF // vw
    q_blk, k_blk = A_Q_OFF // kw, A_K_OFF // kw
    group = span
    return pl.pallas_call(
        kern,
        grid=(batch, GLA_HEADS // nh, ns),
        in_specs=[
            pl.BlockSpec((span, vw), lambda b, g, s: (b * ns + s, v_blk + g)),
            pl.BlockSpec((span, vw), lambda b, g, s: (b * ns + s, z_blk + g)),
            pl.BlockSpec((span, kw), lambda b, g, s: (b * ns + s, q_blk + g)),
            pl.BlockSpec((span, kw), lambda b, g, s: (b * ns + s, k_blk + g)),
            pl.BlockSpec((nh, GLA_DK_PAD, GLA_DK_PAD), lambda b, g, s: (g, 0, 0)),
            pl.BlockSpec((nh, 1, GLA_DK_PAD), lambda b, g, s: (g, 0, 0)),
            pl.BlockSpec((1, GLA_DV), lambda b, g, s: (0, 0)),
            pl.BlockSpec((group, group), lambda b, g, s: (0, 0)),
            pl.BlockSpec((group, group), lambda b, g, s: (0, 0)),
        ],
        out_specs=pl.BlockSpec((span, vw), lambda b, g, s: (b * ns + s, g)),
        out_shape=jax.ShapeDtypeStruct((batch * seq, BRANCH_WIDTH), BF16),
        scratch_shapes=[pltpu.VMEM((nh, GLA_DK_PAD, GLA_DV), F32)],
        compiler_params=pltpu.CompilerParams(
            dimension_semantics=("parallel", "parallel", "arbitrary"), vmem_limit_bytes=VMEM_LIMIT),
        name="gla",
    )(proj, proj, proj, proj, wg, bg, gn, _prefix_matrix(group), _level_matrix(group, GLA_SUB))


def _mem_attn_kernel(q_ref, z_ref, k_ref, v_ref, o_ref):
    scale = MEM_HEAD_DIM ** -0.5
    for h in range(MEM_HEADS):
        cols = slice(h * MEM_HEAD_DIM, (h + 1) * MEM_HEAD_DIM)
        s = _dot_nt(q_ref[:, cols], k_ref[:, cols]) * scale
        p = jnp.exp(s - jnp.max(s, axis=-1, keepdims=True))
        l = jnp.sum(p, axis=-1, keepdims=True)
        o = _dot(p.astype(BF16), v_ref[:, cols]) * (1.0 / l)
        o_ref[:, cols] = (o * _silu(z_ref[:, cols].astype(F32))).astype(o_ref.dtype)


def _mem_attn(proj, memkv, seq, q_off, z_off, kv_blk):
    t = proj.shape[0]
    tm = min(MEM_TM, seq)
    per_batch = seq // tm
    q_blk, z_blk = q_off // MEM_WIDTH, z_off // MEM_WIDTH
    return pl.pallas_call(
        _mem_attn_kernel,
        grid=(t // tm,),
        in_specs=[
            pl.BlockSpec((tm, MEM_WIDTH), lambda i: (i, q_blk)),
            pl.BlockSpec((tm, MEM_WIDTH), lambda i: (i, z_blk)),
            pl.BlockSpec((N_MEM, MEM_WIDTH), lambda i: (i // per_batch, kv_blk)),
            pl.BlockSpec((N_MEM, MEM_WIDTH), lambda i: (i // per_batch, kv_blk + 1)),
        ],
        out_specs=pl.BlockSpec((tm, MEM_WIDTH), lambda i: (i, 0)),
        out_shape=jax.ShapeDtypeStruct((t, MEM_WIDTH), BF16),
        compiler_params=pltpu.CompilerParams(
            dimension_semantics=("parallel",), vmem_limit_bytes=VMEM_LIMIT),
        name="mem_attn",
    )(proj, proj, memkv, memkv)


def _mla_prep_kernel(cq_ref, c_ref, kr_ref, krot_ref, pos_ref, freq_ref, sign_ref, qg_ref, kvg_ref,
                     wuq_ref, wuk_ref, wuvt_ref, q_ref, k_ref, vt_ref, *, tk):
    tm = cq_ref.shape[0]
    n_nope = MLA_HEADS * MLA_NOPE_DIM
    n_rope = MLA_HEADS * MLA_ROPE_DIM

    cq = cq_ref[...].astype(F32)
    cqn = cq * lax.rsqrt(jnp.mean(cq * cq, axis=-1, keepdims=True) + EPS)
    cqn = (cqn * (qg_ref[...] * (MLA_QK_DIM ** -0.5 * LOG2_E))).astype(BF16)
    c = c_ref[...].astype(F32)
    cn = c * lax.rsqrt(jnp.mean(c * c, axis=-1, keepdims=True) + EPS)
    cn = (cn * kvg_ref[...]).astype(BF16)

    q_nope = _dot(cqn, wuq_ref[:, :n_nope])
    for h in range(MLA_HEADS):
        q_ref[h, :, :MLA_NOPE_DIM] = q_nope[:, h * MLA_NOPE_DIM:(h + 1) * MLA_NOPE_DIM].astype(BF16)

    ang = pos_ref[...].astype(F32) * freq_ref[...]
    cos = jnp.cos(ang)
    sin = jnp.sin(ang) * sign_ref[...]

    k_nope = _dot(cn, wuk_ref[...])
    for h in range(MLA_HEADS):
        k_ref[h, :, :MLA_NOPE_DIM] = k_nope[:, h * MLA_NOPE_DIM:(h + 1) * MLA_NOPE_DIM].astype(BF16)
    v_t = _dot_nt(wuvt_ref[...], cn).astype(BF16)
    ones_rows = (lax.broadcasted_iota(jnp.int32, (MLA_VT_ROWS - MLA_V_DIM, tk), 0) == 0).astype(BF16)
    for h in range(MLA_HEADS):
        for u in range(tm // tk):
            vt_ref[h, u, :MLA_V_DIM, :] = v_t[h * MLA_V_DIM:(h + 1) * MLA_V_DIM, u * tk:(u + 1) * tk]
            vt_ref[h, u, MLA_V_DIM:, :] = ones_rows

    lane = lax.broadcasted_iota(jnp.int32, (1, LANES), 1)
    slot_mask = [(lane < MLA_ROPE_DIM).astype(F32), (lane >= MLA_ROPE_DIM).astype(F32)]
    q_r = _dot(cqn, wuq_ref[:, n_nope:n_nope + n_rope])
    q_rot = _dot(cqn, wuq_ref[:, n_nope + n_rope:])
    for u in range(MLA_HEADS // 2):
        cols = slice(u * LANES, (u + 1) * LANES)
        roped = q_r[:, cols] * cos + q_rot[:, cols] * sin
        for half in range(2):
            q_ref[2 * u + half, :, MLA_NOPE_DIM:] = (roped * slot_mask[half]).astype(BF16)
    k_rope = (kr_ref[...].astype(F32) * cos + krot_ref[...].astype(F32) * sin).astype(BF16)
    for h in range(MLA_HEADS):
        k_ref[h, :, MLA_NOPE_DIM:] = k_rope


def _mla_prep(proj, pos_col, freq, sign, q_gain, kv_gain, w_uq, w_uk, w_uv_t, batch, seq, tk):
    tm = min(PREP_TM, seq)
    per_batch = seq // tm
    kern = functools.partial(_mla_prep_kernel, tk=tk)
    const = lambda i: (0, 0)
    head_map = lambda i: (i // per_batch, 0, i % per_batch, 0)
    return pl.pallas_call(
        kern,
        grid=(batch * per_batch,),
        in_specs=[
            pl.BlockSpec((tm, MLA_Q_RANK), lambda i: (i, B_CQ_OFF // MLA_Q_RANK)),
            pl.BlockSpec((tm, MLA_KV_RANK), lambda i: (i, B_C_OFF // MLA_KV_RANK)),
            pl.BlockSpec((tm, LANES), lambda i: (i, B_KR_OFF // LANES)),
            pl.BlockSpec((tm, LANES), lambda i: (i, B_KROT_OFF // LANES)),
            pl.BlockSpec((tm, 1), lambda i: (i, 0)),
            pl.BlockSpec((1, LANES), const),
            pl.BlockSpec((1, LANES), const),
            pl.BlockSpec((1, MLA_Q_RANK), const),
            pl.BlockSpec((1, MLA_KV_RANK), const),
            pl.BlockSpec(w_uq.shape, const),
            pl.BlockSpec(w_uk.shape, const),
            pl.BlockSpec(w_uv_t.shape, const),
        ],
        out_specs=[
            pl.BlockSpec((None, MLA_HEADS, tm, MLA_QK_PAD), head_map),
            pl.BlockSpec((None, MLA_HEADS, tm, MLA_QK_PAD), head_map),
            pl.BlockSpec((None, MLA_HEADS, tm // tk, MLA_VT_ROWS, tk),
                         lambda i: (i // per_batch, 0, i % per_batch, 0, 0)),
        ],
        out_shape=[
            jax.ShapeDtypeStruct((batch, MLA_HEADS, seq, MLA_QK_PAD), BF16),
            jax.ShapeDtypeStruct((batch, MLA_HEADS, seq, MLA_QK_PAD), BF16),
            jax.ShapeDtypeStruct((batch, MLA_HEADS, seq // tk, MLA_VT_ROWS, tk), BF16),
        ],
        compiler_params=pltpu.CompilerParams(
            dimension_semantics=("parallel",), vmem_limit_bytes=VMEM_LIMIT),
        name="mla_prep",
    )(proj, proj, proj, proj, pos_col, freq, sign, q_gain, kv_gain, w_uq, w_uk, w_uv_t)


def _max_over_rows(x, ways=8):
    rows = x.shape[0]
    slab = rows // ways
    parts = [x[i * slab:(i + 1) * slab] for i in range(ways)] if slab >= 8 and rows % ways == 0 else [x]
    while len(parts) > 1:
        parts = [jnp.maximum(parts[i], parts[i + 1]) for i in range(0, len(parts), 2)]
    return jnp.max(parts[0], axis=0, keepdims=True)


def _mla_attn_kernel(q_ref, k_ref, vt_ref, z_ref, o_ref, acc_ref, m_ref, s0_ref, s1_ref, smax0_ref,
                     smax1_ref, *, tq, tk):
    n_heads, seq = q_ref.shape[0], q_ref.shape[1]
    tn = MXU_DIM
    units = [(h, n) for h in range(n_heads) for n in range(tq // tn)]
    s_slots = (s0_ref, s1_ref)
    smax_slots = (smax0_ref, smax1_ref)

    def n_keys(n, diagonal):
        return min(tk, (n + 1) * tn) if diagonal else tk

    def produce(qi, kj, slot, u, diagonal):
        h, n = units[u]
        nk = n_keys(n, diagonal)
        krows = pl.ds(pl.multiple_of(kj * tk, tk), nk)
        qrows = pl.ds(pl.multiple_of(qi * tq + n * tn, tn), tn)
        s_t = _dot_nt(k_ref[h, krows, :], q_ref[h, qrows, :])
        if diagonal:
            kpos = lax.broadcasted_iota(jnp.int32, (nk, tn), 0)
            qpos = lax.broadcasted_iota(jnp.int32, (nk, tn), 1) + n * tn
            s_t = jnp.where(kpos <= qpos, s_t, -jnp.inf)
        s_slots[slot][u, :nk, :] = s_t
        smax_slots[slot][u] = _max_over_rows(s_t)

    def consume(kj, slot, u, diagonal):
        h, n = units[u]
        nk = n_keys(n, diagonal)
        cols = slice(n * tn, (n + 1) * tn)
        m_prev = m_ref[h, :, cols]
        m_new = jnp.maximum(m_prev, smax_slots[slot][u])
        alpha = jnp.exp2(m_prev - m_new)
        p = jnp.exp2(s_slots[slot][u, :nk, :] - m_new).astype(BF16)
        m_ref[h, :, cols] = m_new
        acc_ref[h, :, cols] = alpha * acc_ref[h, :, cols] + _dot(vt_ref[h, kj, :, :nk], p)

    def produce_all(qi, kj, slot, diagonal):
        for u in range(len(units)):
            produce(qi, kj, slot, u, diagonal)

    def consume_all(kj, slot, diagonal):
        for u in range(len(units)):
            consume(kj, slot, u, diagonal)

    def consume_and_produce(qi, kj, slot, next_diagonal):
        produce(qi, kj + 1, 1 - slot, 0, next_diagonal)
        for u in range(len(units)):
            consume(kj, slot, u, False)
            if u + 1 < len(units):
                produce(qi, kj + 1, 1 - slot, u + 1, next_diagonal)

    def q_block(qi, carry):
        rows = pl.ds(pl.multiple_of(qi * tq, tq), tq)
        m_ref[...] = jnp.full_like(m_ref, -jnp.inf)
        acc_ref[...] = jnp.zeros_like(acc_ref)

        @pl.when(qi == 0)
        def _():
            produce_all(qi, 0, 0, True)
            consume_all(0, 0, True)

        @pl.when(qi > 0)
        def _():
            produce_all(qi, 0, 0, False)

        def steady_pair(i, c):
            consume_and_produce(qi, 2 * i, 0, False)
            consume_and_produce(qi, 2 * i + 1, 1, False)
            return c

        n_pairs = jnp.maximum(qi - 1, 0) // 2
        lax.fori_loop(0, n_pairs, steady_pair, 0)
        kj = 2 * n_pairs

        @pl.when(qi - kj == 1)
        def _():
            consume_and_produce(qi, kj, 0, True)
            consume_all(qi, 1, True)

        @pl.when(qi - kj == 2)
        def _():
            consume_and_produce(qi, kj, 0, False)
            consume_and_produce(qi, kj + 1, 1, True)
            consume_all(qi, 0, True)

        for h in range(n_heads):
            acc = acc_ref[h]
            o = jnp.transpose(acc[:MLA_V_DIM] * (1.0 / acc[MLA_V_DIM:MLA_V_DIM + 1]))
            cols = slice(h * MLA_V_DIM, (h + 1) * MLA_V_DIM)
            o_ref[rows, cols] = (o * _silu(z_ref[rows, cols].astype(F32))).astype(o_ref.dtype)
        return carry

    lax.fori_loop(0, seq // tq, q_block, 0)


def _mla_attn(q_cat, k_cat, v_t, proj, batch, seq, tq, tk):
    kern = functools.partial(_mla_attn_kernel, tq=tq, tk=tk)
    nb = seq // tk
    nh = ATT_HEADS_PER_STEP
    n_units = nh * (tq // MXU_DIM)
    width = nh * MLA_V_DIM
    z_blk = B_Z_OFF // width
    return pl.pallas_call(
        kern,
        grid=(batch, MLA_HEADS // nh),
        in_specs=[
            pl.BlockSpec((None, nh, seq, MLA_QK_PAD), lambda b, g: (b, g, 0, 0)),
            pl.BlockSpec((None, nh, seq, MLA_QK_PAD), lambda b, g: (b, g, 0, 0)),
            pl.BlockSpec((None, nh, nb, MLA_VT_ROWS, tk), lambda b, g: (b, g, 0, 0, 0)),
            pl.BlockSpec((seq, width), lambda b, g: (b, z_blk + g)),
        ],
        out_specs=pl.BlockSpec((seq, width), lambda b, g: (b, g)),
        out_shape=jax.ShapeDtypeStruct((batch * seq, BRANCH_WIDTH), BF16),
        scratch_shapes=[
            pltpu.VMEM((nh, MLA_VT_ROWS, tq), F32),
            pltpu.VMEM((nh, 1, tq), F32),
            pltpu.VMEM((n_units, tk, MXU_DIM), F32),
            pltpu.VMEM((n_units, tk, MXU_DIM), F32),
            pltpu.VMEM((n_units, 1, MXU_DIM), F32),
            pltpu.VMEM((n_units, 1, MXU_DIM), F32),
        ],
        compiler_params=pltpu.CompilerParams(
            dimension_semantics=("parallel", "parallel"), vmem_limit_bytes=VMEM_LIMIT),
        name="mla_attn",
    )(q_cat, k_cat, v_t, proj)


def _pad_heads(w, heads, width, padded):
    lead = w.shape[:-1]
    w = w.reshape(lead + (heads, width))
    w = jnp.pad(w, [(0, 0)] * len(lead) + [(0, 0), (0, padded - width)])
    return w.reshape(lead + (heads * padded,))


def _repack_kernel(*refs, n_in, with_gain, pieces, zero_ranges, transposed):
    ins = refs[:n_in]
    gains = refs[n_in:2 * n_in] if with_gain else None
    o_ref = refs[-1]

    def window(lo, width):
        return (slice(lo, lo + width), slice(None)) if transposed else (slice(None), slice(lo, lo + width))

    for dst, idx, src, width in pieces:
        val = ins[idx][window(src, width)]
        if with_gain:
            val = val * gains[idx][...]
        o_ref[window(dst, width)] = val.astype(o_ref.dtype)
    for lo, hi in zero_ranges:
        shape = (hi - lo, o_ref.shape[1]) if transposed else (o_ref.shape[0], hi - lo)
        o_ref[window(lo, hi - lo)] = jnp.zeros(shape, o_ref.dtype)


def _repack(ws, gains, pieces, out_width, zero_ranges=(), transposed=False):
    d = ws[0].shape[1] if transposed else ws[0].shape[0]
    td = min(REPACK_ROWS, d)
    with_gain = gains is not None
    covered = sorted([(dst, dst + w) for dst, _, _, w in pieces] + list(zero_ranges))
    assert covered[0][0] == 0 and covered[-1][1] == out_width
    assert all(a[1] == b[0] for a, b in zip(covered, covered[1:])), "output features written exactly once"
    kern = functools.partial(_repack_kernel, n_in=len(ws), with_gain=with_gain, pieces=tuple(pieces),
                             zero_ranges=tuple(zero_ranges), transposed=transposed)

    def spec(n):
        return (pl.BlockSpec((n, td), lambda i: (0, i)) if transposed
                else pl.BlockSpec((td, n), lambda i: (i, 0)))

    in_specs = [spec(w.shape[0] if transposed else w.shape[1]) for w in ws]
    args = list(ws)
    if with_gain:
        in_specs += [spec(1) for _ in gains]
        args += [g.reshape((1, d) if transposed else (d, 1)) for g in gains]
    return pl.pallas_call(
        kern,
        grid=(d // td,),
        in_specs=in_specs,
        out_specs=spec(out_width),
        out_shape=jax.ShapeDtypeStruct((out_width, d) if transposed else (d, out_width), BF16),
        compiler_params=pltpu.CompilerParams(
            dimension_semantics=("parallel",), vmem_limit_bytes=VMEM_LIMIT),
        name="repack",
    )(*args)


def _layer_a_weights(w_in, gain, w_g2, b_g):
    hk = GLA_HEADS * GLA_DK
    o = [int(v) for v in np.cumsum([0, hk, hk, BRANCH_WIDTH, GLA_GATE_RANK, BRANCH_WIDTH, MEM_WIDTH,
                                    MEM_WIDTH])]
    q_src, k_src, v_src, lr_src, z_src = o[0], o[1], o[2], o[3], o[4]
    pieces = [(A_V_OFF, 0, v_src, BRANCH_WIDTH), (A_Z_OFF, 0, z_src, BRANCH_WIDTH + 2 * MEM_WIDTH)]
    zeros = []
    for h in range(GLA_HEADS):
        q_dst, k_dst = A_Q_OFF + h * GLA_DK_PAD, A_K_OFF + h * GLA_DK_PAD
        pieces += [(q_dst, 0, q_src + h * GLA_DK, GLA_DK), (q_dst + GLA_DK, 0, lr_src, GLA_GATE_RANK),
                   (k_dst, 0, k_src + h * GLA_DK, GLA_DK)]
        zeros += [(q_dst + GLA_DK + GLA_GATE_RANK, q_dst + GLA_DK_PAD), (k_dst + GLA_DK, k_dst + GLA_DK_PAD)]
    w_all_t = _repack([w_in.T], [gain], pieces, A_WIDTH, zeros, transposed=True)
    wg = _pad_heads(w_g2, GLA_HEADS, GLA_DK, GLA_DK_PAD)
    wg = wg.reshape(GLA_GATE_RANK, GLA_HEADS, GLA_DK_PAD).transpose(1, 0, 2)
    wg = jnp.pad(wg, [(0, 0), (GLA_DK, GLA_DK_PAD - GLA_DK - GLA_GATE_RANK), (0, 0)]).astype(BF16)
    bg = _pad_heads(b_g, GLA_HEADS, GLA_DK, GLA_DK_PAD).reshape(GLA_HEADS, 1, GLA_DK_PAD)
    return w_all_t, wg, bg


def _layer_b_weights(w_in, gain_in, w_dkv, gain_dkv):
    half = MLA_ROPE_DIM // 2
    pieces = [(B_Z_OFF, 0, MLA_Q_RANK, BRANCH_WIDTH), (B_CQ_OFF, 0, 0, MLA_Q_RANK),
              (B_MQ_OFF, 0, MLA_Q_RANK + BRANCH_WIDTH, 2 * MEM_WIDTH), (B_C_OFF, 1, 0, MLA_KV_RANK)]
    for rep in range(LANES // MLA_ROPE_DIM):
        kr, krot = B_KR_OFF + rep * MLA_ROPE_DIM, B_KROT_OFF + rep * MLA_ROPE_DIM
        pieces += [(kr, 1, MLA_KV_RANK, MLA_ROPE_DIM),
                   (krot, 1, MLA_KV_RANK + half, half), (krot + half, 1, MLA_KV_RANK, half)]
    return _repack([w_in, w_dkv], [gain_in, gain_dkv], pieces, B_WIDTH)


def _uq_weights(w_uq):
    half = MLA_ROPE_DIM // 2
    n_nope, n_rope = MLA_HEADS * MLA_NOPE_DIM, MLA_HEADS * MLA_ROPE_DIM
    pieces = []
    for h in range(MLA_HEADS):
        src = h * MLA_QK_DIM
        rope, rot = n_nope + h * MLA_ROPE_DIM, n_nope + n_rope + h * MLA_ROPE_DIM
        pieces += [(h * MLA_NOPE_DIM, 0, src, MLA_NOPE_DIM), (rope, 0, src + MLA_NOPE_DIM, MLA_ROPE_DIM),
                   (rot, 0, src + MLA_NOPE_DIM + half, half), (rot + half, 0, src + MLA_NOPE_DIM, half)]
    return _repack([w_uq], None, pieces, n_nope + 2 * n_rope)


def _rope_tables():
    r = MLA_ROPE_DIM
    freqs = ROPE_THETA ** (-jnp.arange(0, r, 2, dtype=F32) / r)
    freq = jnp.tile(freqs, LANES // (r // 2)).reshape(1, LANES)
    sign = jnp.tile(jnp.concatenate([-jnp.ones(r // 2, F32), jnp.ones(r // 2, F32)]),
                    LANES // r).reshape(1, LANES)
    return freq, sign


def kernel(x, mem, positions, a_pre_norm, a_w_in, a_w_g2, a_b_g, a_gla_norm, a_mem_norm, a_w_mem_kv,
           a_w_out, a_post_norm, kv_in_norm, w_dkv, kv_norm, w_uk, w_uv, b_pre_norm, b_w_in, b_q_norm,
           b_w_uq, b_mem_norm, b_w_mem_kv, b_w_out, b_post_norm):
    assert a_w_in.shape[0] == 1 and b_w_in.shape[0] == 1, "one A layer followed by one B layer"
    batch, seq, _ = x.shape
    t = batch * seq
    x2 = x.reshape(t, D_MODEL)

    kv_w = 2 * MEM_WIDTH
    w_memkv = _repack([a_w_mem_kv[0], b_w_mem_kv[0]], [a_mem_norm[0], b_mem_norm[0]],
                      [(0, 0, 0, kv_w), (kv_w, 1, 0, kv_w)], 2 * kv_w)
    memkv = _norm_matmul(mem.reshape(batch * N_MEM, D_MODEL), w_memkv, tm=batch * N_MEM, tn=2 * MEM_WIDTH)

    w_a_t, wg, bg = _layer_a_weights(a_w_in[0], a_pre_norm[0], a_w_g2[0], a_b_g[0])
    proj_a = _norm_matmul(x2, w_a_t, tm=PROJ_TM, tn=A_TN, w_transposed=True)
    gla_out = _gla(proj_a, wg, bg, a_gla_norm, batch, seq)
    mem_out = _mem_attn(proj_a, memkv, seq, A_MQ_OFF, A_MZ_OFF, kv_blk=0)
    x1 = _out_proj(gla_out, mem_out, a_w_out[0].astype(BF16), x2, a_post_norm)

    w_b = _layer_b_weights(b_w_in[0], b_pre_norm[0], w_dkv, kv_in_norm)
    proj_b = _norm_matmul(x1, w_b, tm=PROJ_TM, tn=B_TN)
    freq, sign = _rope_tables()
    tk = min(ATT_TK, seq)
    tq = min(ATT_TQ, seq)
    q_cat, k_cat, v_t = _mla_prep(
        proj_b, positions.reshape(t, 1), freq, sign, b_q_norm, kv_norm[None, :],
        _uq_weights(b_w_uq[0]), w_uk.astype(BF16), w_uv.T.astype(BF16), batch, seq, tk)
    mla_out = _mla_attn(q_cat, k_cat, v_t, proj_b, batch, seq, tq, tk)
    mem_out_b = _mem_attn(proj_b, memkv, seq, B_MQ_OFF, B_MZ_OFF, kv_blk=2)
    out = _out_proj(mla_out, mem_out_b, b_w_out[0].astype(BF16), x1, b_post_norm)
    return out.reshape(batch, seq, D_MODEL)
```

```python
import functools

import numpy as np
import jax
import jax.numpy as jnp
from jax import lax
from jax.experimental import pallas as pl
from jax.experimental.pallas import tpu as pltpu

F32 = jnp.float32
BF16 = jnp.bfloat16

D_MODEL = 2048
N_MEM = 256
MEM_HEADS = 4
MEM_HEAD_DIM = 128
MEM_WIDTH = MEM_HEADS * MEM_HEAD_DIM
BRANCH_WIDTH = D_MODEL - MEM_WIDTH
GLA_HEADS = 4
GLA_DV = BRANCH_WIDTH // GLA_HEADS
GLA_DK = GLA_DV // 2
GLA_GATE_RANK = 16
GLA_GATE_NORM = 16.0
MLA_HEADS = 12
MLA_V_DIM = 128
MLA_NOPE_DIM = 128
MLA_ROPE_DIM = 64
MLA_QK_DIM = MLA_NOPE_DIM + MLA_ROPE_DIM
MLA_Q_RANK = 512
MLA_KV_RANK = 512
ROPE_THETA = 10000.0
EPS = 1e-6

LANES = 128
MXU_DIM = 256
GLA_DK_PAD = MXU_DIM
MLA_QK_PAD = MXU_DIM
BF16_SUBLANES = 16
MLA_VT_ROWS = MLA_V_DIM + BF16_SUBLANES
ATT_HEADS_PER_STEP = 2
LOG2_E = 1.4426950408889634
LN_2 = 0.6931471805599453
VMEM_LIMIT = 56 * 1024 * 1024

PROJ_TM = 1024
PROJ_NORM_ROWS = 256
REPACK_ROWS = 256
OUT_TM = 512
OUT_CHUNK = 256
GLA_SPAN = 256
GLA_SUB = 32
GLA_HEADS_PER_STEP = 4
GLA_STAGE_SKEW = 1
MEM_TM = 512
PREP_TM = 512
ATT_TQ = 512
ATT_TK = 512

A_V_OFF = 0
A_Z_OFF = BRANCH_WIDTH
A_MQ_OFF = 2 * BRANCH_WIDTH
A_MZ_OFF = A_MQ_OFF + MEM_WIDTH
A_Q_OFF = A_MZ_OFF + MEM_WIDTH
A_K_OFF = A_Q_OFF + GLA_HEADS * GLA_DK_PAD
A_WIDTH = A_K_OFF + GLA_HEADS * GLA_DK_PAD
A_TN = 1024

B_Z_OFF = 0
B_CQ_OFF = BRANCH_WIDTH
B_MQ_OFF = B_CQ_OFF + MLA_Q_RANK
B_MZ_OFF = B_MQ_OFF + MEM_WIDTH
B_C_OFF = B_MZ_OFF + MEM_WIDTH
B_KR_OFF = B_C_OFF + MLA_KV_RANK
B_KROT_OFF = B_KR_OFF + LANES
B_WIDTH = B_KROT_OFF + LANES
B_TN = 768


def _silu(z):
    return z * (1.0 / (1.0 + jnp.exp(-z)))


def _scaled_log_sigmoid(g, scale):
    softplus2 = jnp.log2(1.0 + jnp.exp2(jnp.abs(g) * -LOG2_E))
    return jnp.minimum(g, 0.0) * scale - softplus2 * (LN_2 * scale)


def _dot(a, b):
    return jnp.dot(a, b, preferred_element_type=F32)


def _dot_nt(a, b):
    return lax.dot_general(a, b, (((1,), (1,)), ((), ())), preferred_element_type=F32)


def _dot_tn(a, b):
    return lax.dot_general(a, b, (((0,), (0,)), ((), ())), preferred_element_type=F32)


def _norm_matmul_kernel(x_ref, w_ref, o_ref, h_ref, r_ref, *, norm_rows, w_transposed):
    j = pl.program_id(1)

    def scaled_product(h, r):
        acc = _dot_nt(h, w_ref[...]) if w_transposed else _dot(h, w_ref[...])
        return (acc * r).astype(o_ref.dtype)

    @pl.when(j == 0)
    def _():
        tm = x_ref.shape[0]
        for c in range(tm // norm_rows):
            rows = pl.ds(c * norm_rows, norm_rows)
            x = x_ref[rows, :]
            r = lax.rsqrt(jnp.mean(x * x, axis=-1, keepdims=True) + EPS)
            h = x.astype(BF16)
            r_ref[rows, :] = r
            h_ref[rows, :] = h
            o_ref[rows, :] = scaled_product(h, r)

    @pl.when(j > 0)
    def _():
        o_ref[...] = scaled_product(h_ref[...], r_ref[...])


def _norm_matmul(x, w, *, tm, tn, w_transposed=False):
    m, d = x.shape
    n = w.shape[0] if w_transposed else w.shape[1]
    tm = min(tm, m)
    kern = functools.partial(_norm_matmul_kernel, norm_rows=min(PROJ_NORM_ROWS, tm),
                             w_transposed=w_transposed)
    w_spec = (pl.BlockSpec((tn, d), lambda i, j: (j, 0)) if w_transposed
              else pl.BlockSpec((d, tn), lambda i, j: (0, j)))
    return pl.pallas_call(
        kern,
        grid=(m // tm, n // tn),
        in_specs=[
            pl.BlockSpec((tm, d), lambda i, j: (i, 0)),
            w_spec,
        ],
        out_specs=pl.BlockSpec((tm, tn), lambda i, j: (i, j)),
        out_shape=jax.ShapeDtypeStruct((m, n), BF16),
        scratch_shapes=[pltpu.VMEM((tm, d), BF16), pltpu.VMEM((tm, 1), F32)],
        compiler_params=pltpu.CompilerParams(
            dimension_semantics=("parallel", "arbitrary"), vmem_limit_bytes=VMEM_LIMIT),
        name="norm_matmul",
    )(x, w)


def _out_proj_kernel(a_ref, m_ref, wa_ref, wm_ref, x_ref, g_ref, o_ref, *, chunk):
    for c in range(a_ref.shape[0] // chunk):
        rows = pl.ds(c * chunk, chunk)
        y = _dot(a_ref[rows, :], wa_ref[...]) + _dot(m_ref[rows, :], wm_ref[...])
        ms = jnp.mean(y * y, axis=-1, keepdims=True)
        o_ref[rows, :] = x_ref[rows, :] + y * lax.rsqrt(ms + EPS) * g_ref[...]


def _out_proj(a, mo, w_out, x, gain):
    t = x.shape[0]
    tm = min(OUT_TM, t)
    mem_blk = BRANCH_WIDTH // MEM_WIDTH
    return pl.pallas_call(
        functools.partial(_out_proj_kernel, chunk=min(OUT_CHUNK, tm)),
        grid=(t // tm,),
        in_specs=[
            pl.BlockSpec((tm, BRANCH_WIDTH), lambda i: (i, 0)),
            pl.BlockSpec((tm, MEM_WIDTH), lambda i: (i, 0)),
            pl.BlockSpec((BRANCH_WIDTH, D_MODEL), lambda i: (0, 0)),
            pl.BlockSpec((MEM_WIDTH, D_MODEL), lambda i: (mem_blk, 0)),
            pl.BlockSpec((tm, D_MODEL), lambda i: (i, 0)),
            pl.BlockSpec((1, D_MODEL), lambda i: (0, 0)),
        ],
        out_specs=pl.BlockSpec((tm, D_MODEL), lambda i: (i, 0)),
        out_shape=jax.ShapeDtypeStruct((t, D_MODEL), F32),
        compiler_params=pltpu.CompilerParams(
            dimension_semantics=("parallel",), vmem_limit_bytes=VMEM_LIMIT),
        name="out_proj",
    )(a, mo, w_out, w_out, x, gain)


def _gla_head_stages(h, v_ref, z_ref, q_ref, k_ref, wg_ref, bg_ref, gn_ref, tril_ref, lvl_ref, o_ref,
                     s_ref, sub):
    span = q_ref.shape[0]
    n_sub = span // sub
    n_levels = n_sub.bit_length() - 1
    scale = GLA_DK ** -0.5
    kcols = slice(h * GLA_DK_PAD, (h + 1) * GLA_DK_PAD)
    vcols = slice(h * GLA_DV, (h + 1) * GLA_DV)

    qb = q_ref[:, kcols]
    g = _dot(qb, wg_ref[h]) + bg_ref[h]
    yield
    log_a = _scaled_log_sigmoid(g, 1.0 / GLA_GATE_NORM)
    a_hi = log_a.astype(BF16)
    rem = log_a - a_hi.astype(F32)
    a_mid = rem.astype(BF16)
    a_lo = (rem - a_mid.astype(F32)).astype(BF16)
    yield
    tril = tril_ref[...]
    cum = _dot(tril, a_hi) + _dot(tril, a_mid) + _dot(tril, a_lo)
    yield

    refs = jnp.concatenate([cum[i * sub:i * sub + 1] for i in range(n_sub)] + [cum[span - 1:span]], axis=0)
    own = refs[:n_sub]

    def ref_rows(index_of):
        return jnp.concatenate([refs[index_of(i):index_of(i) + 1] for i in range(n_sub)], axis=0)

    def per_row(f):
        return jnp.concatenate(
            [jnp.broadcast_to(f[i:i + 1], (sub, f.shape[1])) for i in range(n_sub)], axis=0)

    own_b = per_row(own)
    qd = qb.astype(F32) * (jnp.exp(cum - own_b) * scale)
    kd = k_ref[:, kcols].astype(F32) * jnp.exp(own_b - cum)
    yield
    q_ops, k_ops = [qd.astype(BF16)], [kd.astype(BF16)]
    for lvl in range(1, n_levels + 1):
        n = 1 << (lvl - 1)
        block_start = ref_rows(lambda i: (i // n) * n)
        next_start = ref_rows(lambda i: (i // n + 1) * n)
        q_ops.append(q_ops[0] if n == 1 else (qd * per_row(jnp.exp(own - block_start))).astype(BF16))
        k_ops.append((kd * per_row(jnp.exp(next_start - own))).astype(BF16))
    q_state = (qd * per_row(jnp.exp(own))).astype(BF16)
    k_state = (kd * per_row(jnp.exp(refs[n_sub:] - own))).astype(BF16)
    yield
    atts = [_dot_nt(qo, ko) for qo, ko in zip(q_ops, k_ops)]
    yield
    level = lvl_ref[...]
    att = jnp.where(level == n_levels, atts[n_levels], 0.0)
    for lvl in reversed(range(n_levels)):
        att = jnp.where(level == lvl, atts[lvl], att)
    yield
    state = s_ref[h]
    v = v_ref[:, vcols]
    o = _dot(jnp.concatenate([att.astype(BF16), q_state], axis=1),
             jnp.concatenate([v, state.astype(BF16)], axis=0))
    decay_col = jnp.transpose(jnp.exp(cum[span - 8:span, :]))[:, 7:8]
    s_ref[h] = decay_col * state + _dot_tn(k_state, v)
    yield
    on = o * lax.rsqrt(jnp.mean(o * o, axis=-1, keepdims=True) + EPS) * gn_ref[...]
    o_ref[:, vcols] = (on * _silu(z_ref[:, vcols].astype(F32))).astype(o_ref.dtype)
    yield


def _gla_kernel(v_ref, z_ref, q_ref, k_ref, wg_ref, bg_ref, gn_ref, tril_ref, lvl_ref, o_ref, s_ref, *,
                sub, skew):
    @pl.when(pl.program_id(2) == 0)
    def _():
        s_ref[...] = jnp.zeros_like(s_ref)

    heads = [_gla_head_stages(h, v_ref, z_ref, q_ref, k_ref, wg_ref, bg_ref, gn_ref, tril_ref, lvl_ref,
                              o_ref, s_ref, sub) for h in range(s_ref.shape[0])]
    live = list(range(len(heads)))
    step = 0
    while live:
        for i in list(live):
            if step >= i * skew:
                try:
                    next(heads[i])
                except StopIteration:
                    live.remove(i)
        step += 1


def _prefix_matrix(rows):
    r = np.arange(rows)
    return jnp.asarray(r[:, None] >= r[None, :], BF16)


def _level_matrix(rows, sub):
    n_levels = (rows // sub).bit_length() - 1
    t, s = np.arange(rows)[:, None], np.arange(rows)[None, :]
    level = np.full((rows, rows), n_levels + 1, np.int32)
    level[(t // sub == s // sub) & (s <= t)] = 0
    for lvl in range(1, n_levels + 1):
        size = sub << (lvl - 1)
        level[((t // size) % 2 == 1) & (s // size == t // size - 1)] = lvl
    return jnp.asarray(level)


def _gla(proj, wg, bg, gn, batch, seq):
    span = min(GLA_SPAN, seq)
    assert span % GLA_SUB == 0 and (span // GLA_SUB) & (span // GLA_SUB - 1) == 0
    ns = seq // span
    nh = GLA_HEADS_PER_STEP
    kern = functools.partial(_gla_kernel, sub=GLA_SUB, skew=GLA_STAGE_SKEW)
    vw, kw = nh * GLA_DV, nh * GLA_DK_PAD
    v_blk, z_blk = A_V_OFF // vw, A_Z_OFF // vw
    q_blk, k_blk = A_Q_OFF // kw, A_K_OFF // kw
    group = span
    return pl.pallas_call(
        kern,
        grid=(batch, GLA_HEADS // nh, ns),
        in_specs=[
            pl.BlockSpec((span, vw), lambda b, g, s: (b * ns + s, v_blk + g)),
            pl.BlockSpec((span, vw), lambda b, g, s: (b * ns + s, z_blk + g)),
            pl.BlockSpec((span, kw), lambda b, g, s: (b * ns + s, q_blk + g)),
            pl.BlockSpec((span, kw), lambda b, g, s: (b * ns + s, k_blk + g)),
            pl.BlockSpec((nh, GLA_DK_PAD, GLA_DK_PAD), lambda b, g, s: (g, 0, 0)),
            pl.BlockSpec((nh, 1, GLA_DK_PAD), lambda b, g, s: (g, 0, 0)),
            pl.BlockSpec((1, GLA_DV), lambda b, g, s: (0, 0)),
            pl.BlockSpec((group, group), lambda b, g, s: (0, 0)),
            pl.BlockSpec((group, group), lambda b, g, s: (0, 0)),
        ],
        out_specs=pl.BlockSpec((span, vw), lambda b, g, s: (b * ns + s, g)),
        out_shape=jax.ShapeDtypeStruct((batch * seq, BRANCH_WIDTH), BF16),
        scratch_shapes=[pltpu.VMEM((nh, GLA_DK_PAD, GLA_DV), F32)],
        compiler_params=pltpu.CompilerParams(
            dimension_semantics=("parallel", "parallel", "arbitrary"), vmem_limit_bytes=VMEM_LIMIT),
        name="gla",
    )(proj, proj, proj, proj, wg, bg, gn, _prefix_matrix(group), _level_matrix(group, GLA_SUB))


def _mem_attn_kernel(q_ref, z_ref, k_ref, v_ref, o_ref):
    scale = MEM_HEAD_DIM ** -0.5
    for h in range(MEM_HEADS):
        cols = slice(h * MEM_HEAD_DIM, (h + 1) * MEM_HEAD_DIM)
        s = _dot_nt(q_ref[:, cols], k_ref[:, cols]) * scale
        p = jnp.exp(s - jnp.max(s, axis=-1, keepdims=True))
        l = jnp.sum(p, axis=-1, keepdims=True)
        o = _dot(p.astype(BF16), v_ref[:, cols]) * (1.0 / l)
        o_ref[:, cols] = (o * _silu(z_ref[:, cols].astype(F32))).astype(o_ref.dtype)


def _mem_attn(proj, memkv, seq, q_off, z_off, kv_blk):
    t = proj.shape[0]
    tm = min(MEM_TM, seq)
    per_batch = seq // tm
    q_blk, z_blk = q_off // MEM_WIDTH, z_off // MEM_WIDTH
    return pl.pallas_call(
        _mem_attn_kernel,
        grid=(t // tm,),
        in_specs=[
            pl.BlockSpec((tm, MEM_WIDTH), lambda i: (i, q_blk)),
            pl.BlockSpec((tm, MEM_WIDTH), lambda i: (i, z_blk)),
            pl.BlockSpec((N_MEM, MEM_WIDTH), lambda i: (i // per_batch, kv_blk)),
            pl.BlockSpec((N_MEM, MEM_WIDTH), lambda i: (i // per_batch, kv_blk + 1)),
        ],
        out_specs=pl.BlockSpec((tm, MEM_WIDTH), lambda i: (i, 0)),
        out_shape=jax.ShapeDtypeStruct((t, MEM_WIDTH), BF16),
        compiler_params=pltpu.CompilerParams(
            dimension_semantics=("parallel",), vmem_limit_bytes=VMEM_LIMIT),
        name="mem_attn",
    )(proj, proj, memkv, memkv)


def _mla_prep_kernel(cq_ref, c_ref, kr_ref, krot_ref, pos_ref, freq_ref, sign_ref, qg_ref, kvg_ref,
                     wuq_ref, wuk_ref, wuvt_ref, q_ref, k_ref, vt_ref, *, tk):
    tm = cq_ref.shape[0]
    n_nope = MLA_HEADS * MLA_NOPE_DIM
    n_rope = MLA_HEADS * MLA_ROPE_DIM

    cq = cq_ref[...].astype(F32)
    cqn = cq * lax.rsqrt(jnp.mean(cq * cq, axis=-1, keepdims=True) + EPS)
    cqn = (cqn * (qg_ref[...] * (MLA_QK_DIM ** -0.5 * LOG2_E))).astype(BF16)
    c = c_ref[...].astype(F32)
    cn = c * lax.rsqrt(jnp.mean(c * c, axis=-1, keepdims=True) + EPS)
    cn = (cn * kvg_ref[...]).astype(BF16)

    q_nope = _dot(cqn, wuq_ref[:, :n_nope])
    for h in range(MLA_HEADS):
        q_ref[h, :, :MLA_NOPE_DIM] = q_nope[:, h * MLA_NOPE_DIM:(h + 1) * MLA_NOPE_DIM].astype(BF16)

    ang = pos_ref[...].astype(F32) * freq_ref[...]
    cos = jnp.cos(ang)
    sin = jnp.sin(ang) * sign_ref[...]

    k_nope = _dot(cn, wuk_ref[...])
    for h in range(MLA_HEADS):
        k_ref[h, :, :MLA_NOPE_DIM] = k_nope[:, h * MLA_NOPE_DIM:(h + 1) * MLA_NOPE_DIM].astype(BF16)
    v_t = _dot_nt(wuvt_ref[...], cn).astype(BF16)
    ones_rows = (lax.broadcasted_iota(jnp.int32, (MLA_VT_ROWS - MLA_V_DIM, tk), 0) == 0).astype(BF16)
    for h in range(MLA_HEADS):
        for u in range(tm // tk):
            vt_ref[h, u, :MLA_V_DIM, :] = v_t[h * MLA_V_DIM:(h + 1) * MLA_V_DIM, u * tk:(u + 1) * tk]
            vt_ref[h, u, MLA_V_DIM:, :] = ones_rows

    lane = lax.broadcasted_iota(jnp.int32, (1, LANES), 1)
    slot_mask = [(lane < MLA_ROPE_DIM).astype(F32), (lane >= MLA_ROPE_DIM).astype(F32)]
    q_r = _dot(cqn, wuq_ref[:, n_nope:n_nope + n_rope])
    q_rot = _dot(cqn, wuq_ref[:, n_nope + n_rope:])
    for u in range(MLA_HEADS // 2):
        cols = slice(u * LANES, (u + 1) * LANES)
        roped = q_r[:, cols] * cos + q_rot[:, cols] * sin
        for half in range(2):
            q_ref[2 * u + half, :, MLA_NOPE_DIM:] = (roped * slot_mask[half]).astype(BF16)
    k_rope = (kr_ref[...].astype(F32) * cos + krot_ref[...].astype(F32) * sin).astype(BF16)
    for h in range(MLA_HEADS):
        k_ref[h, :, MLA_NOPE_DIM:] = k_rope


def _mla_prep(proj, pos_col, freq, sign, q_gain, kv_gain, w_uq, w_uk, w_uv_t, batch, seq, tk):
    tm = min(PREP_TM, seq)
    per_batch = seq // tm
    kern = functools.partial(_mla_prep_kernel, tk=tk)
    const = lambda i: (0, 0)
    head_map = lambda i: (i // per_batch, 0, i % per_batch, 0)
    return pl.pallas_call(
        kern,
        grid=(batch * per_batch,),
        in_specs=[
            pl.BlockSpec((tm, MLA_Q_RANK), lambda i: (i, B_CQ_OFF // MLA_Q_RANK)),
            pl.BlockSpec((tm, MLA_KV_RANK), lambda i: (i, B_C_OFF // MLA_KV_RANK)),
            pl.BlockSpec((tm, LANES), lambda i: (i, B_KR_OFF // LANES)),
            pl.BlockSpec((tm, LANES), lambda i: (i, B_KROT_OFF // LANES)),
            pl.BlockSpec((tm, 1), lambda i: (i, 0)),
            pl.BlockSpec((1, LANES), const),
            pl.BlockSpec((1, LANES), const),
            pl.BlockSpec((1, MLA_Q_RANK), const),
            pl.BlockSpec((1, MLA_KV_RANK), const),
            pl.BlockSpec(w_uq.shape, const),
            pl.BlockSpec(w_uk.shape, const),
            pl.BlockSpec(w_uv_t.shape, const),
        ],
        out_specs=[
            pl.BlockSpec((None, MLA_HEADS, tm, MLA_QK_PAD), head_map),
            pl.BlockSpec((None, MLA_HEADS, tm, MLA_QK_PAD), head_map),
            pl.BlockSpec((None, MLA_HEADS, tm // tk, MLA_VT_ROWS, tk),
                         lambda i: (i // per_batch, 0, i % per_batch, 0, 0)),
        ],
        out_shape=[
            jax.ShapeDtypeStruct((batch, MLA_HEADS, seq, MLA_QK_PAD), BF16),
            jax.ShapeDtypeStruct((batch, MLA_HEADS, seq, MLA_QK_PAD), BF16),
            jax.ShapeDtypeStruct((batch, MLA_HEADS, seq // tk, MLA_VT_ROWS, tk), BF16),
        ],
        compiler_params=pltpu.CompilerParams(
            dimension_semantics=("parallel",), vmem_limit_bytes=VMEM_LIMIT),
        name="mla_prep",
    )(proj, proj, proj, proj, pos_col, freq, sign, q_gain, kv_gain, w_uq, w_uk, w_uv_t)


def _max_over_rows(x, ways=8):
    rows = x.shape[0]
    slab = rows // ways
    parts = [x[i * slab:(i + 1) * slab] for i in range(ways)] if slab >= 8 and rows % ways == 0 else [x]
    while len(parts) > 1:
        parts = [jnp.maximum(parts[i], parts[i + 1]) for i in range(0, len(parts), 2)]
    return jnp.max(parts[0], axis=0, keepdims=True)


def _mla_attn_kernel(q_ref, k_ref, vt_ref, z_ref, o_ref, acc_ref, m_ref, s0_ref, s1_ref, s2_ref,
                     smax0_ref, smax1_ref, smax2_ref, *, tq, tk):
    n_heads, seq = q_ref.shape[0], q_ref.shape[1]
    n_q = seq // tq
    tn = MXU_DIM
    units = [(h, n) for h in range(n_heads) for n in range(tq // tn)]
    s_slots = (s0_ref, s1_ref, s2_ref)
    smax_slots = (smax0_ref, smax1_ref, smax2_ref)
    FIRST = 2

    def n_keys(n, diagonal):
        return min(tk, (n + 1) * tn) if diagonal else tk

    def produce(qi, kj, slot, u, diagonal):
        h, n = units[u]
        nk = n_keys(n, diagonal)
        krows = pl.ds(pl.multiple_of(kj * tk, tk), nk)
        qrows = pl.ds(pl.multiple_of(qi * tq + n * tn, tn), tn)
        s_t = _dot_nt(k_ref[h, krows, :], q_ref[h, qrows, :])
        if diagonal:
            kpos = lax.broadcasted_iota(jnp.int32, (nk, tn), 0)
            qpos = lax.broadcasted_iota(jnp.int32, (nk, tn), 1) + n * tn
            s_t = jnp.where(kpos <= qpos, s_t, -jnp.inf)
        s_slots[slot][u, :nk, :] = s_t
        smax_slots[slot][u] = _max_over_rows(s_t)

    def consume(kj, slot, u, diagonal):
        h, n = units[u]
        nk = n_keys(n, diagonal)
        cols = slice(n * tn, (n + 1) * tn)
        m_prev = m_ref[h, :, cols]
        m_new = jnp.maximum(m_prev, smax_slots[slot][u])
        alpha = jnp.exp2(m_prev - m_new)
        p = jnp.exp2(s_slots[slot][u, :nk, :] - m_new).astype(BF16)
        m_ref[h, :, cols] = m_new
        acc_ref[h, :, cols] = alpha * acc_ref[h, :, cols] + _dot(vt_ref[h, kj, :, :nk], p)

    def produce_all(qi, kj, slot, diagonal):
        for u in range(len(units)):
            produce(qi, kj, slot, u, diagonal)

    def step(kj, slot, diagonal, nxt=None):
        if nxt is not None:
            produce(nxt[0], nxt[1], nxt[2], 0, nxt[3])
        for u in range(len(units)):
            consume(kj, slot, u, diagonal)
            if nxt is not None and u + 1 < len(units):
                produce(nxt[0], nxt[1], nxt[2], u + 1, nxt[3])

    def last_step(qi, slot):
        if n_q == 1:
            step(qi, slot, True)
            return

        @pl.when(qi < n_q - 1)
        def _():
            step(qi, slot, True, (qi + 1, 0, FIRST, False))

        last_q = n_q - 1
        if slot == {1: 0, 2: 1, 3: 0}[last_q - 2 * max(0, (last_q - 2) // 2)]:
            @pl.when(qi == last_q)
            def _():
                step(qi, slot, True)

    def q_block(qi, carry):
        rows = pl.ds(pl.multiple_of(qi * tq, tq), tq)
        m_ref[...] = jnp.full_like(m_ref, -jnp.inf)
        acc_ref[...] = jnp.zeros_like(acc_ref)

        @pl.when(qi == 0)
        def _():
            step(0, FIRST, True)
            if n_q > 1:
                produce_all(1, 0, FIRST, False)

        @pl.when(qi == 1)
        def _():
            step(0, FIRST, False, (qi, 1, 0, True))

        @pl.when(qi > 1)
        def _():
            step(0, FIRST, False, (qi, 1, 0, False))

        def steady_pair(i, c):
            kj = 2 * i + 1
            step(kj, 0, False, (qi, kj + 1, 1, False))
            step(kj + 1, 1, False, (qi, kj + 2, 0, False))
            return c

        n_pairs = jnp.maximum(qi - 2, 0) // 2
        lax.fori_loop(0, n_pairs, steady_pair, 0)
        kj = 2 * n_pairs + 1
        left = qi - kj + 1

        @pl.when((qi > 0) & (left == 1))
        def _():
            last_step(qi, 0)

        @pl.when((qi > 0) & (left == 2))
        def _():
            step(kj, 0, False, (qi, kj + 1, 1, True))
            last_step(qi, 1)

        @pl.when((qi > 0) & (left == 3))
        def _():
            step(kj, 0, False, (qi, kj + 1, 1, False))
            step(kj + 1, 1, False, (qi, kj + 2, 0, True))
            last_step(qi, 0)

        for h in range(n_heads):
            acc = acc_ref[h]
            o = jnp.transpose(acc[:MLA_V_DIM] * (1.0 / acc[MLA_V_DIM:MLA_V_DIM + 1]))
            cols = slice(h * MLA_V_DIM, (h + 1) * MLA_V_DIM)
            o_ref[rows, cols] = (o * _silu(z_ref[rows, cols].astype(F32))).astype(o_ref.dtype)
        return carry

    produce_all(0, 0, FIRST, True)
    lax.fori_loop(0, n_q, q_block, 0)


def _mla_attn(q_cat, k_cat, v_t, proj, batch, seq, tq, tk):
    kern = functools.partial(_mla_attn_kernel, tq=tq, tk=tk)
    nb = seq // tk
    nh = ATT_HEADS_PER_STEP
    n_units = nh * (tq // MXU_DIM)
    width = nh * MLA_V_DIM
    z_blk = B_Z_OFF // width
    return pl.pallas_call(
        kern,
        grid=(batch, MLA_HEADS // nh),
        in_specs=[
            pl.BlockSpec((None, nh, seq, MLA_QK_PAD), lambda b, g: (b, g, 0, 0)),
            pl.BlockSpec((None, nh, seq, MLA_QK_PAD), lambda b, g: (b, g, 0, 0)),
            pl.BlockSpec((None, nh, nb, MLA_VT_ROWS, tk), lambda b, g: (b, g, 0, 0, 0)),
            pl.BlockSpec((seq, width), lambda b, g: (b, z_blk + g)),
        ],
        out_specs=pl.BlockSpec((seq, width), lambda b, g: (b, g)),
        out_shape=jax.ShapeDtypeStruct((batch * seq, BRANCH_WIDTH), BF16),
        scratch_shapes=[
            pltpu.VMEM((nh, MLA_VT_ROWS, tq), F32),
            pltpu.VMEM((nh, 1, tq), F32),
            pltpu.VMEM((n_units, tk, MXU_DIM), F32),
            pltpu.VMEM((n_units, tk, MXU_DIM), F32),
            pltpu.VMEM((n_units, tk, MXU_DIM), F32),
            pltpu.VMEM((n_units, 1, MXU_DIM), F32),
            pltpu.VMEM((n_units, 1, MXU_DIM), F32),
            pltpu.VMEM((n_units, 1, MXU_DIM), F32),
        ],
        compiler_params=pltpu.CompilerParams(
            dimension_semantics=("parallel", "parallel"), vmem_limit_bytes=VMEM_LIMIT),
        name="mla_attn",
    )(q_cat, k_cat, v_t, proj)


def _pad_heads(w, heads, width, padded):
    lead = w.shape[:-1]
    w = w.reshape(lead + (heads, width))
    w = jnp.pad(w, [(0, 0)] * len(lead) + [(0, 0), (0, padded - width)])
    return w.reshape(lead + (heads * padded,))


def _repack_kernel(*refs, n_in, with_gain, pieces, zero_ranges, transposed):
    ins = refs[:n_in]
    gains = refs[n_in:2 * n_in] if with_gain else None
    o_ref = refs[-1]

    def window(lo, width):
        return (slice(lo, lo + width), slice(None)) if transposed else (slice(None), slice(lo, lo + width))

    for dst, idx, src, width in pieces:
        val = ins[idx][window(src, width)]
        if with_gain:
            val = val * gains[idx][...]
        o_ref[window(dst, width)] = val.astype(o_ref.dtype)
    for lo, hi in zero_ranges:
        shape = (hi - lo, o_ref.shape[1]) if transposed else (o_ref.shape[0], hi - lo)
        o_ref[window(lo, hi - lo)] = jnp.zeros(shape, o_ref.dtype)


def _repack(ws, gains, pieces, out_width, zero_ranges=(), transposed=False):
    d = ws[0].shape[1] if transposed else ws[0].shape[0]
    td = min(REPACK_ROWS, d)
    with_gain = gains is not None
    covered = sorted([(dst, dst + w) for dst, _, _, w in pieces] + list(zero_ranges))
    assert covered[0][0] == 0 and covered[-1][1] == out_width
    assert all(a[1] == b[0] for a, b in zip(covered, covered[1:])), "output features written exactly once"
    kern = functools.partial(_repack_kernel, n_in=len(ws), with_gain=with_gain, pieces=tuple(pieces),
                             zero_ranges=tuple(zero_ranges), transposed=transposed)

    def spec(n):
        return (pl.BlockSpec((n, td), lambda i: (0, i)) if transposed
                else pl.BlockSpec((td, n), lambda i: (i, 0)))

    in_specs = [spec(w.shape[0] if transposed else w.shape[1]) for w in ws]
    args = list(ws)
    if with_gain:
        in_specs += [spec(1) for _ in gains]
        args += [g.reshape((1, d) if transposed else (d, 1)) for g in gains]
    return pl.pallas_call(
        kern,
        grid=(d // td,),
        in_specs=in_specs,
        out_specs=spec(out_width),
        out_shape=jax.ShapeDtypeStruct((out_width, d) if transposed else (d, out_width), BF16),
        compiler_params=pltpu.CompilerParams(
            dimension_semantics=("parallel",), vmem_limit_bytes=VMEM_LIMIT),
        name="repack",
    )(*args)


def _layer_a_weights(w_in, gain, w_g2, b_g):
    hk = GLA_HEADS * GLA_DK
    o = [int(v) for v in np.cumsum([0, hk, hk, BRANCH_WIDTH, GLA_GATE_RANK, BRANCH_WIDTH, MEM_WIDTH,
                                    MEM_WIDTH])]
    q_src, k_src, v_src, lr_src, z_src = o[0], o[1], o[2], o[3], o[4]
    pieces = [(A_V_OFF, 0, v_src, BRANCH_WIDTH), (A_Z_OFF, 0, z_src, BRANCH_WIDTH + 2 * MEM_WIDTH)]
    zeros = []
    for h in range(GLA_HEADS):
        q_dst, k_dst = A_Q_OFF + h * GLA_DK_PAD, A_K_OFF + h * GLA_DK_PAD
        pieces += [(q_dst, 0, q_src + h * GLA_DK, GLA_DK), (q_dst + GLA_DK, 0, lr_src, GLA_GATE_RANK),
                   (k_dst, 0, k_src + h * GLA_DK, GLA_DK)]
        zeros += [(q_dst + GLA_DK + GLA_GATE_RANK, q_dst + GLA_DK_PAD), (k_dst + GLA_DK, k_dst + GLA_DK_PAD)]
    w_all_t = _repack([w_in.T], [gain], pieces, A_WIDTH, zeros, transposed=True)
    wg = _pad_heads(w_g2, GLA_HEADS, GLA_DK, GLA_DK_PAD)
    wg = wg.reshape(GLA_GATE_RANK, GLA_HEADS, GLA_DK_PAD).transpose(1, 0, 2)
    wg = jnp.pad(wg, [(0, 0), (GLA_DK, GLA_DK_PAD - GLA_DK - GLA_GATE_RANK), (0, 0)]).astype(BF16)
    bg = _pad_heads(b_g, GLA_HEADS, GLA_DK, GLA_DK_PAD).reshape(GLA_HEADS, 1, GLA_DK_PAD)
    return w_all_t, wg, bg


def _layer_b_weights(w_in, gain_in, w_dkv, gain_dkv):
    half = MLA_ROPE_DIM // 2
    pieces = [(B_Z_OFF, 0, MLA_Q_RANK, BRANCH_WIDTH), (B_CQ_OFF, 0, 0, MLA_Q_RANK),
              (B_MQ_OFF, 0, MLA_Q_RANK + BRANCH_WIDTH, 2 * MEM_WIDTH), (B_C_OFF, 1, 0, MLA_KV_RANK)]
    for rep in range(LANES // MLA_ROPE_DIM):
        kr, krot = B_KR_OFF + rep * MLA_ROPE_DIM, B_KROT_OFF + rep * MLA_ROPE_DIM
        pieces += [(kr, 1, MLA_KV_RANK, MLA_ROPE_DIM),
                   (krot, 1, MLA_KV_RANK + half, half), (krot + half, 1, MLA_KV_RANK, half)]
    return _repack([w_in, w_dkv], [gain_in, gain_dkv], pieces, B_WIDTH)


def _uq_weights(w_uq):
    half = MLA_ROPE_DIM // 2
    n_nope, n_rope = MLA_HEADS * MLA_NOPE_DIM, MLA_HEADS * MLA_ROPE_DIM
    pieces = []
    for h in range(MLA_HEADS):
        src = h * MLA_QK_DIM
        rope, rot = n_nope + h * MLA_ROPE_DIM, n_nope + n_rope + h * MLA_ROPE_DIM
        pieces += [(h * MLA_NOPE_DIM, 0, src, MLA_NOPE_DIM), (rope, 0, src + MLA_NOPE_DIM, MLA_ROPE_DIM),
                   (rot, 0, src + MLA_NOPE_DIM + half, half), (rot + half, 0, src + MLA_NOPE_DIM, half)]
    return _repack([w_uq], None, pieces, n_nope + 2 * n_rope)


def _rope_tables():
    r = MLA_ROPE_DIM
    freqs = ROPE_THETA ** (-jnp.arange(0, r, 2, dtype=F32) / r)
    freq = jnp.tile(freqs, LANES // (r // 2)).reshape(1, LANES)
    sign = jnp.tile(jnp.concatenate([-jnp.ones(r // 2, F32), jnp.ones(r // 2, F32)]),
                    LANES // r).reshape(1, LANES)
    return freq, sign


def kernel(x, mem, positions, a_pre_norm, a_w_in, a_w_g2, a_b_g, a_gla_norm, a_mem_norm, a_w_mem_kv,
           a_w_out, a_post_norm, kv_in_norm, w_dkv, kv_norm, w_uk, w_uv, b_pre_norm, b_w_in, b_q_norm,
           b_w_uq, b_mem_norm, b_w_mem_kv, b_w_out, b_post_norm):
    assert a_w_in.shape[0] == 1 and b_w_in.shape[0] == 1, "one A layer followed by one B layer"
    batch, seq, _ = x.shape
    t = batch * seq
    x2 = x.reshape(t, D_MODEL)

    kv_w = 2 * MEM_WIDTH
    w_memkv = _repack([a_w_mem_kv[0], b_w_mem_kv[0]], [a_mem_norm[0], b_mem_norm[0]],
                      [(0, 0, 0, kv_w), (kv_w, 1, 0, kv_w)], 2 * kv_w)
    memkv = _norm_matmul(mem.reshape(batch * N_MEM, D_MODEL), w_memkv, tm=batch * N_MEM, tn=2 * MEM_WIDTH)

    w_a_t, wg, bg = _layer_a_weights(a_w_in[0], a_pre_norm[0], a_w_g2[0], a_b_g[0])
    proj_a = _norm_matmul(x2, w_a_t, tm=PROJ_TM, tn=A_TN, w_transposed=True)
    gla_out = _gla(proj_a, wg, bg, a_gla_norm, batch, seq)
    mem_out = _mem_attn(proj_a, memkv, seq, A_MQ_OFF, A_MZ_OFF, kv_blk=0)
    x1 = _out_proj(gla_out, mem_out, a_w_out[0].astype(BF16), x2, a_post_norm)

    w_b = _layer_b_weights(b_w_in[0], b_pre_norm[0], w_dkv, kv_in_norm)
    proj_b = _norm_matmul(x1, w_b, tm=PROJ_TM, tn=B_TN)
    freq, sign = _rope_tables()
    tk = min(ATT_TK, seq)
    tq = min(ATT_TQ, seq)
    q_cat, k_cat, v_t = _mla_prep(
        proj_b, positions.reshape(t, 1), freq, sign, b_q_norm, kv_norm[None, :],
        _uq_weights(b_w_uq[0]), w_uk.astype(BF16), w_uv.T.astype(BF16), batch, seq, tk)
    mla_out = _mla_attn(q_cat, k_cat, v_t, proj_b, batch, seq, tq, tk)
    mem_out_b = _mem_attn(proj_b, memkv, seq, B_MQ_OFF, B_MZ_OFF, kv_blk=2)
    out = _out_proj(mla_out, mem_out_b, b_w_out[0].astype(BF16), x1, b_post_norm)
    return out.reshape(batch, seq, D_MODEL)
```

```python
import functools

import numpy as np
import jax
import jax.numpy as jnp
from jax import lax
from jax.experimental import pallas as pl
from jax.experimental.pallas import tpu as pltpu

F32 = jnp.float32
BF16 = jnp.bfloat16

D_MODEL = 2048
N_MEM = 256
MEM_HEADS = 4
MEM_HEAD_DIM = 128
MEM_WIDTH = MEM_HEADS * MEM_HEAD_DIM
BRANCH_WIDTH = D_MODEL - MEM_WIDTH
GLA_HEADS = 4
GLA_DV = BRANCH_WIDTH // GLA_HEADS
GLA_DK = GLA_DV // 2
GLA_GATE_RANK = 16
GLA_GATE_NORM = 16.0
MLA_HEADS = 12
MLA_V_DIM = 128
MLA_NOPE_DIM = 128
MLA_ROPE_DIM = 64
MLA_QK_DIM = MLA_NOPE_DIM + MLA_ROPE_DIM
MLA_Q_RANK = 512
MLA_KV_RANK = 512
ROPE_THETA = 10000.0
EPS = 1e-6

LANES = 128
MXU_DIM = 256
GLA_DK_PAD = MXU_DIM
MLA_QK_PAD = MXU_DIM
BF16_SUBLANES = 16
MLA_VT_ROWS = MLA_V_DIM + BF16_SUBLANES
ATT_HEADS_PER_STEP = 2
LOG2_E = 1.4426950408889634
LN_2 = 0.6931471805599453
MEM_Q_SCALE = MEM_HEAD_DIM ** -0.5 * LOG2_E
VMEM_LIMIT = 56 * 1024 * 1024

PROJ_TM = 1024
PROJ_NORM_ROWS = 256
REPACK_ROWS = 256
OUT_TM = 512
OUT_CHUNK = 256
GLA_SPAN = 256
GLA_SUB = 32
GLA_HEADS_PER_STEP = 4
GLA_STAGE_SKEW = 1
MEM_TM = 512
PREP_TM = 512
ATT_TQ = 512
ATT_TK = 512

A_V_OFF = 0
A_Z_OFF = BRANCH_WIDTH
A_MQ_OFF = 2 * BRANCH_WIDTH
A_MZ_OFF = A_MQ_OFF + MEM_WIDTH
A_Q_OFF = A_MZ_OFF + MEM_WIDTH
A_K_OFF = A_Q_OFF + GLA_HEADS * GLA_DK_PAD
A_WIDTH = A_K_OFF + GLA_HEADS * GLA_DK_PAD
A_TN = 2048

B_Z_OFF = 0
B_CQ_OFF = BRANCH_WIDTH
B_MQ_OFF = B_CQ_OFF + MLA_Q_RANK
B_MZ_OFF = B_MQ_OFF + MEM_WIDTH
B_C_OFF = B_MZ_OFF + MEM_WIDTH
B_KR_OFF = B_C_OFF + MLA_KV_RANK
B_KROT_OFF = B_KR_OFF + LANES
B_WIDTH = B_KROT_OFF + LANES
B_TN = 1280


def _silu(z):
    return z * (1.0 / (1.0 + jnp.exp(-z)))


def _scaled_log_sigmoid(g, scale):
    softplus2 = jnp.log2(1.0 + jnp.exp2(jnp.abs(g) * -LOG2_E))
    return jnp.minimum(g, 0.0) * scale - softplus2 * (LN_2 * scale)


def _dot(a, b):
    return jnp.dot(a, b, preferred_element_type=F32)


def _dot_nt(a, b):
    return lax.dot_general(a, b, (((1,), (1,)), ((), ())), preferred_element_type=F32)


def _dot_tn(a, b):
    return lax.dot_general(a, b, (((0,), (0,)), ((), ())), preferred_element_type=F32)


def _norm_matmul_kernel(x_ref, w_ref, o_ref, h_ref, r_ref, *, norm_rows, w_transposed):
    j = pl.program_id(1)

    def scaled_product(h, r):
        acc = _dot_nt(h, w_ref[...]) if w_transposed else _dot(h, w_ref[...])
        return (acc * r).astype(o_ref.dtype)

    @pl.when(j == 0)
    def _():
        tm = x_ref.shape[0]
        for c in range(tm // norm_rows):
            rows = pl.ds(c * norm_rows, norm_rows)
            x = x_ref[rows, :]
            r = lax.rsqrt(jnp.mean(x * x, axis=-1, keepdims=True) + EPS)
            h = x.astype(BF16)
            r_ref[rows, :] = r
            h_ref[rows, :] = h
            o_ref[rows, :] = scaled_product(h, r)

    @pl.when(j > 0)
    def _():
        o_ref[...] = scaled_product(h_ref[...], r_ref[...])


def _norm_matmul(x, w, *, tm, tn, w_transposed=False):
    m, d = x.shape
    n = w.shape[0] if w_transposed else w.shape[1]
    tm = min(tm, m)
    kern = functools.partial(_norm_matmul_kernel, norm_rows=min(PROJ_NORM_ROWS, tm),
                             w_transposed=w_transposed)
    w_spec = (pl.BlockSpec((tn, d), lambda i, j: (j, 0)) if w_transposed
              else pl.BlockSpec((d, tn), lambda i, j: (0, j)))
    return pl.pallas_call(
        kern,
        grid=(m // tm, n // tn),
        in_specs=[
            pl.BlockSpec((tm, d), lambda i, j: (i, 0)),
            w_spec,
        ],
        out_specs=pl.BlockSpec((tm, tn), lambda i, j: (i, j)),
        out_shape=jax.ShapeDtypeStruct((m, n), BF16),
        scratch_shapes=[pltpu.VMEM((tm, d), BF16), pltpu.VMEM((tm, 1), F32)],
        compiler_params=pltpu.CompilerParams(
            dimension_semantics=("parallel", "arbitrary"), vmem_limit_bytes=VMEM_LIMIT),
        name="norm_matmul",
    )(x, w)


def _out_proj_kernel(a_ref, m_ref, wa_ref, wm_ref, x_ref, g_ref, o_ref, *, chunk):
    for c in range(a_ref.shape[0] // chunk):
        rows = pl.ds(c * chunk, chunk)
        y = _dot(a_ref[rows, :], wa_ref[...]) + _dot(m_ref[rows, :], wm_ref[...])
        ms = jnp.mean(y * y, axis=-1, keepdims=True)
        o_ref[rows, :] = x_ref[rows, :] + y * lax.rsqrt(ms + EPS) * g_ref[...]


def _out_proj(a, mo, w_out, x, gain):
    t = x.shape[0]
    tm = min(OUT_TM, t)
    mem_blk = BRANCH_WIDTH // MEM_WIDTH
    return pl.pallas_call(
        functools.partial(_out_proj_kernel, chunk=min(OUT_CHUNK, tm)),
        grid=(t // tm,),
        in_specs=[
            pl.BlockSpec((tm, BRANCH_WIDTH), lambda i: (i, 0)),
            pl.BlockSpec((tm, MEM_WIDTH), lambda i: (i, 0)),
            pl.BlockSpec((BRANCH_WIDTH, D_MODEL), lambda i: (0, 0)),
            pl.BlockSpec((MEM_WIDTH, D_MODEL), lambda i: (mem_blk, 0)),
            pl.BlockSpec((tm, D_MODEL), lambda i: (i, 0)),
            pl.BlockSpec((1, D_MODEL), lambda i: (0, 0)),
        ],
        out_specs=pl.BlockSpec((tm, D_MODEL), lambda i: (i, 0)),
        out_shape=jax.ShapeDtypeStruct((t, D_MODEL), F32),
        compiler_params=pltpu.CompilerParams(
            dimension_semantics=("parallel",), vmem_limit_bytes=VMEM_LIMIT),
        name="out_proj",
    )(a, mo, w_out, w_out, x, gain)


def _gla_head_stages(h, v_ref, z_ref, q_ref, k_ref, wg_ref, bg_ref, gn_ref, tril_ref, lvl_ref, o_ref,
                     s_ref, sub):
    span = q_ref.shape[0]
    n_sub = span // sub
    n_levels = n_sub.bit_length() - 1
    scale = GLA_DK ** -0.5
    kcols = slice(h * GLA_DK_PAD, (h + 1) * GLA_DK_PAD)
    vcols = slice(h * GLA_DV, (h + 1) * GLA_DV)

    qb = q_ref[:, kcols]
    g = _dot(qb, wg_ref[h]) + bg_ref[h]
    yield
    log_a = _scaled_log_sigmoid(g, 1.0 / GLA_GATE_NORM)
    a_hi = log_a.astype(BF16)
    rem = log_a - a_hi.astype(F32)
    a_mid = rem.astype(BF16)
    a_lo = (rem - a_mid.astype(F32)).astype(BF16)
    yield
    tril = tril_ref[...]
    cum = _dot(tril, a_hi) + _dot(tril, a_mid) + _dot(tril, a_lo)
    yield

    refs = jnp.concatenate([cum[i * sub:i * sub + 1] for i in range(n_sub)] + [cum[span - 1:span]], axis=0)
    own = refs[:n_sub]

    def ref_rows(index_of):
        return jnp.concatenate([refs[index_of(i):index_of(i) + 1] for i in range(n_sub)], axis=0)

    def per_row(f):
        return jnp.concatenate(
            [jnp.broadcast_to(f[i:i + 1], (sub, f.shape[1])) for i in range(n_sub)], axis=0)

    own_b = per_row(own)
    qd = qb.astype(F32) * (jnp.exp(cum - own_b) * scale)
    kd = k_ref[:, kcols].astype(F32) * jnp.exp(own_b - cum)
    yield
    q_ops, k_ops = [qd.astype(BF16)], [kd.astype(BF16)]
    for lvl in range(1, n_levels + 1):
        n = 1 << (lvl - 1)
        block_start = ref_rows(lambda i: (i // n) * n)
        next_start = ref_rows(lambda i: (i // n + 1) * n)
        q_ops.append(q_ops[0] if n == 1 else (qd * per_row(jnp.exp(own - block_start))).astype(BF16))
        k_ops.append((kd * per_row(jnp.exp(next_start - own))).astype(BF16))
    q_state = (qd * per_row(jnp.exp(own))).astype(BF16)
    k_state = (kd * per_row(jnp.exp(refs[n_sub:] - own))).astype(BF16)
    yield
    atts = [_dot_nt(qo, ko) for qo, ko in zip(q_ops, k_ops)]
    yield
    level = lvl_ref[...]
    att = jnp.where(level == n_levels, atts[n_levels], 0.0)
    for lvl in reversed(range(n_levels)):
        att = jnp.where(level == lvl, atts[lvl], att)
    yield
    state = s_ref[h]
    v = v_ref[:, vcols]
    o = _dot(jnp.concatenate([att.astype(BF16), q_state], axis=1),
             jnp.concatenate([v, state.astype(BF16)], axis=0))
    decay_col = jnp.transpose(jnp.exp(cum[span - 8:span, :]))[:, 7:8]
    s_ref[h] = decay_col * state + _dot_tn(k_state, v)
    yield
    on = o * lax.rsqrt(jnp.mean(o * o, axis=-1, keepdims=True) + EPS) * gn_ref[...]
    o_ref[:, vcols] = (on * _silu(z_ref[:, vcols].astype(F32))).astype(o_ref.dtype)
    yield


def _gla_kernel(v_ref, z_ref, q_ref, k_ref, wg_ref, bg_ref, gn_ref, tril_ref, lvl_ref, o_ref, s_ref, *,
                sub, skew):
    @pl.when(pl.program_id(2) == 0)
    def _():
        s_ref[...] = jnp.zeros_like(s_ref)

    heads = [_gla_head_stages(h, v_ref, z_ref, q_ref, k_ref, wg_ref, bg_ref, gn_ref, tril_ref, lvl_ref,
                              o_ref, s_ref, sub) for h in range(s_ref.shape[0])]
    live = list(range(len(heads)))
    step = 0
    while live:
        for i in list(live):
            if step >= i * skew:
                try:
                    next(heads[i])
                except StopIteration:
                    live.remove(i)
        step += 1


def _prefix_matrix(rows):
    r = np.arange(rows)
    return jnp.asarray(r[:, None] >= r[None, :], BF16)


def _level_matrix(rows, sub):
    n_levels = (rows // sub).bit_length() - 1
    t, s = np.arange(rows)[:, None], np.arange(rows)[None, :]
    level = np.full((rows, rows), n_levels + 1, np.int32)
    level[(t // sub == s // sub) & (s <= t)] = 0
    for lvl in range(1, n_levels + 1):
        size = sub << (lvl - 1)
        level[((t // size) % 2 == 1) & (s // size == t // size - 1)] = lvl
    return jnp.asarray(level)


def _gla(proj, wg, bg, gn, batch, seq):
    span = min(GLA_SPAN, seq)
    assert span % GLA_SUB == 0 and (span // GLA_SUB) & (span // GLA_SUB - 1) == 0
    ns = seq // span
    nh = GLA_HEADS_PER_STEP
    kern = functools.partial(_gla_kernel, sub=GLA_SUB, skew=GLA_STAGE_SKEW)
    vw, kw = nh * GLA_DV, nh * GLA_DK_PAD
    v_blk, z_blk = A_V_OFF // vw, A_Z_OFF // vw
    q_blk, k_blk = A_Q_OFF // kw, A_K_OFF // kw
    group = span
    return pl.pallas_call(
        kern,
        grid=(batch, GLA_HEADS // nh, ns),
        in_specs=[
            pl.BlockSpec((span, vw), lambda b, g, s: (b * ns + s, v_blk + g)),
            pl.BlockSpec((span, vw), lambda b, g, s: (b * ns + s, z_blk + g)),
            pl.BlockSpec((span, kw), lambda b, g, s: (b * ns + s, q_blk + g)),
            pl.BlockSpec((span, kw), lambda b, g, s: (b * ns + s, k_blk + g)),
            pl.BlockSpec((nh, GLA_DK_PAD, GLA_DK_PAD), lambda b, g, s: (g, 0, 0)),
            pl.BlockSpec((nh, 1, GLA_DK_PAD), lambda b, g, s: (g, 0, 0)),
            pl.BlockSpec((1, GLA_DV), lambda b, g, s: (0, 0)),
            pl.BlockSpec((group, group), lambda b, g, s: (0, 0)),
            pl.BlockSpec((group, group), lambda b, g, s: (0, 0)),
        ],
        out_specs=pl.BlockSpec((span, vw), lambda b, g, s: (b * ns + s, g)),
        out_shape=jax.ShapeDtypeStruct((batch * seq, BRANCH_WIDTH), BF16),
        scratch_shapes=[pltpu.VMEM((nh, GLA_DK_PAD, GLA_DV), F32)],
        compiler_params=pltpu.CompilerParams(
            dimension_semantics=("parallel", "parallel", "arbitrary"), vmem_limit_bytes=VMEM_LIMIT),
        name="gla",
    )(proj, proj, proj, proj, wg, bg, gn, _prefix_matrix(group), _level_matrix(group, GLA_SUB))


def _mem_attn_kernel(q_ref, z_ref, k_ref, v_ref, o_ref):
    for h in range(MEM_HEADS):
        cols = slice(h * MEM_HEAD_DIM, (h + 1) * MEM_HEAD_DIM)
        s = _dot_nt(q_ref[:, cols], k_ref[:, cols])
        p = jnp.exp2(s - jnp.max(s, axis=-1, keepdims=True))
        l = jnp.sum(p, axis=-1, keepdims=True)
        o = _dot(p.astype(BF16), v_ref[:, cols]) * (1.0 / l)
        o_ref[:, cols] = (o * _silu(z_ref[:, cols].astype(F32))).astype(o_ref.dtype)


def _mem_attn(proj, memkv, seq, q_off, z_off, kv_blk):
    t = proj.shape[0]
    tm = min(MEM_TM, seq)
    per_batch = seq // tm
    q_blk, z_blk = q_off // MEM_WIDTH, z_off // MEM_WIDTH
    return pl.pallas_call(
        _mem_attn_kernel,
        grid=(t // tm,),
        in_specs=[
            pl.BlockSpec((tm, MEM_WIDTH), lambda i: (i, q_blk)),
            pl.BlockSpec((tm, MEM_WIDTH), lambda i: (i, z_blk)),
            pl.BlockSpec((N_MEM, MEM_WIDTH), lambda i: (i // per_batch, kv_blk)),
            pl.BlockSpec((N_MEM, MEM_WIDTH), lambda i: (i // per_batch, kv_blk + 1)),
        ],
        out_specs=pl.BlockSpec((tm, MEM_WIDTH), lambda i: (i, 0)),
        out_shape=jax.ShapeDtypeStruct((t, MEM_WIDTH), BF16),
        compiler_params=pltpu.CompilerParams(
            dimension_semantics=("parallel",), vmem_limit_bytes=VMEM_LIMIT),
        name="mem_attn",
    )(proj, proj, memkv, memkv)


def _rotary_tables(pos, freq, sign):
    rows = pos.shape[0]
    groups = LANES // (MLA_ROPE_DIM // 2)
    if rows % (8 * groups) != 0:
        ang = pos * freq
        return jnp.cos(ang), jnp.sin(ang) * sign
    blk = rows // groups
    lane_group = lax.broadcasted_iota(jnp.int32, (1, LANES), 1) // (MLA_ROPE_DIM // 2)
    packed_pos = jnp.zeros((blk, LANES), F32)
    for g in range(groups):
        packed_pos = jnp.where(lane_group == g, pos[g * blk:(g + 1) * blk], packed_pos)
    ang = packed_pos * freq
    tables = []
    for packed in (jnp.cos(ang), jnp.sin(ang)):
        blocks = []
        for g in range(groups):
            own = jnp.where(lane_group == g, packed, 0.0)
            spread = own
            for shift in range(1, groups):
                spread = spread + pltpu.roll(own, shift * (LANES // groups), axis=1)
            blocks.append(spread)
        tables.append(jnp.concatenate(blocks, axis=0))
    return tables[0], tables[1] * sign


def _mla_prep_kernel(cq_ref, c_ref, kr_ref, krot_ref, pos_ref, freq_ref, sign_ref, qg_ref, kvg_ref,
                     wuq_ref, wuk_ref, wuvt_ref, q_ref, k_ref, vt_ref, *, tk):
    tm = cq_ref.shape[0]
    n_nope = MLA_HEADS * MLA_NOPE_DIM
    n_rope = MLA_HEADS * MLA_ROPE_DIM

    cq = cq_ref[...].astype(F32)
    cqn = cq * lax.rsqrt(jnp.mean(cq * cq, axis=-1, keepdims=True) + EPS)
    cqn = (cqn * (qg_ref[...] * (MLA_QK_DIM ** -0.5 * LOG2_E))).astype(BF16)
    c = c_ref[...].astype(F32)
    cn = c * lax.rsqrt(jnp.mean(c * c, axis=-1, keepdims=True) + EPS)
    cn = (cn * kvg_ref[...]).astype(BF16)

    q_nope = _dot(cqn, wuq_ref[:, :n_nope])
    for h in range(MLA_HEADS):
        q_ref[h, :, :MLA_NOPE_DIM] = q_nope[:, h * MLA_NOPE_DIM:(h + 1) * MLA_NOPE_DIM].astype(BF16)

    cos, sin = _rotary_tables(pos_ref[...].astype(F32), freq_ref[...], sign_ref[...])

    k_nope = _dot(cn, wuk_ref[...])
    for h in range(MLA_HEADS):
        k_ref[h, :, :MLA_NOPE_DIM] = k_nope[:, h * MLA_NOPE_DIM:(h + 1) * MLA_NOPE_DIM].astype(BF16)
    v_t = _dot_nt(wuvt_ref[...], cn).astype(BF16)
    ones_rows = (lax.broadcasted_iota(jnp.int32, (MLA_VT_ROWS - MLA_V_DIM, tk), 0) == 0).astype(BF16)
    for h in range(MLA_HEADS):
        for u in range(tm // tk):
            vt_ref[h, u, :MLA_V_DIM, :] = v_t[h * MLA_V_DIM:(h + 1) * MLA_V_DIM, u * tk:(u + 1) * tk]
            vt_ref[h, u, MLA_V_DIM:, :] = ones_rows

    lane = lax.broadcasted_iota(jnp.int32, (1, LANES), 1)
    slot_mask = [(lane < MLA_ROPE_DIM).astype(F32), (lane >= MLA_ROPE_DIM).astype(F32)]
    q_r = _dot(cqn, wuq_ref[:, n_nope:n_nope + n_rope])
    q_rot = _dot(cqn, wuq_ref[:, n_nope + n_rope:])
    for u in range(MLA_HEADS // 2):
        cols = slice(u * LANES, (u + 1) * LANES)
        roped = q_r[:, cols] * cos + q_rot[:, cols] * sin
        for half in range(2):
            q_ref[2 * u + half, :, MLA_NOPE_DIM:] = (roped * slot_mask[half]).astype(BF16)
    k_rope = (kr_ref[...].astype(F32) * cos + krot_ref[...].astype(F32) * sin).astype(BF16)
    for h in range(MLA_HEADS):
        k_ref[h, :, MLA_NOPE_DIM:] = k_rope


def _mla_prep(proj, pos_col, freq, sign, q_gain, kv_gain, w_uq, w_uk, w_uv_t, batch, seq, tk):
    tm = min(PREP_TM, seq)
    per_batch = seq // tm
    kern = functools.partial(_mla_prep_kernel, tk=tk)
    const = lambda i: (0, 0)
    head_map = lambda i: (i // per_batch, 0, i % per_batch, 0)
    return pl.pallas_call(
        kern,
        grid=(batch * per_batch,),
        in_specs=[
            pl.BlockSpec((tm, MLA_Q_RANK), lambda i: (i, B_CQ_OFF // MLA_Q_RANK)),
            pl.BlockSpec((tm, MLA_KV_RANK), lambda i: (i, B_C_OFF // MLA_KV_RANK)),
            pl.BlockSpec((tm, LANES), lambda i: (i, B_KR_OFF // LANES)),
            pl.BlockSpec((tm, LANES), lambda i: (i, B_KROT_OFF // LANES)),
            pl.BlockSpec((tm, 1), lambda i: (i, 0)),
            pl.BlockSpec((1, LANES), const),
            pl.BlockSpec((1, LANES), const),
            pl.BlockSpec((1, MLA_Q_RANK), const),
            pl.BlockSpec((1, MLA_KV_RANK), const),
            pl.BlockSpec(w_uq.shape, const),
            pl.BlockSpec(w_uk.shape, const),
            pl.BlockSpec(w_uv_t.shape, const),
        ],
        out_specs=[
            pl.BlockSpec((None, MLA_HEADS, tm, MLA_QK_PAD), head_map),
            pl.BlockSpec((None, MLA_HEADS, tm, MLA_QK_PAD), head_map),
            pl.BlockSpec((None, MLA_HEADS, tm // tk, MLA_VT_ROWS, tk),
                         lambda i: (i // per_batch, 0, i % per_batch, 0, 0)),
        ],
        out_shape=[
            jax.ShapeDtypeStruct((batch, MLA_HEADS, seq, MLA_QK_PAD), BF16),
            jax.ShapeDtypeStruct((batch, MLA_HEADS, seq, MLA_QK_PAD), BF16),
            jax.ShapeDtypeStruct((batch, MLA_HEADS, seq // tk, MLA_VT_ROWS, tk), BF16),
        ],
        compiler_params=pltpu.CompilerParams(
            dimension_semantics=("parallel",), vmem_limit_bytes=VMEM_LIMIT),
        name="mla_prep",
    )(proj, proj, proj, proj, pos_col, freq, sign, q_gain, kv_gain, w_uq, w_uk, w_uv_t)


def _max_over_rows(x, ways=8):
    rows = x.shape[0]
    slab = rows // ways
    parts = [x[i * slab:(i + 1) * slab] for i in range(ways)] if slab >= 8 and rows % ways == 0 else [x]
    while len(parts) > 1:
        parts = [jnp.maximum(parts[i], parts[i + 1]) for i in range(0, len(parts), 2)]
    return jnp.max(parts[0], axis=0, keepdims=True)


def _mla_attn_kernel(q_ref, k_ref, vt_ref, z_ref, o_ref, acc_ref, m_ref, s0_ref, s1_ref, s2_ref,
                     smax0_ref, smax1_ref, smax2_ref, *, tq, tk):
    n_heads, seq = q_ref.shape[0], q_ref.shape[1]
    n_q = seq // tq
    tn = MXU_DIM
    units = [(h, n) for h in range(n_heads) for n in range(tq // tn)]
    s_slots = (s0_ref, s1_ref, s2_ref)
    smax_slots = (smax0_ref, smax1_ref, smax2_ref)
    FIRST = 2

    def n_keys(n, diagonal):
        return min(tk, (n + 1) * tn) if diagonal else tk

    def produce(qi, kj, slot, u, diagonal):
        h, n = units[u]
        nk = n_keys(n, diagonal)
        krows = pl.ds(pl.multiple_of(kj * tk, tk), nk)
        qrows = pl.ds(pl.multiple_of(qi * tq + n * tn, tn), tn)
        s_t = _dot_nt(k_ref[h, krows, :], q_ref[h, qrows, :])
        if diagonal:
            kpos = lax.broadcasted_iota(jnp.int32, (nk, tn), 0)
            qpos = lax.broadcasted_iota(jnp.int32, (nk, tn), 1) + n * tn
            s_t = jnp.where(kpos <= qpos, s_t, -jnp.inf)
        s_slots[slot][u, :nk, :] = s_t
        smax_slots[slot][u] = _max_over_rows(s_t)

    def consume(kj, slot, u, diagonal):
        h, n = units[u]
        nk = n_keys(n, diagonal)
        cols = slice(n * tn, (n + 1) * tn)
        m_prev = m_ref[h, :, cols]
        m_new = jnp.maximum(m_prev, smax_slots[slot][u])
        alpha = jnp.exp2(m_prev - m_new)
        p = jnp.exp2(s_slots[slot][u, :nk, :] - m_new).astype(BF16)
        m_ref[h, :, cols] = m_new
        acc_ref[h, :, cols] = alpha * acc_ref[h, :, cols] + _dot(vt_ref[h, kj, :, :nk], p)

    def produce_all(qi, kj, slot, diagonal):
        for u in range(len(units)):
            produce(qi, kj, slot, u, diagonal)

    def step(kj, slot, diagonal, nxt=None):
        if nxt is not None:
            produce(nxt[0], nxt[1], nxt[2], 0, nxt[3])
        for u in range(len(units)):
            consume(kj, slot, u, diagonal)
            if nxt is not None and u + 1 < len(units):
                produce(nxt[0], nxt[1], nxt[2], u + 1, nxt[3])

    def last_step(qi, slot):
        if n_q == 1:
            step(qi, slot, True)
            return

        @pl.when(qi < n_q - 1)
        def _():
            step(qi, slot, True, (qi + 1, 0, FIRST, False))

        last_q = n_q - 1
        if slot == {1: 0, 2: 1, 3: 0}[last_q - 2 * max(0, (last_q - 2) // 2)]:
            @pl.when(qi == last_q)
            def _():
                step(qi, slot, True)

    def q_block(qi, carry):
        rows = pl.ds(pl.multiple_of(qi * tq, tq), tq)
        m_ref[...] = jnp.full_like(m_ref, -jnp.inf)
        acc_ref[...] = jnp.zeros_like(acc_ref)

        @pl.when(qi == 0)
        def _():
            step(0, FIRST, True)
            if n_q > 1:
                produce_all(1, 0, FIRST, False)

        @pl.when(qi == 1)
        def _():
            step(0, FIRST, False, (qi, 1, 0, True))

        @pl.when(qi > 1)
        def _():
            step(0, FIRST, False, (qi, 1, 0, False))

        def steady_pair(i, c):
            kj = 2 * i + 1
            step(kj, 0, False, (qi, kj + 1, 1, False))
            step(kj + 1, 1, False, (qi, kj + 2, 0, False))
            return c

        n_pairs = jnp.maximum(qi - 2, 0) // 2
        lax.fori_loop(0, n_pairs, steady_pair, 0)
        kj = 2 * n_pairs + 1
        left = qi - kj + 1

        @pl.when((qi > 0) & (left == 1))
        def _():
            last_step(qi, 0)

        @pl.when((qi > 0) & (left == 2))
        def _():
            step(kj, 0, False, (qi, kj + 1, 1, True))
            last_step(qi, 1)

        @pl.when((qi > 0) & (left == 3))
        def _():
            step(kj, 0, False, (qi, kj + 1, 1, False))
            step(kj + 1, 1, False, (qi, kj + 2, 0, True))
            last_step(qi, 0)

        for h in range(n_heads):
            acc = acc_ref[h]
            o = jnp.transpose(acc[:MLA_V_DIM] * (1.0 / acc[MLA_V_DIM:MLA_V_DIM + 1]))
            cols = slice(h * MLA_V_DIM, (h + 1) * MLA_V_DIM)
            o_ref[rows, cols] = (o * _silu(z_ref[rows, cols].astype(F32))).astype(o_ref.dtype)
        return carry

    produce_all(0, 0, FIRST, True)
    lax.fori_loop(0, n_q, q_block, 0)


def _mla_attn(q_cat, k_cat, v_t, proj, batch, seq, tq, tk):
    kern = functools.partial(_mla_attn_kernel, tq=tq, tk=tk)
    nb = seq // tk
    nh = ATT_HEADS_PER_STEP
    n_units = nh * (tq // MXU_DIM)
    width = nh * MLA_V_DIM
    z_blk = B_Z_OFF // width
    return pl.pallas_call(
        kern,
        grid=(batch, MLA_HEADS // nh),
        in_specs=[
            pl.BlockSpec((None, nh, seq, MLA_QK_PAD), lambda b, g: (b, g, 0, 0)),
            pl.BlockSpec((None, nh, seq, MLA_QK_PAD), lambda b, g: (b, g, 0, 0)),
            pl.BlockSpec((None, nh, nb, MLA_VT_ROWS, tk), lambda b, g: (b, g, 0, 0, 0)),
            pl.BlockSpec((seq, width), lambda b, g: (b, z_blk + g)),
        ],
        out_specs=pl.BlockSpec((seq, width), lambda b, g: (b, g)),
        out_shape=jax.ShapeDtypeStruct((batch * seq, BRANCH_WIDTH), BF16),
        scratch_shapes=[
            pltpu.VMEM((nh, MLA_VT_ROWS, tq), F32),
            pltpu.VMEM((nh, 1, tq), F32),
            pltpu.VMEM((n_units, tk, MXU_DIM), F32),
            pltpu.VMEM((n_units, tk, MXU_DIM), F32),
            pltpu.VMEM((n_units, tk, MXU_DIM), F32),
            pltpu.VMEM((n_units, 1, MXU_DIM), F32),
            pltpu.VMEM((n_units, 1, MXU_DIM), F32),
            pltpu.VMEM((n_units, 1, MXU_DIM), F32),
        ],
        compiler_params=pltpu.CompilerParams(
            dimension_semantics=("parallel", "parallel"), vmem_limit_bytes=VMEM_LIMIT),
        name="mla_attn",
    )(q_cat, k_cat, v_t, proj)


def _pad_heads(w, heads, width, padded):
    lead = w.shape[:-1]
    w = w.reshape(lead + (heads, width))
    w = jnp.pad(w, [(0, 0)] * len(lead) + [(0, 0), (0, padded - width)])
    return w.reshape(lead + (heads * padded,))


def _repack_kernel(*refs, n_in, with_gain, pieces, zero_ranges, transposed):
    ins = refs[:n_in]
    gains = refs[n_in:2 * n_in] if with_gain else None
    o_ref = refs[-1]

    def window(lo, width):
        return (slice(lo, lo + width), slice(None)) if transposed else (slice(None), slice(lo, lo + width))

    for dst, idx, src, width, scale in pieces:
        val = ins[idx][window(src, width)]
        if with_gain:
            val = val * gains[idx][...]
        if scale != 1.0:
            val = val * scale
        o_ref[window(dst, width)] = val.astype(o_ref.dtype)
    for lo, hi in zero_ranges:
        shape = (hi - lo, o_ref.shape[1]) if transposed else (o_ref.shape[0], hi - lo)
        o_ref[window(lo, hi - lo)] = jnp.zeros(shape, o_ref.dtype)


def _repack(ws, gains, pieces, out_width, zero_ranges=(), transposed=False):
    d = ws[0].shape[1] if transposed else ws[0].shape[0]
    td = min(REPACK_ROWS, d)
    with_gain = gains is not None
    pieces = [tuple(p) if len(p) == 5 else tuple(p) + (1.0,) for p in pieces]
    covered = sorted([(dst, dst + w) for dst, _, _, w, _ in pieces] + list(zero_ranges))
    assert covered[0][0] == 0 and covered[-1][1] == out_width
    assert all(a[1] == b[0] for a, b in zip(covered, covered[1:])), "output features written exactly once"
    kern = functools.partial(_repack_kernel, n_in=len(ws), with_gain=with_gain, pieces=tuple(pieces),
                             zero_ranges=tuple(zero_ranges), transposed=transposed)

    def spec(n):
        return (pl.BlockSpec((n, td), lambda i: (0, i)) if transposed
                else pl.BlockSpec((td, n), lambda i: (i, 0)))

    in_specs = [spec(w.shape[0] if transposed else w.shape[1]) for w in ws]
    args = list(ws)
    if with_gain:
        in_specs += [spec(1) for _ in gains]
        args += [g.reshape((1, d) if transposed else (d, 1)) for g in gains]
    return pl.pallas_call(
        kern,
        grid=(d // td,),
        in_specs=in_specs,
        out_specs=spec(out_width),
        out_shape=jax.ShapeDtypeStruct((out_width, d) if transposed else (d, out_width), BF16),
        compiler_params=pltpu.CompilerParams(
            dimension_semantics=("parallel",), vmem_limit_bytes=VMEM_LIMIT),
        name="repack",
    )(*args)


def _layer_a_weights(w_in, gain, w_g2, b_g):
    hk = GLA_HEADS * GLA_DK
    o = [int(v) for v in np.cumsum([0, hk, hk, BRANCH_WIDTH, GLA_GATE_RANK, BRANCH_WIDTH, MEM_WIDTH,
                                    MEM_WIDTH])]
    q_src, k_src, v_src, lr_src, z_src = o[0], o[1], o[2], o[3], o[4]
    pieces = [(A_V_OFF, 0, v_src, BRANCH_WIDTH), (A_Z_OFF, 0, z_src, BRANCH_WIDTH),
              (A_MQ_OFF, 0, z_src + BRANCH_WIDTH, MEM_WIDTH, MEM_Q_SCALE),
              (A_MZ_OFF, 0, z_src + BRANCH_WIDTH + MEM_WIDTH, MEM_WIDTH)]
    zeros = []
    for h in range(GLA_HEADS):
        q_dst, k_dst = A_Q_OFF + h * GLA_DK_PAD, A_K_OFF + h * GLA_DK_PAD
        pieces += [(q_dst, 0, q_src + h * GLA_DK, GLA_DK), (q_dst + GLA_DK, 0, lr_src, GLA_GATE_RANK),
                   (k_dst, 0, k_src + h * GLA_DK, GLA_DK)]
        zeros += [(q_dst + GLA_DK + GLA_GATE_RANK, q_dst + GLA_DK_PAD), (k_dst + GLA_DK, k_dst + GLA_DK_PAD)]
    w_all_t = _repack([w_in.T], [gain], pieces, A_WIDTH, zeros, transposed=True)
    wg = _pad_heads(w_g2, GLA_HEADS, GLA_DK, GLA_DK_PAD)
    wg = wg.reshape(GLA_GATE_RANK, GLA_HEADS, GLA_DK_PAD).transpose(1, 0, 2)
    wg = jnp.pad(wg, [(0, 0), (GLA_DK, GLA_DK_PAD - GLA_DK - GLA_GATE_RANK), (0, 0)]).astype(BF16)
    bg = _pad_heads(b_g, GLA_HEADS, GLA_DK, GLA_DK_PAD).reshape(GLA_HEADS, 1, GLA_DK_PAD)
    return w_all_t, wg, bg


def _layer_b_weights(w_in, gain_in, w_dkv, gain_dkv):
    half = MLA_ROPE_DIM // 2
    pieces = [(B_Z_OFF, 0, MLA_Q_RANK, BRANCH_WIDTH), (B_CQ_OFF, 0, 0, MLA_Q_RANK),
              (B_MQ_OFF, 0, MLA_Q_RANK + BRANCH_WIDTH, MEM_WIDTH, MEM_Q_SCALE),
              (B_MZ_OFF, 0, MLA_Q_RANK + BRANCH_WIDTH + MEM_WIDTH, MEM_WIDTH), (B_C_OFF, 1, 0, MLA_KV_RANK)]
    for rep in range(LANES // MLA_ROPE_DIM):
        kr, krot = B_KR_OFF + rep * MLA_ROPE_DIM, B_KROT_OFF + rep * MLA_ROPE_DIM
        pieces += [(kr, 1, MLA_KV_RANK, MLA_ROPE_DIM),
                   (krot, 1, MLA_KV_RANK + half, half), (krot + half, 1, MLA_KV_RANK, half)]
    return _repack([w_in, w_dkv], [gain_in, gain_dkv], pieces, B_WIDTH)


def _uq_weights(w_uq):
    half = MLA_ROPE_DIM // 2
    n_nope, n_rope = MLA_HEADS * MLA_NOPE_DIM, MLA_HEADS * MLA_ROPE_DIM
    pieces = []
    for h in range(MLA_HEADS):
        src = h * MLA_QK_DIM
        rope, rot = n_nope + h * MLA_ROPE_DIM, n_nope + n_rope + h * MLA_ROPE_DIM
        pieces += [(h * MLA_NOPE_DIM, 0, src, MLA_NOPE_DIM), (rope, 0, src + MLA_NOPE_DIM, MLA_ROPE_DIM),
                   (rot, 0, src + MLA_NOPE_DIM + half, half), (rot + half, 0, src + MLA_NOPE_DIM, half)]
    return _repack([w_uq], None, pieces, n_nope + 2 * n_rope)


def _rope_tables():
    r = MLA_ROPE_DIM
    freqs = ROPE_THETA ** (-jnp.arange(0, r, 2, dtype=F32) / r)
    freq = jnp.tile(freqs, LANES // (r // 2)).reshape(1, LANES)
    sign = jnp.tile(jnp.concatenate([-jnp.ones(r // 2, F32), jnp.ones(r // 2, F32)]),
                    LANES // r).reshape(1, LANES)
    return freq, sign


def kernel(x, mem, positions, a_pre_norm, a_w_in, a_w_g2, a_b_g, a_gla_norm, a_mem_norm, a_w_mem_kv,
           a_w_out, a_post_norm, kv_in_norm, w_dkv, kv_norm, w_uk, w_uv, b_pre_norm, b_w_in, b_q_norm,
           b_w_uq, b_mem_norm, b_w_mem_kv, b_w_out, b_post_norm):
    assert a_w_in.shape[0] == 1 and b_w_in.shape[0] == 1, "one A layer followed by one B layer"
    batch, seq, _ = x.shape
    t = batch * seq
    x2 = x.reshape(t, D_MODEL)

    kv_w = 2 * MEM_WIDTH
    w_memkv = _repack([a_w_mem_kv[0], b_w_mem_kv[0]], [a_mem_norm[0], b_mem_norm[0]],
                      [(0, 0, 0, kv_w), (kv_w, 1, 0, kv_w)], 2 * kv_w)
    memkv = _norm_matmul(mem.reshape(batch * N_MEM, D_MODEL), w_memkv, tm=batch * N_MEM, tn=2 * MEM_WIDTH)

    w_a_t, wg, bg = _layer_a_weights(a_w_in[0], a_pre_norm[0], a_w_g2[0], a_b_g[0])
    proj_a = _norm_matmul(x2, w_a_t, tm=PROJ_TM, tn=A_TN, w_transposed=True)
    gla_out = _gla(proj_a, wg, bg, a_gla_norm, batch, seq)
    mem_out = _mem_attn(proj_a, memkv, seq, A_MQ_OFF, A_MZ_OFF, kv_blk=0)
    x1 = _out_proj(gla_out, mem_out, a_w_out[0].astype(BF16), x2, a_post_norm)

    w_b = _layer_b_weights(b_w_in[0], b_pre_norm[0], w_dkv, kv_in_norm)
    proj_b = _norm_matmul(x1, w_b, tm=PROJ_TM, tn=B_TN)
    freq, sign = _rope_tables()
    tk = min(ATT_TK, seq)
    tq = min(ATT_TQ, seq)
    q_cat, k_cat, v_t = _mla_prep(
        proj_b, positions.reshape(t, 1), freq, sign, b_q_norm, kv_norm[None, :],
        _uq_weights(b_w_uq[0]), w_uk.astype(BF16), w_uv.T.astype(BF16), batch, seq, tk)
    mla_out = _mla_attn(q_cat, k_cat, v_t, proj_b, batch, seq, tq, tk)
    mem_out_b = _mem_attn(proj_b, memkv, seq, B_MQ_OFF, B_MZ_OFF, kv_blk=2)
    out = _out_proj(mla_out, mem_out_b, b_w_out[0].astype(BF16), x1, b_post_norm)
    return out.reshape(batch, seq, D_MODEL)
```

```python
import functools

import numpy as np
import jax
import jax.numpy as jnp
from jax import lax
from jax.experimental import pallas as pl
from jax.experimental.pallas import tpu as pltpu

F32 = jnp.float32
BF16 = jnp.bfloat16

D_MODEL = 2048
N_MEM = 256
MEM_HEADS = 4
MEM_HEAD_DIM = 128
MEM_WIDTH = MEM_HEADS * MEM_HEAD_DIM
BRANCH_WIDTH = D_MODEL - MEM_WIDTH
GLA_HEADS = 4
GLA_DV = BRANCH_WIDTH // GLA_HEADS
GLA_DK = GLA_DV // 2
GLA_GATE_RANK = 16
GLA_GATE_NORM = 16.0
MLA_HEADS = 12
MLA_V_DIM = 128
MLA_NOPE_DIM = 128
MLA_ROPE_DIM = 64
MLA_QK_DIM = MLA_NOPE_DIM + MLA_ROPE_DIM
MLA_Q_RANK = 512
MLA_KV_RANK = 512
ROPE_THETA = 10000.0
EPS = 1e-6

LANES = 128
MXU_DIM = 256
GLA_DK_PAD = MXU_DIM
MLA_QK_PAD = MXU_DIM
BF16_SUBLANES = 16
MLA_VT_ROWS = MLA_V_DIM + BF16_SUBLANES
ATT_HEADS_PER_STEP = 2
LOG2_E = 1.4426950408889634
LN_2 = 0.6931471805599453
MEM_Q_SCALE = MEM_HEAD_DIM ** -0.5 * LOG2_E
VMEM_LIMIT = 56 * 1024 * 1024

PROJ_TM = 1024
PROJ_NORM_ROWS = 256
REPACK_ROWS = 256
OUT_TM = 512
OUT_CHUNK = 256
GLA_SPAN = 256
GLA_SUB = 32
GLA_HEADS_PER_STEP = 4
GLA_SPANS_PER_STEP = 2
GLA_STAGE_SKEW = 1
MEM_TM = 512
PREP_TM = 512
ATT_TQ = 512
ATT_TK = 512

A_V_OFF = 0
A_Z_OFF = BRANCH_WIDTH
A_MQ_OFF = 2 * BRANCH_WIDTH
A_MZ_OFF = A_MQ_OFF + MEM_WIDTH
A_Q_OFF = A_MZ_OFF + MEM_WIDTH
A_K_OFF = A_Q_OFF + GLA_HEADS * GLA_DK_PAD
A_WIDTH = A_K_OFF + GLA_HEADS * GLA_DK_PAD
A_TN = 2048

B_Z_OFF = 0
B_CQ_OFF = BRANCH_WIDTH
B_MQ_OFF = B_CQ_OFF + MLA_Q_RANK
B_MZ_OFF = B_MQ_OFF + MEM_WIDTH
B_C_OFF = B_MZ_OFF + MEM_WIDTH
B_KR_OFF = B_C_OFF + MLA_KV_RANK
B_KROT_OFF = B_KR_OFF + LANES
B_WIDTH = B_KROT_OFF + LANES
B_TN = 1280


def _silu(z):
    return z * (1.0 / (1.0 + jnp.exp(-z)))


def _scaled_log_sigmoid(g, scale):
    softplus2 = jnp.log2(1.0 + jnp.exp2(jnp.abs(g) * -LOG2_E))
    return jnp.minimum(g, 0.0) * scale - softplus2 * (LN_2 * scale)


def _dot(a, b):
    return jnp.dot(a, b, preferred_element_type=F32)


def _dot_nt(a, b):
    return lax.dot_general(a, b, (((1,), (1,)), ((), ())), preferred_element_type=F32)


def _dot_tn(a, b):
    return lax.dot_general(a, b, (((0,), (0,)), ((), ())), preferred_element_type=F32)


def _norm_matmul_kernel(x_ref, w_ref, o_ref, h_ref, r_ref, *, norm_rows, w_transposed):
    j = pl.program_id(1)

    def scaled_product(h, r):
        acc = _dot_nt(h, w_ref[...]) if w_transposed else _dot(h, w_ref[...])
        return (acc * r).astype(o_ref.dtype)

    @pl.when(j == 0)
    def _():
        tm = x_ref.shape[0]
        for c in range(tm // norm_rows):
            rows = pl.ds(c * norm_rows, norm_rows)
            x = x_ref[rows, :]
            r = lax.rsqrt(jnp.mean(x * x, axis=-1, keepdims=True) + EPS)
            h = x.astype(BF16)
            r_ref[rows, :] = r
            h_ref[rows, :] = h
            o_ref[rows, :] = scaled_product(h, r)

    @pl.when(j > 0)
    def _():
        o_ref[...] = scaled_product(h_ref[...], r_ref[...])


def _norm_matmul(x, w, *, tm, tn, w_transposed=False):
    m, d = x.shape
    n = w.shape[0] if w_transposed else w.shape[1]
    tm = min(tm, m)
    kern = functools.partial(_norm_matmul_kernel, norm_rows=min(PROJ_NORM_ROWS, tm),
                             w_transposed=w_transposed)
    w_spec = (pl.BlockSpec((tn, d), lambda i, j: (j, 0)) if w_transposed
              else pl.BlockSpec((d, tn), lambda i, j: (0, j)))
    return pl.pallas_call(
        kern,
        grid=(m // tm, n // tn),
        in_specs=[
            pl.BlockSpec((tm, d), lambda i, j: (i, 0)),
            w_spec,
        ],
        out_specs=pl.BlockSpec((tm, tn), lambda i, j: (i, j)),
        out_shape=jax.ShapeDtypeStruct((m, n), BF16),
        scratch_shapes=[pltpu.VMEM((tm, d), BF16), pltpu.VMEM((tm, 1), F32)],
        compiler_params=pltpu.CompilerParams(
            dimension_semantics=("parallel", "arbitrary"), vmem_limit_bytes=VMEM_LIMIT),
        name="norm_matmul",
    )(x, w)


def _out_proj_kernel(a_ref, m_ref, wa_ref, wm_ref, x_ref, g_ref, o_ref, *, chunk):
    for c in range(a_ref.shape[0] // chunk):
        rows = pl.ds(c * chunk, chunk)
        y = _dot(a_ref[rows, :], wa_ref[...]) + _dot(m_ref[rows, :], wm_ref[...])
        ms = jnp.mean(y * y, axis=-1, keepdims=True)
        o_ref[rows, :] = x_ref[rows, :] + y * lax.rsqrt(ms + EPS) * g_ref[...]


def _out_proj(a, mo, w_out, x, gain):
    t = x.shape[0]
    tm = min(OUT_TM, t)
    mem_blk = BRANCH_WIDTH // MEM_WIDTH
    return pl.pallas_call(
        functools.partial(_out_proj_kernel, chunk=min(OUT_CHUNK, tm)),
        grid=(t // tm,),
        in_specs=[
            pl.BlockSpec((tm, BRANCH_WIDTH), lambda i: (i, 0)),
            pl.BlockSpec((tm, MEM_WIDTH), lambda i: (i, 0)),
            pl.BlockSpec((BRANCH_WIDTH, D_MODEL), lambda i: (0, 0)),
            pl.BlockSpec((MEM_WIDTH, D_MODEL), lambda i: (mem_blk, 0)),
            pl.BlockSpec((tm, D_MODEL), lambda i: (i, 0)),
            pl.BlockSpec((1, D_MODEL), lambda i: (0, 0)),
        ],
        out_specs=pl.BlockSpec((tm, D_MODEL), lambda i: (i, 0)),
        out_shape=jax.ShapeDtypeStruct((t, D_MODEL), F32),
        compiler_params=pltpu.CompilerParams(
            dimension_semantics=("parallel",), vmem_limit_bytes=VMEM_LIMIT),
        name="out_proj",
    )(a, mo, w_out, w_out, x, gain)


def _gla_head_stages(h, row0, v_ref, z_ref, q_ref, k_ref, wg_ref, bg_ref, gn_ref, tril_ref, lvl_ref, o_ref,
                     s_ref, sub):
    span = tril_ref.shape[0]
    rows = slice(row0, row0 + span)
    n_sub = span // sub
    n_levels = n_sub.bit_length() - 1
    scale = GLA_DK ** -0.5
    kcols = slice(h * GLA_DK_PAD, (h + 1) * GLA_DK_PAD)
    vcols = slice(h * GLA_DV, (h + 1) * GLA_DV)

    qb = q_ref[rows, kcols]
    g = _dot(qb, wg_ref[h]) + bg_ref[h]
    yield
    log_a = _scaled_log_sigmoid(g, 1.0 / GLA_GATE_NORM)
    a_hi = log_a.astype(BF16)
    rem = log_a - a_hi.astype(F32)
    a_mid = rem.astype(BF16)
    a_lo = (rem - a_mid.astype(F32)).astype(BF16)
    yield
    tril = tril_ref[...]
    cum = _dot(tril, a_hi) + _dot(tril, a_mid) + _dot(tril, a_lo)
    yield

    refs = jnp.concatenate([cum[i * sub:i * sub + 1] for i in range(n_sub)] + [cum[span - 1:span]], axis=0)
    own = refs[:n_sub]

    def ref_rows(index_of):
        return jnp.concatenate([refs[index_of(i):index_of(i) + 1] for i in range(n_sub)], axis=0)

    def per_row(f):
        return jnp.concatenate(
            [jnp.broadcast_to(f[i:i + 1], (sub, f.shape[1])) for i in range(n_sub)], axis=0)

    own_b = per_row(own)
    qd = qb.astype(F32) * (jnp.exp(cum - own_b) * scale)
    kd = k_ref[rows, kcols].astype(F32) * jnp.exp(own_b - cum)
    yield
    q_ops, k_ops = [qd.astype(BF16)], [kd.astype(BF16)]
    for lvl in range(1, n_levels + 1):
        n = 1 << (lvl - 1)
        block_start = ref_rows(lambda i: (i // n) * n)
        next_start = ref_rows(lambda i: (i // n + 1) * n)
        q_ops.append(q_ops[0] if n == 1 else (qd * per_row(jnp.exp(own - block_start))).astype(BF16))
        k_ops.append((kd * per_row(jnp.exp(next_start - own))).astype(BF16))
    q_state = (qd * per_row(jnp.exp(own))).astype(BF16)
    k_state = (kd * per_row(jnp.exp(refs[n_sub:] - own))).astype(BF16)
    yield
    atts = [_dot_nt(qo, ko) for qo, ko in zip(q_ops, k_ops)]
    yield
    level = lvl_ref[...]
    att = jnp.where(level == n_levels, atts[n_levels], 0.0)
    for lvl in reversed(range(n_levels)):
        att = jnp.where(level == lvl, atts[lvl], att)
    yield
    state = s_ref[h]
    v = v_ref[rows, vcols]
    o = _dot(jnp.concatenate([att.astype(BF16), q_state], axis=1),
             jnp.concatenate([v, state.astype(BF16)], axis=0))
    decay_col = jnp.transpose(jnp.exp(cum[span - 8:span, :]))[:, 7:8]
    s_ref[h] = decay_col * state + _dot_tn(k_state, v)
    yield
    on = o * lax.rsqrt(jnp.mean(o * o, axis=-1, keepdims=True) + EPS) * gn_ref[...]
    o_ref[rows, vcols] = (on * _silu(z_ref[rows, vcols].astype(F32))).astype(o_ref.dtype)
    yield


def _gla_kernel(v_ref, z_ref, q_ref, k_ref, wg_ref, bg_ref, gn_ref, tril_ref, lvl_ref, o_ref, s_ref, *,
                sub, skew):
    @pl.when(pl.program_id(2) == 0)
    def _():
        s_ref[...] = jnp.zeros_like(s_ref)

    span = tril_ref.shape[0]
    heads = [_gla_head_stages(h, row0, v_ref, z_ref, q_ref, k_ref, wg_ref, bg_ref, gn_ref, tril_ref,
                              lvl_ref, o_ref, s_ref, sub)
             for row0 in range(0, q_ref.shape[0], span) for h in range(s_ref.shape[0])]
    live = list(range(len(heads)))
    step = 0
    while live:
        for i in list(live):
            if step >= i * skew:
                try:
                    next(heads[i])
                except StopIteration:
                    live.remove(i)
        step += 1


def _prefix_matrix(rows):
    r = np.arange(rows)
    return jnp.asarray(r[:, None] >= r[None, :], BF16)


def _level_matrix(rows, sub):
    n_levels = (rows // sub).bit_length() - 1
    t, s = np.arange(rows)[:, None], np.arange(rows)[None, :]
    level = np.full((rows, rows), n_levels + 1, np.int32)
    level[(t // sub == s // sub) & (s <= t)] = 0
    for lvl in range(1, n_levels + 1):
        size = sub << (lvl - 1)
        level[((t // size) % 2 == 1) & (s // size == t // size - 1)] = lvl
    return jnp.asarray(level)


def _gla(proj, wg, bg, gn, batch, seq):
    span = min(GLA_SPAN, seq)
    assert span % GLA_SUB == 0 and (span // GLA_SUB) & (span // GLA_SUB - 1) == 0
    rows = span * min(GLA_SPANS_PER_STEP, seq // span)
    ns = seq // rows
    nh = GLA_HEADS_PER_STEP
    kern = functools.partial(_gla_kernel, sub=GLA_SUB, skew=GLA_STAGE_SKEW)
    vw, kw = nh * GLA_DV, nh * GLA_DK_PAD
    v_blk, z_blk = A_V_OFF // vw, A_Z_OFF // vw
    q_blk, k_blk = A_Q_OFF // kw, A_K_OFF // kw
    group = span
    return pl.pallas_call(
        kern,
        grid=(batch, GLA_HEADS // nh, ns),
        in_specs=[
            pl.BlockSpec((rows, vw), lambda b, g, s: (b * ns + s, v_blk + g)),
            pl.BlockSpec((rows, vw), lambda b, g, s: (b * ns + s, z_blk + g)),
            pl.BlockSpec((rows, kw), lambda b, g, s: (b * ns + s, q_blk + g)),
            pl.BlockSpec((rows, kw), lambda b, g, s: (b * ns + s, k_blk + g)),
            pl.BlockSpec((nh, GLA_DK_PAD, GLA_DK_PAD), lambda b, g, s: (g, 0, 0)),
            pl.BlockSpec((nh, 1, GLA_DK_PAD), lambda b, g, s: (g, 0, 0)),
            pl.BlockSpec((1, GLA_DV), lambda b, g, s: (0, 0)),
            pl.BlockSpec((group, group), lambda b, g, s: (0, 0)),
            pl.BlockSpec((group, group), lambda b, g, s: (0, 0)),
        ],
        out_specs=pl.BlockSpec((rows, vw), lambda b, g, s: (b * ns + s, g)),
        out_shape=jax.ShapeDtypeStruct((batch * seq, BRANCH_WIDTH), BF16),
        scratch_shapes=[pltpu.VMEM((nh, GLA_DK_PAD, GLA_DV), F32)],
        compiler_params=pltpu.CompilerParams(
            dimension_semantics=("parallel", "parallel", "arbitrary"), vmem_limit_bytes=VMEM_LIMIT),
        name="gla",
    )(proj, proj, proj, proj, wg, bg, gn, _prefix_matrix(group), _level_matrix(group, GLA_SUB))


def _mem_attn_kernel(q_ref, z_ref, k_ref, v_ref, o_ref):
    for h in range(MEM_HEADS):
        cols = slice(h * MEM_HEAD_DIM, (h + 1) * MEM_HEAD_DIM)
        s = _dot_nt(q_ref[:, cols], k_ref[:, cols])
        p = jnp.exp2(s - jnp.max(s, axis=-1, keepdims=True))
        l = jnp.sum(p, axis=-1, keepdims=True)
        o = _dot(p.astype(BF16), v_ref[:, cols]) * (1.0 / l)
        o_ref[:, cols] = (o * _silu(z_ref[:, cols].astype(F32))).astype(o_ref.dtype)


def _mem_attn(proj, memkv, seq, q_off, z_off, kv_blk):
    t = proj.shape[0]
    tm = min(MEM_TM, seq)
    per_batch = seq // tm
    q_blk, z_blk = q_off // MEM_WIDTH, z_off // MEM_WIDTH
    return pl.pallas_call(
        _mem_attn_kernel,
        grid=(t // tm,),
        in_specs=[
            pl.BlockSpec((tm, MEM_WIDTH), lambda i: (i, q_blk)),
            pl.BlockSpec((tm, MEM_WIDTH), lambda i: (i, z_blk)),
            pl.BlockSpec((N_MEM, MEM_WIDTH), lambda i: (i // per_batch, kv_blk)),
            pl.BlockSpec((N_MEM, MEM_WIDTH), lambda i: (i // per_batch, kv_blk + 1)),
        ],
        out_specs=pl.BlockSpec((tm, MEM_WIDTH), lambda i: (i, 0)),
        out_shape=jax.ShapeDtypeStruct((t, MEM_WIDTH), BF16),
        compiler_params=pltpu.CompilerParams(
            dimension_semantics=("parallel",), vmem_limit_bytes=VMEM_LIMIT),
        name="mem_attn",
    )(proj, proj, memkv, memkv)


def _rotary_tables(pos, freq, sign):
    rows = pos.shape[0]
    groups = LANES // (MLA_ROPE_DIM // 2)
    if rows % (8 * groups) != 0:
        ang = pos * freq
        return jnp.cos(ang), jnp.sin(ang) * sign
    blk = rows // groups
    lane_group = lax.broadcasted_iota(jnp.int32, (1, LANES), 1) // (MLA_ROPE_DIM // 2)
    packed_pos = jnp.zeros((blk, LANES), F32)
    for g in range(groups):
        packed_pos = jnp.where(lane_group == g, pos[g * blk:(g + 1) * blk], packed_pos)
    ang = packed_pos * freq
    tables = []
    for packed in (jnp.cos(ang), jnp.sin(ang)):
        blocks = []
        for g in range(groups):
            own = jnp.where(lane_group == g, packed, 0.0)
            spread = own
            for shift in range(1, groups):
                spread = spread + pltpu.roll(own, shift * (LANES // groups), axis=1)
            blocks.append(spread)
        tables.append(jnp.concatenate(blocks, axis=0))
    return tables[0], tables[1] * sign


def _mla_prep_kernel(cq_ref, c_ref, kr_ref, krot_ref, pos_ref, freq_ref, sign_ref, qg_ref, kvg_ref,
                     wuq_ref, wuk_ref, wuvt_ref, q_ref, k_ref, vt_ref, *, tk):
    tm = cq_ref.shape[0]
    n_nope = MLA_HEADS * MLA_NOPE_DIM
    n_rope = MLA_HEADS * MLA_ROPE_DIM

    cq = cq_ref[...].astype(F32)
    cqn = cq * lax.rsqrt(jnp.mean(cq * cq, axis=-1, keepdims=True) + EPS)
    cqn = (cqn * (qg_ref[...] * (MLA_QK_DIM ** -0.5 * LOG2_E))).astype(BF16)
    c = c_ref[...].astype(F32)
    cn = c * lax.rsqrt(jnp.mean(c * c, axis=-1, keepdims=True) + EPS)
    cn = (cn * kvg_ref[...]).astype(BF16)

    q_nope = _dot(cqn, wuq_ref[:, :n_nope])
    for h in range(MLA_HEADS):
        q_ref[h, :, :MLA_NOPE_DIM] = q_nope[:, h * MLA_NOPE_DIM:(h + 1) * MLA_NOPE_DIM].astype(BF16)

    cos, sin = _rotary_tables(pos_ref[...].astype(F32), freq_ref[...], sign_ref[...])

    k_nope = _dot(cn, wuk_ref[...])
    for h in range(MLA_HEADS):
        k_ref[h, :, :MLA_NOPE_DIM] = k_nope[:, h * MLA_NOPE_DIM:(h + 1) * MLA_NOPE_DIM].astype(BF16)
    v_t = _dot_nt(wuvt_ref[...], cn).astype(BF16)
    ones_rows = (lax.broadcasted_iota(jnp.int32, (MLA_VT_ROWS - MLA_V_DIM, tk), 0) == 0).astype(BF16)
    for h in range(MLA_HEADS):
        for u in range(tm // tk):
            vt_ref[h, u, :MLA_V_DIM, :] = v_t[h * MLA_V_DIM:(h + 1) * MLA_V_DIM, u * tk:(u + 1) * tk]
            vt_ref[h, u, MLA_V_DIM:, :] = ones_rows

    lane = lax.broadcasted_iota(jnp.int32, (1, LANES), 1)
    slot_mask = [(lane < MLA_ROPE_DIM).astype(F32), (lane >= MLA_ROPE_DIM).astype(F32)]
    q_r = _dot(cqn, wuq_ref[:, n_nope:n_nope + n_rope])
    q_rot = _dot(cqn, wuq_ref[:, n_nope + n_rope:])
    for u in range(MLA_HEADS // 2):
        cols = slice(u * LANES, (u + 1) * LANES)
        roped = q_r[:, cols] * cos + q_rot[:, cols] * sin
        for half in range(2):
            q_ref[2 * u + half, :, MLA_NOPE_DIM:] = (roped * slot_mask[half]).astype(BF16)
    k_rope = (kr_ref[...].astype(F32) * cos + krot_ref[...].astype(F32) * sin).astype(BF16)
    for h in range(MLA_HEADS):
        k_ref[h, :, MLA_NOPE_DIM:] = k_rope


def _mla_prep(proj, pos_col, freq, sign, q_gain, kv_gain, w_uq, w_uk, w_uv_t, batch, seq, tk):
    tm = min(PREP_TM, seq)
    per_batch = seq // tm
    kern = functools.partial(_mla_prep_kernel, tk=tk)
    const = lambda i: (0, 0)
    head_map = lambda i: (i // per_batch, 0, i % per_batch, 0)
    return pl.pallas_call(
        kern,
        grid=(batch * per_batch,),
        in_specs=[
            pl.BlockSpec((tm, MLA_Q_RANK), lambda i: (i, B_CQ_OFF // MLA_Q_RANK)),
            pl.BlockSpec((tm, MLA_KV_RANK), lambda i: (i, B_C_OFF // MLA_KV_RANK)),
            pl.BlockSpec((tm, LANES), lambda i: (i, B_KR_OFF // LANES)),
            pl.BlockSpec((tm, LANES), lambda i: (i, B_KROT_OFF // LANES)),
            pl.BlockSpec((tm, 1), lambda i: (i, 0)),
            pl.BlockSpec((1, LANES), const),
            pl.BlockSpec((1, LANES), const),
            pl.BlockSpec((1, MLA_Q_RANK), const),
            pl.BlockSpec((1, MLA_KV_RANK), const),
            pl.BlockSpec(w_uq.shape, const),
            pl.BlockSpec(w_uk.shape, const),
            pl.BlockSpec(w_uv_t.shape, const),
        ],
        out_specs=[
            pl.BlockSpec((None, MLA_HEADS, tm, MLA_QK_PAD), head_map),
            pl.BlockSpec((None, MLA_HEADS, tm, MLA_QK_PAD), head_map),
            pl.BlockSpec((None, MLA_HEADS, tm // tk, MLA_VT_ROWS, tk),
                         lambda i: (i // per_batch, 0, i % per_batch, 0, 0)),
        ],
        out_shape=[
            jax.ShapeDtypeStruct((batch, MLA_HEADS, seq, MLA_QK_PAD), BF16),
            jax.ShapeDtypeStruct((batch, MLA_HEADS, seq, MLA_QK_PAD), BF16),
            jax.ShapeDtypeStruct((batch, MLA_HEADS, seq // tk, MLA_VT_ROWS, tk), BF16),
        ],
        compiler_params=pltpu.CompilerParams(
            dimension_semantics=("parallel",), vmem_limit_bytes=VMEM_LIMIT),
        name="mla_prep",
    )(proj, proj, proj, proj, pos_col, freq, sign, q_gain, kv_gain, w_uq, w_uk, w_uv_t)


def _max_over_rows(x, ways=8):
    rows = x.shape[0]
    slab = rows // ways
    parts = [x[i * slab:(i + 1) * slab] for i in range(ways)] if slab >= 8 and rows % ways == 0 else [x]
    while len(parts) > 1:
        parts = [jnp.maximum(parts[i], parts[i + 1]) for i in range(0, len(parts), 2)]
    return jnp.max(parts[0], axis=0, keepdims=True)


def _mla_attn_kernel(q_ref, k_ref, vt_ref, z_ref, o_ref, acc_ref, m_ref, done_ref, s0_ref, s1_ref, s2_ref,
                     smax0_ref, smax1_ref, smax2_ref, *, tq, tk):
    n_heads, seq = q_ref.shape[0], q_ref.shape[1]
    n_q = seq // tq
    tn = MXU_DIM
    units = [(h, n) for h in range(n_heads) for n in range(tq // tn)]
    s_slots = (s0_ref, s1_ref, s2_ref)
    smax_slots = (smax0_ref, smax1_ref, smax2_ref)
    FIRST = 2

    def n_keys(n, diagonal):
        return min(tk, (n + 1) * tn) if diagonal else tk

    def produce(qi, kj, slot, u, diagonal):
        h, n = units[u]
        nk = n_keys(n, diagonal)
        krows = pl.ds(pl.multiple_of(kj * tk, tk), nk)
        qrows = pl.ds(pl.multiple_of(qi * tq + n * tn, tn), tn)
        s_t = _dot_nt(k_ref[h, krows, :], q_ref[h, qrows, :])
        if diagonal:
            kpos = lax.broadcasted_iota(jnp.int32, (nk, tn), 0)
            qpos = lax.broadcasted_iota(jnp.int32, (nk, tn), 1) + n * tn
            s_t = jnp.where(kpos <= qpos, s_t, -jnp.inf)
        s_slots[slot][u, :nk, :] = s_t
        smax_slots[slot][u] = _max_over_rows(s_t)

    def consume(kj, slot, u, diagonal):
        h, n = units[u]
        nk = n_keys(n, diagonal)
        cols = slice(n * tn, (n + 1) * tn)
        m_prev = m_ref[h, :, cols]
        m_new = jnp.maximum(m_prev, smax_slots[slot][u])
        alpha = jnp.exp2(m_prev - m_new)
        p = jnp.exp2(s_slots[slot][u, :nk, :] - m_new).astype(BF16)
        m_ref[h, :, cols] = m_new
        acc_ref[h, :, cols] = alpha * acc_ref[h, :, cols] + _dot(vt_ref[h, kj, :, :nk], p)

    def produce_all(qi, kj, slot, diagonal):
        for u in range(len(units)):
            produce(qi, kj, slot, u, diagonal)

    def write_out(h, qp):
        rows = pl.ds(qp * tq if isinstance(qp, int) else pl.multiple_of(qp * tq, tq), tq)
        cols = slice(h * MLA_V_DIM, (h + 1) * MLA_V_DIM)
        o = jnp.transpose(done_ref[h])
        o_ref[rows, cols] = (o * _silu(z_ref[rows, cols].astype(F32))).astype(o_ref.dtype)

    def step(kj, slot, diagonal, nxt=None, write_out_of=None):
        if nxt is not None:
            produce(nxt[0], nxt[1], nxt[2], 0, nxt[3])
        per_head = len(units) // n_heads
        for u in range(len(units)):
            consume(kj, slot, u, diagonal)
            if write_out_of is not None and u % per_head == per_head - 1:
                write_out(u // per_head, write_out_of)
            if nxt is not None and u + 1 < len(units):
                produce(nxt[0], nxt[1], nxt[2], u + 1, nxt[3])

    def last_step(qi, slot):
        if n_q == 1:
            step(qi, slot, True)
            return

        @pl.when(qi < n_q - 1)
        def _():
            step(qi, slot, True, (qi + 1, 0, FIRST, False))

        last_q = n_q - 1
        if slot == {1: 0, 2: 1, 3: 0}[last_q - 2 * max(0, (last_q - 2) // 2)]:
            @pl.when(qi == last_q)
            def _():
                step(qi, slot, True)

    def q_block(qi, carry):
        m_ref[...] = jnp.full_like(m_ref, -jnp.inf)
        acc_ref[...] = jnp.zeros_like(acc_ref)

        @pl.when(qi == 0)
        def _():
            step(0, FIRST, True)
            if n_q > 1:
                produce_all(1, 0, FIRST, False)

        @pl.when(qi == 1)
        def _():
            step(0, FIRST, False, (qi, 1, 0, True), write_out_of=qi - 1)

        @pl.when(qi > 1)
        def _():
            step(0, FIRST, False, (qi, 1, 0, False), write_out_of=qi - 1)

        def steady_pair(i, c):
            kj = 2 * i + 1
            step(kj, 0, False, (qi, kj + 1, 1, False))
            step(kj + 1, 1, False, (qi, kj + 2, 0, False))
            return c

        n_pairs = jnp.maximum(qi - 2, 0) // 2
        lax.fori_loop(0, n_pairs, steady_pair, 0)
        kj = 2 * n_pairs + 1
        left = qi - kj + 1

        @pl.when((qi > 0) & (left == 1))
        def _():
            last_step(qi, 0)

        @pl.when((qi > 0) & (left == 2))
        def _():
            step(kj, 0, False, (qi, kj + 1, 1, True))
            last_step(qi, 1)

        @pl.when((qi > 0) & (left == 3))
        def _():
            step(kj, 0, False, (qi, kj + 1, 1, False))
            step(kj + 1, 1, False, (qi, kj + 2, 0, True))
            last_step(qi, 0)

        for h in range(n_heads):
            acc = acc_ref[h]
            done_ref[h] = acc[:MLA_V_DIM] * (1.0 / acc[MLA_V_DIM:MLA_V_DIM + 1])
        return carry

    produce_all(0, 0, FIRST, True)
    lax.fori_loop(0, n_q, q_block, 0)
    for h in range(n_heads):
        write_out(h, n_q - 1)


def _mla_attn(q_cat, k_cat, v_t, proj, batch, seq, tq, tk):
    kern = functools.partial(_mla_attn_kernel, tq=tq, tk=tk)
    nb = seq // tk
    nh = ATT_HEADS_PER_STEP
    n_units = nh * (tq // MXU_DIM)
    width = nh * MLA_V_DIM
    z_blk = B_Z_OFF // width
    return pl.pallas_call(
        kern,
        grid=(batch, MLA_HEADS // nh),
        in_specs=[
            pl.BlockSpec((None, nh, seq, MLA_QK_PAD), lambda b, g: (b, g, 0, 0)),
            pl.BlockSpec((None, nh, seq, MLA_QK_PAD), lambda b, g: (b, g, 0, 0)),
            pl.BlockSpec((None, nh, nb, MLA_VT_ROWS, tk), lambda b, g: (b, g, 0, 0, 0)),
            pl.BlockSpec((seq, width), lambda b, g: (b, z_blk + g)),
        ],
        out_specs=pl.BlockSpec((seq, width), lambda b, g: (b, g)),
        out_shape=jax.ShapeDtypeStruct((batch * seq, BRANCH_WIDTH), BF16),
        scratch_shapes=[
            pltpu.VMEM((nh, MLA_VT_ROWS, tq), F32),
            pltpu.VMEM((nh, 1, tq), F32),
            pltpu.VMEM((nh, MLA_V_DIM, tq), F32),
            pltpu.VMEM((n_units, tk, MXU_DIM), F32),
            pltpu.VMEM((n_units, tk, MXU_DIM), F32),
            pltpu.VMEM((n_units, tk, MXU_DIM), F32),
            pltpu.VMEM((n_units, 1, MXU_DIM), F32),
            pltpu.VMEM((n_units, 1, MXU_DIM), F32),
            pltpu.VMEM((n_units, 1, MXU_DIM), F32),
        ],
        compiler_params=pltpu.CompilerParams(
            dimension_semantics=("parallel", "parallel"), vmem_limit_bytes=VMEM_LIMIT),
        name="mla_attn",
    )(q_cat, k_cat, v_t, proj)


def _pad_heads(w, heads, width, padded):
    lead = w.shape[:-1]
    w = w.reshape(lead + (heads, width))
    w = jnp.pad(w, [(0, 0)] * len(lead) + [(0, 0), (0, padded - width)])
    return w.reshape(lead + (heads * padded,))


def _repack_kernel(*refs, n_in, with_gain, pieces, zero_ranges, transposed):
    ins = refs[:n_in]
    gains = refs[n_in:2 * n_in] if with_gain else None
    o_ref = refs[-1]

    def window(lo, width):
        return (slice(lo, lo + width), slice(None)) if transposed else (slice(None), slice(lo, lo + width))

    for dst, idx, src, width, scale in pieces:
        val = ins[idx][window(src, width)]
        if with_gain:
            val = val * gains[idx][...]
        if scale != 1.0:
            val = val * scale
        o_ref[window(dst, width)] = val.astype(o_ref.dtype)
    for lo, hi in zero_ranges:
        shape = (hi - lo, o_ref.shape[1]) if transposed else (o_ref.shape[0], hi - lo)
        o_ref[window(lo, hi - lo)] = jnp.zeros(shape, o_ref.dtype)


def _repack(ws, gains, pieces, out_width, zero_ranges=(), transposed=False):
    d = ws[0].shape[1] if transposed else ws[0].shape[0]
    td = min(REPACK_ROWS, d)
    with_gain = gains is not None
    pieces = [tuple(p) if len(p) == 5 else tuple(p) + (1.0,) for p in pieces]
    covered = sorted([(dst, dst + w) for dst, _, _, w, _ in pieces] + list(zero_ranges))
    assert covered[0][0] == 0 and covered[-1][1] == out_width
    assert all(a[1] == b[0] for a, b in zip(covered, covered[1:])), "output features written exactly once"
    kern = functools.partial(_repack_kernel, n_in=len(ws), with_gain=with_gain, pieces=tuple(pieces),
                             zero_ranges=tuple(zero_ranges), transposed=transposed)

    def spec(n):
        return (pl.BlockSpec((n, td), lambda i: (0, i)) if transposed
                else pl.BlockSpec((td, n), lambda i: (i, 0)))

    in_specs = [spec(w.shape[0] if transposed else w.shape[1]) for w in ws]
    args = list(ws)
    if with_gain:
        in_specs += [spec(1) for _ in gains]
        args += [g.reshape((1, d) if transposed else (d, 1)) for g in gains]
    return pl.pallas_call(
        kern,
        grid=(d // td,),
        in_specs=in_specs,
        out_specs=spec(out_width),
        out_shape=jax.ShapeDtypeStruct((out_width, d) if transposed else (d, out_width), BF16),
        compiler_params=pltpu.CompilerParams(
            dimension_semantics=("parallel",), vmem_limit_bytes=VMEM_LIMIT),
        name="repack",
    )(*args)


def _layer_a_weights(w_in, gain, w_g2, b_g):
    hk = GLA_HEADS * GLA_DK
    o = [int(v) for v in np.cumsum([0, hk, hk, BRANCH_WIDTH, GLA_GATE_RANK, BRANCH_WIDTH, MEM_WIDTH,
                                    MEM_WIDTH])]
    q_src, k_src, v_src, lr_src, z_src = o[0], o[1], o[2], o[3], o[4]
    pieces = [(A_V_OFF, 0, v_src, BRANCH_WIDTH), (A_Z_OFF, 0, z_src, BRANCH_WIDTH),
              (A_MQ_OFF, 0, z_src + BRANCH_WIDTH, MEM_WIDTH, MEM_Q_SCALE),
              (A_MZ_OFF, 0, z_src + BRANCH_WIDTH + MEM_WIDTH, MEM_WIDTH)]
    zeros = []
    for h in range(GLA_HEADS):
        q_dst, k_dst = A_Q_OFF + h * GLA_DK_PAD, A_K_OFF + h * GLA_DK_PAD
        pieces += [(q_dst, 0, q_src + h * GLA_DK, GLA_DK), (q_dst + GLA_DK, 0, lr_src, GLA_GATE_RANK),
                   (k_dst, 0, k_src + h * GLA_DK, GLA_DK)]
        zeros += [(q_dst + GLA_DK + GLA_GATE_RANK, q_dst + GLA_DK_PAD), (k_dst + GLA_DK, k_dst + GLA_DK_PAD)]
    w_all_t = _repack([w_in.T], [gain], pieces, A_WIDTH, zeros, transposed=True)
    wg = _pad_heads(w_g2, GLA_HEADS, GLA_DK, GLA_DK_PAD)
    wg = wg.reshape(GLA_GATE_RANK, GLA_HEADS, GLA_DK_PAD).transpose(1, 0, 2)
    wg = jnp.pad(wg, [(0, 0), (GLA_DK, GLA_DK_PAD - GLA_DK - GLA_GATE_RANK), (0, 0)]).astype(BF16)
    bg = _pad_heads(b_g, GLA_HEADS, GLA_DK, GLA_DK_PAD).reshape(GLA_HEADS, 1, GLA_DK_PAD)
    return w_all_t, wg, bg


def _layer_b_weights(w_in, gain_in, w_dkv, gain_dkv):
    half = MLA_ROPE_DIM // 2
    pieces = [(B_Z_OFF, 0, MLA_Q_RANK, BRANCH_WIDTH), (B_CQ_OFF, 0, 0, MLA_Q_RANK),
              (B_MQ_OFF, 0, MLA_Q_RANK + BRANCH_WIDTH, MEM_WIDTH, MEM_Q_SCALE),
              (B_MZ_OFF, 0, MLA_Q_RANK + BRANCH_WIDTH + MEM_WIDTH, MEM_WIDTH), (B_C_OFF, 1, 0, MLA_KV_RANK)]
    for rep in range(LANES // MLA_ROPE_DIM):
        kr, krot = B_KR_OFF + rep * MLA_ROPE_DIM, B_KROT_OFF + rep * MLA_ROPE_DIM
        pieces += [(kr, 1, MLA_KV_RANK, MLA_ROPE_DIM),
                   (krot, 1, MLA_KV_RANK + half, half), (krot + half, 1, MLA_KV_RANK, half)]
    return _repack([w_in, w_dkv], [gain_in, gain_dkv], pieces, B_WIDTH)


def _uq_weights(w_uq):
    half = MLA_ROPE_DIM // 2
    n_nope, n_rope = MLA_HEADS * MLA_NOPE_DIM, MLA_HEADS * MLA_ROPE_DIM
    pieces = []
    for h in range(MLA_HEADS):
        src = h * MLA_QK_DIM
        rope, rot = n_nope + h * MLA_ROPE_DIM, n_nope + n_rope + h * MLA_ROPE_DIM
        pieces += [(h * MLA_NOPE_DIM, 0, src, MLA_NOPE_DIM), (rope, 0, src + MLA_NOPE_DIM, MLA_ROPE_DIM),
                   (rot, 0, src + MLA_NOPE_DIM + half, half), (rot + half, 0, src + MLA_NOPE_DIM, half)]
    return _repack([w_uq], None, pieces, n_nope + 2 * n_rope)


def _rope_tables():
    r = MLA_ROPE_DIM
    freqs = ROPE_THETA ** (-jnp.arange(0, r, 2, dtype=F32) / r)
    freq = jnp.tile(freqs, LANES // (r // 2)).reshape(1, LANES)
    sign = jnp.tile(jnp.concatenate([-jnp.ones(r // 2, F32), jnp.ones(r // 2, F32)]),
                    LANES // r).reshape(1, LANES)
    return freq, sign


def kernel(x, mem, positions, a_pre_norm, a_w_in, a_w_g2, a_b_g, a_gla_norm, a_mem_norm, a_w_mem_kv,
           a_w_out, a_post_norm, kv_in_norm, w_dkv, kv_norm, w_uk, w_uv, b_pre_norm, b_w_in, b_q_norm,
           b_w_uq, b_mem_norm, b_w_mem_kv, b_w_out, b_post_norm):
    assert a_w_in.shape[0] == 1 and b_w_in.shape[0] == 1, "one A layer followed by one B layer"
    batch, seq, _ = x.shape
    t = batch * seq
    x2 = x.reshape(t, D_MODEL)

    kv_w = 2 * MEM_WIDTH
    w_memkv = _repack([a_w_mem_kv[0], b_w_mem_kv[0]], [a_mem_norm[0], b_mem_norm[0]],
                      [(0, 0, 0, kv_w), (kv_w, 1, 0, kv_w)], 2 * kv_w)
    memkv = _norm_matmul(mem.reshape(batch * N_MEM, D_MODEL), w_memkv, tm=batch * N_MEM, tn=2 * MEM_WIDTH)

    w_a_t, wg, bg = _layer_a_weights(a_w_in[0], a_pre_norm[0], a_w_g2[0], a_b_g[0])
    proj_a = _norm_matmul(x2, w_a_t, tm=PROJ_TM, tn=A_TN, w_transposed=True)
    gla_out = _gla(proj_a, wg, bg, a_gla_norm, batch, seq)
    mem_out = _mem_attn(proj_a, memkv, seq, A_MQ_OFF, A_MZ_OFF, kv_blk=0)
    x1 = _out_proj(gla_out, mem_out, a_w_out[0].astype(BF16), x2, a_post_norm)

    w_b = _layer_b_weights(b_w_in[0], b_pre_norm[0], w_dkv, kv_in_norm)
    proj_b = _norm_matmul(x1, w_b, tm=PROJ_TM, tn=B_TN)
    freq, sign = _rope_tables()
    tk = min(ATT_TK, seq)
    tq = min(ATT_TQ, seq)
    q_cat, k_cat, v_t = _mla_prep(
        proj_b, positions.reshape(t, 1), freq, sign, b_q_norm, kv_norm[None, :],
        _uq_weights(b_w_uq[0]), w_uk.astype(BF16), w_uv.T.astype(BF16), batch, seq, tk)
    mla_out = _mla_attn(q_cat, k_cat, v_t, proj_b, batch, seq, tq, tk)
    mem_out_b = _mem_attn(proj_b, memkv, seq, B_MQ_OFF, B_MZ_OFF, kv_blk=2)
    out = _out_proj(mla_out, mem_out_b, b_w_out[0].astype(BF16), x1, b_post_norm)
    return out.reshape(batch, seq, D_MODEL)
```

```python
import functools

import numpy as np
import jax
import jax.numpy as jnp
from jax import lax
from jax.experimental import pallas as pl
from jax.experimental.pallas import tpu as pltpu

F32 = jnp.float32
BF16 = jnp.bfloat16

D_MODEL = 2048
N_MEM = 256
MEM_HEADS = 4
MEM_HEAD_DIM = 128
MEM_WIDTH = MEM_HEADS * MEM_HEAD_DIM
BRANCH_WIDTH = D_MODEL - MEM_WIDTH
GLA_HEADS = 4
GLA_DV = BRANCH_WIDTH // GLA_HEADS
GLA_DK = GLA_DV // 2
GLA_GATE_RANK = 16
GLA_GATE_NORM = 16.0
MLA_HEADS = 12
MLA_V_DIM = 128
MLA_NOPE_DIM = 128
MLA_ROPE_DIM = 64
MLA_QK_DIM = MLA_NOPE_DIM + MLA_ROPE_DIM
MLA_Q_RANK = 512
MLA_KV_RANK = 512
ROPE_THETA = 10000.0
EPS = 1e-6

LANES = 128
MXU_DIM = 256
GLA_DK_PAD = MXU_DIM
MLA_QK_PAD = MXU_DIM
BF16_SUBLANES = 16
MLA_VT_ROWS = MLA_V_DIM + BF16_SUBLANES
ATT_HEADS_PER_STEP = 2
ATT_PRODUCE_LEAD = 2
LOG2_E = 1.4426950408889634
LN_2 = 0.6931471805599453
MEM_Q_SCALE = MEM_HEAD_DIM ** -0.5 * LOG2_E
VMEM_LIMIT = 56 * 1024 * 1024

PROJ_TM = 1024
PROJ_NORM_ROWS = 256
REPACK_ROWS = 256
OUT_TM = 512
OUT_CHUNK = 256
GLA_SPAN = 256
GLA_SUB = 32
GLA_HEADS_PER_STEP = 4
GLA_SPANS_PER_STEP = 4
GLA_STAGE_SKEW = 0
MEM_TM = 1024
PREP_TM = 512
ATT_TQ = 512
ATT_TK = 512

A_V_OFF = 0
A_Z_OFF = BRANCH_WIDTH
A_MQ_OFF = 2 * BRANCH_WIDTH
A_MZ_OFF = A_MQ_OFF + MEM_WIDTH
A_Q_OFF = A_MZ_OFF + MEM_WIDTH
A_K_OFF = A_Q_OFF + GLA_HEADS * GLA_DK_PAD
A_WIDTH = A_K_OFF + GLA_HEADS * GLA_DK_PAD
A_TN = 2048

B_Z_OFF = 0
B_CQ_OFF = BRANCH_WIDTH
B_MQ_OFF = B_CQ_OFF + MLA_Q_RANK
B_MZ_OFF = B_MQ_OFF + MEM_WIDTH
B_C_OFF = B_MZ_OFF + MEM_WIDTH
B_KR_OFF = B_C_OFF + MLA_KV_RANK
B_KROT_OFF = B_KR_OFF + LANES
B_WIDTH = B_KROT_OFF + LANES
B_TN = 1280


def _silu(z):
    return z * (1.0 / (1.0 + jnp.exp(-z)))


def _scaled_log_sigmoid(g, scale):
    softplus2 = jnp.log2(1.0 + jnp.exp2(jnp.abs(g) * -LOG2_E))
    return jnp.minimum(g, 0.0) * scale - softplus2 * (LN_2 * scale)


def _dot(a, b):
    return jnp.dot(a, b, preferred_element_type=F32)


def _dot_nt(a, b):
    return lax.dot_general(a, b, (((1,), (1,)), ((), ())), preferred_element_type=F32)


def _dot_tn(a, b):
    return lax.dot_general(a, b, (((0,), (0,)), ((), ())), preferred_element_type=F32)


def _norm_matmul_kernel(x_ref, w_ref, o_ref, h_ref, r_ref, *, norm_rows, w_transposed):
    j = pl.program_id(1)

    def scaled_product(h, r):
        acc = _dot_nt(h, w_ref[...]) if w_transposed else _dot(h, w_ref[...])
        return (acc * r).astype(o_ref.dtype)

    @pl.when(j == 0)
    def _():
        tm = x_ref.shape[0]
        for c in range(tm // norm_rows):
            rows = pl.ds(c * norm_rows, norm_rows)
            x = x_ref[rows, :]
            r = lax.rsqrt(jnp.mean(x * x, axis=-1, keepdims=True) + EPS)
            h = x.astype(BF16)
            r_ref[rows, :] = r
            h_ref[rows, :] = h
            o_ref[rows, :] = scaled_product(h, r)

    @pl.when(j > 0)
    def _():
        o_ref[...] = scaled_product(h_ref[...], r_ref[...])


def _norm_matmul(x, w, *, tm, tn, w_transposed=False):
    m, d = x.shape
    n = w.shape[0] if w_transposed else w.shape[1]
    tm = min(tm, m)
    kern = functools.partial(_norm_matmul_kernel, norm_rows=min(PROJ_NORM_ROWS, tm),
                             w_transposed=w_transposed)
    w_spec = (pl.BlockSpec((tn, d), lambda i, j: (j, 0)) if w_transposed
              else pl.BlockSpec((d, tn), lambda i, j: (0, j)))
    return pl.pallas_call(
        kern,
        grid=(m // tm, n // tn),
        in_specs=[
            pl.BlockSpec((tm, d), lambda i, j: (i, 0)),
            w_spec,
        ],
        out_specs=pl.BlockSpec((tm, tn), lambda i, j: (i, j)),
        out_shape=jax.ShapeDtypeStruct((m, n), BF16),
        scratch_shapes=[pltpu.VMEM((tm, d), BF16), pltpu.VMEM((tm, 1), F32)],
        compiler_params=pltpu.CompilerParams(
            dimension_semantics=("parallel", "arbitrary"), vmem_limit_bytes=VMEM_LIMIT),
        name="norm_matmul",
    )(x, w)


def _out_proj_kernel(a_ref, m_ref, wa_ref, wm_ref, x_ref, g_ref, o_ref, *, chunk):
    for c in range(a_ref.shape[0] // chunk):
        rows = pl.ds(c * chunk, chunk)
        y = _dot(a_ref[rows, :], wa_ref[...]) + _dot(m_ref[rows, :], wm_ref[...])
        ms = jnp.mean(y * y, axis=-1, keepdims=True)
        o_ref[rows, :] = x_ref[rows, :] + y * lax.rsqrt(ms + EPS) * g_ref[...]


def _out_proj(a, mo, w_out, x, gain):
    t = x.shape[0]
    tm = min(OUT_TM, t)
    mem_blk = BRANCH_WIDTH // MEM_WIDTH
    return pl.pallas_call(
        functools.partial(_out_proj_kernel, chunk=min(OUT_CHUNK, tm)),
        grid=(t // tm,),
        in_specs=[
            pl.BlockSpec((tm, BRANCH_WIDTH), lambda i: (i, 0)),
            pl.BlockSpec((tm, MEM_WIDTH), lambda i: (i, 0)),
            pl.BlockSpec((BRANCH_WIDTH, D_MODEL), lambda i: (0, 0)),
            pl.BlockSpec((MEM_WIDTH, D_MODEL), lambda i: (mem_blk, 0)),
            pl.BlockSpec((tm, D_MODEL), lambda i: (i, 0)),
            pl.BlockSpec((1, D_MODEL), lambda i: (0, 0)),
        ],
        out_specs=pl.BlockSpec((tm, D_MODEL), lambda i: (i, 0)),
        out_shape=jax.ShapeDtypeStruct((t, D_MODEL), F32),
        compiler_params=pltpu.CompilerParams(
            dimension_semantics=("parallel",), vmem_limit_bytes=VMEM_LIMIT),
        name="out_proj",
    )(a, mo, w_out, w_out, x, gain)


def _gla_head_stages(h, row0, v_ref, z_ref, q_ref, k_ref, wg_ref, bg_ref, gn_ref, tril_ref, lvl_ref, o_ref,
                     s_ref, sub):
    span = tril_ref.shape[0]
    rows = slice(row0, row0 + span)
    n_sub = span // sub
    n_levels = n_sub.bit_length() - 1
    scale = GLA_DK ** -0.5
    kcols = slice(h * GLA_DK_PAD, (h + 1) * GLA_DK_PAD)
    vcols = slice(h * GLA_DV, (h + 1) * GLA_DV)

    qb = q_ref[rows, kcols]
    g = _dot(qb, wg_ref[h]) + bg_ref[h]
    yield
    log_a = _scaled_log_sigmoid(g, 1.0 / GLA_GATE_NORM)
    a_hi = log_a.astype(BF16)
    rem = log_a - a_hi.astype(F32)
    a_mid = rem.astype(BF16)
    a_lo = (rem - a_mid.astype(F32)).astype(BF16)
    yield
    tril = tril_ref[...]
    cum = _dot(tril, a_hi) + _dot(tril, a_mid) + _dot(tril, a_lo)
    yield

    refs = jnp.concatenate([cum[i * sub:i * sub + 1] for i in range(n_sub)] + [cum[span - 1:span]], axis=0)
    own = refs[:n_sub]

    def ref_rows(index_of):
        return jnp.concatenate([refs[index_of(i):index_of(i) + 1] for i in range(n_sub)], axis=0)

    def per_row(f):
        return jnp.concatenate(
            [jnp.broadcast_to(f[i:i + 1], (sub, f.shape[1])) for i in range(n_sub)], axis=0)

    own_b = per_row(own)
    qd = qb.astype(F32) * (jnp.exp(cum - own_b) * scale)
    kd = k_ref[rows, kcols].astype(F32) * jnp.exp(own_b - cum)
    yield
    q_ops, k_ops = [qd.astype(BF16)], [kd.astype(BF16)]
    for lvl in range(1, n_levels + 1):
        n = 1 << (lvl - 1)
        block_start = ref_rows(lambda i: (i // n) * n)
        next_start = ref_rows(lambda i: (i // n + 1) * n)
        q_ops.append(q_ops[0] if n == 1 else (qd * per_row(jnp.exp(own - block_start))).astype(BF16))
        k_ops.append((kd * per_row(jnp.exp(next_start - own))).astype(BF16))
    q_state = (qd * per_row(jnp.exp(own))).astype(BF16)
    k_state = (kd * per_row(jnp.exp(refs[n_sub:] - own))).astype(BF16)
    yield
    atts = [_dot_nt(qo, ko) for qo, ko in zip(q_ops, k_ops)]
    yield
    level = lvl_ref[...]
    att = jnp.where(level == n_levels, atts[n_levels], 0.0)
    for lvl in reversed(range(n_levels)):
        att = jnp.where(level == lvl, atts[lvl], att)
    yield
    state = s_ref[h]
    v = v_ref[rows, vcols]
    o = _dot(jnp.concatenate([att.astype(BF16), q_state], axis=1),
             jnp.concatenate([v, state.astype(BF16)], axis=0))
    decay_col = jnp.transpose(jnp.exp(cum[span - 8:span, :]))[:, 7:8]
    s_ref[h] = decay_col * state + _dot_tn(k_state, v)
    yield
    on = o * lax.rsqrt(jnp.mean(o * o, axis=-1, keepdims=True) + EPS) * gn_ref[...]
    o_ref[rows, vcols] = (on * _silu(z_ref[rows, vcols].astype(F32))).astype(o_ref.dtype)
    yield


def _gla_kernel(v_ref, z_ref, q_ref, k_ref, wg_ref, bg_ref, gn_ref, tril_ref, lvl_ref, o_ref, s_ref, *,
                sub, skew):
    @pl.when(pl.program_id(2) == 0)
    def _():
        s_ref[...] = jnp.zeros_like(s_ref)

    span = tril_ref.shape[0]
    heads = [_gla_head_stages(h, row0, v_ref, z_ref, q_ref, k_ref, wg_ref, bg_ref, gn_ref, tril_ref,
                              lvl_ref, o_ref, s_ref, sub)
             for row0 in range(0, q_ref.shape[0], span) for h in range(s_ref.shape[0])]
    live = list(range(len(heads)))
    step = 0
    while live:
        for i in list(live):
            if step >= i * skew:
                try:
                    next(heads[i])
                except StopIteration:
                    live.remove(i)
        step += 1


def _prefix_matrix(rows):
    r = np.arange(rows)
    return jnp.asarray(r[:, None] >= r[None, :], BF16)


def _level_matrix(rows, sub):
    n_levels = (rows // sub).bit_length() - 1
    t, s = np.arange(rows)[:, None], np.arange(rows)[None, :]
    level = np.full((rows, rows), n_levels + 1, np.int32)
    level[(t // sub == s // sub) & (s <= t)] = 0
    for lvl in range(1, n_levels + 1):
        size = sub << (lvl - 1)
        level[((t // size) % 2 == 1) & (s // size == t // size - 1)] = lvl
    return jnp.asarray(level)


def _gla(proj, wg, bg, gn, batch, seq):
    span = min(GLA_SPAN, seq)
    assert span % GLA_SUB == 0 and (span // GLA_SUB) & (span // GLA_SUB - 1) == 0
    rows = span * min(GLA_SPANS_PER_STEP, seq // span)
    ns = seq // rows
    nh = GLA_HEADS_PER_STEP
    kern = functools.partial(_gla_kernel, sub=GLA_SUB, skew=GLA_STAGE_SKEW)
    vw, kw = nh * GLA_DV, nh * GLA_DK_PAD
    v_blk, z_blk = A_V_OFF // vw, A_Z_OFF // vw
    q_blk, k_blk = A_Q_OFF // kw, A_K_OFF // kw
    group = span
    return pl.pallas_call(
        kern,
        grid=(batch, GLA_HEADS // nh, ns),
        in_specs=[
            pl.BlockSpec((rows, vw), lambda b, g, s: (b * ns + s, v_blk + g)),
            pl.BlockSpec((rows, vw), lambda b, g, s: (b * ns + s, z_blk + g)),
            pl.BlockSpec((rows, kw), lambda b, g, s: (b * ns + s, q_blk + g)),
            pl.BlockSpec((rows, kw), lambda b, g, s: (b * ns + s, k_blk + g)),
            pl.BlockSpec((nh, GLA_DK_PAD, GLA_DK_PAD), lambda b, g, s: (g, 0, 0)),
            pl.BlockSpec((nh, 1, GLA_DK_PAD), lambda b, g, s: (g, 0, 0)),
            pl.BlockSpec((1, GLA_DV), lambda b, g, s: (0, 0)),
            pl.BlockSpec((group, group), lambda b, g, s: (0, 0)),
            pl.BlockSpec((group, group), lambda b, g, s: (0, 0)),
        ],
        out_specs=pl.BlockSpec((rows, vw), lambda b, g, s: (b * ns + s, g)),
        out_shape=jax.ShapeDtypeStruct((batch * seq, BRANCH_WIDTH), BF16),
        scratch_shapes=[pltpu.VMEM((nh, GLA_DK_PAD, GLA_DV), F32)],
        compiler_params=pltpu.CompilerParams(
            dimension_semantics=("parallel", "parallel", "arbitrary"), vmem_limit_bytes=VMEM_LIMIT),
        name="gla",
    )(proj, proj, proj, proj, wg, bg, gn, _prefix_matrix(group), _level_matrix(group, GLA_SUB))


def _mem_attn_kernel(q_ref, z_ref, k_ref, v_ref, o_ref):
    for h in range(MEM_HEADS):
        cols = slice(h * MEM_HEAD_DIM, (h + 1) * MEM_HEAD_DIM)
        s = _dot_nt(q_ref[:, cols], k_ref[:, cols])
        p = jnp.exp2(s - jnp.max(s, axis=-1, keepdims=True))
        l = jnp.sum(p, axis=-1, keepdims=True)
        o = _dot(p.astype(BF16), v_ref[:, cols]) * (1.0 / l)
        o_ref[:, cols] = (o * _silu(z_ref[:, cols].astype(F32))).astype(o_ref.dtype)


def _mem_attn(proj, memkv, seq, q_off, z_off, kv_blk):
    t = proj.shape[0]
    tm = min(MEM_TM, seq)
    per_batch = seq // tm
    q_blk, z_blk = q_off // MEM_WIDTH, z_off // MEM_WIDTH
    return pl.pallas_call(
        _mem_attn_kernel,
        grid=(t // tm,),
        in_specs=[
            pl.BlockSpec((tm, MEM_WIDTH), lambda i: (i, q_blk)),
            pl.BlockSpec((tm, MEM_WIDTH), lambda i: (i, z_blk)),
            pl.BlockSpec((N_MEM, MEM_WIDTH), lambda i: (i // per_batch, kv_blk)),
            pl.BlockSpec((N_MEM, MEM_WIDTH), lambda i: (i // per_batch, kv_blk + 1)),
        ],
        out_specs=pl.BlockSpec((tm, MEM_WIDTH), lambda i: (i, 0)),
        out_shape=jax.ShapeDtypeStruct((t, MEM_WIDTH), BF16),
        compiler_params=pltpu.CompilerParams(
            dimension_semantics=("parallel",), vmem_limit_bytes=VMEM_LIMIT),
        name="mem_attn",
    )(proj, proj, memkv, memkv)


def _rotary_tables(pos, freq, sign):
    rows = pos.shape[0]
    groups = LANES // (MLA_ROPE_DIM // 2)
    if rows % (8 * groups) != 0:
        ang = pos * freq
        return jnp.cos(ang), jnp.sin(ang) * sign
    blk = rows // groups
    lane_group = lax.broadcasted_iota(jnp.int32, (1, LANES), 1) // (MLA_ROPE_DIM // 2)
    packed_pos = jnp.zeros((blk, LANES), F32)
    for g in range(groups):
        packed_pos = jnp.where(lane_group == g, pos[g * blk:(g + 1) * blk], packed_pos)
    ang = packed_pos * freq
    tables = []
    for packed in (jnp.cos(ang), jnp.sin(ang)):
        blocks = []
        for g in range(groups):
            own = jnp.where(lane_group == g, packed, 0.0)
            spread = own
            for shift in range(1, groups):
                spread = spread + pltpu.roll(own, shift * (LANES // groups), axis=1)
            blocks.append(spread)
        tables.append(jnp.concatenate(blocks, axis=0))
    return tables[0], tables[1] * sign


def _mla_prep_kernel(cq_ref, c_ref, kr_ref, krot_ref, pos_ref, freq_ref, sign_ref, qg_ref, kvg_ref,
                     wuq_ref, wuk_ref, wuvt_ref, q_ref, k_ref, vt_ref, *, tk):
    tm = cq_ref.shape[0]
    n_nope = MLA_HEADS * MLA_NOPE_DIM
    n_rope = MLA_HEADS * MLA_ROPE_DIM

    cq = cq_ref[...].astype(F32)
    cqn = cq * lax.rsqrt(jnp.mean(cq * cq, axis=-1, keepdims=True) + EPS)
    cqn = (cqn * (qg_ref[...] * (MLA_QK_DIM ** -0.5 * LOG2_E))).astype(BF16)
    c = c_ref[...].astype(F32)
    cn = c * lax.rsqrt(jnp.mean(c * c, axis=-1, keepdims=True) + EPS)
    cn = (cn * kvg_ref[...]).astype(BF16)

    q_nope = _dot(cqn, wuq_ref[:, :n_nope])
    for h in range(MLA_HEADS):
        q_ref[h, :, :MLA_NOPE_DIM] = q_nope[:, h * MLA_NOPE_DIM:(h + 1) * MLA_NOPE_DIM].astype(BF16)

    cos, sin = _rotary_tables(pos_ref[...].astype(F32), freq_ref[...], sign_ref[...])

    k_nope = _dot(cn, wuk_ref[...])
    for h in range(MLA_HEADS):
        k_ref[h, :, :MLA_NOPE_DIM] = k_nope[:, h * MLA_NOPE_DIM:(h + 1) * MLA_NOPE_DIM].astype(BF16)
    v_t = _dot_nt(wuvt_ref[...], cn).astype(BF16)
    ones_rows = (lax.broadcasted_iota(jnp.int32, (MLA_VT_ROWS - MLA_V_DIM, tk), 0) == 0).astype(BF16)
    for h in range(MLA_HEADS):
        for u in range(tm // tk):
            vt_ref[h, u, :MLA_V_DIM, :] = v_t[h * MLA_V_DIM:(h + 1) * MLA_V_DIM, u * tk:(u + 1) * tk]
            vt_ref[h, u, MLA_V_DIM:, :] = ones_rows

    lane = lax.broadcasted_iota(jnp.int32, (1, LANES), 1)
    slot_mask = [(lane < MLA_ROPE_DIM).astype(F32), (lane >= MLA_ROPE_DIM).astype(F32)]
    q_r = _dot(cqn, wuq_ref[:, n_nope:n_nope + n_rope])
    q_rot = _dot(cqn, wuq_ref[:, n_nope + n_rope:])
    for u in range(MLA_HEADS // 2):
        cols = slice(u * LANES, (u + 1) * LANES)
        roped = q_r[:, cols] * cos + q_rot[:, cols] * sin
        for half in range(2):
            q_ref[2 * u + half, :, MLA_NOPE_DIM:] = (roped * slot_mask[half]).astype(BF16)
    k_rope = (kr_ref[...].astype(F32) * cos + krot_ref[...].astype(F32) * sin).astype(BF16)
    for h in range(MLA_HEADS):
        k_ref[h, :, MLA_NOPE_DIM:] = k_rope


def _mla_prep(proj, pos_col, freq, sign, q_gain, kv_gain, w_uq, w_uk, w_uv_t, batch, seq, tk):
    tm = min(PREP_TM, seq)
    per_batch = seq // tm
    kern = functools.partial(_mla_prep_kernel, tk=tk)
    const = lambda i: (0, 0)
    head_map = lambda i: (i // per_batch, 0, i % per_batch, 0)
    return pl.pallas_call(
        kern,
        grid=(batch * per_batch,),
        in_specs=[
            pl.BlockSpec((tm, MLA_Q_RANK), lambda i: (i, B_CQ_OFF // MLA_Q_RANK)),
            pl.BlockSpec((tm, MLA_KV_RANK), lambda i: (i, B_C_OFF // MLA_KV_RANK)),
            pl.BlockSpec((tm, LANES), lambda i: (i, B_KR_OFF // LANES)),
            pl.BlockSpec((tm, LANES), lambda i: (i, B_KROT_OFF // LANES)),
            pl.BlockSpec((tm, 1), lambda i: (i, 0)),
            pl.BlockSpec((1, LANES), const),
            pl.BlockSpec((1, LANES), const),
            pl.BlockSpec((1, MLA_Q_RANK), const),
            pl.BlockSpec((1, MLA_KV_RANK), const),
            pl.BlockSpec(w_uq.shape, const),
            pl.BlockSpec(w_uk.shape, const),
            pl.BlockSpec(w_uv_t.shape, const),
        ],
        out_specs=[
            pl.BlockSpec((None, MLA_HEADS, tm, MLA_QK_PAD), head_map),
            pl.BlockSpec((None, MLA_HEADS, tm, MLA_QK_PAD), head_map),
            pl.BlockSpec((None, MLA_HEADS, tm // tk, MLA_VT_ROWS, tk),
                         lambda i: (i // per_batch, 0, i % per_batch, 0, 0)),
        ],
        out_shape=[
            jax.ShapeDtypeStruct((batch, MLA_HEADS, seq, MLA_QK_PAD), BF16),
            jax.ShapeDtypeStruct((batch, MLA_HEADS, seq, MLA_QK_PAD), BF16),
            jax.ShapeDtypeStruct((batch, MLA_HEADS, seq // tk, MLA_VT_ROWS, tk), BF16),
        ],
        compiler_params=pltpu.CompilerParams(
            dimension_semantics=("parallel",), vmem_limit_bytes=VMEM_LIMIT),
        name="mla_prep",
    )(proj, proj, proj, proj, pos_col, freq, sign, q_gain, kv_gain, w_uq, w_uk, w_uv_t)


def _max_over_rows(x, ways=8):
    rows = x.shape[0]
    slab = rows // ways
    parts = [x[i * slab:(i + 1) * slab] for i in range(ways)] if slab >= 8 and rows % ways == 0 else [x]
    while len(parts) > 1:
        parts = [jnp.maximum(parts[i], parts[i + 1]) for i in range(0, len(parts), 2)]
    return jnp.max(parts[0], axis=0, keepdims=True)


def _mla_attn_kernel(q_ref, k_ref, vt_ref, z_ref, o_ref, acc_ref, m_ref, done_ref, s0_ref, s1_ref, s2_ref,
                     smax0_ref, smax1_ref, smax2_ref, *, tq, tk):
    n_heads, seq = q_ref.shape[0], q_ref.shape[1]
    n_q = seq // tq
    tn = MXU_DIM
    units = [(h, n) for h in range(n_heads) for n in range(tq // tn)]
    s_slots = (s0_ref, s1_ref, s2_ref)
    smax_slots = (smax0_ref, smax1_ref, smax2_ref)
    FIRST = 2

    def n_keys(n, diagonal):
        return min(tk, (n + 1) * tn) if diagonal else tk

    def produce(qi, kj, slot, u, diagonal):
        h, n = units[u]
        nk = n_keys(n, diagonal)
        krows = pl.ds(pl.multiple_of(kj * tk, tk), nk)
        qrows = pl.ds(pl.multiple_of(qi * tq + n * tn, tn), tn)
        s_t = _dot_nt(k_ref[h, krows, :], q_ref[h, qrows, :])
        if diagonal:
            kpos = lax.broadcasted_iota(jnp.int32, (nk, tn), 0)
            qpos = lax.broadcasted_iota(jnp.int32, (nk, tn), 1) + n * tn
            s_t = jnp.where(kpos <= qpos, s_t, -jnp.inf)
        s_slots[slot][u, :nk, :] = s_t
        smax_slots[slot][u] = _max_over_rows(s_t)

    def consume(kj, slot, u, diagonal):
        h, n = units[u]
        nk = n_keys(n, diagonal)
        cols = slice(n * tn, (n + 1) * tn)
        m_prev = m_ref[h, :, cols]
        m_new = jnp.maximum(m_prev, smax_slots[slot][u])
        alpha = jnp.exp2(m_prev - m_new)
        p = jnp.exp2(s_slots[slot][u, :nk, :] - m_new).astype(BF16)
        m_ref[h, :, cols] = m_new
        acc_ref[h, :, cols] = alpha * acc_ref[h, :, cols] + _dot(vt_ref[h, kj, :, :nk], p)

    def produce_all(qi, kj, slot, diagonal):
        for u in range(len(units)):
            produce(qi, kj, slot, u, diagonal)

    def write_out(h, qp):
        rows = pl.ds(qp * tq if isinstance(qp, int) else pl.multiple_of(qp * tq, tq), tq)
        cols = slice(h * MLA_V_DIM, (h + 1) * MLA_V_DIM)
        o = jnp.transpose(done_ref[h])
        o_ref[rows, cols] = (o * _silu(z_ref[rows, cols].astype(F32))).astype(o_ref.dtype)

    def step(kj, slot, diagonal, nxt=None, write_out_of=None):
        lead = ATT_PRODUCE_LEAD
        if nxt is not None:
            for u in range(min(lead, len(units))):
                produce(nxt[0], nxt[1], nxt[2], u, nxt[3])
        per_head = len(units) // n_heads
        for u in range(len(units)):
            consume(kj, slot, u, diagonal)
            if write_out_of is not None and u % per_head == per_head - 1:
                write_out(u // per_head, write_out_of)
            if nxt is not None and u + lead < len(units):
                produce(nxt[0], nxt[1], nxt[2], u + lead, nxt[3])

    def last_step(qi, slot):
        if n_q == 1:
            step(qi, slot, True)
            return

        @pl.when(qi < n_q - 1)
        def _():
            step(qi, slot, True, (qi + 1, 0, FIRST, False))

        last_q = n_q - 1
        if slot == {1: 0, 2: 1, 3: 0}[last_q - 2 * max(0, (last_q - 2) // 2)]:
            @pl.when(qi == last_q)
            def _():
                step(qi, slot, True)

    def q_block(qi, carry):
        m_ref[...] = jnp.full_like(m_ref, -jnp.inf)
        acc_ref[...] = jnp.zeros_like(acc_ref)

        @pl.when(qi == 0)
        def _():
            step(0, FIRST, True)
            if n_q > 1:
                produce_all(1, 0, FIRST, False)

        @pl.when(qi == 1)
        def _():
            step(0, FIRST, False, (qi, 1, 0, True), write_out_of=qi - 1)

        @pl.when(qi > 1)
        def _():
            step(0, FIRST, False, (qi, 1, 0, False), write_out_of=qi - 1)

        def steady_pair(i, c):
            kj = 2 * i + 1
            step(kj, 0, False, (qi, kj + 1, 1, False))
            step(kj + 1, 1, False, (qi, kj + 2, 0, False))
            return c

        n_pairs = jnp.maximum(qi - 2, 0) // 2
        lax.fori_loop(0, n_pairs, steady_pair, 0)
        kj = 2 * n_pairs + 1
        left = qi - kj + 1

        @pl.when((qi > 0) & (left == 1))
        def _():
            last_step(qi, 0)

        @pl.when((qi > 0) & (left == 2))
        def _():
            step(kj, 0, False, (qi, kj + 1, 1, True))
            last_step(qi, 1)

        @pl.when((qi > 0) & (left == 3))
        def _():
            step(kj, 0, False, (qi, kj + 1, 1, False))
            step(kj + 1, 1, False, (qi, kj + 2, 0, True))
            last_step(qi, 0)

        for h in range(n_heads):
            acc = acc_ref[h]
            done_ref[h] = acc[:MLA_V_DIM] * (1.0 / acc[MLA_V_DIM:MLA_V_DIM + 1])
        return carry

    produce_all(0, 0, FIRST, True)
    lax.fori_loop(0, n_q, q_block, 0)
    for h in range(n_heads):
        write_out(h, n_q - 1)


def _mla_attn(q_cat, k_cat, v_t, proj, batch, seq, tq, tk):
    kern = functools.partial(_mla_attn_kernel, tq=tq, tk=tk)
    nb = seq // tk
    nh = ATT_HEADS_PER_STEP
    n_units = nh * (tq // MXU_DIM)
    width = nh * MLA_V_DIM
    z_blk = B_Z_OFF // width
    return pl.pallas_call(
        kern,
        grid=(batch, MLA_HEADS // nh),
        in_specs=[
            pl.BlockSpec((None, nh, seq, MLA_QK_PAD), lambda b, g: (b, g, 0, 0)),
            pl.BlockSpec((None, nh, seq, MLA_QK_PAD), lambda b, g: (b, g, 0, 0)),
            pl.BlockSpec((None, nh, nb, MLA_VT_ROWS, tk), lambda b, g: (b, g, 0, 0, 0)),
            pl.BlockSpec((seq, width), lambda b, g: (b, z_blk + g)),
        ],
        out_specs=pl.BlockSpec((seq, width), lambda b, g: (b, g)),
        out_shape=jax.ShapeDtypeStruct((batch * seq, BRANCH_WIDTH), BF16),
        scratch_shapes=[
            pltpu.VMEM((nh, MLA_VT_ROWS, tq), F32),
            pltpu.VMEM((nh, 1, tq), F32),
            pltpu.VMEM((nh, MLA_V_DIM, tq), F32),
            pltpu.VMEM((n_units, tk, MXU_DIM), F32),
            pltpu.VMEM((n_units, tk, MXU_DIM), F32),
            pltpu.VMEM((n_units, tk, MXU_DIM), F32),
            pltpu.VMEM((n_units, 1, MXU_DIM), F32),
            pltpu.VMEM((n_units, 1, MXU_DIM), F32),
            pltpu.VMEM((n_units, 1, MXU_DIM), F32),
        ],
        compiler_params=pltpu.CompilerParams(
            dimension_semantics=("parallel", "parallel"), vmem_limit_bytes=VMEM_LIMIT),
        name="mla_attn",
    )(q_cat, k_cat, v_t, proj)


def _pad_heads(w, heads, width, padded):
    lead = w.shape[:-1]
    w = w.reshape(lead + (heads, width))
    w = jnp.pad(w, [(0, 0)] * len(lead) + [(0, 0), (0, padded - width)])
    return w.reshape(lead + (heads * padded,))


def _repack_kernel(*refs, n_in, with_gain, pieces, zero_ranges, transposed):
    ins = refs[:n_in]
    gains = refs[n_in:2 * n_in] if with_gain else None
    o_ref = refs[-1]

    def window(lo, width):
        return (slice(lo, lo + width), slice(None)) if transposed else (slice(None), slice(lo, lo + width))

    for dst, idx, src, width, scale in pieces:
        val = ins[idx][window(src, width)]
        if with_gain:
            val = val * gains[idx][...]
        if scale != 1.0:
            val = val * scale
        o_ref[window(dst, width)] = val.astype(o_ref.dtype)
    for lo, hi in zero_ranges:
        shape = (hi - lo, o_ref.shape[1]) if transposed else (o_ref.shape[0], hi - lo)
        o_ref[window(lo, hi - lo)] = jnp.zeros(shape, o_ref.dtype)


def _repack(ws, gains, pieces, out_width, zero_ranges=(), transposed=False):
    d = ws[0].shape[1] if transposed else ws[0].shape[0]
    td = min(REPACK_ROWS, d)
    with_gain = gains is not None
    pieces = [tuple(p) if len(p) == 5 else tuple(p) + (1.0,) for p in pieces]
    covered = sorted([(dst, dst + w) for dst, _, _, w, _ in pieces] + list(zero_ranges))
    assert covered[0][0] == 0 and covered[-1][1] == out_width
    assert all(a[1] == b[0] for a, b in zip(covered, covered[1:])), "output features written exactly once"
    kern = functools.partial(_repack_kernel, n_in=len(ws), with_gain=with_gain, pieces=tuple(pieces),
                             zero_ranges=tuple(zero_ranges), transposed=transposed)

    def spec(n):
        return (pl.BlockSpec((n, td), lambda i: (0, i)) if transposed
                else pl.BlockSpec((td, n), lambda i: (i, 0)))

    in_specs = [spec(w.shape[0] if transposed else w.shape[1]) for w in ws]
    args = list(ws)
    if with_gain:
        in_specs += [spec(1) for _ in gains]
        args += [g.reshape((1, d) if transposed else (d, 1)) for g in gains]
    return pl.pallas_call(
        kern,
        grid=(d // td,),
        in_specs=in_specs,
        out_specs=spec(out_width),
        out_shape=jax.ShapeDtypeStruct((out_width, d) if transposed else (d, out_width), BF16),
        compiler_params=pltpu.CompilerParams(
            dimension_semantics=("parallel",), vmem_limit_bytes=VMEM_LIMIT),
        name="repack",
    )(*args)


def _layer_a_weights(w_in, gain, w_g2, b_g):
    hk = GLA_HEADS * GLA_DK
    o = [int(v) for v in np.cumsum([0, hk, hk, BRANCH_WIDTH, GLA_GATE_RANK, BRANCH_WIDTH, MEM_WIDTH,
                                    MEM_WIDTH])]
    q_src, k_src, v_src, lr_src, z_src = o[0], o[1], o[2], o[3], o[4]
    pieces = [(A_V_OFF, 0, v_src, BRANCH_WIDTH), (A_Z_OFF, 0, z_src, BRANCH_WIDTH),
              (A_MQ_OFF, 0, z_src + BRANCH_WIDTH, MEM_WIDTH, MEM_Q_SCALE),
              (A_MZ_OFF, 0, z_src + BRANCH_WIDTH + MEM_WIDTH, MEM_WIDTH)]
    zeros = []
    for h in range(GLA_HEADS):
        q_dst, k_dst = A_Q_OFF + h * GLA_DK_PAD, A_K_OFF + h * GLA_DK_PAD
        pieces += [(q_dst, 0, q_src + h * GLA_DK, GLA_DK), (q_dst + GLA_DK, 0, lr_src, GLA_GATE_RANK),
                   (k_dst, 0, k_src + h * GLA_DK, GLA_DK)]
        zeros += [(q_dst + GLA_DK + GLA_GATE_RANK, q_dst + GLA_DK_PAD), (k_dst + GLA_DK, k_dst + GLA_DK_PAD)]
    w_all_t = _repack([w_in.T], [gain], pieces, A_WIDTH, zeros, transposed=True)
    wg = _pad_heads(w_g2, GLA_HEADS, GLA_DK, GLA_DK_PAD)
    wg = wg.reshape(GLA_GATE_RANK, GLA_HEADS, GLA_DK_PAD).transpose(1, 0, 2)
    wg = jnp.pad(wg, [(0, 0), (GLA_DK, GLA_DK_PAD - GLA_DK - GLA_GATE_RANK), (0, 0)]).astype(BF16)
    bg = _pad_heads(b_g, GLA_HEADS, GLA_DK, GLA_DK_PAD).reshape(GLA_HEADS, 1, GLA_DK_PAD)
    return w_all_t, wg, bg


def _layer_b_weights(w_in, gain_in, w_dkv, gain_dkv):
    half = MLA_ROPE_DIM // 2
    pieces = [(B_Z_OFF, 0, MLA_Q_RANK, BRANCH_WIDTH), (B_CQ_OFF, 0, 0, MLA_Q_RANK),
              (B_MQ_OFF, 0, MLA_Q_RANK + BRANCH_WIDTH, MEM_WIDTH, MEM_Q_SCALE),
              (B_MZ_OFF, 0, MLA_Q_RANK + BRANCH_WIDTH + MEM_WIDTH, MEM_WIDTH), (B_C_OFF, 1, 0, MLA_KV_RANK)]
    for rep in range(LANES // MLA_ROPE_DIM):
        kr, krot = B_KR_OFF + rep * MLA_ROPE_DIM, B_KROT_OFF + rep * MLA_ROPE_DIM
        pieces += [(kr, 1, MLA_KV_RANK, MLA_ROPE_DIM),
                   (krot, 1, MLA_KV_RANK + half, half), (krot + half, 1, MLA_KV_RANK, half)]
    return _repack([w_in, w_dkv], [gain_in, gain_dkv], pieces, B_WIDTH)


def _uq_weights(w_uq):
    half = MLA_ROPE_DIM // 2
    n_nope, n_rope = MLA_HEADS * MLA_NOPE_DIM, MLA_HEADS * MLA_ROPE_DIM
    pieces = []
    for h in range(MLA_HEADS):
        src = h * MLA_QK_DIM
        rope, rot = n_nope + h * MLA_ROPE_DIM, n_nope + n_rope + h * MLA_ROPE_DIM
        pieces += [(h * MLA_NOPE_DIM, 0, src, MLA_NOPE_DIM), (rope, 0, src + MLA_NOPE_DIM, MLA_ROPE_DIM),
                   (rot, 0, src + MLA_NOPE_DIM + half, half), (rot + half, 0, src + MLA_NOPE_DIM, half)]
    return _repack([w_uq], None, pieces, n_nope + 2 * n_rope)


def _rope_tables():
    r = MLA_ROPE_DIM
    freqs = ROPE_THETA ** (-jnp.arange(0, r, 2, dtype=F32) / r)
    freq = jnp.tile(freqs, LANES // (r // 2)).reshape(1, LANES)
    sign = jnp.tile(jnp.concatenate([-jnp.ones(r // 2, F32), jnp.ones(r // 2, F32)]),
                    LANES // r).reshape(1, LANES)
    return freq, sign


def kernel(x, mem, positions, a_pre_norm, a_w_in, a_w_g2, a_b_g, a_gla_norm, a_mem_norm, a_w_mem_kv,
           a_w_out, a_post_norm, kv_in_norm, w_dkv, kv_norm, w_uk, w_uv, b_pre_norm, b_w_in, b_q_norm,
           b_w_uq, b_mem_norm, b_w_mem_kv, b_w_out, b_post_norm):
    assert a_w_in.shape[0] == 1 and b_w_in.shape[0] == 1, "one A layer followed by one B layer"
    batch, seq, _ = x.shape
    t = batch * seq
    x2 = x.reshape(t, D_MODEL)

    kv_w = 2 * MEM_WIDTH
    w_memkv = _repack([a_w_mem_kv[0], b_w_mem_kv[0]], [a_mem_norm[0], b_mem_norm[0]],
                      [(0, 0, 0, kv_w), (kv_w, 1, 0, kv_w)], 2 * kv_w)
    memkv = _norm_matmul(mem.reshape(batch * N_MEM, D_MODEL), w_memkv, tm=batch * N_MEM, tn=2 * MEM_WIDTH)

    w_a_t, wg, bg = _layer_a_weights(a_w_in[0], a_pre_norm[0], a_w_g2[0], a_b_g[0])
    proj_a = _norm_matmul(x2, w_a_t, tm=PROJ_TM, tn=A_TN, w_transposed=True)
    gla_out = _gla(proj_a, wg, bg, a_gla_norm, batch, seq)
    mem_out = _mem_attn(proj_a, memkv, seq, A_MQ_OFF, A_MZ_OFF, kv_blk=0)
    x1 = _out_proj(gla_out, mem_out, a_w_out[0].astype(BF16), x2, a_post_norm)

    w_b = _layer_b_weights(b_w_in[0], b_pre_norm[0], w_dkv, kv_in_norm)
    proj_b = _norm_matmul(x1, w_b, tm=PROJ_TM, tn=B_TN)
    freq, sign = _rope_tables()
    tk = min(ATT_TK, seq)
    tq = min(ATT_TQ, seq)
    q_cat, k_cat, v_t = _mla_prep(
        proj_b, positions.reshape(t, 1), freq, sign, b_q_norm, kv_norm[None, :],
        _uq_weights(b_w_uq[0]), w_uk.astype(BF16), w_uv.T.astype(BF16), batch, seq, tk)
    mla_out = _mla_attn(q_cat, k_cat, v_t, proj_b, batch, seq, tq, tk)
    mem_out_b = _mem_attn(proj_b, memkv, seq, B_MQ_OFF, B_MZ_OFF, kv_blk=2)
    out = _out_proj(mla_out, mem_out_b, b_w_out[0].astype(BF16), x1, b_post_norm)
    return out.reshape(batch, seq, D_MODEL)
```

```python
import functools

import numpy as np
import jax
import jax.numpy as jnp
from jax import lax
from jax.experimental import pallas as pl
from jax.experimental.pallas import tpu as pltpu

F32 = jnp.float32
BF16 = jnp.bfloat16

D_MODEL = 2048
N_MEM = 256
MEM_HEADS = 4
MEM_HEAD_DIM = 128
MEM_WIDTH = MEM_HEADS * MEM_HEAD_DIM
BRANCH_WIDTH = D_MODEL - MEM_WIDTH
GLA_HEADS = 4
GLA_DV = BRANCH_WIDTH // GLA_HEADS
GLA_DK = GLA_DV // 2
GLA_GATE_RANK = 16
GLA_GATE_NORM = 16.0
MLA_HEADS = 12
MLA_V_DIM = 128
MLA_NOPE_DIM = 128
MLA_ROPE_DIM = 64
MLA_QK_DIM = MLA_NOPE_DIM + MLA_ROPE_DIM
MLA_Q_RANK = 512
MLA_KV_RANK = 512
ROPE_THETA = 10000.0
EPS = 1e-6

LANES = 128
MXU_DIM = 256
GLA_DK_PAD = MXU_DIM
MLA_QK_PAD = MXU_DIM
BF16_SUBLANES = 16
MLA_VT_ROWS = MLA_V_DIM + BF16_SUBLANES
ATT_HEADS_PER_STEP = 3
ATT_PRODUCE_LEAD = 2
LOG2_E = 1.4426950408889634
MEM_Q_SCALE = MEM_HEAD_DIM ** -0.5 * LOG2_E
VMEM_LIMIT = 56 * 1024 * 1024

PROJ_TM = 1024
PROJ_NORM_ROWS = 256
REPACK_ROWS = 256
OUT_TM = 512
OUT_CHUNK = 256
GLA_SPAN = 256
GLA_SUB = 32
GLA_HEADS_PER_STEP = 4
GLA_SPANS_PER_STEP = 4
GLA_STAGE_SKEW = 0
MEM_TM = 1024
PREP_TM = 512
ATT_TQ = 512
ATT_TK = 512

A_V_OFF = 0
A_Z_OFF = BRANCH_WIDTH
A_MQ_OFF = 2 * BRANCH_WIDTH
A_MZ_OFF = A_MQ_OFF + MEM_WIDTH
A_Q_OFF = A_MZ_OFF + MEM_WIDTH
A_K_OFF = A_Q_OFF + GLA_HEADS * GLA_DK_PAD
A_WIDTH = A_K_OFF + GLA_HEADS * GLA_DK_PAD
A_TN = 2048

B_Z_OFF = 0
B_CQ_OFF = BRANCH_WIDTH
B_MQ_OFF = B_CQ_OFF + MLA_Q_RANK
B_MZ_OFF = B_MQ_OFF + MEM_WIDTH
B_C_OFF = B_MZ_OFF + MEM_WIDTH
B_KR_OFF = B_C_OFF + MLA_KV_RANK
B_KROT_OFF = B_KR_OFF + LANES
B_WIDTH = B_KROT_OFF + LANES
B_TN = 1280


def _silu(z):
    return z * (1.0 / (1.0 + jnp.exp2(z * -LOG2_E)))


def _scaled_log_sigmoid(g, scale):
    return (jnp.minimum(g, 0.0) - jnp.log(1.0 + jnp.exp2(jnp.abs(g) * -LOG2_E))) * scale


def _dot(a, b):
    return jnp.dot(a, b, preferred_element_type=F32)


def _dot_nt(a, b):
    return lax.dot_general(a, b, (((1,), (1,)), ((), ())), preferred_element_type=F32)


def _dot_tn(a, b):
    return lax.dot_general(a, b, (((0,), (0,)), ((), ())), preferred_element_type=F32)


def _norm_matmul_kernel(x_ref, w_ref, o_ref, h_ref, r_ref, *, norm_rows, w_transposed):
    j = pl.program_id(1)

    def scaled_product(h, r):
        acc = _dot_nt(h, w_ref[...]) if w_transposed else _dot(h, w_ref[...])
        return (acc * r).astype(o_ref.dtype)

    @pl.when(j == 0)
    def _():
        tm = x_ref.shape[0]
        for c in range(tm // norm_rows):
            rows = pl.ds(c * norm_rows, norm_rows)
            x = x_ref[rows, :]
            r = lax.rsqrt(jnp.mean(x * x, axis=-1, keepdims=True) + EPS)
            h = x.astype(BF16)
            r_ref[rows, :] = r
            h_ref[rows, :] = h
            o_ref[rows, :] = scaled_product(h, r)

    @pl.when(j > 0)
    def _():
        o_ref[...] = scaled_product(h_ref[...], r_ref[...])


def _norm_matmul(x, w, *, tm, tn, w_transposed=False):
    m, d = x.shape
    n = w.shape[0] if w_transposed else w.shape[1]
    tm = min(tm, m)
    kern = functools.partial(_norm_matmul_kernel, norm_rows=min(PROJ_NORM_ROWS, tm),
                             w_transposed=w_transposed)
    w_spec = (pl.BlockSpec((tn, d), lambda i, j: (j, 0)) if w_transposed
              else pl.BlockSpec((d, tn), lambda i, j: (0, j)))
    return pl.pallas_call(
        kern,
        grid=(m // tm, n // tn),
        in_specs=[
            pl.BlockSpec((tm, d), lambda i, j: (i, 0)),
            w_spec,
        ],
        out_specs=pl.BlockSpec((tm, tn), lambda i, j: (i, j)),
        out_shape=jax.ShapeDtypeStruct((m, n), BF16),
        scratch_shapes=[pltpu.VMEM((tm, d), BF16), pltpu.VMEM((tm, 1), F32)],
        compiler_params=pltpu.CompilerParams(
            dimension_semantics=("parallel", "arbitrary"), vmem_limit_bytes=VMEM_LIMIT),
        name="norm_matmul",
    )(x, w)


def _out_proj_kernel(a_ref, m_ref, wa_ref, wm_ref, x_ref, g_ref, o_ref, *, chunk):
    for c in range(a_ref.shape[0] // chunk):
        rows = pl.ds(c * chunk, chunk)
        y = _dot(a_ref[rows, :], wa_ref[...]) + _dot(m_ref[rows, :], wm_ref[...])
        ms = jnp.mean(y * y, axis=-1, keepdims=True)
        o_ref[rows, :] = x_ref[rows, :] + y * lax.rsqrt(ms + EPS) * g_ref[...]


def _out_proj(a, mo, w_out, x, gain):
    t = x.shape[0]
    tm = min(OUT_TM, t)
    mem_blk = BRANCH_WIDTH // MEM_WIDTH
    return pl.pallas_call(
        functools.partial(_out_proj_kernel, chunk=min(OUT_CHUNK, tm)),
        grid=(t // tm,),
        in_specs=[
            pl.BlockSpec((tm, BRANCH_WIDTH), lambda i: (i, 0)),
            pl.BlockSpec((tm, MEM_WIDTH), lambda i: (i, 0)),
            pl.BlockSpec((BRANCH_WIDTH, D_MODEL), lambda i: (0, 0)),
            pl.BlockSpec((MEM_WIDTH, D_MODEL), lambda i: (mem_blk, 0)),
            pl.BlockSpec((tm, D_MODEL), lambda i: (i, 0)),
            pl.BlockSpec((1, D_MODEL), lambda i: (0, 0)),
        ],
        out_specs=pl.BlockSpec((tm, D_MODEL), lambda i: (i, 0)),
        out_shape=jax.ShapeDtypeStruct((t, D_MODEL), F32),
        compiler_params=pltpu.CompilerParams(
            dimension_semantics=("parallel",), vmem_limit_bytes=VMEM_LIMIT),
        name="out_proj",
    )(a, mo, w_out, w_out, x, gain)


def _gla_head_stages(h, row0, v_ref, z_ref, q_ref, k_ref, wg_ref, bg_ref, gn_ref, tril_ref, lvl_ref, o_ref,
                     s_ref, sub):
    span = tril_ref.shape[0]
    rows = slice(row0, row0 + span)
    n_sub = span // sub
    n_levels = n_sub.bit_length() - 1
    scale = GLA_DK ** -0.5
    kcols = slice(h * GLA_DK_PAD, (h + 1) * GLA_DK_PAD)
    vcols = slice(h * GLA_DV, (h + 1) * GLA_DV)

    qb = q_ref[rows, kcols]
    g = _dot(qb, wg_ref[h]) + bg_ref[h]
    yield
    log_a = _scaled_log_sigmoid(g, LOG2_E / GLA_GATE_NORM)
    a_hi = log_a.astype(BF16)
    rem = log_a - a_hi.astype(F32)
    a_mid = rem.astype(BF16)
    a_lo = (rem - a_mid.astype(F32)).astype(BF16)
    yield
    tril = tril_ref[...]
    cum = _dot(tril, a_hi) + _dot(tril, a_mid) + _dot(tril, a_lo)
    yield

    refs = jnp.concatenate([cum[i * sub:i * sub + 1] for i in range(n_sub)] + [cum[span - 1:span]], axis=0)
    own = refs[:n_sub]

    def ref_rows(index_of):
        return jnp.concatenate([refs[index_of(i):index_of(i) + 1] for i in range(n_sub)], axis=0)

    def per_row(f):
        return jnp.concatenate(
            [jnp.broadcast_to(f[i:i + 1], (sub, f.shape[1])) for i in range(n_sub)], axis=0)

    own_b = per_row(own)
    qd = qb.astype(F32) * jnp.exp2(cum - per_row(own - float(np.log2(scale))))
    kd = k_ref[rows, kcols].astype(F32) * jnp.exp2(own_b - cum)
    yield
    q_ops, k_ops = [qd.astype(BF16)], [kd.astype(BF16)]
    for lvl in range(1, n_levels + 1):
        n = 1 << (lvl - 1)
        block_start = ref_rows(lambda i: (i // n) * n)
        next_start = ref_rows(lambda i: (i // n + 1) * n)
        q_ops.append(q_ops[0] if n == 1 else (qd * per_row(jnp.exp2(own - block_start))).astype(BF16))
        k_ops.append((kd * per_row(jnp.exp2(next_start - own))).astype(BF16))
    q_state = (qd * per_row(jnp.exp2(own))).astype(BF16)
    k_state = (kd * per_row(jnp.exp2(refs[n_sub:] - own))).astype(BF16)
    yield
    atts = [_dot_nt(qo, ko) for qo, ko in zip(q_ops, k_ops)]
    yield
    level = lvl_ref[...]
    att = jnp.where(level == n_levels, atts[n_levels], 0.0)
    for lvl in reversed(range(n_levels)):
        att = jnp.where(level == lvl, atts[lvl], att)
    yield
    state = s_ref[h]
    v = v_ref[rows, vcols]
    o = _dot(jnp.concatenate([att.astype(BF16), q_state], axis=1),
             jnp.concatenate([v, state.astype(BF16)], axis=0))
    decay_col = jnp.transpose(jnp.exp2(cum[span - 8:span, :]))[:, 7:8]
    s_ref[h] = decay_col * state + _dot_tn(k_state, v)
    yield
    on = o * lax.rsqrt(jnp.mean(o * o, axis=-1, keepdims=True) + EPS) * gn_ref[...]
    o_ref[rows, vcols] = (on * _silu(z_ref[rows, vcols].astype(F32))).astype(o_ref.dtype)
    yield


def _gla_kernel(v_ref, z_ref, q_ref, k_ref, wg_ref, bg_ref, gn_ref, tril_ref, lvl_ref, o_ref, s_ref, *,
                sub, skew):
    @pl.when(pl.program_id(2) == 0)
    def _():
        s_ref[...] = jnp.zeros_like(s_ref)

    span = tril_ref.shape[0]
    heads = [_gla_head_stages(h, row0, v_ref, z_ref, q_ref, k_ref, wg_ref, bg_ref, gn_ref, tril_ref,
                              lvl_ref, o_ref, s_ref, sub)
             for row0 in range(0, q_ref.shape[0], span) for h in range(s_ref.shape[0])]
    live = list(range(len(heads)))
    step = 0
    while live:
        for i in list(live):
            if step >= i * skew:
                try:
                    next(heads[i])
                except StopIteration:
                    live.remove(i)
        step += 1


def _prefix_matrix(rows):
    r = np.arange(rows)
    return jnp.asarray(r[:, None] >= r[None, :], BF16)


def _level_matrix(rows, sub):
    n_levels = (rows // sub).bit_length() - 1
    t, s = np.arange(rows)[:, None], np.arange(rows)[None, :]
    level = np.full((rows, rows), n_levels + 1, np.int32)
    level[(t // sub == s // sub) & (s <= t)] = 0
    for lvl in range(1, n_levels + 1):
        size = sub << (lvl - 1)
        level[((t // size) % 2 == 1) & (s // size == t // size - 1)] = lvl
    return jnp.asarray(level)


def _gla(proj, wg, bg, gn, batch, seq):
    span = min(GLA_SPAN, seq)
    assert span % GLA_SUB == 0 and (span // GLA_SUB) & (span // GLA_SUB - 1) == 0
    rows = span * min(GLA_SPANS_PER_STEP, seq // span)
    ns = seq // rows
    nh = GLA_HEADS_PER_STEP
    kern = functools.partial(_gla_kernel, sub=GLA_SUB, skew=GLA_STAGE_SKEW)
    vw, kw = nh * GLA_DV, nh * GLA_DK_PAD
    v_blk, z_blk = A_V_OFF // vw, A_Z_OFF // vw
    q_blk, k_blk = A_Q_OFF // kw, A_K_OFF // kw
    group = span
    return pl.pallas_call(
        kern,
        grid=(batch, GLA_HEADS // nh, ns),
        in_specs=[
            pl.BlockSpec((rows, vw), lambda b, g, s: (b * ns + s, v_blk + g)),
            pl.BlockSpec((rows, vw), lambda b, g, s: (b * ns + s, z_blk + g)),
            pl.BlockSpec((rows, kw), lambda b, g, s: (b * ns + s, q_blk + g)),
            pl.BlockSpec((rows, kw), lambda b, g, s: (b * ns + s, k_blk + g)),
            pl.BlockSpec((nh, GLA_DK_PAD, GLA_DK_PAD), lambda b, g, s: (g, 0, 0)),
            pl.BlockSpec((nh, 1, GLA_DK_PAD), lambda b, g, s: (g, 0, 0)),
            pl.BlockSpec((1, GLA_DV), lambda b, g, s: (0, 0)),
            pl.BlockSpec((group, group), lambda b, g, s: (0, 0)),
            pl.BlockSpec((group, group), lambda b, g, s: (0, 0)),
        ],
        out_specs=pl.BlockSpec((rows, vw), lambda b, g, s: (b * ns + s, g)),
        out_shape=jax.ShapeDtypeStruct((batch * seq, BRANCH_WIDTH), BF16),
        scratch_shapes=[pltpu.VMEM((nh, GLA_DK_PAD, GLA_DV), F32)],
        compiler_params=pltpu.CompilerParams(
            dimension_semantics=("parallel", "parallel", "arbitrary"), vmem_limit_bytes=VMEM_LIMIT),
        name="gla",
    )(proj, proj, proj, proj, wg, bg, gn, _prefix_matrix(group), _level_matrix(group, GLA_SUB))


def _mem_attn_kernel(q_ref, z_ref, k_ref, v_ref, o_ref):
    for h in range(MEM_HEADS):
        cols = slice(h * MEM_HEAD_DIM, (h + 1) * MEM_HEAD_DIM)
        s = _dot_nt(q_ref[:, cols], k_ref[:, cols])
        p = jnp.exp2(s - jnp.max(s, axis=-1, keepdims=True))
        l = jnp.sum(p, axis=-1, keepdims=True)
        o = _dot(p.astype(BF16), v_ref[:, cols]) * (1.0 / l)
        o_ref[:, cols] = (o * _silu(z_ref[:, cols].astype(F32))).astype(o_ref.dtype)


def _mem_attn(proj, memkv, seq, q_off, z_off, kv_blk):
    t = proj.shape[0]
    tm = min(MEM_TM, seq)
    per_batch = seq // tm
    q_blk, z_blk = q_off // MEM_WIDTH, z_off // MEM_WIDTH
    return pl.pallas_call(
        _mem_attn_kernel,
        grid=(t // tm,),
        in_specs=[
            pl.BlockSpec((tm, MEM_WIDTH), lambda i: (i, q_blk)),
            pl.BlockSpec((tm, MEM_WIDTH), lambda i: (i, z_blk)),
            pl.BlockSpec((N_MEM, MEM_WIDTH), lambda i: (i // per_batch, kv_blk)),
            pl.BlockSpec((N_MEM, MEM_WIDTH), lambda i: (i // per_batch, kv_blk + 1)),
        ],
        out_specs=pl.BlockSpec((tm, MEM_WIDTH), lambda i: (i, 0)),
        out_shape=jax.ShapeDtypeStruct((t, MEM_WIDTH), BF16),
        compiler_params=pltpu.CompilerParams(
            dimension_semantics=("parallel",), vmem_limit_bytes=VMEM_LIMIT),
        name="mem_attn",
    )(proj, proj, memkv, memkv)


def _rotary_tables(pos, freq, sign):
    rows = pos.shape[0]
    groups = LANES // (MLA_ROPE_DIM // 2)
    if rows % (8 * groups) != 0:
        ang = pos * freq
        return jnp.cos(ang), jnp.sin(ang) * sign
    blk = rows // groups
    lane_group = lax.broadcasted_iota(jnp.int32, (1, LANES), 1) // (MLA_ROPE_DIM // 2)
    packed_pos = jnp.zeros((blk, LANES), F32)
    for g in range(groups):
        packed_pos = jnp.where(lane_group == g, pos[g * blk:(g + 1) * blk], packed_pos)
    ang = packed_pos * freq
    tables = []
    for packed in (jnp.cos(ang), jnp.sin(ang)):
        blocks = []
        for g in range(groups):
            own = jnp.where(lane_group == g, packed, 0.0)
            spread = own
            for shift in range(1, groups):
                spread = spread + pltpu.roll(own, shift * (LANES // groups), axis=1)
            blocks.append(spread)
        tables.append(jnp.concatenate(blocks, axis=0))
    return tables[0], tables[1] * sign


def _mla_prep_kernel(cq_ref, c_ref, kr_ref, krot_ref, pos_ref, freq_ref, sign_ref, qg_ref, kvg_ref,
                     wuq_ref, wuk_ref, wuvt_ref, q_ref, k_ref, vt_ref, *, tk):
    tm = cq_ref.shape[0]
    n_nope = MLA_HEADS * MLA_NOPE_DIM
    n_rope = MLA_HEADS * MLA_ROPE_DIM

    cq = cq_ref[...].astype(F32)
    cqn = cq * lax.rsqrt(jnp.mean(cq * cq, axis=-1, keepdims=True) + EPS)
    cqn = (cqn * (qg_ref[...] * (MLA_QK_DIM ** -0.5 * LOG2_E))).astype(BF16)
    c = c_ref[...].astype(F32)
    cn = c * lax.rsqrt(jnp.mean(c * c, axis=-1, keepdims=True) + EPS)
    cn = (cn * kvg_ref[...]).astype(BF16)

    q_nope = _dot(cqn, wuq_ref[:, :n_nope])
    for h in range(MLA_HEADS):
        q_ref[h, :, :MLA_NOPE_DIM] = q_nope[:, h * MLA_NOPE_DIM:(h + 1) * MLA_NOPE_DIM].astype(BF16)

    cos, sin = _rotary_tables(pos_ref[...].astype(F32), freq_ref[...], sign_ref[...])

    k_nope = _dot(cn, wuk_ref[...])
    for h in range(MLA_HEADS):
        k_ref[h, :, :MLA_NOPE_DIM] = k_nope[:, h * MLA_NOPE_DIM:(h + 1) * MLA_NOPE_DIM].astype(BF16)
    v_t = _dot_nt(wuvt_ref[...], cn).astype(BF16)
    ones_rows = (lax.broadcasted_iota(jnp.int32, (MLA_VT_ROWS - MLA_V_DIM, tk), 0) == 0).astype(BF16)
    for h in range(MLA_HEADS):
        for u in range(tm // tk):
            vt_ref[h, u, :MLA_V_DIM, :] = v_t[h * MLA_V_DIM:(h + 1) * MLA_V_DIM, u * tk:(u + 1) * tk]
            vt_ref[h, u, MLA_V_DIM:, :] = ones_rows

    lane = lax.broadcasted_iota(jnp.int32, (1, LANES), 1)
    slot_mask = [(lane < MLA_ROPE_DIM).astype(F32), (lane >= MLA_ROPE_DIM).astype(F32)]
    q_r = _dot(cqn, wuq_ref[:, n_nope:n_nope + n_rope])
    q_rot = _dot(cqn, wuq_ref[:, n_nope + n_rope:])
    for u in range(MLA_HEADS // 2):
        cols = slice(u * LANES, (u + 1) * LANES)
        roped = q_r[:, cols] * cos + q_rot[:, cols] * sin
        for half in range(2):
            q_ref[2 * u + half, :, MLA_NOPE_DIM:] = (roped * slot_mask[half]).astype(BF16)
    k_rope = (kr_ref[...].astype(F32) * cos + krot_ref[...].astype(F32) * sin).astype(BF16)
    for h in range(MLA_HEADS):
        k_ref[h, :, MLA_NOPE_DIM:] = k_rope


def _mla_prep(proj, pos_col, freq, sign, q_gain, kv_gain, w_uq, w_uk, w_uv_t, batch, seq, tk):
    tm = min(PREP_TM, seq)
    per_batch = seq // tm
    kern = functools.partial(_mla_prep_kernel, tk=tk)
    const = lambda i: (0, 0)
    head_map = lambda i: (i // per_batch, 0, i % per_batch, 0)
    return pl.pallas_call(
        kern,
        grid=(batch * per_batch,),
        in_specs=[
            pl.BlockSpec((tm, MLA_Q_RANK), lambda i: (i, B_CQ_OFF // MLA_Q_RANK)),
            pl.BlockSpec((tm, MLA_KV_RANK), lambda i: (i, B_C_OFF // MLA_KV_RANK)),
            pl.BlockSpec((tm, LANES), lambda i: (i, B_KR_OFF // LANES)),
            pl.BlockSpec((tm, LANES), lambda i: (i, B_KROT_OFF // LANES)),
            pl.BlockSpec((tm, 1), lambda i: (i, 0)),
            pl.BlockSpec((1, LANES), const),
            pl.BlockSpec((1, LANES), const),
            pl.BlockSpec((1, MLA_Q_RANK), const),
            pl.BlockSpec((1, MLA_KV_RANK), const),
            pl.BlockSpec(w_uq.shape, const),
            pl.BlockSpec(w_uk.shape, const),
            pl.BlockSpec(w_uv_t.shape, const),
        ],
        out_specs=[
            pl.BlockSpec((None, MLA_HEADS, tm, MLA_QK_PAD), head_map),
            pl.BlockSpec((None, MLA_HEADS, tm, MLA_QK_PAD), head_map),
            pl.BlockSpec((None, MLA_HEADS, tm // tk, MLA_VT_ROWS, tk),
                         lambda i: (i // per_batch, 0, i % per_batch, 0, 0)),
        ],
        out_shape=[
            jax.ShapeDtypeStruct((batch, MLA_HEADS, seq, MLA_QK_PAD), BF16),
            jax.ShapeDtypeStruct((batch, MLA_HEADS, seq, MLA_QK_PAD), BF16),
            jax.ShapeDtypeStruct((batch, MLA_HEADS, seq // tk, MLA_VT_ROWS, tk), BF16),
        ],
        compiler_params=pltpu.CompilerParams(
            dimension_semantics=("parallel",), vmem_limit_bytes=VMEM_LIMIT),
        name="mla_prep",
    )(proj, proj, proj, proj, pos_col, freq, sign, q_gain, kv_gain, w_uq, w_uk, w_uv_t)


def _max_over_rows(x, ways=8):
    rows = x.shape[0]
    slab = rows // ways
    parts = [x[i * slab:(i + 1) * slab] for i in range(ways)] if slab >= 8 and rows % ways == 0 else [x]
    while len(parts) > 1:
        parts = [jnp.maximum(parts[i], parts[i + 1]) for i in range(0, len(parts), 2)]
    return jnp.max(parts[0], axis=0, keepdims=True)


def _mla_attn_kernel(q_ref, k_ref, vt_ref, z_ref, o_ref, acc_ref, m_ref, done_ref, s0_ref, s1_ref, s2_ref,
                     smax0_ref, smax1_ref, smax2_ref, *, tq, tk):
    n_heads, seq = q_ref.shape[0], q_ref.shape[1]
    n_q = seq // tq
    tn = MXU_DIM
    units = [(h, n) for h in range(n_heads) for n in range(tq // tn)]
    s_slots = (s0_ref, s1_ref, s2_ref)
    smax_slots = (smax0_ref, smax1_ref, smax2_ref)
    FIRST = 2

    def n_keys(n, diagonal):
        return min(tk, (n + 1) * tn) if diagonal else tk

    def produce(qi, kj, slot, u, diagonal):
        h, n = units[u]
        nk = n_keys(n, diagonal)
        krows = pl.ds(pl.multiple_of(kj * tk, tk), nk)
        qrows = pl.ds(pl.multiple_of(qi * tq + n * tn, tn), tn)
        s_t = _dot_nt(k_ref[h, krows, :], q_ref[h, qrows, :])
        if diagonal:
            kpos = lax.broadcasted_iota(jnp.int32, (nk, tn), 0)
            qpos = lax.broadcasted_iota(jnp.int32, (nk, tn), 1) + n * tn
            s_t = jnp.where(kpos <= qpos, s_t, -jnp.inf)
        s_slots[slot][u, :nk, :] = s_t
        smax_slots[slot][u] = _max_over_rows(s_t)

    def consume(kj, slot, u, diagonal):
        h, n = units[u]
        nk = n_keys(n, diagonal)
        cols = slice(n * tn, (n + 1) * tn)
        m_prev = m_ref[h, :, cols]
        m_new = jnp.maximum(m_prev, smax_slots[slot][u])
        alpha = jnp.exp2(m_prev - m_new)
        p = jnp.exp2(s_slots[slot][u, :nk, :] - m_new).astype(BF16)
        m_ref[h, :, cols] = m_new
        acc_ref[h, :, cols] = alpha * acc_ref[h, :, cols] + _dot(vt_ref[h, kj, :, :nk], p)

    def produce_all(qi, kj, slot, diagonal):
        for u in range(len(units)):
            produce(qi, kj, slot, u, diagonal)

    def write_out(h, qp):
        rows = pl.ds(qp * tq if isinstance(qp, int) else pl.multiple_of(qp * tq, tq), tq)
        cols = slice(h * MLA_V_DIM, (h + 1) * MLA_V_DIM)
        o = jnp.transpose(done_ref[h])
        o_ref[rows, cols] = (o * _silu(z_ref[rows, cols].astype(F32))).astype(o_ref.dtype)

    def step(kj, slot, diagonal, nxt=None, write_out_of=None):
        lead = ATT_PRODUCE_LEAD
        if nxt is not None:
            for u in range(min(lead, len(units))):
                produce(nxt[0], nxt[1], nxt[2], u, nxt[3])
        per_head = len(units) // n_heads
        for u in range(len(units)):
            consume(kj, slot, u, diagonal)
            if write_out_of is not None and u % per_head == per_head - 1:
                write_out(u // per_head, write_out_of)
            if nxt is not None and u + lead < len(units):
                produce(nxt[0], nxt[1], nxt[2], u + lead, nxt[3])

    def last_step(qi, slot):
        if n_q == 1:
            step(qi, slot, True)
            return

        @pl.when(qi < n_q - 1)
        def _():
            step(qi, slot, True, (qi + 1, 0, FIRST, False))

        last_q = n_q - 1
        if slot == {1: 0, 2: 1, 3: 0}[last_q - 2 * max(0, (last_q - 2) // 2)]:
            @pl.when(qi == last_q)
            def _():
                step(qi, slot, True)

    def q_block(qi, carry):
        m_ref[...] = jnp.full_like(m_ref, -jnp.inf)
        acc_ref[...] = jnp.zeros_like(acc_ref)

        @pl.when(qi == 0)
        def _():
            step(0, FIRST, True)
            if n_q > 1:
                produce_all(1, 0, FIRST, False)

        @pl.when(qi == 1)
        def _():
            step(0, FIRST, False, (qi, 1, 0, True), write_out_of=qi - 1)

        @pl.when(qi > 1)
        def _():
            step(0, FIRST, False, (qi, 1, 0, False), write_out_of=qi - 1)

        def steady_pair(i, c):
            kj = 2 * i + 1
            step(kj, 0, False, (qi, kj + 1, 1, False))
            step(kj + 1, 1, False, (qi, kj + 2, 0, False))
            return c

        n_pairs = jnp.maximum(qi - 2, 0) // 2
        lax.fori_loop(0, n_pairs, steady_pair, 0)
        kj = 2 * n_pairs + 1
        left = qi - kj + 1

        @pl.when((qi > 0) & (left == 1))
        def _():
            last_step(qi, 0)

        @pl.when((qi > 0) & (left == 2))
        def _():
            step(kj, 0, False, (qi, kj + 1, 1, True))
            last_step(qi, 1)

        @pl.when((qi > 0) & (left == 3))
        def _():
            step(kj, 0, False, (qi, kj + 1, 1, False))
            step(kj + 1, 1, False, (qi, kj + 2, 0, True))
            last_step(qi, 0)

        for h in range(n_heads):
            acc = acc_ref[h]
            done_ref[h] = acc[:MLA_V_DIM] * (1.0 / acc[MLA_V_DIM:MLA_V_DIM + 1])
        return carry

    produce_all(0, 0, FIRST, True)
    lax.fori_loop(0, n_q, q_block, 0)
    for h in range(n_heads):
        write_out(h, n_q - 1)


def _mla_attn(q_cat, k_cat, v_t, proj, batch, seq, tq, tk):
    kern = functools.partial(_mla_attn_kernel, tq=tq, tk=tk)
    nb = seq // tk
    nh = ATT_HEADS_PER_STEP
    n_units = nh * (tq // MXU_DIM)
    width = nh * MLA_V_DIM
    z_blk = B_Z_OFF // width
    return pl.pallas_call(
        kern,
        grid=(batch, MLA_HEADS // nh),
        in_specs=[
            pl.BlockSpec((None, nh, seq, MLA_QK_PAD), lambda b, g: (b, g, 0, 0)),
            pl.BlockSpec((None, nh, seq, MLA_QK_PAD), lambda b, g: (b, g, 0, 0)),
            pl.BlockSpec((None, nh, nb, MLA_VT_ROWS, tk), lambda b, g: (b, g, 0, 0, 0)),
            pl.BlockSpec((seq, width), lambda b, g: (b, z_blk + g)),
        ],
        out_specs=pl.BlockSpec((seq, width), lambda b, g: (b, g)),
        out_shape=jax.ShapeDtypeStruct((batch * seq, BRANCH_WIDTH), BF16),
        scratch_shapes=[
            pltpu.VMEM((nh, MLA_VT_ROWS, tq), F32),
            pltpu.VMEM((nh, 1, tq), F32),
            pltpu.VMEM((nh, MLA_V_DIM, tq), F32),
            pltpu.VMEM((n_units, tk, MXU_DIM), F32),
            pltpu.VMEM((n_units, tk, MXU_DIM), F32),
            pltpu.VMEM((n_units, tk, MXU_DIM), F32),
            pltpu.VMEM((n_units, 1, MXU_DIM), F32),
            pltpu.VMEM((n_units, 1, MXU_DIM), F32),
            pltpu.VMEM((n_units, 1, MXU_DIM), F32),
        ],
        compiler_params=pltpu.CompilerParams(
            dimension_semantics=("parallel", "parallel"), vmem_limit_bytes=VMEM_LIMIT),
        name="mla_attn",
    )(q_cat, k_cat, v_t, proj)


def _pad_heads(w, heads, width, padded):
    lead = w.shape[:-1]
    w = w.reshape(lead + (heads, width))
    w = jnp.pad(w, [(0, 0)] * len(lead) + [(0, 0), (0, padded - width)])
    return w.reshape(lead + (heads * padded,))


def _repack_kernel(*refs, n_in, with_gain, pieces, zero_ranges, transposed):
    ins = refs[:n_in]
    gains = refs[n_in:2 * n_in] if with_gain else None
    o_ref = refs[-1]

    def window(lo, width):
        return (slice(lo, lo + width), slice(None)) if transposed else (slice(None), slice(lo, lo + width))

    for dst, idx, src, width, scale in pieces:
        val = ins[idx][window(src, width)]
        if with_gain:
            val = val * gains[idx][...]
        if scale != 1.0:
            val = val * scale
        o_ref[window(dst, width)] = val.astype(o_ref.dtype)
    for lo, hi in zero_ranges:
        shape = (hi - lo, o_ref.shape[1]) if transposed else (o_ref.shape[0], hi - lo)
        o_ref[window(lo, hi - lo)] = jnp.zeros(shape, o_ref.dtype)


def _repack(ws, gains, pieces, out_width, zero_ranges=(), transposed=False):
    d = ws[0].shape[1] if transposed else ws[0].shape[0]
    td = min(REPACK_ROWS, d)
    with_gain = gains is not None
    pieces = [tuple(p) if len(p) == 5 else tuple(p) + (1.0,) for p in pieces]
    covered = sorted([(dst, dst + w) for dst, _, _, w, _ in pieces] + list(zero_ranges))
    assert covered[0][0] == 0 and covered[-1][1] == out_width
    assert all(a[1] == b[0] for a, b in zip(covered, covered[1:])), "output features written exactly once"
    kern = functools.partial(_repack_kernel, n_in=len(ws), with_gain=with_gain, pieces=tuple(pieces),
                             zero_ranges=tuple(zero_ranges), transposed=transposed)

    def spec(n):
        return (pl.BlockSpec((n, td), lambda i: (0, i)) if transposed
                else pl.BlockSpec((td, n), lambda i: (i, 0)))

    in_specs = [spec(w.shape[0] if transposed else w.shape[1]) for w in ws]
    args = list(ws)
    if with_gain:
        in_specs += [spec(1) for _ in gains]
        args += [g.reshape((1, d) if transposed else (d, 1)) for g in gains]
    return pl.pallas_call(
        kern,
        grid=(d // td,),
        in_specs=in_specs,
        out_specs=spec(out_width),
        out_shape=jax.ShapeDtypeStruct((out_width, d) if transposed else (d, out_width), BF16),
        compiler_params=pltpu.CompilerParams(
            dimension_semantics=("parallel",), vmem_limit_bytes=VMEM_LIMIT),
        name="repack",
    )(*args)


def _layer_a_weights(w_in, gain, w_g2, b_g):
    hk = GLA_HEADS * GLA_DK
    o = [int(v) for v in np.cumsum([0, hk, hk, BRANCH_WIDTH, GLA_GATE_RANK, BRANCH_WIDTH, MEM_WIDTH,
                                    MEM_WIDTH])]
    q_src, k_src, v_src, lr_src, z_src = o[0], o[1], o[2], o[3], o[4]
    pieces = [(A_V_OFF, 0, v_src, BRANCH_WIDTH), (A_Z_OFF, 0, z_src, BRANCH_WIDTH),
              (A_MQ_OFF, 0, z_src + BRANCH_WIDTH, MEM_WIDTH, MEM_Q_SCALE),
              (A_MZ_OFF, 0, z_src + BRANCH_WIDTH + MEM_WIDTH, MEM_WIDTH)]
    zeros = []
    for h in range(GLA_HEADS):
        q_dst, k_dst = A_Q_OFF + h * GLA_DK_PAD, A_K_OFF + h * GLA_DK_PAD
        pieces += [(q_dst, 0, q_src + h * GLA_DK, GLA_DK), (q_dst + GLA_DK, 0, lr_src, GLA_GATE_RANK),
                   (k_dst, 0, k_src + h * GLA_DK, GLA_DK)]
        zeros += [(q_dst + GLA_DK + GLA_GATE_RANK, q_dst + GLA_DK_PAD), (k_dst + GLA_DK, k_dst + GLA_DK_PAD)]
    w_all_t = _repack([w_in.T], [gain], pieces, A_WIDTH, zeros, transposed=True)
    wg = _pad_heads(w_g2, GLA_HEADS, GLA_DK, GLA_DK_PAD)
    wg = wg.reshape(GLA_GATE_RANK, GLA_HEADS, GLA_DK_PAD).transpose(1, 0, 2)
    wg = jnp.pad(wg, [(0, 0), (GLA_DK, GLA_DK_PAD - GLA_DK - GLA_GATE_RANK), (0, 0)]).astype(BF16)
    bg = _pad_heads(b_g, GLA_HEADS, GLA_DK, GLA_DK_PAD).reshape(GLA_HEADS, 1, GLA_DK_PAD)
    return w_all_t, wg, bg


def _layer_b_weights(w_in, gain_in, w_dkv, gain_dkv):
    half = MLA_ROPE_DIM // 2
    pieces = [(B_Z_OFF, 0, MLA_Q_RANK, BRANCH_WIDTH), (B_CQ_OFF, 0, 0, MLA_Q_RANK),
              (B_MQ_OFF, 0, MLA_Q_RANK + BRANCH_WIDTH, MEM_WIDTH, MEM_Q_SCALE),
              (B_MZ_OFF, 0, MLA_Q_RANK + BRANCH_WIDTH + MEM_WIDTH, MEM_WIDTH), (B_C_OFF, 1, 0, MLA_KV_RANK)]
    for rep in range(LANES // MLA_ROPE_DIM):
        kr, krot = B_KR_OFF + rep * MLA_ROPE_DIM, B_KROT_OFF + rep * MLA_ROPE_DIM
        pieces += [(kr, 1, MLA_KV_RANK, MLA_ROPE_DIM),
                   (krot, 1, MLA_KV_RANK + half, half), (krot + half, 1, MLA_KV_RANK, half)]
    return _repack([w_in, w_dkv], [gain_in, gain_dkv], pieces, B_WIDTH)


def _uq_weights(w_uq):
    half = MLA_ROPE_DIM // 2
    n_nope, n_rope = MLA_HEADS * MLA_NOPE_DIM, MLA_HEADS * MLA_ROPE_DIM
    pieces = []
    for h in range(MLA_HEADS):
        src = h * MLA_QK_DIM
        rope, rot = n_nope + h * MLA_ROPE_DIM, n_nope + n_rope + h * MLA_ROPE_DIM
        pieces += [(h * MLA_NOPE_DIM, 0, src, MLA_NOPE_DIM), (rope, 0, src + MLA_NOPE_DIM, MLA_ROPE_DIM),
                   (rot, 0, src + MLA_NOPE_DIM + half, half), (rot + half, 0, src + MLA_NOPE_DIM, half)]
    return _repack([w_uq], None, pieces, n_nope + 2 * n_rope)


def _rope_tables():
    r = MLA_ROPE_DIM
    freqs = ROPE_THETA ** (-jnp.arange(0, r, 2, dtype=F32) / r)
    freq = jnp.tile(freqs, LANES // (r // 2)).reshape(1, LANES)
    sign = jnp.tile(jnp.concatenate([-jnp.ones(r // 2, F32), jnp.ones(r // 2, F32)]),
                    LANES // r).reshape(1, LANES)
    return freq, sign


def kernel(x, mem, positions, a_pre_norm, a_w_in, a_w_g2, a_b_g, a_gla_norm, a_mem_norm, a_w_mem_kv,
           a_w_out, a_post_norm, kv_in_norm, w_dkv, kv_norm, w_uk, w_uv, b_pre_norm, b_w_in, b_q_norm,
           b_w_uq, b_mem_norm, b_w_mem_kv, b_w_out, b_post_norm):
    assert a_w_in.shape[0] == 1 and b_w_in.shape[0] == 1, "one A layer followed by one B layer"
    batch, seq, _ = x.shape
    t = batch * seq
    x2 = x.reshape(t, D_MODEL)

    kv_w = 2 * MEM_WIDTH
    w_memkv = _repack([a_w_mem_kv[0], b_w_mem_kv[0]], [a_mem_norm[0], b_mem_norm[0]],
                      [(0, 0, 0, kv_w), (kv_w, 1, 0, kv_w)], 2 * kv_w)
    memkv = _norm_matmul(mem.reshape(batch * N_MEM, D_MODEL), w_memkv, tm=batch * N_MEM, tn=2 * MEM_WIDTH)

    w_a_t, wg, bg = _layer_a_weights(a_w_in[0], a_pre_norm[0], a_w_g2[0], a_b_g[0])
    proj_a = _norm_matmul(x2, w_a_t, tm=PROJ_TM, tn=A_TN, w_transposed=True)
    gla_out = _gla(proj_a, wg, bg, a_gla_norm, batch, seq)
    mem_out = _mem_attn(proj_a, memkv, seq, A_MQ_OFF, A_MZ_OFF, kv_blk=0)
    x1 = _out_proj(gla_out, mem_out, a_w_out[0].astype(BF16), x2, a_post_norm)

    w_b = _layer_b_weights(b_w_in[0], b_pre_norm[0], w_dkv, kv_in_norm)
    proj_b = _norm_matmul(x1, w_b, tm=PROJ_TM, tn=B_TN)
    freq, sign = _rope_tables()
    tk = min(ATT_TK, seq)
    tq = min(ATT_TQ, seq)
    q_cat, k_cat, v_t = _mla_prep(
        proj_b, positions.reshape(t, 1), freq, sign, b_q_norm, kv_norm[None, :],
        _uq_weights(b_w_uq[0]), w_uk.astype(BF16), w_uv.T.astype(BF16), batch, seq, tk)
    mla_out = _mla_attn(q_cat, k_cat, v_t, proj_b, batch, seq, tq, tk)
    mem_out_b = _mem_attn(proj_b, memkv, seq, B_MQ_OFF, B_MZ_OFF, kv_blk=2)
    out = _out_proj(mla_out, mem_out_b, b_w_out[0].astype(BF16), x1, b_post_norm)
    return out.reshape(batch, seq, D_MODEL)
```

```python
import functools

import numpy as np
import jax
import jax.numpy as jnp
from jax import lax
from jax.experimental import pallas as pl
from jax.experimental.pallas import tpu as pltpu

F32 = jnp.float32
BF16 = jnp.bfloat16

D_MODEL = 2048
N_MEM = 256
MEM_HEADS = 4
MEM_HEAD_DIM = 128
MEM_WIDTH = MEM_HEADS * MEM_HEAD_DIM
BRANCH_WIDTH = D_MODEL - MEM_WIDTH
GLA_HEADS = 4
GLA_DV = BRANCH_WIDTH // GLA_HEADS
GLA_DK = GLA_DV // 2
GLA_GATE_RANK = 16
GLA_GATE_NORM = 16.0
MLA_HEADS = 12
MLA_V_DIM = 128
MLA_NOPE_DIM = 128
MLA_ROPE_DIM = 64
MLA_QK_DIM = MLA_NOPE_DIM + MLA_ROPE_DIM
MLA_Q_RANK = 512
MLA_KV_RANK = 512
ROPE_THETA = 10000.0
EPS = 1e-6

LANES = 128
MXU_DIM = 256
GLA_DK_PAD = MXU_DIM
MLA_QK_PAD = MXU_DIM
BF16_SUBLANES = 16
MLA_VT_ROWS = MLA_V_DIM + BF16_SUBLANES
ATT_HEADS_PER_STEP = 3
ATT_PRODUCE_LEAD = 2
LOG2_E = 1.4426950408889634
MEM_Q_SCALE = MEM_HEAD_DIM ** -0.5 * LOG2_E
VMEM_LIMIT = 56 * 1024 * 1024

PROJ_TM = 1024
PROJ_NORM_ROWS = 256
REPACK_ROWS = 256
OUT_TM = 512
OUT_CHUNK = 256
GLA_SPAN = 256
GLA_SUB = 32
GLA_HEADS_PER_STEP = 4
GLA_SPANS_PER_STEP = 4
GLA_STAGE_SKEW = 0
MEM_TM = 2048
PREP_TM = 1024
ATT_TQ = 512
ATT_TK = 512

A_V_OFF = 0
A_Z_OFF = BRANCH_WIDTH
A_MQ_OFF = 2 * BRANCH_WIDTH
A_MZ_OFF = A_MQ_OFF + MEM_WIDTH
A_Q_OFF = A_MZ_OFF + MEM_WIDTH
A_K_OFF = A_Q_OFF + GLA_HEADS * GLA_DK_PAD
A_WIDTH = A_K_OFF + GLA_HEADS * GLA_DK_PAD
A_TN = 2048

B_Z_OFF = 0
B_CQ_OFF = BRANCH_WIDTH
B_MQ_OFF = B_CQ_OFF + MLA_Q_RANK
B_MZ_OFF = B_MQ_OFF + MEM_WIDTH
B_C_OFF = B_MZ_OFF + MEM_WIDTH
B_KR_OFF = B_C_OFF + MLA_KV_RANK
B_KROT_OFF = B_KR_OFF + LANES
B_WIDTH = B_KROT_OFF + LANES
B_TN = 1280


def _silu(z):
    return z * (1.0 / (1.0 + jnp.exp2(z * -LOG2_E)))


def _scaled_log_sigmoid(g, scale):
    return (jnp.minimum(g, 0.0) - jnp.log(1.0 + jnp.exp2(jnp.abs(g) * -LOG2_E))) * scale


def _dot(a, b):
    return jnp.dot(a, b, preferred_element_type=F32)


def _dot_nt(a, b):
    return lax.dot_general(a, b, (((1,), (1,)), ((), ())), preferred_element_type=F32)


def _dot_tn(a, b):
    return lax.dot_general(a, b, (((0,), (0,)), ((), ())), preferred_element_type=F32)


def _norm_matmul_kernel(x_ref, w_ref, o_ref, h_ref, r_ref, *, norm_rows, w_transposed):
    j = pl.program_id(1)

    def scaled_product(h, r):
        acc = _dot_nt(h, w_ref[...]) if w_transposed else _dot(h, w_ref[...])
        return (acc * r).astype(o_ref.dtype)

    @pl.when(j == 0)
    def _():
        tm = x_ref.shape[0]
        for c in range(tm // norm_rows):
            rows = pl.ds(c * norm_rows, norm_rows)
            x = x_ref[rows, :]
            r = lax.rsqrt(jnp.mean(x * x, axis=-1, keepdims=True) + EPS)
            h = x.astype(BF16)
            r_ref[rows, :] = r
            h_ref[rows, :] = h
            o_ref[rows, :] = scaled_product(h, r)

    @pl.when(j > 0)
    def _():
        o_ref[...] = scaled_product(h_ref[...], r_ref[...])


def _norm_matmul(x, w, *, tm, tn, w_transposed=False):
    m, d = x.shape
    n = w.shape[0] if w_transposed else w.shape[1]
    tm = min(tm, m)
    kern = functools.partial(_norm_matmul_kernel, norm_rows=min(PROJ_NORM_ROWS, tm),
                             w_transposed=w_transposed)
    w_spec = (pl.BlockSpec((tn, d), lambda i, j: (j, 0)) if w_transposed
              else pl.BlockSpec((d, tn), lambda i, j: (0, j)))
    return pl.pallas_call(
        kern,
        grid=(m // tm, n // tn),
        in_specs=[
            pl.BlockSpec((tm, d), lambda i, j: (i, 0)),
            w_spec,
        ],
        out_specs=pl.BlockSpec((tm, tn), lambda i, j: (i, j)),
        out_shape=jax.ShapeDtypeStruct((m, n), BF16),
        scratch_shapes=[pltpu.VMEM((tm, d), BF16), pltpu.VMEM((tm, 1), F32)],
        compiler_params=pltpu.CompilerParams(
            dimension_semantics=("parallel", "arbitrary"), vmem_limit_bytes=VMEM_LIMIT),
        name="norm_matmul",
    )(x, w)


def _out_proj_kernel(a_ref, m_ref, wa_ref, wm_ref, x_ref, g_ref, o_ref, *, chunk):
    for c in range(a_ref.shape[0] // chunk):
        rows = pl.ds(c * chunk, chunk)
        y = _dot(a_ref[rows, :], wa_ref[...]) + _dot(m_ref[rows, :], wm_ref[...])
        ms = jnp.mean(y * y, axis=-1, keepdims=True)
        o_ref[rows, :] = x_ref[rows, :] + y * lax.rsqrt(ms + EPS) * g_ref[...]


def _out_proj(a, mo, w_out, x, gain):
    t = x.shape[0]
    tm = min(OUT_TM, t)
    mem_blk = BRANCH_WIDTH // MEM_WIDTH
    return pl.pallas_call(
        functools.partial(_out_proj_kernel, chunk=min(OUT_CHUNK, tm)),
        grid=(t // tm,),
        in_specs=[
            pl.BlockSpec((tm, BRANCH_WIDTH), lambda i: (i, 0)),
            pl.BlockSpec((tm, MEM_WIDTH), lambda i: (i, 0)),
            pl.BlockSpec((BRANCH_WIDTH, D_MODEL), lambda i: (0, 0)),
            pl.BlockSpec((MEM_WIDTH, D_MODEL), lambda i: (mem_blk, 0)),
            pl.BlockSpec((tm, D_MODEL), lambda i: (i, 0)),
            pl.BlockSpec((1, D_MODEL), lambda i: (0, 0)),
        ],
        out_specs=pl.BlockSpec((tm, D_MODEL), lambda i: (i, 0)),
        out_shape=jax.ShapeDtypeStruct((t, D_MODEL), F32),
        compiler_params=pltpu.CompilerParams(
            dimension_semantics=("parallel",), vmem_limit_bytes=VMEM_LIMIT),
        name="out_proj",
    )(a, mo, w_out, w_out, x, gain)


def _gla_head_stages(h, row0, v_ref, z_ref, q_ref, k_ref, wg_ref, bg_ref, gn_ref, tril_ref, lvl_ref, o_ref,
                     s_ref, sub):
    span = tril_ref.shape[0]
    rows = slice(row0, row0 + span)
    n_sub = span // sub
    n_levels = n_sub.bit_length() - 1
    scale = GLA_DK ** -0.5
    kcols = slice(h * GLA_DK_PAD, (h + 1) * GLA_DK_PAD)
    vcols = slice(h * GLA_DV, (h + 1) * GLA_DV)

    qb = q_ref[rows, kcols]
    g = _dot(qb, wg_ref[h]) + bg_ref[h]
    yield
    log_a = _scaled_log_sigmoid(g, LOG2_E / GLA_GATE_NORM)
    a_hi = log_a.astype(BF16)
    rem = log_a - a_hi.astype(F32)
    a_mid = rem.astype(BF16)
    a_lo = (rem - a_mid.astype(F32)).astype(BF16)
    yield
    tril = tril_ref[...]
    cum = _dot(tril, a_hi) + _dot(tril, a_mid) + _dot(tril, a_lo)
    yield

    refs = jnp.concatenate([cum[i * sub:i * sub + 1] for i in range(n_sub)] + [cum[span - 1:span]], axis=0)
    own = refs[:n_sub]

    def ref_rows(index_of):
        return jnp.concatenate([refs[index_of(i):index_of(i) + 1] for i in range(n_sub)], axis=0)

    def per_row(f):
        return jnp.concatenate(
            [jnp.broadcast_to(f[i:i + 1], (sub, f.shape[1])) for i in range(n_sub)], axis=0)

    own_b = per_row(own)
    qd = qb.astype(F32) * jnp.exp2(cum - per_row(own - float(np.log2(scale))))
    kd = k_ref[rows, kcols].astype(F32) * jnp.exp2(own_b - cum)
    yield
    q_ops, k_ops = [qd.astype(BF16)], [kd.astype(BF16)]
    for lvl in range(1, n_levels + 1):
        n = 1 << (lvl - 1)
        block_start = ref_rows(lambda i: (i // n) * n)
        next_start = ref_rows(lambda i: (i // n + 1) * n)
        q_ops.append(q_ops[0] if n == 1 else (qd * per_row(jnp.exp2(own - block_start))).astype(BF16))
        k_ops.append((kd * per_row(jnp.exp2(next_start - own))).astype(BF16))
    q_state = (qd * per_row(jnp.exp2(own))).astype(BF16)
    k_state = (kd * per_row(jnp.exp2(refs[n_sub:] - own))).astype(BF16)
    yield
    atts = [_dot_nt(qo, ko) for qo, ko in zip(q_ops, k_ops)]
    yield
    level = lvl_ref[...]
    att = jnp.where(level == n_levels, atts[n_levels], 0.0)
    for lvl in reversed(range(n_levels)):
        att = jnp.where(level == lvl, atts[lvl], att)
    yield
    state = s_ref[h]
    v = v_ref[rows, vcols]
    o = _dot(jnp.concatenate([att.astype(BF16), q_state], axis=1),
             jnp.concatenate([v, state.astype(BF16)], axis=0))
    decay_col = jnp.transpose(jnp.exp2(cum[span - 8:span, :]))[:, 7:8]
    s_ref[h] = decay_col * state + _dot_tn(k_state, v)
    yield
    on = o * lax.rsqrt(jnp.mean(o * o, axis=-1, keepdims=True) + EPS) * gn_ref[...]
    o_ref[rows, vcols] = (on * _silu(z_ref[rows, vcols].astype(F32))).astype(o_ref.dtype)
    yield


def _gla_kernel(v_ref, z_ref, q_ref, k_ref, wg_ref, bg_ref, gn_ref, tril_ref, lvl_ref, o_ref, s_ref, *,
                sub, skew):
    @pl.when(pl.program_id(2) == 0)
    def _():
        s_ref[...] = jnp.zeros_like(s_ref)

    span = tril_ref.shape[0]
    heads = [_gla_head_stages(h, row0, v_ref, z_ref, q_ref, k_ref, wg_ref, bg_ref, gn_ref, tril_ref,
                              lvl_ref, o_ref, s_ref, sub)
             for row0 in range(0, q_ref.shape[0], span) for h in range(s_ref.shape[0])]
    live = list(range(len(heads)))
    step = 0
    while live:
        for i in list(live):
            if step >= i * skew:
                try:
                    next(heads[i])
                except StopIteration:
                    live.remove(i)
        step += 1


def _prefix_matrix(rows):
    r = np.arange(rows)
    return jnp.asarray(r[:, None] >= r[None, :], BF16)


def _level_matrix(rows, sub):
    n_levels = (rows // sub).bit_length() - 1
    t, s = np.arange(rows)[:, None], np.arange(rows)[None, :]
    level = np.full((rows, rows), n_levels + 1, np.int32)
    level[(t // sub == s // sub) & (s <= t)] = 0
    for lvl in range(1, n_levels + 1):
        size = sub << (lvl - 1)
        level[((t // size) % 2 == 1) & (s // size == t // size - 1)] = lvl
    return jnp.asarray(level)


def _gla(proj, wg, bg, gn, batch, seq):
    span = min(GLA_SPAN, seq)
    assert span % GLA_SUB == 0 and (span // GLA_SUB) & (span // GLA_SUB - 1) == 0
    rows = span * min(GLA_SPANS_PER_STEP, seq // span)
    ns = seq // rows
    nh = GLA_HEADS_PER_STEP
    kern = functools.partial(_gla_kernel, sub=GLA_SUB, skew=GLA_STAGE_SKEW)
    vw, kw = nh * GLA_DV, nh * GLA_DK_PAD
    v_blk, z_blk = A_V_OFF // vw, A_Z_OFF // vw
    q_blk, k_blk = A_Q_OFF // kw, A_K_OFF // kw
    group = span
    return pl.pallas_call(
        kern,
        grid=(batch, GLA_HEADS // nh, ns),
        in_specs=[
            pl.BlockSpec((rows, vw), lambda b, g, s: (b * ns + s, v_blk + g)),
            pl.BlockSpec((rows, vw), lambda b, g, s: (b * ns + s, z_blk + g)),
            pl.BlockSpec((rows, kw), lambda b, g, s: (b * ns + s, q_blk + g)),
            pl.BlockSpec((rows, kw), lambda b, g, s: (b * ns + s, k_blk + g)),
            pl.BlockSpec((nh, GLA_DK_PAD, GLA_DK_PAD), lambda b, g, s: (g, 0, 0)),
            pl.BlockSpec((nh, 1, GLA_DK_PAD), lambda b, g, s: (g, 0, 0)),
            pl.BlockSpec((1, GLA_DV), lambda b, g, s: (0, 0)),
            pl.BlockSpec((group, group), lambda b, g, s: (0, 0)),
            pl.BlockSpec((group, group), lambda b, g, s: (0, 0)),
        ],
        out_specs=pl.BlockSpec((rows, vw), lambda b, g, s: (b * ns + s, g)),
        out_shape=jax.ShapeDtypeStruct((batch * seq, BRANCH_WIDTH), BF16),
        scratch_shapes=[pltpu.VMEM((nh, GLA_DK_PAD, GLA_DV), F32)],
        compiler_params=pltpu.CompilerParams(
            dimension_semantics=("parallel", "parallel", "arbitrary"), vmem_limit_bytes=VMEM_LIMIT),
        name="gla",
    )(proj, proj, proj, proj, wg, bg, gn, _prefix_matrix(group), _level_matrix(group, GLA_SUB))


def _mem_attn_kernel(q_ref, z_ref, k_ref, v_ref, o_ref):
    for h in range(MEM_HEADS):
        cols = slice(h * MEM_HEAD_DIM, (h + 1) * MEM_HEAD_DIM)
        s = _dot_nt(q_ref[:, cols], k_ref[:, cols])
        p = jnp.exp2(s - jnp.max(s, axis=-1, keepdims=True))
        l = jnp.sum(p, axis=-1, keepdims=True)
        o = _dot(p.astype(BF16), v_ref[:, cols]) * (1.0 / l)
        o_ref[:, cols] = (o * _silu(z_ref[:, cols].astype(F32))).astype(o_ref.dtype)


def _mem_attn(proj, memkv, seq, q_off, z_off, kv_blk):
    t = proj.shape[0]
    tm = min(MEM_TM, seq)
    per_batch = seq // tm
    q_blk, z_blk = q_off // MEM_WIDTH, z_off // MEM_WIDTH
    return pl.pallas_call(
        _mem_attn_kernel,
        grid=(t // tm,),
        in_specs=[
            pl.BlockSpec((tm, MEM_WIDTH), lambda i: (i, q_blk)),
            pl.BlockSpec((tm, MEM_WIDTH), lambda i: (i, z_blk)),
            pl.BlockSpec((N_MEM, MEM_WIDTH), lambda i: (i // per_batch, kv_blk)),
            pl.BlockSpec((N_MEM, MEM_WIDTH), lambda i: (i // per_batch, kv_blk + 1)),
        ],
        out_specs=pl.BlockSpec((tm, MEM_WIDTH), lambda i: (i, 0)),
        out_shape=jax.ShapeDtypeStruct((t, MEM_WIDTH), BF16),
        compiler_params=pltpu.CompilerParams(
            dimension_semantics=("parallel",), vmem_limit_bytes=VMEM_LIMIT),
        name="mem_attn",
    )(proj, proj, memkv, memkv)


def _rotary_tables(pos, freq, sign):
    rows = pos.shape[0]
    groups = LANES // (MLA_ROPE_DIM // 2)
    if rows % (8 * groups) != 0:
        ang = pos * freq
        return jnp.cos(ang), jnp.sin(ang) * sign
    blk = rows // groups
    lane_group = lax.broadcasted_iota(jnp.int32, (1, LANES), 1) // (MLA_ROPE_DIM // 2)
    packed_pos = jnp.zeros((blk, LANES), F32)
    for g in range(groups):
        packed_pos = jnp.where(lane_group == g, pos[g * blk:(g + 1) * blk], packed_pos)
    ang = packed_pos * freq
    tables = []
    for packed in (jnp.cos(ang), jnp.sin(ang)):
        blocks = []
        for g in range(groups):
            own = jnp.where(lane_group == g, packed, 0.0)
            spread = own
            for shift in range(1, groups):
                spread = spread + pltpu.roll(own, shift * (LANES // groups), axis=1)
            blocks.append(spread)
        tables.append(jnp.concatenate(blocks, axis=0))
    return tables[0], tables[1] * sign


def _mla_prep_kernel(cq_ref, c_ref, kr_ref, krot_ref, pos_ref, freq_ref, sign_ref, qg_ref, kvg_ref,
                     wuq_ref, wuk_ref, wuvt_ref, q_ref, k_ref, vt_ref, *, tk):
    tm = cq_ref.shape[0]
    n_nope = MLA_HEADS * MLA_NOPE_DIM
    n_rope = MLA_HEADS * MLA_ROPE_DIM

    cq = cq_ref[...].astype(F32)
    cqn = cq * lax.rsqrt(jnp.mean(cq * cq, axis=-1, keepdims=True) + EPS)
    cqn = (cqn * (qg_ref[...] * (MLA_QK_DIM ** -0.5 * LOG2_E))).astype(BF16)
    c = c_ref[...].astype(F32)
    cn = c * lax.rsqrt(jnp.mean(c * c, axis=-1, keepdims=True) + EPS)
    cn = (cn * kvg_ref[...]).astype(BF16)

    q_nope = _dot(cqn, wuq_ref[:, :n_nope])
    for h in range(MLA_HEADS):
        q_ref[h, :, :MLA_NOPE_DIM] = q_nope[:, h * MLA_NOPE_DIM:(h + 1) * MLA_NOPE_DIM].astype(BF16)

    cos, sin = _rotary_tables(pos_ref[...].astype(F32), freq_ref[...], sign_ref[...])

    k_nope = _dot(cn, wuk_ref[...])
    for h in range(MLA_HEADS):
        k_ref[h, :, :MLA_NOPE_DIM] = k_nope[:, h * MLA_NOPE_DIM:(h + 1) * MLA_NOPE_DIM].astype(BF16)
    v_t = _dot_nt(wuvt_ref[...], cn).astype(BF16)
    ones_rows = (lax.broadcasted_iota(jnp.int32, (MLA_VT_ROWS - MLA_V_DIM, tk), 0) == 0).astype(BF16)
    for h in range(MLA_HEADS):
        for u in range(tm // tk):
            vt_ref[h, u, :MLA_V_DIM, :] = v_t[h * MLA_V_DIM:(h + 1) * MLA_V_DIM, u * tk:(u + 1) * tk]
            vt_ref[h, u, MLA_V_DIM:, :] = ones_rows

    lane = lax.broadcasted_iota(jnp.int32, (1, LANES), 1)
    slot_mask = [(lane < MLA_ROPE_DIM).astype(F32), (lane >= MLA_ROPE_DIM).astype(F32)]
    q_r = _dot(cqn, wuq_ref[:, n_nope:n_nope + n_rope])
    q_rot = _dot(cqn, wuq_ref[:, n_nope + n_rope:])
    for u in range(MLA_HEADS // 2):
        cols = slice(u * LANES, (u + 1) * LANES)
        roped = q_r[:, cols] * cos + q_rot[:, cols] * sin
        for half in range(2):
            q_ref[2 * u + half, :, MLA_NOPE_DIM:] = (roped * slot_mask[half]).astype(BF16)
    k_rope = (kr_ref[...].astype(F32) * cos + krot_ref[...].astype(F32) * sin).astype(BF16)
    for h in range(MLA_HEADS):
        k_ref[h, :, MLA_NOPE_DIM:] = k_rope


def _mla_prep(proj, pos_col, freq, sign, q_gain, kv_gain, w_uq, w_uk, w_uv_t, batch, seq, tk):
    tm = min(PREP_TM, seq)
    per_batch = seq // tm
    kern = functools.partial(_mla_prep_kernel, tk=tk)
    const = lambda i: (0, 0)
    head_map = lambda i: (i // per_batch, 0, i % per_batch, 0)
    return pl.pallas_call(
        kern,
        grid=(batch * per_batch,),
        in_specs=[
            pl.BlockSpec((tm, MLA_Q_RANK), lambda i: (i, B_CQ_OFF // MLA_Q_RANK)),
            pl.BlockSpec((tm, MLA_KV_RANK), lambda i: (i, B_C_OFF // MLA_KV_RANK)),
            pl.BlockSpec((tm, LANES), lambda i: (i, B_KR_OFF // LANES)),
            pl.BlockSpec((tm, LANES), lambda i: (i, B_KROT_OFF // LANES)),
            pl.BlockSpec((tm, 1), lambda i: (i, 0)),
            pl.BlockSpec((1, LANES), const),
            pl.BlockSpec((1, LANES), const),
            pl.BlockSpec((1, MLA_Q_RANK), const),
            pl.BlockSpec((1, MLA_KV_RANK), const),
            pl.BlockSpec(w_uq.shape, const),
            pl.BlockSpec(w_uk.shape, const),
            pl.BlockSpec(w_uv_t.shape, const),
        ],
        out_specs=[
            pl.BlockSpec((None, MLA_HEADS, tm, MLA_QK_PAD), head_map),
            pl.BlockSpec((None, MLA_HEADS, tm, MLA_QK_PAD), head_map),
            pl.BlockSpec((None, MLA_HEADS, tm // tk, MLA_VT_ROWS, tk),
                         lambda i: (i // per_batch, 0, i % per_batch, 0, 0)),
        ],
        out_shape=[
            jax.ShapeDtypeStruct((batch, MLA_HEADS, seq, MLA_QK_PAD), BF16),
            jax.ShapeDtypeStruct((batch, MLA_HEADS, seq, MLA_QK_PAD), BF16),
            jax.ShapeDtypeStruct((batch, MLA_HEADS, seq // tk, MLA_VT_ROWS, tk), BF16),
        ],
        compiler_params=pltpu.CompilerParams(
            dimension_semantics=("parallel",), vmem_limit_bytes=VMEM_LIMIT),
        name="mla_prep",
    )(proj, proj, proj, proj, pos_col, freq, sign, q_gain, kv_gain, w_uq, w_uk, w_uv_t)


def _max_over_rows(x, ways=8):
    rows = x.shape[0]
    slab = rows // ways
    parts = [x[i * slab:(i + 1) * slab] for i in range(ways)] if slab >= 8 and rows % ways == 0 else [x]
    while len(parts) > 1:
        parts = [jnp.maximum(parts[i], parts[i + 1]) for i in range(0, len(parts), 2)]
    return jnp.max(parts[0], axis=0, keepdims=True)


def _mla_attn_kernel(q_ref, k_ref, vt_ref, z_ref, o_ref, acc_ref, m_ref, done_ref, s0_ref, s1_ref, s2_ref,
                     smax0_ref, smax1_ref, smax2_ref, *, tq, tk):
    n_heads, seq = q_ref.shape[0], q_ref.shape[1]
    n_q = seq // tq
    tn = MXU_DIM
    units = [(h, n) for h in range(n_heads) for n in range(tq // tn)]
    s_slots = (s0_ref, s1_ref, s2_ref)
    smax_slots = (smax0_ref, smax1_ref, smax2_ref)
    FIRST = 2

    def n_keys(n, diagonal):
        return min(tk, (n + 1) * tn) if diagonal else tk

    def produce(qi, kj, slot, u, diagonal):
        h, n = units[u]
        nk = n_keys(n, diagonal)
        krows = pl.ds(pl.multiple_of(kj * tk, tk), nk)
        qrows = pl.ds(pl.multiple_of(qi * tq + n * tn, tn), tn)
        s_t = _dot_nt(k_ref[h, krows, :], q_ref[h, qrows, :])
        if diagonal:
            kpos = lax.broadcasted_iota(jnp.int32, (nk, tn), 0)
            qpos = lax.broadcasted_iota(jnp.int32, (nk, tn), 1) + n * tn
            s_t = jnp.where(kpos <= qpos, s_t, -jnp.inf)
        s_slots[slot][u, :nk, :] = s_t
        smax_slots[slot][u] = _max_over_rows(s_t)

    def consume(kj, slot, u, diagonal):
        h, n = units[u]
        nk = n_keys(n, diagonal)
        cols = slice(n * tn, (n + 1) * tn)
        m_prev = m_ref[h, :, cols]
        m_new = jnp.maximum(m_prev, smax_slots[slot][u])
        alpha = jnp.exp2(m_prev - m_new)
        p = jnp.exp2(s_slots[slot][u, :nk, :] - m_new).astype(BF16)
        m_ref[h, :, cols] = m_new
        acc_ref[h, :, cols] = alpha * acc_ref[h, :, cols] + _dot(vt_ref[h, kj, :, :nk], p)

    def produce_all(qi, kj, slot, diagonal):
        for u in range(len(units)):
            produce(qi, kj, slot, u, diagonal)

    def write_out(h, qp):
        rows = pl.ds(qp * tq if isinstance(qp, int) else pl.multiple_of(qp * tq, tq), tq)
        cols = slice(h * MLA_V_DIM, (h + 1) * MLA_V_DIM)
        o = jnp.transpose(done_ref[h])
        o_ref[rows, cols] = (o * _silu(z_ref[rows, cols].astype(F32))).astype(o_ref.dtype)

    def step(kj, slot, diagonal, nxt=None, write_out_of=None):
        lead = ATT_PRODUCE_LEAD
        if nxt is not None:
            for u in range(min(lead, len(units))):
                produce(nxt[0], nxt[1], nxt[2], u, nxt[3])
        per_head = len(units) // n_heads
        for u in range(len(units)):
            consume(kj, slot, u, diagonal)
            if write_out_of is not None and u % per_head == per_head - 1:
                write_out(u // per_head, write_out_of)
            if nxt is not None and u + lead < len(units):
                produce(nxt[0], nxt[1], nxt[2], u + lead, nxt[3])

    def last_step(qi, slot):
        if n_q == 1:
            step(qi, slot, True)
            return

        @pl.when(qi < n_q - 1)
        def _():
            step(qi, slot, True, (qi + 1, 0, FIRST, False))

        last_q = n_q - 1
        if slot == {1: 0, 2: 1, 3: 0}[last_q - 2 * max(0, (last_q - 2) // 2)]:
            @pl.when(qi == last_q)
            def _():
                step(qi, slot, True)

    def q_block(qi, carry):
        m_ref[...] = jnp.full_like(m_ref, -jnp.inf)
        acc_ref[...] = jnp.zeros_like(acc_ref)

        @pl.when(qi == 0)
        def _():
            step(0, FIRST, True)
            if n_q > 1:
                produce_all(1, 0, FIRST, False)

        @pl.when(qi == 1)
        def _():
            step(0, FIRST, False, (qi, 1, 0, True), write_out_of=qi - 1)

        @pl.when(qi > 1)
        def _():
            step(0, FIRST, False, (qi, 1, 0, False), write_out_of=qi - 1)

        def steady_pair(i, c):
            kj = 2 * i + 1
            step(kj, 0, False, (qi, kj + 1, 1, False))
            step(kj + 1, 1, False, (qi, kj + 2, 0, False))
            return c

        n_pairs = jnp.maximum(qi - 2, 0) // 2
        lax.fori_loop(0, n_pairs, steady_pair, 0)
        kj = 2 * n_pairs + 1
        left = qi - kj + 1

        @pl.when((qi > 0) & (left == 1))
        def _():
            last_step(qi, 0)

        @pl.when((qi > 0) & (left == 2))
        def _():
            step(kj, 0, False, (qi, kj + 1, 1, True))
            last_step(qi, 1)

        @pl.when((qi > 0) & (left == 3))
        def _():
            step(kj, 0, False, (qi, kj + 1, 1, False))
            step(kj + 1, 1, False, (qi, kj + 2, 0, True))
            last_step(qi, 0)

        for h in range(n_heads):
            acc = acc_ref[h]
            done_ref[h] = acc[:MLA_V_DIM] * (1.0 / acc[MLA_V_DIM:MLA_V_DIM + 1])
        return carry

    produce_all(0, 0, FIRST, True)
    lax.fori_loop(0, n_q, q_block, 0)
    for h in range(n_heads):
        write_out(h, n_q - 1)


def _mla_attn(q_cat, k_cat, v_t, proj, batch, seq, tq, tk):
    kern = functools.partial(_mla_attn_kernel, tq=tq, tk=tk)
    nb = seq // tk
    nh = ATT_HEADS_PER_STEP
    n_units = nh * (tq // MXU_DIM)
    width = nh * MLA_V_DIM
    z_blk = B_Z_OFF // width
    return pl.pallas_call(
        kern,
        grid=(batch, MLA_HEADS // nh),
        in_specs=[
            pl.BlockSpec((None, nh, seq, MLA_QK_PAD), lambda b, g: (b, g, 0, 0)),
            pl.BlockSpec((None, nh, seq, MLA_QK_PAD), lambda b, g: (b, g, 0, 0)),
            pl.BlockSpec((None, nh, nb, MLA_VT_ROWS, tk), lambda b, g: (b, g, 0, 0, 0)),
            pl.BlockSpec((seq, width), lambda b, g: (b, z_blk + g)),
        ],
        out_specs=pl.BlockSpec((seq, width), lambda b, g: (b, g)),
        out_shape=jax.ShapeDtypeStruct((batch * seq, BRANCH_WIDTH), BF16),
        scratch_shapes=[
            pltpu.VMEM((nh, MLA_VT_ROWS, tq), F32),
            pltpu.VMEM((nh, 1, tq), F32),
            pltpu.VMEM((nh, MLA_V_DIM, tq), F32),
            pltpu.VMEM((n_units, tk, MXU_DIM), F32),
            pltpu.VMEM((n_units, tk, MXU_DIM), F32),
            pltpu.VMEM((n_units, tk, MXU_DIM), F32),
            pltpu.VMEM((n_units, 1, MXU_DIM), F32),
            pltpu.VMEM((n_units, 1, MXU_DIM), F32),
            pltpu.VMEM((n_units, 1, MXU_DIM), F32),
        ],
        compiler_params=pltpu.CompilerParams(
            dimension_semantics=("parallel", "parallel"), vmem_limit_bytes=VMEM_LIMIT),
        name="mla_attn",
    )(q_cat, k_cat, v_t, proj)


def _pad_heads(w, heads, width, padded):
    lead = w.shape[:-1]
    w = w.reshape(lead + (heads, width))
    w = jnp.pad(w, [(0, 0)] * len(lead) + [(0, 0), (0, padded - width)])
    return w.reshape(lead + (heads * padded,))


def _repack_kernel(*refs, n_in, with_gain, pieces, zero_ranges, transposed):
    ins = refs[:n_in]
    gains = refs[n_in:2 * n_in] if with_gain else None
    o_ref = refs[-1]

    def window(lo, width):
        return (slice(lo, lo + width), slice(None)) if transposed else (slice(None), slice(lo, lo + width))

    for dst, idx, src, width, scale in pieces:
        val = ins[idx][window(src, width)]
        if with_gain:
            val = val * gains[idx][...]
        if scale != 1.0:
            val = val * scale
        o_ref[window(dst, width)] = val.astype(o_ref.dtype)
    for lo, hi in zero_ranges:
        shape = (hi - lo, o_ref.shape[1]) if transposed else (o_ref.shape[0], hi - lo)
        o_ref[window(lo, hi - lo)] = jnp.zeros(shape, o_ref.dtype)


def _repack(ws, gains, pieces, out_width, zero_ranges=(), transposed=False):
    d = ws[0].shape[1] if transposed else ws[0].shape[0]
    td = min(REPACK_ROWS, d)
    with_gain = gains is not None
    pieces = [tuple(p) if len(p) == 5 else tuple(p) + (1.0,) for p in pieces]
    covered = sorted([(dst, dst + w) for dst, _, _, w, _ in pieces] + list(zero_ranges))
    assert covered[0][0] == 0 and covered[-1][1] == out_width
    assert all(a[1] == b[0] for a, b in zip(covered, covered[1:])), "output features written exactly once"
    kern = functools.partial(_repack_kernel, n_in=len(ws), with_gain=with_gain, pieces=tuple(pieces),
                             zero_ranges=tuple(zero_ranges), transposed=transposed)

    def spec(n):
        return (pl.BlockSpec((n, td), lambda i: (0, i)) if transposed
                else pl.BlockSpec((td, n), lambda i: (i, 0)))

    in_specs = [spec(w.shape[0] if transposed else w.shape[1]) for w in ws]
    args = list(ws)
    if with_gain:
        in_specs += [spec(1) for _ in gains]
        args += [g.reshape((1, d) if transposed else (d, 1)) for g in gains]
    return pl.pallas_call(
        kern,
        grid=(d // td,),
        in_specs=in_specs,
        out_specs=spec(out_width),
        out_shape=jax.ShapeDtypeStruct((out_width, d) if transposed else (d, out_width), BF16),
        compiler_params=pltpu.CompilerParams(
            dimension_semantics=("parallel",), vmem_limit_bytes=VMEM_LIMIT),
        name="repack",
    )(*args)


def _layer_a_weights(w_in, gain, w_g2, b_g):
    hk = GLA_HEADS * GLA_DK
    o = [int(v) for v in np.cumsum([0, hk, hk, BRANCH_WIDTH, GLA_GATE_RANK, BRANCH_WIDTH, MEM_WIDTH,
                                    MEM_WIDTH])]
    q_src, k_src, v_src, lr_src, z_src = o[0], o[1], o[2], o[3], o[4]
    pieces = [(A_V_OFF, 0, v_src, BRANCH_WIDTH), (A_Z_OFF, 0, z_src, BRANCH_WIDTH),
              (A_MQ_OFF, 0, z_src + BRANCH_WIDTH, MEM_WIDTH, MEM_Q_SCALE),
              (A_MZ_OFF, 0, z_src + BRANCH_WIDTH + MEM_WIDTH, MEM_WIDTH)]
    zeros = []
    for h in range(GLA_HEADS):
        q_dst, k_dst = A_Q_OFF + h * GLA_DK_PAD, A_K_OFF + h * GLA_DK_PAD
        pieces += [(q_dst, 0, q_src + h * GLA_DK, GLA_DK), (q_dst + GLA_DK, 0, lr_src, GLA_GATE_RANK),
                   (k_dst, 0, k_src + h * GLA_DK, GLA_DK)]
        zeros += [(q_dst + GLA_DK + GLA_GATE_RANK, q_dst + GLA_DK_PAD), (k_dst + GLA_DK, k_dst + GLA_DK_PAD)]
    w_all_t = _repack([w_in.T], [gain], pieces, A_WIDTH, zeros, transposed=True)
    wg = _pad_heads(w_g2, GLA_HEADS, GLA_DK, GLA_DK_PAD)
    wg = wg.reshape(GLA_GATE_RANK, GLA_HEADS, GLA_DK_PAD).transpose(1, 0, 2)
    wg = jnp.pad(wg, [(0, 0), (GLA_DK, GLA_DK_PAD - GLA_DK - GLA_GATE_RANK), (0, 0)]).astype(BF16)
    bg = _pad_heads(b_g, GLA_HEADS, GLA_DK, GLA_DK_PAD).reshape(GLA_HEADS, 1, GLA_DK_PAD)
    return w_all_t, wg, bg


def _layer_b_weights(w_in, gain_in, w_dkv, gain_dkv):
    half = MLA_ROPE_DIM // 2
    pieces = [(B_Z_OFF, 0, MLA_Q_RANK, BRANCH_WIDTH), (B_CQ_OFF, 0, 0, MLA_Q_RANK),
              (B_MQ_OFF, 0, MLA_Q_RANK + BRANCH_WIDTH, MEM_WIDTH, MEM_Q_SCALE),
              (B_MZ_OFF, 0, MLA_Q_RANK + BRANCH_WIDTH + MEM_WIDTH, MEM_WIDTH), (B_C_OFF, 1, 0, MLA_KV_RANK)]
    for rep in range(LANES // MLA_ROPE_DIM):
        kr, krot = B_KR_OFF + rep * MLA_ROPE_DIM, B_KROT_OFF + rep * MLA_ROPE_DIM
        pieces += [(kr, 1, MLA_KV_RANK, MLA_ROPE_DIM),
                   (krot, 1, MLA_KV_RANK + half, half), (krot + half, 1, MLA_KV_RANK, half)]
    return _repack([w_in, w_dkv], [gain_in, gain_dkv], pieces, B_WIDTH)


def _uq_weights(w_uq):
    half = MLA_ROPE_DIM // 2
    n_nope, n_rope = MLA_HEADS * MLA_NOPE_DIM, MLA_HEADS * MLA_ROPE_DIM
    pieces = []
    for h in range(MLA_HEADS):
        src = h * MLA_QK_DIM
        rope, rot = n_nope + h * MLA_ROPE_DIM, n_nope + n_rope + h * MLA_ROPE_DIM
        pieces += [(h * MLA_NOPE_DIM, 0, src, MLA_NOPE_DIM), (rope, 0, src + MLA_NOPE_DIM, MLA_ROPE_DIM),
                   (rot, 0, src + MLA_NOPE_DIM + half, half), (rot + half, 0, src + MLA_NOPE_DIM, half)]
    return _repack([w_uq], None, pieces, n_nope + 2 * n_rope)


def _rope_tables():
    r = MLA_ROPE_DIM
    freqs = ROPE_THETA ** (-jnp.arange(0, r, 2, dtype=F32) / r)
    freq = jnp.tile(freqs, LANES // (r // 2)).reshape(1, LANES)
    sign = jnp.tile(jnp.concatenate([-jnp.ones(r // 2, F32), jnp.ones(r // 2, F32)]),
                    LANES // r).reshape(1, LANES)
    return freq, sign


def kernel(x, mem, positions, a_pre_norm, a_w_in, a_w_g2, a_b_g, a_gla_norm, a_mem_norm, a_w_mem_kv,
           a_w_out, a_post_norm, kv_in_norm, w_dkv, kv_norm, w_uk, w_uv, b_pre_norm, b_w_in, b_q_norm,
           b_w_uq, b_mem_norm, b_w_mem_kv, b_w_out, b_post_norm):
    assert a_w_in.shape[0] == 1 and b_w_in.shape[0] == 1, "one A layer followed by one B layer"
    batch, seq, _ = x.shape
    t = batch * seq
    x2 = x.reshape(t, D_MODEL)

    kv_w = 2 * MEM_WIDTH
    w_memkv = _repack([a_w_mem_kv[0], b_w_mem_kv[0]], [a_mem_norm[0], b_mem_norm[0]],
                      [(0, 0, 0, kv_w), (kv_w, 1, 0, kv_w)], 2 * kv_w)
    memkv = _norm_matmul(mem.reshape(batch * N_MEM, D_MODEL), w_memkv, tm=batch * N_MEM, tn=2 * MEM_WIDTH)

    w_a_t, wg, bg = _layer_a_weights(a_w_in[0], a_pre_norm[0], a_w_g2[0], a_b_g[0])
    proj_a = _norm_matmul(x2, w_a_t, tm=PROJ_TM, tn=A_TN, w_transposed=True)
    gla_out = _gla(proj_a, wg, bg, a_gla_norm, batch, seq)
    mem_out = _mem_attn(proj_a, memkv, seq, A_MQ_OFF, A_MZ_OFF, kv_blk=0)
    x1 = _out_proj(gla_out, mem_out, a_w_out[0].astype(BF16), x2, a_post_norm)

    w_b = _layer_b_weights(b_w_in[0], b_pre_norm[0], w_dkv, kv_in_norm)
    proj_b = _norm_matmul(x1, w_b, tm=PROJ_TM, tn=B_TN)
    freq, sign = _rope_tables()
    tk = min(ATT_TK, seq)
    tq = min(ATT_TQ, seq)
    q_cat, k_cat, v_t = _mla_prep(
        proj_b, positions.reshape(t, 1), freq, sign, b_q_norm, kv_norm[None, :],
        _uq_weights(b_w_uq[0]), w_uk.astype(BF16), w_uv.T.astype(BF16), batch, seq, tk)
    mla_out = _mla_attn(q_cat, k_cat, v_t, proj_b, batch, seq, tq, tk)
    mem_out_b = _mem_attn(proj_b, memkv, seq, B_MQ_OFF, B_MZ_OFF, kv_blk=2)
    out = _out_proj(mla_out, mem_out_b, b_w_out[0].astype(BF16), x1, b_post_norm)
    return out.reshape(batch, seq, D_MODEL)
```

```python
import functools

import numpy as np
import jax
import jax.numpy as jnp
from jax import lax
from jax.experimental import pallas as pl
from jax.experimental.pallas import tpu as pltpu

F32 = jnp.float32
BF16 = jnp.bfloat16

D_MODEL = 2048
N_MEM = 256
MEM_HEADS = 4
MEM_HEAD_DIM = 128
MEM_WIDTH = MEM_HEADS * MEM_HEAD_DIM
BRANCH_WIDTH = D_MODEL - MEM_WIDTH
GLA_HEADS = 4
GLA_DV = BRANCH_WIDTH // GLA_HEADS
GLA_DK = GLA_DV // 2
GLA_GATE_RANK = 16
GLA_GATE_NORM = 16.0
MLA_HEADS = 12
MLA_V_DIM = 128
MLA_NOPE_DIM = 128
MLA_ROPE_DIM = 64
MLA_QK_DIM = MLA_NOPE_DIM + MLA_ROPE_DIM
MLA_Q_RANK = 512
MLA_KV_RANK = 512
ROPE_THETA = 10000.0
EPS = 1e-6

LANES = 128
MXU_DIM = 256
GLA_DK_PAD = MXU_DIM
MLA_QK_PAD = MXU_DIM
BF16_SUBLANES = 16
MLA_VT_ROWS = MLA_V_DIM + BF16_SUBLANES
ATT_HEADS_PER_STEP = 3
ATT_PRODUCE_LEAD = 2
LOG2_E = 1.4426950408889634
MEM_Q_SCALE = MEM_HEAD_DIM ** -0.5 * LOG2_E
VMEM_LIMIT = 56 * 1024 * 1024

PROJ_TM = 1024
PROJ_NORM_ROWS = 256
REPACK_ROWS = 256
OUT_TM = 512
OUT_CHUNK = 256
GLA_SPAN = 256
GLA_SUB = 32
GLA_HEADS_PER_STEP = 4
GLA_SPANS_PER_STEP = 4
GLA_STAGE_SKEW = 0
MEM_TM = 4096
PREP_TM = 1024
ATT_TQ = 512
ATT_TK = 512

A_V_OFF = 0
A_Z_OFF = BRANCH_WIDTH
A_MQ_OFF = 2 * BRANCH_WIDTH
A_MZ_OFF = A_MQ_OFF + MEM_WIDTH
A_Q_OFF = A_MZ_OFF + MEM_WIDTH
A_K_OFF = A_Q_OFF + GLA_HEADS * GLA_DK_PAD
A_WIDTH = A_K_OFF + GLA_HEADS * GLA_DK_PAD
A_TN = 2048

B_Z_OFF = 0
B_CQ_OFF = BRANCH_WIDTH
B_MQ_OFF = B_CQ_OFF + MLA_Q_RANK
B_MZ_OFF = B_MQ_OFF + MEM_WIDTH
B_C_OFF = B_MZ_OFF + MEM_WIDTH
B_KR_OFF = B_C_OFF + MLA_KV_RANK
B_KROT_OFF = B_KR_OFF + LANES
B_WIDTH = B_KROT_OFF + LANES
B_TN = 1280


def _silu(z):
    return z * (1.0 / (1.0 + jnp.exp2(z * -LOG2_E)))


def _scaled_log_sigmoid(g, scale):
    return (jnp.minimum(g, 0.0) - jnp.log(1.0 + jnp.exp2(jnp.abs(g) * -LOG2_E))) * scale


def _dot(a, b):
    return jnp.dot(a, b, preferred_element_type=F32)


def _dot_nt(a, b):
    return lax.dot_general(a, b, (((1,), (1,)), ((), ())), preferred_element_type=F32)


def _dot_tn(a, b):
    return lax.dot_general(a, b, (((0,), (0,)), ((), ())), preferred_element_type=F32)


def _norm_matmul_kernel(x_ref, w_ref, o_ref, h_ref, r_ref, *, norm_rows, w_transposed):
    j = pl.program_id(1)

    def scaled_product(h, r):
        acc = _dot_nt(h, w_ref[...]) if w_transposed else _dot(h, w_ref[...])
        return (acc * r).astype(o_ref.dtype)

    @pl.when(j == 0)
    def _():
        tm = x_ref.shape[0]
        for c in range(tm // norm_rows):
            rows = pl.ds(c * norm_rows, norm_rows)
            x = x_ref[rows, :]
            r = lax.rsqrt(jnp.mean(x * x, axis=-1, keepdims=True) + EPS)
            h = x.astype(BF16)
            r_ref[rows, :] = r
            h_ref[rows, :] = h
            o_ref[rows, :] = scaled_product(h, r)

    @pl.when(j > 0)
    def _():
        o_ref[...] = scaled_product(h_ref[...], r_ref[...])


def _norm_matmul(x, w, *, tm, tn, w_transposed=False):
    m, d = x.shape
    n = w.shape[0] if w_transposed else w.shape[1]
    tm = min(tm, m)
    kern = functools.partial(_norm_matmul_kernel, norm_rows=min(PROJ_NORM_ROWS, tm),
                             w_transposed=w_transposed)
    w_spec = (pl.BlockSpec((tn, d), lambda i, j: (j, 0)) if w_transposed
              else pl.BlockSpec((d, tn), lambda i, j: (0, j)))
    return pl.pallas_call(
        kern,
        grid=(m // tm, n // tn),
        in_specs=[
            pl.BlockSpec((tm, d), lambda i, j: (i, 0)),
            w_spec,
        ],
        out_specs=pl.BlockSpec((tm, tn), lambda i, j: (i, j)),
        out_shape=jax.ShapeDtypeStruct((m, n), BF16),
        scratch_shapes=[pltpu.VMEM((tm, d), BF16), pltpu.VMEM((tm, 1), F32)],
        compiler_params=pltpu.CompilerParams(
            dimension_semantics=("parallel", "arbitrary"), vmem_limit_bytes=VMEM_LIMIT),
        name="norm_matmul",
    )(x, w)


def _out_proj_kernel(a_ref, m_ref, wa_ref, wm_ref, x_ref, g_ref, o_ref, *, chunk):
    for c in range(a_ref.shape[0] // chunk):
        rows = pl.ds(c * chunk, chunk)
        y = _dot(a_ref[rows, :], wa_ref[...]) + _dot(m_ref[rows, :], wm_ref[...])
        ms = jnp.mean(y * y, axis=-1, keepdims=True)
        o_ref[rows, :] = x_ref[rows, :] + y * lax.rsqrt(ms + EPS) * g_ref[...]


def _out_proj(a, mo, w_out, x, gain):
    t = x.shape[0]
    tm = min(OUT_TM, t)
    mem_blk = BRANCH_WIDTH // MEM_WIDTH
    return pl.pallas_call(
        functools.partial(_out_proj_kernel, chunk=min(OUT_CHUNK, tm)),
        grid=(t // tm,),
        in_specs=[
            pl.BlockSpec((tm, BRANCH_WIDTH), lambda i: (i, 0)),
            pl.BlockSpec((tm, MEM_WIDTH), lambda i: (i, 0)),
            pl.BlockSpec((BRANCH_WIDTH, D_MODEL), lambda i: (0, 0)),
            pl.BlockSpec((MEM_WIDTH, D_MODEL), lambda i: (mem_blk, 0)),
            pl.BlockSpec((tm, D_MODEL), lambda i: (i, 0)),
            pl.BlockSpec((1, D_MODEL), lambda i: (0, 0)),
        ],
        out_specs=pl.BlockSpec((tm, D_MODEL), lambda i: (i, 0)),
        out_shape=jax.ShapeDtypeStruct((t, D_MODEL), F32),
        compiler_params=pltpu.CompilerParams(
            dimension_semantics=("parallel",), vmem_limit_bytes=VMEM_LIMIT),
        name="out_proj",
    )(a, mo, w_out, w_out, x, gain)


def _gla_head_stages(h, row0, v_ref, z_ref, q_ref, k_ref, wg_ref, bg_ref, gn_ref, tril_ref, lvl_ref, o_ref,
                     s_ref, sub):
    span = tril_ref.shape[0]
    rows = slice(row0, row0 + span)
    n_sub = span // sub
    n_levels = n_sub.bit_length() - 1
    scale = GLA_DK ** -0.5
    kcols = slice(h * GLA_DK_PAD, (h + 1) * GLA_DK_PAD)
    vcols = slice(h * GLA_DV, (h + 1) * GLA_DV)

    qb = q_ref[rows, kcols]
    g = _dot(qb, wg_ref[h]) + bg_ref[h]
    yield
    log_a = _scaled_log_sigmoid(g, LOG2_E / GLA_GATE_NORM)
    a_hi = log_a.astype(BF16)
    rem = log_a - a_hi.astype(F32)
    a_mid = rem.astype(BF16)
    a_lo = (rem - a_mid.astype(F32)).astype(BF16)
    yield
    tril = tril_ref[...]
    cum = _dot(tril, a_hi) + _dot(tril, a_mid) + _dot(tril, a_lo)
    yield

    refs = jnp.concatenate([cum[i * sub:i * sub + 1] for i in range(n_sub)] + [cum[span - 1:span]], axis=0)
    own = refs[:n_sub]

    def ref_rows(index_of):
        return jnp.concatenate([refs[index_of(i):index_of(i) + 1] for i in range(n_sub)], axis=0)

    def per_row(f):
        return jnp.concatenate(
            [jnp.broadcast_to(f[i:i + 1], (sub, f.shape[1])) for i in range(n_sub)], axis=0)

    own_b = per_row(own)
    qd = qb.astype(F32) * jnp.exp2(cum - per_row(own - float(np.log2(scale))))
    kd = k_ref[rows, kcols].astype(F32) * jnp.exp2(own_b - cum)
    yield
    q_ops, k_ops = [qd.astype(BF16)], [kd.astype(BF16)]
    for lvl in range(1, n_levels + 1):
        n = 1 << (lvl - 1)
        block_start = ref_rows(lambda i: (i // n) * n)
        next_start = ref_rows(lambda i: (i // n + 1) * n)
        q_ops.append(q_ops[0] if n == 1 else (qd * per_row(jnp.exp2(own - block_start))).astype(BF16))
        k_ops.append((kd * per_row(jnp.exp2(next_start - own))).astype(BF16))
    q_state = (qd * per_row(jnp.exp2(own))).astype(BF16)
    k_state = (kd * per_row(jnp.exp2(refs[n_sub:] - own))).astype(BF16)
    yield
    atts = [_dot_nt(qo, ko) for qo, ko in zip(q_ops, k_ops)]
    yield
    level = lvl_ref[...]
    att = jnp.where(level == n_levels, atts[n_levels], 0.0)
    for lvl in reversed(range(n_levels)):
        att = jnp.where(level == lvl, atts[lvl], att)
    yield
    state = s_ref[h]
    v = v_ref[rows, vcols]
    o = _dot(jnp.concatenate([att.astype(BF16), q_state], axis=1),
             jnp.concatenate([v, state.astype(BF16)], axis=0))
    decay_col = jnp.transpose(jnp.exp2(cum[span - 8:span, :]))[:, 7:8]
    s_ref[h] = decay_col * state + _dot_tn(k_state, v)
    yield
    on = o * lax.rsqrt(jnp.mean(o * o, axis=-1, keepdims=True) + EPS) * gn_ref[...]
    o_ref[rows, vcols] = (on * _silu(z_ref[rows, vcols].astype(F32))).astype(o_ref.dtype)
    yield


def _gla_kernel(v_ref, z_ref, q_ref, k_ref, wg_ref, bg_ref, gn_ref, tril_ref, lvl_ref, o_ref, s_ref, *,
                sub, skew):
    @pl.when(pl.program_id(2) == 0)
    def _():
        s_ref[...] = jnp.zeros_like(s_ref)

    span = tril_ref.shape[0]
    heads = [_gla_head_stages(h, row0, v_ref, z_ref, q_ref, k_ref, wg_ref, bg_ref, gn_ref, tril_ref,
                              lvl_ref, o_ref, s_ref, sub)
             for row0 in range(0, q_ref.shape[0], span) for h in range(s_ref.shape[0])]
    live = list(range(len(heads)))
    step = 0
    while live:
        for i in list(live):
            if step >= i * skew:
                try:
                    next(heads[i])
                except StopIteration:
                    live.remove(i)
        step += 1


def _prefix_matrix(rows):
    r = np.arange(rows)
    return jnp.asarray(r[:, None] >= r[None, :], BF16)


def _level_matrix(rows, sub):
    n_levels = (rows // sub).bit_length() - 1
    t, s = np.arange(rows)[:, None], np.arange(rows)[None, :]
    level = np.full((rows, rows), n_levels + 1, np.int32)
    level[(t // sub == s // sub) & (s <= t)] = 0
    for lvl in range(1, n_levels + 1):
        size = sub << (lvl - 1)
        level[((t // size) % 2 == 1) & (s // size == t // size - 1)] = lvl
    return jnp.asarray(level)


def _gla(proj, wg, bg, gn, batch, seq):
    span = min(GLA_SPAN, seq)
    assert span % GLA_SUB == 0 and (span // GLA_SUB) & (span // GLA_SUB - 1) == 0
    rows = span * min(GLA_SPANS_PER_STEP, seq // span)
    ns = seq // rows
    nh = GLA_HEADS_PER_STEP
    kern = functools.partial(_gla_kernel, sub=GLA_SUB, skew=GLA_STAGE_SKEW)
    vw, kw = nh * GLA_DV, nh * GLA_DK_PAD
    v_blk, z_blk = A_V_OFF // vw, A_Z_OFF // vw
    q_blk, k_blk = A_Q_OFF // kw, A_K_OFF // kw
    group = span
    return pl.pallas_call(
        kern,
        grid=(batch, GLA_HEADS // nh, ns),
        in_specs=[
            pl.BlockSpec((rows, vw), lambda b, g, s: (b * ns + s, v_blk + g)),
            pl.BlockSpec((rows, vw), lambda b, g, s: (b * ns + s, z_blk + g)),
            pl.BlockSpec((rows, kw), lambda b, g, s: (b * ns + s, q_blk + g)),
            pl.BlockSpec((rows, kw), lambda b, g, s: (b * ns + s, k_blk + g)),
            pl.BlockSpec((nh, GLA_DK_PAD, GLA_DK_PAD), lambda b, g, s: (g, 0, 0)),
            pl.BlockSpec((nh, 1, GLA_DK_PAD), lambda b, g, s: (g, 0, 0)),
            pl.BlockSpec((1, GLA_DV), lambda b, g, s: (0, 0)),
            pl.BlockSpec((group, group), lambda b, g, s: (0, 0)),
            pl.BlockSpec((group, group), lambda b, g, s: (0, 0)),
        ],
        out_specs=pl.BlockSpec((rows, vw), lambda b, g, s: (b * ns + s, g)),
        out_shape=jax.ShapeDtypeStruct((batch * seq, BRANCH_WIDTH), BF16),
        scratch_shapes=[pltpu.VMEM((nh, GLA_DK_PAD, GLA_DV), F32)],
        compiler_params=pltpu.CompilerParams(
            dimension_semantics=("parallel", "parallel", "arbitrary"), vmem_limit_bytes=VMEM_LIMIT),
        name="gla",
    )(proj, proj, proj, proj, wg, bg, gn, _prefix_matrix(group), _level_matrix(group, GLA_SUB))


def _mem_attn_kernel(q_ref, z_ref, k_ref, v_ref, o_ref):
    for h in range(MEM_HEADS):
        cols = slice(h * MEM_HEAD_DIM, (h + 1) * MEM_HEAD_DIM)
        s = _dot_nt(q_ref[:, cols], k_ref[:, cols])
        p = jnp.exp2(s - jnp.max(s, axis=-1, keepdims=True))
        l = jnp.sum(p, axis=-1, keepdims=True)
        o = _dot(p.astype(BF16), v_ref[:, cols]) * (1.0 / l)
        o_ref[:, cols] = (o * _silu(z_ref[:, cols].astype(F32))).astype(o_ref.dtype)


def _mem_attn(proj, memkv, seq, q_off, z_off, kv_blk):
    t = proj.shape[0]
    tm = min(MEM_TM, seq)
    per_batch = seq // tm
    q_blk, z_blk = q_off // MEM_WIDTH, z_off // MEM_WIDTH
    return pl.pallas_call(
        _mem_attn_kernel,
        grid=(t // tm,),
        in_specs=[
            pl.BlockSpec((tm, MEM_WIDTH), lambda i: (i, q_blk)),
            pl.BlockSpec((tm, MEM_WIDTH), lambda i: (i, z_blk)),
            pl.BlockSpec((N_MEM, MEM_WIDTH), lambda i: (i // per_batch, kv_blk)),
            pl.BlockSpec((N_MEM, MEM_WIDTH), lambda i: (i // per_batch, kv_blk + 1)),
        ],
        out_specs=pl.BlockSpec((tm, MEM_WIDTH), lambda i: (i, 0)),
        out_shape=jax.ShapeDtypeStruct((t, MEM_WIDTH), BF16),
        compiler_params=pltpu.CompilerParams(
            dimension_semantics=("parallel",), vmem_limit_bytes=VMEM_LIMIT),
        name="mem_attn",
    )(proj, proj, memkv, memkv)


def _rotary_tables(pos, freq, sign):
    rows = pos.shape[0]
    groups = LANES // (MLA_ROPE_DIM // 2)
    if rows % (8 * groups) != 0:
        ang = pos * freq
        return jnp.cos(ang), jnp.sin(ang) * sign
    blk = rows // groups
    lane_group = lax.broadcasted_iota(jnp.int32, (1, LANES), 1) // (MLA_ROPE_DIM // 2)
    packed_pos = jnp.zeros((blk, LANES), F32)
    for g in range(groups):
        packed_pos = jnp.where(lane_group == g, pos[g * blk:(g + 1) * blk], packed_pos)
    ang = packed_pos * freq
    tables = []
    for packed in (jnp.cos(ang), jnp.sin(ang)):
        blocks = []
        for g in range(groups):
            own = jnp.where(lane_group == g, packed, 0.0)
            spread = own
            for shift in range(1, groups):
                spread = spread + pltpu.roll(own, shift * (LANES // groups), axis=1)
            blocks.append(spread)
        tables.append(jnp.concatenate(blocks, axis=0))
    return tables[0], tables[1] * sign


def _mla_prep_kernel(cq_ref, c_ref, kr_ref, krot_ref, pos_ref, freq_ref, sign_ref, qg_ref, kvg_ref,
                     wuq_ref, wuk_ref, wuvt_ref, q_ref, k_ref, vt_ref, *, tk):
    tm = cq_ref.shape[0]
    n_nope = MLA_HEADS * MLA_NOPE_DIM
    n_rope = MLA_HEADS * MLA_ROPE_DIM

    cq = cq_ref[...].astype(F32)
    cqn = cq * lax.rsqrt(jnp.mean(cq * cq, axis=-1, keepdims=True) + EPS)
    cqn = (cqn * (qg_ref[...] * (MLA_QK_DIM ** -0.5 * LOG2_E))).astype(BF16)
    c = c_ref[...].astype(F32)
    cn = c * lax.rsqrt(jnp.mean(c * c, axis=-1, keepdims=True) + EPS)
    cn = (cn * kvg_ref[...]).astype(BF16)

    q_nope = _dot(cqn, wuq_ref[:, :n_nope])
    for h in range(MLA_HEADS):
        q_ref[h, :, :MLA_NOPE_DIM] = q_nope[:, h * MLA_NOPE_DIM:(h + 1) * MLA_NOPE_DIM].astype(BF16)

    cos, sin = _rotary_tables(pos_ref[...].astype(F32), freq_ref[...], sign_ref[...])

    k_nope = _dot(cn, wuk_ref[...])
    for h in range(MLA_HEADS):
        k_ref[h, :, :MLA_NOPE_DIM] = k_nope[:, h * MLA_NOPE_DIM:(h + 1) * MLA_NOPE_DIM].astype(BF16)
    v_t = _dot_nt(wuvt_ref[...], cn).astype(BF16)
    ones_rows = (lax.broadcasted_iota(jnp.int32, (MLA_VT_ROWS - MLA_V_DIM, tk), 0) == 0).astype(BF16)
    for h in range(MLA_HEADS):
        for u in range(tm // tk):
            vt_ref[h, u, :MLA_V_DIM, :] = v_t[h * MLA_V_DIM:(h + 1) * MLA_V_DIM, u * tk:(u + 1) * tk]
            vt_ref[h, u, MLA_V_DIM:, :] = ones_rows

    lane = lax.broadcasted_iota(jnp.int32, (1, LANES), 1)
    slot_mask = [(lane < MLA_ROPE_DIM).astype(F32), (lane >= MLA_ROPE_DIM).astype(F32)]
    q_r = _dot(cqn, wuq_ref[:, n_nope:n_nope + n_rope])
    q_rot = _dot(cqn, wuq_ref[:, n_nope + n_rope:])
    for u in range(MLA_HEADS // 2):
        cols = slice(u * LANES, (u + 1) * LANES)
        roped = q_r[:, cols] * cos + q_rot[:, cols] * sin
        for half in range(2):
            q_ref[2 * u + half, :, MLA_NOPE_DIM:] = (roped * slot_mask[half]).astype(BF16)
    k_rope = (kr_ref[...].astype(F32) * cos + krot_ref[...].astype(F32) * sin).astype(BF16)
    for h in range(MLA_HEADS):
        k_ref[h, :, MLA_NOPE_DIM:] = k_rope


def _mla_prep(proj, pos_col, freq, sign, q_gain, kv_gain, w_uq, w_uk, w_uv_t, batch, seq, tk):
    tm = min(PREP_TM, seq)
    per_batch = seq // tm
    kern = functools.partial(_mla_prep_kernel, tk=tk)
    const = lambda i: (0, 0)
    head_map = lambda i: (i // per_batch, 0, i % per_batch, 0)
    return pl.pallas_call(
        kern,
        grid=(batch * per_batch,),
        in_specs=[
            pl.BlockSpec((tm, MLA_Q_RANK), lambda i: (i, B_CQ_OFF // MLA_Q_RANK)),
            pl.BlockSpec((tm, MLA_KV_RANK), lambda i: (i, B_C_OFF // MLA_KV_RANK)),
            pl.BlockSpec((tm, LANES), lambda i: (i, B_KR_OFF // LANES)),
            pl.BlockSpec((tm, LANES), lambda i: (i, B_KROT_OFF // LANES)),
            pl.BlockSpec((tm, 1), lambda i: (i, 0)),
            pl.BlockSpec((1, LANES), const),
            pl.BlockSpec((1, LANES), const),
            pl.BlockSpec((1, MLA_Q_RANK), const),
            pl.BlockSpec((1, MLA_KV_RANK), const),
            pl.BlockSpec(w_uq.shape, const),
            pl.BlockSpec(w_uk.shape, const),
            pl.BlockSpec(w_uv_t.shape, const),
        ],
        out_specs=[
            pl.BlockSpec((None, MLA_HEADS, tm, MLA_QK_PAD), head_map),
            pl.BlockSpec((None, MLA_HEADS, tm, MLA_QK_PAD), head_map),
            pl.BlockSpec((None, MLA_HEADS, tm // tk, MLA_VT_ROWS, tk),
                         lambda i: (i // per_batch, 0, i % per_batch, 0, 0)),
        ],
        out_shape=[
            jax.ShapeDtypeStruct((batch, MLA_HEADS, seq, MLA_QK_PAD), BF16),
            jax.ShapeDtypeStruct((batch, MLA_HEADS, seq, MLA_QK_PAD), BF16),
            jax.ShapeDtypeStruct((batch, MLA_HEADS, seq // tk, MLA_VT_ROWS, tk), BF16),
        ],
        compiler_params=pltpu.CompilerParams(
            dimension_semantics=("parallel",), vmem_limit_bytes=VMEM_LIMIT),
        name="mla_prep",
    )(proj, proj, proj, proj, pos_col, freq, sign, q_gain, kv_gain, w_uq, w_uk, w_uv_t)


def _max_over_rows(x, ways=8):
    rows = x.shape[0]
    slab = rows // ways
    parts = [x[i * slab:(i + 1) * slab] for i in range(ways)] if slab >= 8 and rows % ways == 0 else [x]
    while len(parts) > 1:
        parts = [jnp.maximum(parts[i], parts[i + 1]) for i in range(0, len(parts), 2)]
    return jnp.max(parts[0], axis=0, keepdims=True)


def _mla_attn_kernel(q_ref, k_ref, vt_ref, z_ref, o_ref, acc_ref, m_ref, done_ref, s0_ref, s1_ref, s2_ref,
                     smax0_ref, smax1_ref, smax2_ref, *, tq, tk):
    n_heads, seq = q_ref.shape[0], q_ref.shape[1]
    n_q = seq // tq
    tn = MXU_DIM
    units = [(h, n) for h in range(n_heads) for n in range(tq // tn)]
    s_slots = (s0_ref, s1_ref, s2_ref)
    smax_slots = (smax0_ref, smax1_ref, smax2_ref)
    FIRST = 2

    def n_keys(n, diagonal):
        return min(tk, (n + 1) * tn) if diagonal else tk

    def produce(qi, kj, slot, u, diagonal):
        h, n = units[u]
        nk = n_keys(n, diagonal)
        krows = pl.ds(pl.multiple_of(kj * tk, tk), nk)
        qrows = pl.ds(pl.multiple_of(qi * tq + n * tn, tn), tn)
        s_t = _dot_nt(k_ref[h, krows, :], q_ref[h, qrows, :])
        if diagonal:
            kpos = lax.broadcasted_iota(jnp.int32, (nk, tn), 0)
            qpos = lax.broadcasted_iota(jnp.int32, (nk, tn), 1) + n * tn
            s_t = jnp.where(kpos <= qpos, s_t, -jnp.inf)
        s_slots[slot][u, :nk, :] = s_t
        smax_slots[slot][u] = _max_over_rows(s_t)

    def consume(kj, slot, u, diagonal):
        h, n = units[u]
        nk = n_keys(n, diagonal)
        cols = slice(n * tn, (n + 1) * tn)
        m_prev = m_ref[h, :, cols]
        m_new = jnp.maximum(m_prev, smax_slots[slot][u])
        alpha = jnp.exp2(m_prev - m_new)
        p = jnp.exp2(s_slots[slot][u, :nk, :] - m_new).astype(BF16)
        m_ref[h, :, cols] = m_new
        acc_ref[h, :, cols] = alpha * acc_ref[h, :, cols] + _dot(vt_ref[h, kj, :, :nk], p)

    def produce_all(qi, kj, slot, diagonal):
        for u in range(len(units)):
            produce(qi, kj, slot, u, diagonal)

    def write_out(h, qp):
        rows = pl.ds(qp * tq if isinstance(qp, int) else pl.multiple_of(qp * tq, tq), tq)
        cols = slice(h * MLA_V_DIM, (h + 1) * MLA_V_DIM)
        o = jnp.transpose(done_ref[h])
        o_ref[rows, cols] = (o * _silu(z_ref[rows, cols].astype(F32))).astype(o_ref.dtype)

    def step(kj, slot, diagonal, nxt=None, write_out_of=None):
        lead = ATT_PRODUCE_LEAD
        if nxt is not None:
            for u in range(min(lead, len(units))):
                produce(nxt[0], nxt[1], nxt[2], u, nxt[3])
        per_head = len(units) // n_heads
        for u in range(len(units)):
            consume(kj, slot, u, diagonal)
            if write_out_of is not None and u % per_head == per_head - 1:
                write_out(u // per_head, write_out_of)
            if nxt is not None and u + lead < len(units):
                produce(nxt[0], nxt[1], nxt[2], u + lead, nxt[3])

    def last_step(qi, slot):
        if n_q == 1:
            step(qi, slot, True)
            return

        @pl.when(qi < n_q - 1)
        def _():
            step(qi, slot, True, (qi + 1, 0, FIRST, False))

        last_q = n_q - 1
        if slot == (last_q - 1) % 2:
            @pl.when(qi == last_q)
            def _():
                step(qi, slot, True)

    def q_block(qi, carry):
        m_ref[...] = jnp.full_like(m_ref, -jnp.inf)
        acc_ref[...] = jnp.zeros_like(acc_ref)

        @pl.when(qi == 0)
        def _():
            step(0, FIRST, True)
            if n_q > 1:
                produce_all(1, 0, FIRST, False)

        @pl.when(qi == 1)
        def _():
            step(0, FIRST, False, (qi, 1, 0, True), write_out_of=qi - 1)

        @pl.when(qi > 1)
        def _():
            step(0, FIRST, False, (qi, 1, 0, False), write_out_of=qi - 1)

        def steady_pair(i, c):
            kj = 2 * i + 1
            step(kj, 0, False, (qi, kj + 1, 1, False))
            step(kj + 1, 1, False, (qi, kj + 2, 0, False))
            return c

        n_pairs = jnp.maximum(qi - 2, 0) // 2
        lax.fori_loop(0, n_pairs, steady_pair, 0)
        kj = 2 * n_pairs + 1
        left = qi - kj + 1

        @pl.when((qi > 0) & (left == 1))
        def _():
            last_step(qi, 0)

        @pl.when((qi > 0) & (left == 2))
        def _():
            step(kj, 0, False, (qi, kj + 1, 1, True))
            last_step(qi, 1)

        @pl.when((qi > 0) & (left == 3))
        def _():
            step(kj, 0, False, (qi, kj + 1, 1, False))
            step(kj + 1, 1, False, (qi, kj + 2, 0, True))
            last_step(qi, 0)

        for h in range(n_heads):
            acc = acc_ref[h]
            done_ref[h] = acc[:MLA_V_DIM] * (1.0 / acc[MLA_V_DIM:MLA_V_DIM + 1])
        return carry

    produce_all(0, 0, FIRST, True)
    lax.fori_loop(0, n_q, q_block, 0)
    for h in range(n_heads):
        write_out(h, n_q - 1)


def _mla_attn(q_cat, k_cat, v_t, proj, batch, seq, tq, tk):
    kern = functools.partial(_mla_attn_kernel, tq=tq, tk=tk)
    nb = seq // tk
    nh = ATT_HEADS_PER_STEP
    n_units = nh * (tq // MXU_DIM)
    width = nh * MLA_V_DIM
    z_blk = B_Z_OFF // width
    return pl.pallas_call(
        kern,
        grid=(batch, MLA_HEADS // nh),
        in_specs=[
            pl.BlockSpec((None, nh, seq, MLA_QK_PAD), lambda b, g: (b, g, 0, 0)),
            pl.BlockSpec((None, nh, seq, MLA_QK_PAD), lambda b, g: (b, g, 0, 0)),
            pl.BlockSpec((None, nh, nb, MLA_VT_ROWS, tk), lambda b, g: (b, g, 0, 0, 0)),
            pl.BlockSpec((seq, width), lambda b, g: (b, z_blk + g)),
        ],
        out_specs=pl.BlockSpec((seq, width), lambda b, g: (b, g)),
        out_shape=jax.ShapeDtypeStruct((batch * seq, BRANCH_WIDTH), BF16),
        scratch_shapes=[
            pltpu.VMEM((nh, MLA_VT_ROWS, tq), F32),
            pltpu.VMEM((nh, 1, tq), F32),
            pltpu.VMEM((nh, MLA_V_DIM, tq), F32),
            pltpu.VMEM((n_units, tk, MXU_DIM), F32),
            pltpu.VMEM((n_units, tk, MXU_DIM), F32),
            pltpu.VMEM((n_units, tk, MXU_DIM), F32),
            pltpu.VMEM((n_units, 1, MXU_DIM), F32),
            pltpu.VMEM((n_units, 1, MXU_DIM), F32),
            pltpu.VMEM((n_units, 1, MXU_DIM), F32),
        ],
        compiler_params=pltpu.CompilerParams(
            dimension_semantics=("parallel", "parallel"), vmem_limit_bytes=VMEM_LIMIT),
        name="mla_attn",
    )(q_cat, k_cat, v_t, proj)


def _pad_heads(w, heads, width, padded):
    lead = w.shape[:-1]
    w = w.reshape(lead + (heads, width))
    w = jnp.pad(w, [(0, 0)] * len(lead) + [(0, 0), (0, padded - width)])
    return w.reshape(lead + (heads * padded,))


def _repack_kernel(*refs, n_in, with_gain, pieces, zero_ranges, transposed):
    ins = refs[:n_in]
    gains = refs[n_in:2 * n_in] if with_gain else None
    o_ref = refs[-1]

    def window(lo, width):
        return (slice(lo, lo + width), slice(None)) if transposed else (slice(None), slice(lo, lo + width))

    for dst, idx, src, width, scale in pieces:
        val = ins[idx][window(src, width)]
        if with_gain:
            val = val * gains[idx][...]
        if scale != 1.0:
            val = val * scale
        o_ref[window(dst, width)] = val.astype(o_ref.dtype)
    for lo, hi in zero_ranges:
        shape = (hi - lo, o_ref.shape[1]) if transposed else (o_ref.shape[0], hi - lo)
        o_ref[window(lo, hi - lo)] = jnp.zeros(shape, o_ref.dtype)


def _repack(ws, gains, pieces, out_width, zero_ranges=(), transposed=False):
    d = ws[0].shape[1] if transposed else ws[0].shape[0]
    td = min(REPACK_ROWS, d)
    with_gain = gains is not None
    pieces = [tuple(p) if len(p) == 5 else tuple(p) + (1.0,) for p in pieces]
    covered = sorted([(dst, dst + w) for dst, _, _, w, _ in pieces] + list(zero_ranges))
    assert covered[0][0] == 0 and covered[-1][1] == out_width
    assert all(a[1] == b[0] for a, b in zip(covered, covered[1:])), "output features written exactly once"
    kern = functools.partial(_repack_kernel, n_in=len(ws), with_gain=with_gain, pieces=tuple(pieces),
                             zero_ranges=tuple(zero_ranges), transposed=transposed)

    def spec(n):
        return (pl.BlockSpec((n, td), lambda i: (0, i)) if transposed
                else pl.BlockSpec((td, n), lambda i: (i, 0)))

    in_specs = [spec(w.shape[0] if transposed else w.shape[1]) for w in ws]
    args = list(ws)
    if with_gain:
        in_specs += [spec(1) for _ in gains]
        args += [g.reshape((1, d) if transposed else (d, 1)) for g in gains]
    return pl.pallas_call(
        kern,
        grid=(d // td,),
        in_specs=in_specs,
        out_specs=spec(out_width),
        out_shape=jax.ShapeDtypeStruct((out_width, d) if transposed else (d, out_width), BF16),
        compiler_params=pltpu.CompilerParams(
            dimension_semantics=("parallel",), vmem_limit_bytes=VMEM_LIMIT),
        name="repack",
    )(*args)


def _layer_a_weights(w_in, gain, w_g2, b_g):
    hk = GLA_HEADS * GLA_DK
    o = [int(v) for v in np.cumsum([0, hk, hk, BRANCH_WIDTH, GLA_GATE_RANK, BRANCH_WIDTH, MEM_WIDTH,
                                    MEM_WIDTH])]
    q_src, k_src, v_src, lr_src, z_src = o[0], o[1], o[2], o[3], o[4]
    pieces = [(A_V_OFF, 0, v_src, BRANCH_WIDTH), (A_Z_OFF, 0, z_src, BRANCH_WIDTH),
              (A_MQ_OFF, 0, z_src + BRANCH_WIDTH, MEM_WIDTH, MEM_Q_SCALE),
              (A_MZ_OFF, 0, z_src + BRANCH_WIDTH + MEM_WIDTH, MEM_WIDTH)]
    zeros = []
    for h in range(GLA_HEADS):
        q_dst, k_dst = A_Q_OFF + h * GLA_DK_PAD, A_K_OFF + h * GLA_DK_PAD
        pieces += [(q_dst, 0, q_src + h * GLA_DK, GLA_DK), (q_dst + GLA_DK, 0, lr_src, GLA_GATE_RANK),
                   (k_dst, 0, k_src + h * GLA_DK, GLA_DK)]
        zeros += [(q_dst + GLA_DK + GLA_GATE_RANK, q_dst + GLA_DK_PAD), (k_dst + GLA_DK, k_dst + GLA_DK_PAD)]
    w_all_t = _repack([w_in.T], [gain], pieces, A_WIDTH, zeros, transposed=True)
    wg = _pad_heads(w_g2, GLA_HEADS, GLA_DK, GLA_DK_PAD)
    wg = wg.reshape(GLA_GATE_RANK, GLA_HEADS, GLA_DK_PAD).transpose(1, 0, 2)
    wg = jnp.pad(wg, [(0, 0), (GLA_DK, GLA_DK_PAD - GLA_DK - GLA_GATE_RANK), (0, 0)]).astype(BF16)
    bg = _pad_heads(b_g, GLA_HEADS, GLA_DK, GLA_DK_PAD).reshape(GLA_HEADS, 1, GLA_DK_PAD)
    return w_all_t, wg, bg


def _layer_b_weights(w_in, gain_in, w_dkv, gain_dkv):
    half = MLA_ROPE_DIM // 2
    pieces = [(B_Z_OFF, 0, MLA_Q_RANK, BRANCH_WIDTH), (B_CQ_OFF, 0, 0, MLA_Q_RANK),
              (B_MQ_OFF, 0, MLA_Q_RANK + BRANCH_WIDTH, MEM_WIDTH, MEM_Q_SCALE),
              (B_MZ_OFF, 0, MLA_Q_RANK + BRANCH_WIDTH + MEM_WIDTH, MEM_WIDTH), (B_C_OFF, 1, 0, MLA_KV_RANK)]
    for rep in range(LANES // MLA_ROPE_DIM):
        kr, krot = B_KR_OFF + rep * MLA_ROPE_DIM, B_KROT_OFF + rep * MLA_ROPE_DIM
        pieces += [(kr, 1, MLA_KV_RANK, MLA_ROPE_DIM),
                   (krot, 1, MLA_KV_RANK + half, half), (krot + half, 1, MLA_KV_RANK, half)]
    return _repack([w_in, w_dkv], [gain_in, gain_dkv], pieces, B_WIDTH)


def _uq_weights(w_uq):
    half = MLA_ROPE_DIM // 2
    n_nope, n_rope = MLA_HEADS * MLA_NOPE_DIM, MLA_HEADS * MLA_ROPE_DIM
    pieces = []
    for h in range(MLA_HEADS):
        src = h * MLA_QK_DIM
        rope, rot = n_nope + h * MLA_ROPE_DIM, n_nope + n_rope + h * MLA_ROPE_DIM
        pieces += [(h * MLA_NOPE_DIM, 0, src, MLA_NOPE_DIM), (rope, 0, src + MLA_NOPE_DIM, MLA_ROPE_DIM),
                   (rot, 0, src + MLA_NOPE_DIM + half, half), (rot + half, 0, src + MLA_NOPE_DIM, half)]
    return _repack([w_uq], None, pieces, n_nope + 2 * n_rope)


def _rope_tables():
    r = MLA_ROPE_DIM
    freqs = ROPE_THETA ** (-jnp.arange(0, r, 2, dtype=F32) / r)
    freq = jnp.tile(freqs, LANES // (r // 2)).reshape(1, LANES)
    sign = jnp.tile(jnp.concatenate([-jnp.ones(r // 2, F32), jnp.ones(r // 2, F32)]),
                    LANES // r).reshape(1, LANES)
    return freq, sign


def kernel(x, mem, positions, a_pre_norm, a_w_in, a_w_g2, a_b_g, a_gla_norm, a_mem_norm, a_w_mem_kv,
           a_w_out, a_post_norm, kv_in_norm, w_dkv, kv_norm, w_uk, w_uv, b_pre_norm, b_w_in, b_q_norm,
           b_w_uq, b_mem_norm, b_w_mem_kv, b_w_out, b_post_norm):
    assert a_w_in.shape[0] == 1 and b_w_in.shape[0] == 1, "one A layer followed by one B layer"
    batch, seq, _ = x.shape
    t = batch * seq
    x2 = x.reshape(t, D_MODEL)

    kv_w = 2 * MEM_WIDTH
    w_memkv = _repack([a_w_mem_kv[0], b_w_mem_kv[0]], [a_mem_norm[0], b_mem_norm[0]],
                      [(0, 0, 0, kv_w), (kv_w, 1, 0, kv_w)], 2 * kv_w)
    memkv = _norm_matmul(mem.reshape(batch * N_MEM, D_MODEL), w_memkv, tm=batch * N_MEM, tn=2 * MEM_WIDTH)

    w_a_t, wg, bg = _layer_a_weights(a_w_in[0], a_pre_norm[0], a_w_g2[0], a_b_g[0])
    proj_a = _norm_matmul(x2, w_a_t, tm=PROJ_TM, tn=A_TN, w_transposed=True)
    gla_out = _gla(proj_a, wg, bg, a_gla_norm, batch, seq)
    mem_out = _mem_attn(proj_a, memkv, seq, A_MQ_OFF, A_MZ_OFF, kv_blk=0)
    x1 = _out_proj(gla_out, mem_out, a_w_out[0].astype(BF16), x2, a_post_norm)

    w_b = _layer_b_weights(b_w_in[0], b_pre_norm[0], w_dkv, kv_in_norm)
    proj_b = _norm_matmul(x1, w_b, tm=PROJ_TM, tn=B_TN)
    freq, sign = _rope_tables()
    tk = min(ATT_TK, seq)
    tq = min(ATT_TQ, seq)
    q_cat, k_cat, v_t = _mla_prep(
        proj_b, positions.reshape(t, 1), freq, sign, b_q_norm, kv_norm[None, :],
        _uq_weights(b_w_uq[0]), w_uk.astype(BF16), w_uv.T.astype(BF16), batch, seq, tk)
    mla_out = _mla_attn(q_cat, k_cat, v_t, proj_b, batch, seq, tq, tk)
    mem_out_b = _mem_attn(proj_b, memkv, seq, B_MQ_OFF, B_MZ_OFF, kv_blk=2)
    out = _out_proj(mla_out, mem_out_b, b_w_out[0].astype(BF16), x1, b_post_norm)
    return out.reshape(batch, seq, D_MODEL)
```

```python
import functools

import numpy as np
import jax
import jax.numpy as jnp
from jax import lax
from jax.experimental import pallas as pl
from jax.experimental.pallas import tpu as pltpu

F32 = jnp.float32
BF16 = jnp.bfloat16

D_MODEL = 2048
N_MEM = 256
MEM_HEADS = 4
MEM_HEAD_DIM = 128
MEM_WIDTH = MEM_HEADS * MEM_HEAD_DIM
BRANCH_WIDTH = D_MODEL - MEM_WIDTH
GLA_HEADS = 4
GLA_DV = BRANCH_WIDTH // GLA_HEADS
GLA_DK = GLA_DV // 2
GLA_GATE_RANK = 16
GLA_GATE_NORM = 16.0
MLA_HEADS = 12
MLA_V_DIM = 128
MLA_NOPE_DIM = 128
MLA_ROPE_DIM = 64
MLA_QK_DIM = MLA_NOPE_DIM + MLA_ROPE_DIM
MLA_Q_RANK = 512
MLA_KV_RANK = 512
ROPE_THETA = 10000.0
EPS = 1e-6

LANES = 128
MXU_DIM = 256
GLA_DK_PAD = MXU_DIM
MLA_QK_PAD = MXU_DIM
BF16_SUBLANES = 16
MLA_VT_ROWS = MLA_V_DIM + BF16_SUBLANES
ATT_HEADS_PER_STEP = 3
ATT_PRODUCE_LEAD = 3
LOG2_E = 1.4426950408889634
MEM_Q_SCALE = MEM_HEAD_DIM ** -0.5 * LOG2_E
VMEM_LIMIT = 56 * 1024 * 1024

PROJ_TM = 1024
PROJ_NORM_ROWS = 256
REPACK_ROWS = 256
OUT_TM = 512
OUT_CHUNK = 256
GLA_SPAN = 256
GLA_SUB = 32
GLA_HEADS_PER_STEP = 4
GLA_SPANS_PER_STEP = 4
GLA_STAGE_SKEW = 0
MEM_TM = 4096
PREP_TM = 1024
ATT_TQ = 512
ATT_TK = 512

A_V_OFF = 0
A_Z_OFF = BRANCH_WIDTH
A_MQ_OFF = 2 * BRANCH_WIDTH
A_MZ_OFF = A_MQ_OFF + MEM_WIDTH
A_Q_OFF = A_MZ_OFF + MEM_WIDTH
A_K_OFF = A_Q_OFF + GLA_HEADS * GLA_DK_PAD
A_WIDTH = A_K_OFF + GLA_HEADS * GLA_DK_PAD
A_TN = 2048

B_Z_OFF = 0
B_CQ_OFF = BRANCH_WIDTH
B_MQ_OFF = B_CQ_OFF + MLA_Q_RANK
B_MZ_OFF = B_MQ_OFF + MEM_WIDTH
B_C_OFF = B_MZ_OFF + MEM_WIDTH
B_KR_OFF = B_C_OFF + MLA_KV_RANK
B_KROT_OFF = B_KR_OFF + LANES
B_WIDTH = B_KROT_OFF + LANES
B_TN = 1280


def _silu(z):
    return z * (1.0 / (1.0 + jnp.exp2(z * -LOG2_E)))


def _scaled_log_sigmoid(g, scale):
    return (jnp.minimum(g, 0.0) - jnp.log(1.0 + jnp.exp2(jnp.abs(g) * -LOG2_E))) * scale


def _dot(a, b):
    return jnp.dot(a, b, preferred_element_type=F32)


def _dot_nt(a, b):
    return lax.dot_general(a, b, (((1,), (1,)), ((), ())), preferred_element_type=F32)


def _dot_tn(a, b):
    return lax.dot_general(a, b, (((0,), (0,)), ((), ())), preferred_element_type=F32)


def _norm_matmul_kernel(x_ref, w_ref, o_ref, h_ref, r_ref, *, norm_rows, w_transposed):
    j = pl.program_id(1)

    def scaled_product(h, r):
        acc = _dot_nt(h, w_ref[...]) if w_transposed else _dot(h, w_ref[...])
        return (acc * r).astype(o_ref.dtype)

    @pl.when(j == 0)
    def _():
        tm = x_ref.shape[0]
        for c in range(tm // norm_rows):
            rows = pl.ds(c * norm_rows, norm_rows)
            x = x_ref[rows, :]
            r = lax.rsqrt(jnp.mean(x * x, axis=-1, keepdims=True) + EPS)
            h = x.astype(BF16)
            r_ref[rows, :] = r
            h_ref[rows, :] = h
            o_ref[rows, :] = scaled_product(h, r)

    @pl.when(j > 0)
    def _():
        o_ref[...] = scaled_product(h_ref[...], r_ref[...])


def _norm_matmul(x, w, *, tm, tn, w_transposed=False):
    m, d = x.shape
    n = w.shape[0] if w_transposed else w.shape[1]
    tm = min(tm, m)
    kern = functools.partial(_norm_matmul_kernel, norm_rows=min(PROJ_NORM_ROWS, tm),
                             w_transposed=w_transposed)
    w_spec = (pl.BlockSpec((tn, d), lambda i, j: (j, 0)) if w_transposed
              else pl.BlockSpec((d, tn), lambda i, j: (0, j)))
    return pl.pallas_call(
        kern,
        grid=(m // tm, n // tn),
        in_specs=[
            pl.BlockSpec((tm, d), lambda i, j: (i, 0)),
            w_spec,
        ],
        out_specs=pl.BlockSpec((tm, tn), lambda i, j: (i, j)),
        out_shape=jax.ShapeDtypeStruct((m, n), BF16),
        scratch_shapes=[pltpu.VMEM((tm, d), BF16), pltpu.VMEM((tm, 1), F32)],
        compiler_params=pltpu.CompilerParams(
            dimension_semantics=("parallel", "arbitrary"), vmem_limit_bytes=VMEM_LIMIT),
        name="norm_matmul",
    )(x, w)


def _out_proj_kernel(a_ref, m_ref, wa_ref, wm_ref, x_ref, g_ref, o_ref, *, chunk):
    for c in range(a_ref.shape[0] // chunk):
        rows = pl.ds(c * chunk, chunk)
        y = _dot(a_ref[rows, :], wa_ref[...]) + _dot(m_ref[rows, :], wm_ref[...])
        ms = jnp.mean(y * y, axis=-1, keepdims=True)
        o_ref[rows, :] = x_ref[rows, :] + y * lax.rsqrt(ms + EPS) * g_ref[...]


def _out_proj(a, mo, w_out, x, gain):
    t = x.shape[0]
    tm = min(OUT_TM, t)
    mem_blk = BRANCH_WIDTH // MEM_WIDTH
    return pl.pallas_call(
        functools.partial(_out_proj_kernel, chunk=min(OUT_CHUNK, tm)),
        grid=(t // tm,),
        in_specs=[
            pl.BlockSpec((tm, BRANCH_WIDTH), lambda i: (i, 0)),
            pl.BlockSpec((tm, MEM_WIDTH), lambda i: (i, 0)),
            pl.BlockSpec((BRANCH_WIDTH, D_MODEL), lambda i: (0, 0)),
            pl.BlockSpec((MEM_WIDTH, D_MODEL), lambda i: (mem_blk, 0)),
            pl.BlockSpec((tm, D_MODEL), lambda i: (i, 0)),
            pl.BlockSpec((1, D_MODEL), lambda i: (0, 0)),
        ],
        out_specs=pl.BlockSpec((tm, D_MODEL), lambda i: (i, 0)),
        out_shape=jax.ShapeDtypeStruct((t, D_MODEL), F32),
        compiler_params=pltpu.CompilerParams(
            dimension_semantics=("parallel",), vmem_limit_bytes=VMEM_LIMIT),
        name="out_proj",
    )(a, mo, w_out, w_out, x, gain)


def _gla_head_stages(h, row0, v_ref, z_ref, q_ref, k_ref, wg_ref, bg_ref, gn_ref, tril_ref, lvl_ref, o_ref,
                     s_ref, sub):
    span = tril_ref.shape[0]
    rows = slice(row0, row0 + span)
    n_sub = span // sub
    n_levels = n_sub.bit_length() - 1
    scale = GLA_DK ** -0.5
    kcols = slice(h * GLA_DK_PAD, (h + 1) * GLA_DK_PAD)
    vcols = slice(h * GLA_DV, (h + 1) * GLA_DV)

    qb = q_ref[rows, kcols]
    g = _dot(qb, wg_ref[h]) + bg_ref[h]
    yield
    log_a = _scaled_log_sigmoid(g, LOG2_E / GLA_GATE_NORM)
    a_hi = log_a.astype(BF16)
    rem = log_a - a_hi.astype(F32)
    a_mid = rem.astype(BF16)
    a_lo = (rem - a_mid.astype(F32)).astype(BF16)
    yield
    tril = tril_ref[...]
    cum = _dot(tril, a_hi) + _dot(tril, a_mid) + _dot(tril, a_lo)
    yield

    refs = jnp.concatenate([cum[i * sub:i * sub + 1] for i in range(n_sub)] + [cum[span - 1:span]], axis=0)
    own = refs[:n_sub]

    def ref_rows(index_of):
        return jnp.concatenate([refs[index_of(i):index_of(i) + 1] for i in range(n_sub)], axis=0)

    def per_row(f):
        return jnp.concatenate(
            [jnp.broadcast_to(f[i:i + 1], (sub, f.shape[1])) for i in range(n_sub)], axis=0)

    own_b = per_row(own)
    qd = qb.astype(F32) * jnp.exp2(cum - per_row(own - float(np.log2(scale))))
    kd = k_ref[rows, kcols].astype(F32) * jnp.exp2(own_b - cum)
    yield
    q_ops, k_ops = [qd.astype(BF16)], [kd.astype(BF16)]
    for lvl in range(1, n_levels + 1):
        n = 1 << (lvl - 1)
        block_start = ref_rows(lambda i: (i // n) * n)
        next_start = ref_rows(lambda i: (i // n + 1) * n)
        q_ops.append(q_ops[0] if n == 1 else (qd * per_row(jnp.exp2(own - block_start))).astype(BF16))
        k_ops.append((kd * per_row(jnp.exp2(next_start - own))).astype(BF16))
    q_state = (qd * per_row(jnp.exp2(own))).astype(BF16)
    k_state = (kd * per_row(jnp.exp2(refs[n_sub:] - own))).astype(BF16)
    yield
    atts = [_dot_nt(qo, ko) for qo, ko in zip(q_ops, k_ops)]
    yield
    level = lvl_ref[...]
    att = jnp.where(level == n_levels, atts[n_levels], 0.0)
    for lvl in reversed(range(n_levels)):
        att = jnp.where(level == lvl, atts[lvl], att)
    yield
    state = s_ref[h]
    v = v_ref[rows, vcols]
    o = _dot(jnp.concatenate([att.astype(BF16), q_state], axis=1),
             jnp.concatenate([v, state.astype(BF16)], axis=0))
    decay_col = jnp.transpose(jnp.exp2(cum[span - 8:span, :]))[:, 7:8]
    s_ref[h] = decay_col * state + _dot_tn(k_state, v)
    yield
    on = o * lax.rsqrt(jnp.mean(o * o, axis=-1, keepdims=True) + EPS) * gn_ref[...]
    o_ref[rows, vcols] = (on * _silu(z_ref[rows, vcols].astype(F32))).astype(o_ref.dtype)
    yield


def _gla_kernel(v_ref, z_ref, q_ref, k_ref, wg_ref, bg_ref, gn_ref, tril_ref, lvl_ref, o_ref, s_ref, *,
                sub, skew):
    @pl.when(pl.program_id(2) == 0)
    def _():
        s_ref[...] = jnp.zeros_like(s_ref)

    span = tril_ref.shape[0]
    heads = [_gla_head_stages(h, row0, v_ref, z_ref, q_ref, k_ref, wg_ref, bg_ref, gn_ref, tril_ref,
                              lvl_ref, o_ref, s_ref, sub)
             for row0 in range(0, q_ref.shape[0], span) for h in range(s_ref.shape[0])]
    live = list(range(len(heads)))
    step = 0
    while live:
        for i in list(live):
            if step >= i * skew:
                try:
                    next(heads[i])
                except StopIteration:
                    live.remove(i)
        step += 1


def _prefix_matrix(rows):
    r = np.arange(rows)
    return jnp.asarray(r[:, None] >= r[None, :], BF16)


def _level_matrix(rows, sub):
    n_levels = (rows // sub).bit_length() - 1
    t, s = np.arange(rows)[:, None], np.arange(rows)[None, :]
    level = np.full((rows, rows), n_levels + 1, np.int32)
    level[(t // sub == s // sub) & (s <= t)] = 0
    for lvl in range(1, n_levels + 1):
        size = sub << (lvl - 1)
        level[((t // size) % 2 == 1) & (s // size == t // size - 1)] = lvl
    return jnp.asarray(level)


def _gla(proj, wg, bg, gn, batch, seq):
    span = min(GLA_SPAN, seq)
    assert span % GLA_SUB == 0 and (span // GLA_SUB) & (span // GLA_SUB - 1) == 0
    rows = span * min(GLA_SPANS_PER_STEP, seq // span)
    ns = seq // rows
    nh = GLA_HEADS_PER_STEP
    kern = functools.partial(_gla_kernel, sub=GLA_SUB, skew=GLA_STAGE_SKEW)
    vw, kw = nh * GLA_DV, nh * GLA_DK_PAD
    v_blk, z_blk = A_V_OFF // vw, A_Z_OFF // vw
    q_blk, k_blk = A_Q_OFF // kw, A_K_OFF // kw
    group = span
    return pl.pallas_call(
        kern,
        grid=(batch, GLA_HEADS // nh, ns),
        in_specs=[
            pl.BlockSpec((rows, vw), lambda b, g, s: (b * ns + s, v_blk + g)),
            pl.BlockSpec((rows, vw), lambda b, g, s: (b * ns + s, z_blk + g)),
            pl.BlockSpec((rows, kw), lambda b, g, s: (b * ns + s, q_blk + g)),
            pl.BlockSpec((rows, kw), lambda b, g, s: (b * ns + s, k_blk + g)),
            pl.BlockSpec((nh, GLA_DK_PAD, GLA_DK_PAD), lambda b, g, s: (g, 0, 0)),
            pl.BlockSpec((nh, 1, GLA_DK_PAD), lambda b, g, s: (g, 0, 0)),
            pl.BlockSpec((1, GLA_DV), lambda b, g, s: (0, 0)),
            pl.BlockSpec((group, group), lambda b, g, s: (0, 0)),
            pl.BlockSpec((group, group), lambda b, g, s: (0, 0)),
        ],
        out_specs=pl.BlockSpec((rows, vw), lambda b, g, s: (b * ns + s, g)),
        out_shape=jax.ShapeDtypeStruct((batch * seq, BRANCH_WIDTH), BF16),
        scratch_shapes=[pltpu.VMEM((nh, GLA_DK_PAD, GLA_DV), F32)],
        compiler_params=pltpu.CompilerParams(
            dimension_semantics=("parallel", "parallel", "arbitrary"), vmem_limit_bytes=VMEM_LIMIT),
        name="gla",
    )(proj, proj, proj, proj, wg, bg, gn, _prefix_matrix(group), _level_matrix(group, GLA_SUB))


def _mem_attn_kernel(q_ref, z_ref, k_ref, v_ref, o_ref):
    for h in range(MEM_HEADS):
        cols = slice(h * MEM_HEAD_DIM, (h + 1) * MEM_HEAD_DIM)
        s = _dot_nt(q_ref[:, cols], k_ref[:, cols])
        p = jnp.exp2(s - jnp.max(s, axis=-1, keepdims=True))
        l = jnp.sum(p, axis=-1, keepdims=True)
        o = _dot(p.astype(BF16), v_ref[:, cols]) * (1.0 / l)
        o_ref[:, cols] = (o * _silu(z_ref[:, cols].astype(F32))).astype(o_ref.dtype)


def _mem_attn(proj, memkv, seq, q_off, z_off, kv_blk):
    t = proj.shape[0]
    tm = min(MEM_TM, seq)
    per_batch = seq // tm
    q_blk, z_blk = q_off // MEM_WIDTH, z_off // MEM_WIDTH
    return pl.pallas_call(
        _mem_attn_kernel,
        grid=(t // tm,),
        in_specs=[
            pl.BlockSpec((tm, MEM_WIDTH), lambda i: (i, q_blk)),
            pl.BlockSpec((tm, MEM_WIDTH), lambda i: (i, z_blk)),
            pl.BlockSpec((N_MEM, MEM_WIDTH), lambda i: (i // per_batch, kv_blk)),
            pl.BlockSpec((N_MEM, MEM_WIDTH), lambda i: (i // per_batch, kv_blk + 1)),
        ],
        out_specs=pl.BlockSpec((tm, MEM_WIDTH), lambda i: (i, 0)),
        out_shape=jax.ShapeDtypeStruct((t, MEM_WIDTH), BF16),
        compiler_params=pltpu.CompilerParams(
            dimension_semantics=("parallel",), vmem_limit_bytes=VMEM_LIMIT),
        name="mem_attn",
    )(proj, proj, memkv, memkv)


def _rotary_tables(pos, freq, sign):
    rows = pos.shape[0]
    groups = LANES // (MLA_ROPE_DIM // 2)
    if rows % (8 * groups) != 0:
        ang = pos * freq
        return jnp.cos(ang), jnp.sin(ang) * sign
    blk = rows // groups
    lane_group = lax.broadcasted_iota(jnp.int32, (1, LANES), 1) // (MLA_ROPE_DIM // 2)
    packed_pos = jnp.zeros((blk, LANES), F32)
    for g in range(groups):
        packed_pos = jnp.where(lane_group == g, pos[g * blk:(g + 1) * blk], packed_pos)
    ang = packed_pos * freq
    tables = []
    for packed in (jnp.cos(ang), jnp.sin(ang)):
        blocks = []
        for g in range(groups):
            own = jnp.where(lane_group == g, packed, 0.0)
            spread = own
            for shift in range(1, groups):
                spread = spread + pltpu.roll(own, shift * (LANES // groups), axis=1)
            blocks.append(spread)
        tables.append(jnp.concatenate(blocks, axis=0))
    return tables[0], tables[1] * sign


def _mla_prep_kernel(cq_ref, c_ref, kr_ref, krot_ref, pos_ref, freq_ref, sign_ref, qg_ref, kvg_ref,
                     wuq_ref, wuk_ref, wuvt_ref, q_ref, k_ref, vt_ref, *, tk):
    tm = cq_ref.shape[0]
    n_nope = MLA_HEADS * MLA_NOPE_DIM
    n_rope = MLA_HEADS * MLA_ROPE_DIM

    cq = cq_ref[...].astype(F32)
    cqn = cq * lax.rsqrt(jnp.mean(cq * cq, axis=-1, keepdims=True) + EPS)
    cqn = (cqn * (qg_ref[...] * (MLA_QK_DIM ** -0.5 * LOG2_E))).astype(BF16)
    c = c_ref[...].astype(F32)
    cn = c * lax.rsqrt(jnp.mean(c * c, axis=-1, keepdims=True) + EPS)
    cn = (cn * kvg_ref[...]).astype(BF16)

    q_nope = _dot(cqn, wuq_ref[:, :n_nope])
    for h in range(MLA_HEADS):
        q_ref[h, :, :MLA_NOPE_DIM] = q_nope[:, h * MLA_NOPE_DIM:(h + 1) * MLA_NOPE_DIM].astype(BF16)

    cos, sin = _rotary_tables(pos_ref[...].astype(F32), freq_ref[...], sign_ref[...])

    k_nope = _dot(cn, wuk_ref[...])
    for h in range(MLA_HEADS):
        k_ref[h, :, :MLA_NOPE_DIM] = k_nope[:, h * MLA_NOPE_DIM:(h + 1) * MLA_NOPE_DIM].astype(BF16)
    v_t = _dot_nt(wuvt_ref[...], cn).astype(BF16)
    ones_rows = (lax.broadcasted_iota(jnp.int32, (MLA_VT_ROWS - MLA_V_DIM, tk), 0) == 0).astype(BF16)
    for h in range(MLA_HEADS):
        for u in range(tm // tk):
            vt_ref[h, u, :MLA_V_DIM, :] = v_t[h * MLA_V_DIM:(h + 1) * MLA_V_DIM, u * tk:(u + 1) * tk]
            vt_ref[h, u, MLA_V_DIM:, :] = ones_rows

    lane = lax.broadcasted_iota(jnp.int32, (1, LANES), 1)
    slot_mask = [(lane < MLA_ROPE_DIM).astype(F32), (lane >= MLA_ROPE_DIM).astype(F32)]
    q_r = _dot(cqn, wuq_ref[:, n_nope:n_nope + n_rope])
    q_rot = _dot(cqn, wuq_ref[:, n_nope + n_rope:])
    for u in range(MLA_HEADS // 2):
        cols = slice(u * LANES, (u + 1) * LANES)
        roped = q_r[:, cols] * cos + q_rot[:, cols] * sin
        for half in range(2):
            q_ref[2 * u + half, :, MLA_NOPE_DIM:] = (roped * slot_mask[half]).astype(BF16)
    k_rope = (kr_ref[...].astype(F32) * cos + krot_ref[...].astype(F32) * sin).astype(BF16)
    for h in range(MLA_HEADS):
        k_ref[h, :, MLA_NOPE_DIM:] = k_rope


def _mla_prep(proj, pos_col, freq, sign, q_gain, kv_gain, w_uq, w_uk, w_uv_t, batch, seq, tk):
    tm = min(PREP_TM, seq)
    per_batch = seq // tm
    kern = functools.partial(_mla_prep_kernel, tk=tk)
    const = lambda i: (0, 0)
    head_map = lambda i: (i // per_batch, 0, i % per_batch, 0)
    return pl.pallas_call(
        kern,
        grid=(batch * per_batch,),
        in_specs=[
            pl.BlockSpec((tm, MLA_Q_RANK), lambda i: (i, B_CQ_OFF // MLA_Q_RANK)),
            pl.BlockSpec((tm, MLA_KV_RANK), lambda i: (i, B_C_OFF // MLA_KV_RANK)),
            pl.BlockSpec((tm, LANES), lambda i: (i, B_KR_OFF // LANES)),
            pl.BlockSpec((tm, LANES), lambda i: (i, B_KROT_OFF // LANES)),
            pl.BlockSpec((tm, 1), lambda i: (i, 0)),
            pl.BlockSpec((1, LANES), const),
            pl.BlockSpec((1, LANES), const),
            pl.BlockSpec((1, MLA_Q_RANK), const),
            pl.BlockSpec((1, MLA_KV_RANK), const),
            pl.BlockSpec(w_uq.shape, const),
            pl.BlockSpec(w_uk.shape, const),
            pl.BlockSpec(w_uv_t.shape, const),
        ],
        out_specs=[
            pl.BlockSpec((None, MLA_HEADS, tm, MLA_QK_PAD), head_map),
            pl.BlockSpec((None, MLA_HEADS, tm, MLA_QK_PAD), head_map),
            pl.BlockSpec((None, MLA_HEADS, tm // tk, MLA_VT_ROWS, tk),
                         lambda i: (i // per_batch, 0, i % per_batch, 0, 0)),
        ],
        out_shape=[
            jax.ShapeDtypeStruct((batch, MLA_HEADS, seq, MLA_QK_PAD), BF16),
            jax.ShapeDtypeStruct((batch, MLA_HEADS, seq, MLA_QK_PAD), BF16),
            jax.ShapeDtypeStruct((batch, MLA_HEADS, seq // tk, MLA_VT_ROWS, tk), BF16),
        ],
        compiler_params=pltpu.CompilerParams(
            dimension_semantics=("parallel",), vmem_limit_bytes=VMEM_LIMIT),
        name="mla_prep",
    )(proj, proj, proj, proj, pos_col, freq, sign, q_gain, kv_gain, w_uq, w_uk, w_uv_t)


def _max_over_rows(x, ways=8):
    rows = x.shape[0]
    slab = rows // ways
    parts = [x[i * slab:(i + 1) * slab] for i in range(ways)] if slab >= 8 and rows % ways == 0 else [x]
    while len(parts) > 1:
        parts = [jnp.maximum(parts[i], parts[i + 1]) for i in range(0, len(parts), 2)]
    return jnp.max(parts[0], axis=0, keepdims=True)


def _mla_attn_kernel(q_ref, k_ref, vt_ref, z_ref, o_ref, acc_ref, m_ref, done_ref, s0_ref, s1_ref, s2_ref,
                     smax0_ref, smax1_ref, smax2_ref, *, tq, tk):
    n_heads, seq = q_ref.shape[0], q_ref.shape[1]
    n_q = seq // tq
    tn = MXU_DIM
    units = [(h, n) for h in range(n_heads) for n in range(tq // tn)]
    s_slots = (s0_ref, s1_ref, s2_ref)
    smax_slots = (smax0_ref, smax1_ref, smax2_ref)
    FIRST = 2

    def n_keys(n, diagonal):
        return min(tk, (n + 1) * tn) if diagonal else tk

    def produce(qi, kj, slot, u, diagonal):
        h, n = units[u]
        nk = n_keys(n, diagonal)
        krows = pl.ds(pl.multiple_of(kj * tk, tk), nk)
        qrows = pl.ds(pl.multiple_of(qi * tq + n * tn, tn), tn)
        s_t = _dot_nt(k_ref[h, krows, :], q_ref[h, qrows, :])
        if diagonal:
            kpos = lax.broadcasted_iota(jnp.int32, (nk, tn), 0)
            qpos = lax.broadcasted_iota(jnp.int32, (nk, tn), 1) + n * tn
            s_t = jnp.where(kpos <= qpos, s_t, -jnp.inf)
        s_slots[slot][u, :nk, :] = s_t
        smax_slots[slot][u] = _max_over_rows(s_t)

    def consume(kj, slot, u, diagonal):
        h, n = units[u]
        nk = n_keys(n, diagonal)
        cols = slice(n * tn, (n + 1) * tn)
        m_prev = m_ref[h, :, cols]
        m_new = jnp.maximum(m_prev, smax_slots[slot][u])
        alpha = jnp.exp2(m_prev - m_new)
        p = jnp.exp2(s_slots[slot][u, :nk, :] - m_new).astype(BF16)
        m_ref[h, :, cols] = m_new
        acc_ref[h, :, cols] = alpha * acc_ref[h, :, cols] + _dot(vt_ref[h, kj, :, :nk], p)

    def produce_all(qi, kj, slot, diagonal):
        for u in range(len(units)):
            produce(qi, kj, slot, u, diagonal)

    def write_out(h, qp):
        rows = pl.ds(qp * tq if isinstance(qp, int) else pl.multiple_of(qp * tq, tq), tq)
        cols = slice(h * MLA_V_DIM, (h + 1) * MLA_V_DIM)
        o = jnp.transpose(done_ref[h])
        o_ref[rows, cols] = (o * _silu(z_ref[rows, cols].astype(F32))).astype(o_ref.dtype)

    def step(kj, slot, diagonal, nxt=None, write_out_of=None):
        lead = ATT_PRODUCE_LEAD
        if nxt is not None:
            for u in range(min(lead, len(units))):
                produce(nxt[0], nxt[1], nxt[2], u, nxt[3])
        per_head = len(units) // n_heads
        for u in range(len(units)):
            consume(kj, slot, u, diagonal)
            if write_out_of is not None and u % per_head == per_head - 1:
                write_out(u // per_head, write_out_of)
            if nxt is not None and u + lead < len(units):
                produce(nxt[0], nxt[1], nxt[2], u + lead, nxt[3])

    def last_step(qi, slot):
        if n_q == 1:
            step(qi, slot, True)
            return

        @pl.when(qi < n_q - 1)
        def _():
            step(qi, slot, True, (qi + 1, 0, FIRST, False))

        last_q = n_q - 1
        if slot == (last_q - 1) % 2:
            @pl.when(qi == last_q)
            def _():
                step(qi, slot, True)

    def q_block(qi, carry):
        m_ref[...] = jnp.full_like(m_ref, -jnp.inf)
        acc_ref[...] = jnp.zeros_like(acc_ref)

        @pl.when(qi == 0)
        def _():
            step(0, FIRST, True)
            if n_q > 1:
                produce_all(1, 0, FIRST, False)

        @pl.when(qi == 1)
        def _():
            step(0, FIRST, False, (qi, 1, 0, True), write_out_of=qi - 1)

        @pl.when(qi > 1)
        def _():
            step(0, FIRST, False, (qi, 1, 0, False), write_out_of=qi - 1)

        def steady_pair(i, c):
            kj = 2 * i + 1
            step(kj, 0, False, (qi, kj + 1, 1, False))
            step(kj + 1, 1, False, (qi, kj + 2, 0, False))
            return c

        n_pairs = jnp.maximum(qi - 2, 0) // 2
        lax.fori_loop(0, n_pairs, steady_pair, 0)
        kj = 2 * n_pairs + 1
        left = qi - kj + 1

        @pl.when((qi > 0) & (left == 1))
        def _():
            last_step(qi, 0)

        @pl.when((qi > 0) & (left == 2))
        def _():
            step(kj, 0, False, (qi, kj + 1, 1, True))
            last_step(qi, 1)

        @pl.when((qi > 0) & (left == 3))
        def _():
            step(kj, 0, False, (qi, kj + 1, 1, False))
            step(kj + 1, 1, False, (qi, kj + 2, 0, True))
            last_step(qi, 0)

        for h in range(n_heads):
            acc = acc_ref[h]
            done_ref[h] = acc[:MLA_V_DIM] * (1.0 / acc[MLA_V_DIM:MLA_V_DIM + 1])
        return carry

    produce_all(0, 0, FIRST, True)
    lax.fori_loop(0, n_q, q_block, 0)
    for h in range(n_heads):
        write_out(h, n_q - 1)


def _mla_attn(q_cat, k_cat, v_t, proj, batch, seq, tq, tk):
    kern = functools.partial(_mla_attn_kernel, tq=tq, tk=tk)
    nb = seq // tk
    nh = ATT_HEADS_PER_STEP
    n_units = nh * (tq // MXU_DIM)
    width = nh * MLA_V_DIM
    z_blk = B_Z_OFF // width
    return pl.pallas_call(
        kern,
        grid=(batch, MLA_HEADS // nh),
        in_specs=[
            pl.BlockSpec((None, nh, seq, MLA_QK_PAD), lambda b, g: (b, g, 0, 0)),
            pl.BlockSpec((None, nh, seq, MLA_QK_PAD), lambda b, g: (b, g, 0, 0)),
            pl.BlockSpec((None, nh, nb, MLA_VT_ROWS, tk), lambda b, g: (b, g, 0, 0, 0)),
            pl.BlockSpec((seq, width), lambda b, g: (b, z_blk + g)),
        ],
        out_specs=pl.BlockSpec((seq, width), lambda b, g: (b, g)),
        out_shape=jax.ShapeDtypeStruct((batch * seq, BRANCH_WIDTH), BF16),
        scratch_shapes=[
            pltpu.VMEM((nh, MLA_VT_ROWS, tq), F32),
            pltpu.VMEM((nh, 1, tq), F32),
            pltpu.VMEM((nh, MLA_V_DIM, tq), F32),
            pltpu.VMEM((n_units, tk, MXU_DIM), F32),
            pltpu.VMEM((n_units, tk, MXU_DIM), F32),
            pltpu.VMEM((n_units, tk, MXU_DIM), F32),
            pltpu.VMEM((n_units, 1, MXU_DIM), F32),
            pltpu.VMEM((n_units, 1, MXU_DIM), F32),
            pltpu.VMEM((n_units, 1, MXU_DIM), F32),
        ],
        compiler_params=pltpu.CompilerParams(
            dimension_semantics=("parallel", "parallel"), vmem_limit_bytes=VMEM_LIMIT),
        name="mla_attn",
    )(q_cat, k_cat, v_t, proj)


def _pad_heads(w, heads, width, padded):
    lead = w.shape[:-1]
    w = w.reshape(lead + (heads, width))
    w = jnp.pad(w, [(0, 0)] * len(lead) + [(0, 0), (0, padded - width)])
    return w.reshape(lead + (heads * padded,))


def _repack_kernel(*refs, n_in, with_gain, pieces, zero_ranges, transposed):
    ins = refs[:n_in]
    gains = refs[n_in:2 * n_in] if with_gain else None
    o_ref = refs[-1]

    def window(lo, width):
        return (slice(lo, lo + width), slice(None)) if transposed else (slice(None), slice(lo, lo + width))

    for dst, idx, src, width, scale in pieces:
        val = ins[idx][window(src, width)]
        if with_gain:
            val = val * gains[idx][...]
        if scale != 1.0:
            val = val * scale
        o_ref[window(dst, width)] = val.astype(o_ref.dtype)
    for lo, hi in zero_ranges:
        shape = (hi - lo, o_ref.shape[1]) if transposed else (o_ref.shape[0], hi - lo)
        o_ref[window(lo, hi - lo)] = jnp.zeros(shape, o_ref.dtype)


def _repack(ws, gains, pieces, out_width, zero_ranges=(), transposed=False):
    d = ws[0].shape[1] if transposed else ws[0].shape[0]
    td = min(REPACK_ROWS, d)
    with_gain = gains is not None
    pieces = [tuple(p) if len(p) == 5 else tuple(p) + (1.0,) for p in pieces]
    covered = sorted([(dst, dst + w) for dst, _, _, w, _ in pieces] + list(zero_ranges))
    assert covered[0][0] == 0 and covered[-1][1] == out_width
    assert all(a[1] == b[0] for a, b in zip(covered, covered[1:])), "output features written exactly once"
    kern = functools.partial(_repack_kernel, n_in=len(ws), with_gain=with_gain, pieces=tuple(pieces),
                             zero_ranges=tuple(zero_ranges), transposed=transposed)

    def spec(n):
        return (pl.BlockSpec((n, td), lambda i: (0, i)) if transposed
                else pl.BlockSpec((td, n), lambda i: (i, 0)))

    in_specs = [spec(w.shape[0] if transposed else w.shape[1]) for w in ws]
    args = list(ws)
    if with_gain:
        in_specs += [spec(1) for _ in gains]
        args += [g.reshape((1, d) if transposed else (d, 1)) for g in gains]
    return pl.pallas_call(
        kern,
        grid=(d // td,),
        in_specs=in_specs,
        out_specs=spec(out_width),
        out_shape=jax.ShapeDtypeStruct((out_width, d) if transposed else (d, out_width), BF16),
        compiler_params=pltpu.CompilerParams(
            dimension_semantics=("parallel",), vmem_limit_bytes=VMEM_LIMIT),
        name="repack",
    )(*args)


def _layer_a_weights(w_in, gain, w_g2, b_g):
    hk = GLA_HEADS * GLA_DK
    o = [int(v) for v in np.cumsum([0, hk, hk, BRANCH_WIDTH, GLA_GATE_RANK, BRANCH_WIDTH, MEM_WIDTH,
                                    MEM_WIDTH])]
    q_src, k_src, v_src, lr_src, z_src = o[0], o[1], o[2], o[3], o[4]
    pieces = [(A_V_OFF, 0, v_src, BRANCH_WIDTH), (A_Z_OFF, 0, z_src, BRANCH_WIDTH),
              (A_MQ_OFF, 0, z_src + BRANCH_WIDTH, MEM_WIDTH, MEM_Q_SCALE),
              (A_MZ_OFF, 0, z_src + BRANCH_WIDTH + MEM_WIDTH, MEM_WIDTH)]
    zeros = []
    for h in range(GLA_HEADS):
        q_dst, k_dst = A_Q_OFF + h * GLA_DK_PAD, A_K_OFF + h * GLA_DK_PAD
        pieces += [(q_dst, 0, q_src + h * GLA_DK, GLA_DK), (q_dst + GLA_DK, 0, lr_src, GLA_GATE_RANK),
                   (k_dst, 0, k_src + h * GLA_DK, GLA_DK)]
        zeros += [(q_dst + GLA_DK + GLA_GATE_RANK, q_dst + GLA_DK_PAD), (k_dst + GLA_DK, k_dst + GLA_DK_PAD)]
    w_all_t = _repack([w_in.T], [gain], pieces, A_WIDTH, zeros, transposed=True)
    wg = _pad_heads(w_g2, GLA_HEADS, GLA_DK, GLA_DK_PAD)
    wg = wg.reshape(GLA_GATE_RANK, GLA_HEADS, GLA_DK_PAD).transpose(1, 0, 2)
    wg = jnp.pad(wg, [(0, 0), (GLA_DK, GLA_DK_PAD - GLA_DK - GLA_GATE_RANK), (0, 0)]).astype(BF16)
    bg = _pad_heads(b_g, GLA_HEADS, GLA_DK, GLA_DK_PAD).reshape(GLA_HEADS, 1, GLA_DK_PAD)
    return w_all_t, wg, bg


def _layer_b_weights(w_in, gain_in, w_dkv, gain_dkv):
    half = MLA_ROPE_DIM // 2
    pieces = [(B_Z_OFF, 0, MLA_Q_RANK, BRANCH_WIDTH), (B_CQ_OFF, 0, 0, MLA_Q_RANK),
              (B_MQ_OFF, 0, MLA_Q_RANK + BRANCH_WIDTH, MEM_WIDTH, MEM_Q_SCALE),
              (B_MZ_OFF, 0, MLA_Q_RANK + BRANCH_WIDTH + MEM_WIDTH, MEM_WIDTH), (B_C_OFF, 1, 0, MLA_KV_RANK)]
    for rep in range(LANES // MLA_ROPE_DIM):
        kr, krot = B_KR_OFF + rep * MLA_ROPE_DIM, B_KROT_OFF + rep * MLA_ROPE_DIM
        pieces += [(kr, 1, MLA_KV_RANK, MLA_ROPE_DIM),
                   (krot, 1, MLA_KV_RANK + half, half), (krot + half, 1, MLA_KV_RANK, half)]
    return _repack([w_in, w_dkv], [gain_in, gain_dkv], pieces, B_WIDTH)


def _uq_weights(w_uq):
    half = MLA_ROPE_DIM // 2
    n_nope, n_rope = MLA_HEADS * MLA_NOPE_DIM, MLA_HEADS * MLA_ROPE_DIM
    pieces = []
    for h in range(MLA_HEADS):
        src = h * MLA_QK_DIM
        rope, rot = n_nope + h * MLA_ROPE_DIM, n_nope + n_rope + h * MLA_ROPE_DIM
        pieces += [(h * MLA_NOPE_DIM, 0, src, MLA_NOPE_DIM), (rope, 0, src + MLA_NOPE_DIM, MLA_ROPE_DIM),
                   (rot, 0, src + MLA_NOPE_DIM + half, half), (rot + half, 0, src + MLA_NOPE_DIM, half)]
    return _repack([w_uq], None, pieces, n_nope + 2 * n_rope)


def _rope_tables():
    r = MLA_ROPE_DIM
    freqs = ROPE_THETA ** (-jnp.arange(0, r, 2, dtype=F32) / r)
    freq = jnp.tile(freqs, LANES // (r // 2)).reshape(1, LANES)
    sign = jnp.tile(jnp.concatenate([-jnp.ones(r // 2, F32), jnp.ones(r // 2, F32)]),
                    LANES // r).reshape(1, LANES)
    return freq, sign


def kernel(x, mem, positions, a_pre_norm, a_w_in, a_w_g2, a_b_g, a_gla_norm, a_mem_norm, a_w_mem_kv,
           a_w_out, a_post_norm, kv_in_norm, w_dkv, kv_norm, w_uk, w_uv, b_pre_norm, b_w_in, b_q_norm,
           b_w_uq, b_mem_norm, b_w_mem_kv, b_w_out, b_post_norm):
    assert a_w_in.shape[0] == 1 and b_w_in.shape[0] == 1, "one A layer followed by one B layer"
    batch, seq, _ = x.shape
    t = batch * seq
    x2 = x.reshape(t, D_MODEL)

    kv_w = 2 * MEM_WIDTH
    w_memkv = _repack([a_w_mem_kv[0], b_w_mem_kv[0]], [a_mem_norm[0], b_mem_norm[0]],
                      [(0, 0, 0, kv_w), (kv_w, 1, 0, kv_w)], 2 * kv_w)
    memkv = _norm_matmul(mem.reshape(batch * N_MEM, D_MODEL), w_memkv, tm=batch * N_MEM, tn=2 * MEM_WIDTH)

    w_a_t, wg, bg = _layer_a_weights(a_w_in[0], a_pre_norm[0], a_w_g2[0], a_b_g[0])
    proj_a = _norm_matmul(x2, w_a_t, tm=PROJ_TM, tn=A_TN, w_transposed=True)
    gla_out = _gla(proj_a, wg, bg, a_gla_norm, batch, seq)
    mem_out = _mem_attn(proj_a, memkv, seq, A_MQ_OFF, A_MZ_OFF, kv_blk=0)
    x1 = _out_proj(gla_out, mem_out, a_w_out[0].astype(BF16), x2, a_post_norm)

    w_b = _layer_b_weights(b_w_in[0], b_pre_norm[0], w_dkv, kv_in_norm)
    proj_b = _norm_matmul(x1, w_b, tm=PROJ_TM, tn=B_TN)
    freq, sign = _rope_tables()
    tk = min(ATT_TK, seq)
    tq = min(ATT_TQ, seq)
    q_cat, k_cat, v_t = _mla_prep(
        proj_b, positions.reshape(t, 1), freq, sign, b_q_norm, kv_norm[None, :],
        _uq_weights(b_w_uq[0]), w_uk.astype(BF16), w_uv.T.astype(BF16), batch, seq, tk)
    mla_out = _mla_attn(q_cat, k_cat, v_t, proj_b, batch, seq, tq, tk)
    mem_out_b = _mem_attn(proj_b, memkv, seq, B_MQ_OFF, B_MZ_OFF, kv_blk=2)
    out = _out_proj(mla_out, mem_out_b, b_w_out[0].astype(BF16), x1, b_post_norm)
    return out.reshape(batch, seq, D_MODEL)
```
